```python
import jax
import jax.numpy as jnp
from jax import lax
import numpy as np

D_MODEL = 1024
BATCH = 8
SEQ = 2048
DEPTH = 1

GRID_W = 64
CTX_LEN = 256
ATT_HEADS = 16
ATT_KV_HEADS = 4
ATT_HEAD_DIM = 64
ATT_GROUP = ATT_HEADS // ATT_KV_HEADS
WINDOW = 128
ATT_BLOCK = 128
ROPE_BASE = 10000.0
ATT_SCALE = ATT_HEAD_DIM ** -0.5
ML_HEADS = 4
ML_QK_DIM = 128
ML_V_DIM = 256
ML_CHUNK = 128
D_FF = 2816
CONV_W = 3
EPS = 1e-6
NEG_INF = -1e30

ATT_Q_W = ATT_HEADS * ATT_HEAD_DIM
ATT_KV_W = ATT_KV_HEADS * ATT_HEAD_DIM
ML_QK_W = ML_HEADS * ML_QK_DIM
ML_V_W = ML_HEADS * ML_V_DIM
ML_GATE_W = 2 * 2 * ML_HEADS
IN_SPLITS = (ATT_Q_W, ATT_KV_W, ATT_KV_W, ML_QK_W, ML_QK_W, ML_V_W, ML_V_W, ML_GATE_W, D_MODEL, D_MODEL)
IN_W = ATT_Q_W + 2 * ATT_KV_W + 2 * ML_QK_W + 2 * ML_V_W + ML_GATE_W + 2 * D_MODEL

kernel_name = "hybrid_mlstm_swa_dit_layer"


def rms_norm(x, w):
    xf = x.astype(jnp.float32)
    y = xf * lax.rsqrt(jnp.mean(xf * xf, axis=-1, keepdims=True) + EPS)
    return (y * w.astype(jnp.float32)).astype(x.dtype)


def modulate(h, shift, scale):
    return h * (1 + scale[:, None, :]) + shift[:, None, :]


def heads(a, n_heads, head_dim):
    return a.reshape(a.shape[0], a.shape[1], n_heads, head_dim)


def split_in_proj(p):
    parts, start = [], 0
    for width in IN_SPLITS:
        parts.append(p[..., start:start + width])
        start += width
    return parts


def axial_rope(rows):
    row = jnp.repeat(jnp.arange(rows, dtype=jnp.float32), GRID_W)
    col = jnp.tile(jnp.arange(GRID_W, dtype=jnp.float32), rows)
    n_freq = ATT_HEAD_DIM // 4
    inv_freq = ROPE_BASE ** (-jnp.arange(n_freq, dtype=jnp.float32) / n_freq)
    ang = jnp.concatenate([row[:, None] * inv_freq, col[:, None] * inv_freq], axis=-1)
    return jnp.cos(ang)[:, None, :], jnp.sin(ang)[:, None, :]


def apply_rope(x, cos, sin):
    xf = x.astype(jnp.float32)
    half = x.shape[-1] // 2
    x1, x2 = xf[..., :half], xf[..., half:]
    return jnp.concatenate([x1 * cos - x2 * sin, x1 * sin + x2 * cos], axis=-1).astype(x.dtype)


def windowed_gqa(q, k, v, k_ctx, v_ctx, sink):
    B, L, H, dh = q.shape
    T = ATT_BLOCK
    nb = L // T
    C = k_ctx.shape[1]
    qb = q.reshape(B, nb, T, ATT_KV_HEADS, ATT_GROUP, dh).swapaxes(0, 1)

    def band(a):
        ap = jnp.pad(a, ((0, 0), (T, T), (0, 0), (0, 0))).reshape(B, nb + 2, T, ATT_KV_HEADS, dh)
        w = jnp.concatenate([ap[:, :-2], ap[:, 1:-1], ap[:, 2:]], axis=2)
        return w.swapaxes(0, 1)

    kb, vb = band(k), band(v)
    sink_b = jnp.broadcast_to(sink.astype(jnp.float32).reshape(1, ATT_KV_HEADS, ATT_GROUP, 1, 1),
                              (B, ATT_KV_HEADS, ATT_GROUP, T, 1))
    key_off = jnp.arange(3 * T) - T
    rel = key_off[None, :] - jnp.arange(T)[:, None]

    def one_block(args):
        blk, qblk, kblk, vblk = args
        k_abs = blk * T + key_off
        valid = (jnp.abs(rel) <= WINDOW) & ((k_abs >= 0) & (k_abs < L))[None, :]
        s_lat = jnp.einsum('btkgd,bskd->bkgts', qblk, kblk).astype(jnp.float32) * ATT_SCALE
        s_lat = jnp.where(valid, s_lat, NEG_INF)
        s_ctx = jnp.einsum('btkgd,bskd->bkgts', qblk, k_ctx).astype(jnp.float32) * ATT_SCALE
        p = jax.nn.softmax(jnp.concatenate([s_lat, s_ctx, sink_b], axis=-1), axis=-1).astype(v.dtype)
        return (jnp.einsum('bkgts,bskd->btkgd', p[..., :3 * T], vblk)
                + jnp.einsum('bkgts,bskd->btkgd', p[..., 3 * T:3 * T + C], v_ctx))

    o = lax.map(one_block, (jnp.arange(nb), qb, kb, vb))
    return o.swapaxes(0, 1).reshape(B, L, H * dh)


def context_gqa(q_ctx, k_ctx, v_ctx, sink):
    B, C, H, dh = q_ctx.shape
    qg = q_ctx.reshape(B, C, ATT_KV_HEADS, ATT_GROUP, dh)
    s = jnp.einsum('btkgd,bskd->bkgts', qg, k_ctx).astype(jnp.float32) * ATT_SCALE
    sink_b = jnp.broadcast_to(sink.astype(jnp.float32).reshape(1, ATT_KV_HEADS, ATT_GROUP, 1, 1),
                              (B, ATT_KV_HEADS, ATT_GROUP, C, 1))
    p = jax.nn.softmax(jnp.concatenate([s, sink_b], axis=-1), axis=-1).astype(v_ctx.dtype)
    return jnp.einsum('bkgts,bskd->btkgd', p[..., :C], v_ctx).reshape(B, C, H * dh)


def mlstm_chunkwise(q, k, v, i_pre, f_pre, state):
    B, L, H, dqk = q.shape
    dv = v.shape[-1]
    T = ML_CHUNK
    nc = L // T

    def chunks(a):
        return a.astype(jnp.float32).reshape(B, nc, T, *a.shape[2:]).swapaxes(0, 1)

    qs, vs = chunks(q), chunks(v)
    ks = chunks(k) * (dqk ** -0.5)
    i_s = chunks(i_pre)
    lf_s = jax.nn.log_sigmoid(chunks(f_pre))
    tril = jnp.tril(jnp.ones((T, T), dtype=bool))

    def step(carry, xs):
        Cm, n, m = carry
        qc, kc, vc, ic, lfc = xs
        b = jnp.cumsum(lfc, axis=1).transpose(0, 2, 1)
        ih = ic.transpose(0, 2, 1)
        dmat = jnp.where(tril, b[..., :, None] - b[..., None, :] + ih[..., None, :], -jnp.inf)
        inter = b + m[..., None]
        m_t = jnp.maximum(inter, dmat.max(axis=-1))
        s = jnp.einsum('bthd,bshd->bhts', qc, kc) * jnp.exp(dmat - m_t[..., None])
        w_inter = jnp.exp(inter - m_t)
        num = (jnp.einsum('bhts,bshv->bthv', s, vc)
               + jnp.einsum('bthd,bhdv->bthv', qc, Cm) * w_inter.transpose(0, 2, 1)[..., None])
        nq = s.sum(axis=-1) + w_inter * jnp.einsum('bthd,bhd->bht', qc, n)
        den = jnp.maximum(jnp.abs(nq), jnp.exp(-m_t))
        h = num / den.transpose(0, 2, 1)[..., None]
        b_last = b[..., -1]
        a = b_last[..., None] - b + ih
        m_new = jnp.maximum(b_last + m, a.max(axis=-1))
        wk = jnp.exp(a - m_new[..., None]).transpose(0, 2, 1)[..., None]
        decay = jnp.exp(b_last + m - m_new)
        C_new = decay[..., None, None] * Cm + jnp.einsum('bshd,bshv->bhdv', kc * wk, vc)
        n_new = decay[..., None] * n + jnp.einsum('bshd->bhd', kc * wk)
        return (C_new, n_new, m_new), h

    state, hs = lax.scan(step, state, (qs, ks, vs, i_s, lf_s))
    return hs.swapaxes(0, 1).reshape(B, L, H, dv), state


def mlstm_bidirectional(q, k, v, g, q_c, k_c, v_c, g_c):
    B = q.shape[0]
    init = (jnp.zeros((B, ML_HEADS, ML_QK_DIM, ML_V_DIM), jnp.float32),
            jnp.zeros((B, ML_HEADS, ML_QK_DIM), jnp.float32),
            jnp.zeros((B, ML_HEADS), jnp.float32))

    def fl(a):
        return jnp.flip(a, axis=1)

    hc_f, st_f = mlstm_chunkwise(q_c, k_c, v_c, g_c[:, :, 0, 0], g_c[:, :, 0, 1], init)
    hc_b, st_b = mlstm_chunkwise(fl(q_c), fl(k_c), fl(v_c), fl(g_c[:, :, 1, 0]), fl(g_c[:, :, 1, 1]), init)
    h_f, _ = mlstm_chunkwise(q, k, v, g[:, :, 0, 0], g[:, :, 0, 1], st_f)
    h_b, _ = mlstm_chunkwise(fl(q), fl(k), fl(v), fl(g[:, :, 1, 0]), fl(g[:, :, 1, 1]), st_b)
    return h_f + fl(h_b), hc_f + fl(hc_b)


def mlstm_readout(h_tilde, o_pre, norm_w):
    B, L = h_tilde.shape[:2]
    hn = rms_norm(h_tilde, norm_w.reshape(ML_HEADS, ML_V_DIM)).reshape(B, L, ML_V_W)
    return hn.astype(o_pre.dtype) * jax.nn.sigmoid(o_pre)


def merge_branches(att, ml, ga_pre, gm_pre, w_branch_att, w_branch_ml, w_out):
    y = jax.nn.sigmoid(ga_pre) * (att @ w_branch_att) + jax.nn.sigmoid(gm_pre) * (ml @ w_branch_ml)
    return y @ w_out


def conv_ffn(h, w_up, conv_w, conv_b, w_down):
    L = h.shape[1]
    u = h @ w_up
    r = CONV_W // 2
    up = jnp.pad(u, ((0, 0), (r, r), (0, 0)))
    acc = conv_b
    for j in range(CONV_W):
        acc = acc + up[:, j:j + L] * conv_w[j]
    a, g = jnp.split(acc, 2, axis=-1)
    return (jax.nn.silu(g) * a) @ w_down


def setup_inputs(seed: int = 0) -> dict:
    key = jax.random.key(seed)
    ks = jax.random.split(key, 24)
    f32 = jnp.float32

    def nrm(k, shape, scale):
        return jax.random.normal(k, shape, f32) * scale

    gate_offset = jnp.array([0.0, 3.0], f32).reshape(1, 1, 2, 1)
    return {
        "x": nrm(ks[0], (BATCH, SEQ, D_MODEL), 1.0),
        "c": nrm(ks[1], (BATCH, D_MODEL), 1.0),
        "ctx": nrm(ks[2], (BATCH, CTX_LEN, D_MODEL), 1.0),
        "c_ctx": nrm(ks[3], (D_MODEL,), 1.0),
        "w_mod": nrm(ks[4], (DEPTH, D_MODEL, 6 * D_MODEL), 0.5 * D_MODEL ** -0.5),
        "b_mod": nrm(ks[5], (DEPTH, 6 * D_MODEL), 0.02),
        "norm1_w": 1.0 + nrm(ks[6], (DEPTH, D_MODEL), 0.02),
        "w_in": nrm(ks[7], (DEPTH, D_MODEL, IN_W), D_MODEL ** -0.5),
        "q_norm_w": 1.0 + nrm(ks[8], (DEPTH, ATT_HEAD_DIM), 0.02),
        "k_norm_w": 1.0 + nrm(ks[9], (DEPTH, ATT_HEAD_DIM), 0.02),
        "attn_sink": nrm(ks[10], (DEPTH, ATT_HEADS), 0.5),
        "ml_gate_b": gate_offset + nrm(ks[11], (DEPTH, 2, 2, ML_HEADS), 0.3),
        "ml_norm_w": 1.0 + nrm(ks[12], (DEPTH, ML_V_W), 0.02),
        "w_branch_att": nrm(ks[13], (DEPTH, ATT_Q_W, D_MODEL), ATT_Q_W ** -0.5),
        "w_branch_ml": nrm(ks[14], (DEPTH, ML_V_W, D_MODEL), ML_V_W ** -0.5),
        "w_out": nrm(ks[15], (DEPTH, D_MODEL, D_MODEL), D_MODEL ** -0.5),
        "norm2_w": 1.0 + nrm(ks[16], (DEPTH, D_MODEL), 0.02),
        "w_up": nrm(ks[17], (DEPTH, D_MODEL, 2 * D_FF), D_MODEL ** -0.5),
        "conv_w": nrm(ks[18], (DEPTH, CONV_W, 2 * D_FF), CONV_W ** -0.5),
        "conv_b": nrm(ks[19], (DEPTH, 2 * D_FF), 0.02),
        "w_down": nrm(ks[20], (DEPTH, D_FF, D_MODEL), D_FF ** -0.5),
    }


def reference(x, c, ctx, c_ctx, w_mod, b_mod, norm1_w, w_in, q_norm_w, k_norm_w, attn_sink,
              ml_gate_b, ml_norm_w, w_branch_att, w_branch_ml, w_out, norm2_w, w_up, conv_w, conv_b, w_down):
    B, L, _ = x.shape
    C = ctx.shape[1]
    rows = L // GRID_W
    cos, sin = axial_rope(rows)
    for l in range(DEPTH):
        mod = jax.nn.silu(c) @ w_mod[l] + b_mod[l]
        mod_c = jax.nn.silu(c_ctx)[None, :] @ w_mod[l] + b_mod[l]
        sh1, sc1, g1, sh2, sc2, g2 = jnp.split(mod, 6, axis=-1)
        sh1c, sc1c, g1c, sh2c, sc2c, g2c = jnp.split(mod_c, 6, axis=-1)

        h = modulate(rms_norm(x, norm1_w[l]), sh1, sc1)
        hc = modulate(rms_norm(ctx, norm1_w[l]), sh1c, sc1c)
        aq, ak, av, mq, mk, mv, mo, mg, ga, gm = split_in_proj(h @ w_in[l])
        aqc, akc, avc, mqc, mkc, mvc, moc, mgc, gac, gmc = split_in_proj(hc @ w_in[l])

        q = apply_rope(rms_norm(heads(aq, ATT_HEADS, ATT_HEAD_DIM), q_norm_w[l]), cos, sin)
        k = apply_rope(rms_norm(heads(ak, ATT_KV_HEADS, ATT_HEAD_DIM), k_norm_w[l]), cos, sin)
        v = heads(av, ATT_KV_HEADS, ATT_HEAD_DIM)
        k_ctx = rms_norm(heads(akc, ATT_KV_HEADS, ATT_HEAD_DIM), k_norm_w[l])
        v_ctx = heads(avc, ATT_KV_HEADS, ATT_HEAD_DIM)
        att = windowed_gqa(q, k, v, k_ctx, v_ctx, attn_sink[l])

        gates = mg.reshape(B, L, 2, 2, ML_HEADS) + ml_gate_b[l]
        gates_c = mgc.reshape(B, C, 2, 2, ML_HEADS) + ml_gate_b[l]
        ht, ht_c = mlstm_bidirectional(
            heads(mq, ML_HEADS, ML_QK_DIM), heads(mk, ML_HEADS, ML_QK_DIM), heads(mv, ML_HEADS, ML_V_DIM), gates,
            heads(mqc, ML_HEADS, ML_QK_DIM), heads(mkc, ML_HEADS, ML_QK_DIM), heads(mvc, ML_HEADS, ML_V_DIM), gates_c)
        ml = mlstm_readout(ht, mo, ml_norm_w[l])

        y = merge_branches(att, ml, ga, gm, w_branch_att[l], w_branch_ml[l], w_out[l])
        x_mid = x + g1[:, None, :] * y

        if l < DEPTH - 1:
            q_c = rms_norm(heads(aqc, ATT_HEADS, ATT_HEAD_DIM), q_norm_w[l])
            att_c = context_gqa(q_c, k_ctx, v_ctx, attn_sink[l])
            ml_c = mlstm_readout(ht_c, moc, ml_norm_w[l])
            y_c = merge_branches(att_c, ml_c, gac, gmc, w_branch_att[l], w_branch_ml[l], w_out[l])
            ctx = ctx + g1c[:, None, :] * y_c
            ctx = ctx + g2c[:, None, :] * conv_ffn(modulate(rms_norm(ctx, norm2_w[l]), sh2c, sc2c),
                                                   w_up[l], conv_w[l], conv_b[l], w_down[l])

        x = x_mid + g2[:, None, :] * conv_ffn(modulate(rms_norm(x_mid, norm2_w[l]), sh2, sc2),
                                              w_up[l], conv_w[l], conv_b[l], w_down[l])
    return x
```

```python
import functools

import jax
import jax.numpy as jnp
import numpy as np
from jax import lax
from jax.experimental import pallas as pl
from jax.experimental.pallas import tpu as pltpu

D = 1024
GRID_W = 64
ATT_HEADS = 16
ATT_KV_HEADS = 4
ATT_HEAD_DIM = 64
ATT_GROUP = ATT_HEADS // ATT_KV_HEADS
ATT_BLOCK = 128
WINDOW = 128
ROPE_BASE = 10000.0
ATT_SCALE = ATT_HEAD_DIM ** -0.5
ML_HEADS = 4
ML_QK_DIM = 128
ML_V_DIM = 256
ML_CHUNK = 128
D_FF = 2816
EPS = 1e-6
NEG_INF = -1e30

ATT_Q_W = ATT_HEADS * ATT_HEAD_DIM
ATT_KV_W = ATT_KV_HEADS * ATT_HEAD_DIM
ML_QK_W = ML_HEADS * ML_QK_DIM
ML_V_W = ML_HEADS * ML_V_DIM
ML_GATE_W = 2 * 2 * ML_HEADS

LANES = 128
VMEM_LIMIT = 56 * 1024 * 1024

BF16 = jnp.bfloat16
F32 = jnp.float32

_O_AQ = 0
_O_AK = _O_AQ + ATT_Q_W
_O_AV = _O_AK + ATT_KV_W
_O_MQ = _O_AV + ATT_KV_W
_O_MK = _O_MQ + ML_QK_W
_O_MV = _O_MK + ML_QK_W
_O_MO = _O_MV + ML_V_W
_O_MG = _O_MO + ML_V_W
_O_GA = _O_MG + ML_GATE_W
_O_GM = _O_GA + D

QK_W = ATT_Q_W + ATT_KV_W
_P_QK = 0
_P_V = _P_QK + QK_W
_P_MQ = _P_V + ATT_KV_W
_P_MV = _P_MQ + ML_QK_W
_P_MO = _P_MV + ML_V_W
_P_GA = _P_MO + ML_V_W
_P_GM = _P_GA + D
_P_MG = _P_GM + D
_P_END = _P_MG + LANES


def _dot(a, b):
    return jnp.dot(a, b, preferred_element_type=F32)


def _dot_nt(a, b):
    return lax.dot_general(a, b, (((1,), (1,)), ((), ())), preferred_element_type=F32)


def _cparams(sem):
    return pltpu.CompilerParams(dimension_semantics=sem, vmem_limit_bytes=VMEM_LIMIT)


def _mod_kernel(c_ref, w_ref, b_ref, o_ref):
    c = c_ref[...]
    a = c * jax.nn.sigmoid(c)
    o_ref[...] = jnp.dot(a, w_ref[...], preferred_element_type=F32,
                         precision=lax.Precision.HIGHEST) + b_ref[...]


def _mod_call(cc, w_mod, b_mod):
    rows = cc.shape[0]
    n = w_mod.shape[1]
    tn = 1536
    return pl.pallas_call(
        _mod_kernel,
        grid=(n // tn,),
        in_specs=[pl.BlockSpec((rows, D), lambda j: (0, 0)),
                  pl.BlockSpec((D, tn), lambda j: (0, j)),
                  pl.BlockSpec((1, tn), lambda j: (0, j))],
        out_specs=pl.BlockSpec((rows, tn), lambda j: (0, j)),
        out_shape=jax.ShapeDtypeStruct((rows, n), F32),
        compiler_params=_cparams(("arbitrary",)),
        name="mod",
    )(cc, w_mod, b_mod)


def _inproj_kernel(ctx_ref, x_ref, mod_ref, n1w_ref, w_ref, wkt_ref, wgt_ref, e_ref, et_ref, qkw_ref,
                   cos_ref, sin_ref, gb_ref, gbt_ref,
                   q_ref, k_ref, v_ref, mq_ref, mkt_ref, mv_ref, so_ref, sga_ref, sgm_ref, g_ref, gt_ref,
                   hn_ref, *, n_ctx_tiles):
    i = pl.program_id(1)
    is_ctx = i < n_ctx_tiles
    xin = jnp.where(is_ctx, ctx_ref[0], x_ref[0])
    ms = jnp.mean(xin * xin, axis=-1, keepdims=True)
    y = xin * lax.rsqrt(ms + EPS) * n1w_ref[...]
    y = y * mod_ref[0, 0, 1:2, :] + mod_ref[0, 0, 0:1, :]
    hn_ref[...] = y.astype(BF16)
    hn = hn_ref[...]

    acc = _dot(hn, w_ref[:, _P_QK:_P_QK + QK_W])
    ss = _dot((acc * acc).astype(BF16), e_ref[...])
    r = lax.rsqrt(ss * (1.0 / ATT_HEAD_DIM) + EPS)
    r_hi = r.astype(BF16)
    r_lo = (r - r_hi.astype(F32)).astype(BF16)
    rb = _dot(jnp.concatenate([r_hi, r_lo], axis=1), et_ref[...])
    qn = acc * rb * qkw_ref[...]
    cos = cos_ref[...]
    sin = sin_ref[...]
    for gi in range(QK_W // LANES):
        xs = qn[:, gi * LANES:(gi + 1) * LANES]
        o = (xs * cos + pltpu.roll(xs, LANES // 2, 1) * sin).astype(BF16)
        if gi < ATT_Q_W // LANES:
            q_ref[0, :, gi * LANES:(gi + 1) * LANES] = o
        else:
            k_ref[0, :, gi * LANES - ATT_Q_W:(gi + 1) * LANES - ATT_Q_W] = o

    v_ref[0] = _dot(hn, w_ref[:, _P_V:_P_V + ATT_KV_W]).astype(BF16)
    mq_ref[0] = _dot(hn, w_ref[:, _P_MQ:_P_MQ + ML_QK_W]).astype(BF16)
    mv_ref[0] = _dot(hn, w_ref[:, _P_MV:_P_MV + ML_V_W]).astype(BF16)
    so_ref[0] = jax.nn.sigmoid(_dot(hn, w_ref[:, _P_MO:_P_MO + ML_V_W])).astype(BF16)
    sga_ref[0] = jax.nn.sigmoid(_dot(hn, w_ref[:, _P_GA:_P_GA + D])).astype(BF16)
    sgm_ref[0] = jax.nn.sigmoid(_dot(hn, w_ref[:, _P_GM:_P_GM + D])).astype(BF16)
    gfull = _dot(hn, w_ref[:, _P_MG:_P_MG + LANES])
    g_ref[0] = gfull[:, :ML_GATE_W] + gb_ref[...]
    mkt_ref[0] = _dot_nt(wkt_ref[...], hn).astype(BF16)
    gt_ref[0] = _dot_nt(wgt_ref[...], hn) + gbt_ref[...]


def _inproj_call(ctx, x, modsel, n1w, w_p, wkt, wgt, e_mat, et_mat, qkw, cos_t, sin_t, gb, gbt, tm):
    B, L, _ = x.shape
    C = ctx.shape[1]
    Lt = L + C
    nct = C // tm
    nt = Lt // tm

    def const(shape):
        return pl.BlockSpec(shape, lambda b, i: (0,) * len(shape))

    def rows(w):
        return pl.BlockSpec((1, tm, w), lambda b, i: (b, i, 0))

    in_specs = [
        pl.BlockSpec((1, tm, D), lambda b, i: (b, jnp.minimum(i, nct - 1), 0)),
        pl.BlockSpec((1, tm, D), lambda b, i: (b, jnp.maximum(i - nct, 0), 0)),
        pl.BlockSpec((1, 1, 2, D), lambda b, i: (b, jnp.where(i < nct, 0, 1), 0, 0)),
        const((1, D)),
        const(w_p.shape), const(wkt.shape), const(wgt.shape), const(e_mat.shape), const(et_mat.shape),
        const(qkw.shape),
        pl.BlockSpec((tm, LANES), lambda b, i: (i, 0)),
        pl.BlockSpec((tm, LANES), lambda b, i: (i, 0)),
        const(gb.shape), const(gbt.shape),
    ]
    out_specs = [
        rows(ATT_Q_W), rows(ATT_KV_W), rows(ATT_KV_W), rows(ML_QK_W),
        pl.BlockSpec((1, ML_QK_W, tm), lambda b, i: (b, 0, i)),
        rows(ML_V_W), rows(ML_V_W), rows(D), rows(D), rows(ML_GATE_W),
        pl.BlockSpec((1, ML_GATE_W, tm), lambda b, i: (b, 0, i)),
    ]
    out_shape = [
        jax.ShapeDtypeStruct((B, Lt, ATT_Q_W), BF16),
        jax.ShapeDtypeStruct((B, Lt, ATT_KV_W), BF16),
        jax.ShapeDtypeStruct((B, Lt, ATT_KV_W), BF16),
        jax.ShapeDtypeStruct((B, Lt, ML_QK_W), BF16),
        jax.ShapeDtypeStruct((B, ML_QK_W, Lt), BF16),
        jax.ShapeDtypeStruct((B, Lt, ML_V_W), BF16),
        jax.ShapeDtypeStruct((B, Lt, ML_V_W), BF16),
        jax.ShapeDtypeStruct((B, Lt, D), BF16),
        jax.ShapeDtypeStruct((B, Lt, D), BF16),
        jax.ShapeDtypeStruct((B, Lt, ML_GATE_W), F32),
        jax.ShapeDtypeStruct((B, ML_GATE_W, Lt), F32),
    ]
    return pl.pallas_call(
        functools.partial(_inproj_kernel, n_ctx_tiles=nct),
        grid=(B, nt),
        in_specs=in_specs,
        out_specs=out_specs,
        out_shape=out_shape,
        scratch_shapes=[pltpu.VMEM((tm, D), BF16)],
        compiler_params=_cparams(("arbitrary", "arbitrary")),
        name="inproj",
    )(ctx, x, modsel, n1w, w_p, wkt, wgt, e_mat, et_mat, qkw, cos_t, sin_t, gb, gbt)


def _attn_kernel(sink_ref, q_ref, kc_ref, kp_ref, k0_ref, kn_ref, vc_ref, vp_ref, v0_ref, vn_ref, o_ref,
                 *, n_blocks):
    i = pl.program_id(1)
    T = ATT_BLOCK
    C = kc_ref.shape[1]
    S = 3 * T + C
    lane = lax.broadcasted_iota(jnp.int32, (1, LANES), 1)
    keep = ((lane // 32) % 2) == 0
    low = lane < (LANES // 2)

    t_idx = lax.broadcasted_iota(jnp.int32, (ATT_GROUP * T, T), 0) % T
    s_idx = lax.broadcasted_iota(jnp.int32, (ATT_GROUP * T, T), 1)
    ok_prev = (s_idx >= t_idx) & (i > 0)
    ok_next = (s_idx <= t_idx) & (i < n_blocks - 1)

    k_all = jnp.concatenate([kp_ref[0], k0_ref[0], kn_ref[0], kc_ref[0]], axis=0).astype(F32)
    v_all = jnp.concatenate([vp_ref[0], v0_ref[0], vn_ref[0], vc_ref[0]], axis=0).astype(F32)
    qf = q_ref[0].astype(F32)

    for kh in range(ATT_KV_HEADS):
        j = kh // 2
        kpair = k_all[:, j * LANES:(j + 1) * LANES]
        vpair = v_all[:, j * LANES:(j + 1) * LANES]
        if kh % 2 == 1:
            kpair = pltpu.roll(kpair, 96, 1)
        kk = jnp.where(keep, kpair, 0.0).astype(BF16)
        vrot = pltpu.roll(vpair, LANES // 2, 1)
        if kh % 2 == 0:
            vv = jnp.where(low, vpair, vrot).astype(BF16)
        else:
            vv = jnp.where(low, vrot, vpair).astype(BF16)
        parts = []
        for p in (2 * kh, 2 * kh + 1):
            qp = qf[:, p * LANES:(p + 1) * LANES]
            parts.append(qp)
            parts.append(pltpu.roll(qp, 96, 1))
        q4 = jnp.concatenate(parts, axis=0).astype(BF16)
        s = _dot_nt(q4, kk)
        s_prev = jnp.where(ok_prev, s[:, 0:T], NEG_INF)
        s_cur = s[:, T:2 * T]
        s_next = jnp.where(ok_next, s[:, 2 * T:3 * T], NEG_INF)
        s_ctx = s[:, 3 * T:S]
        sink = jnp.concatenate(
            [jnp.full((T, 1), sink_ref[kh * ATT_GROUP + g], F32) for g in range(ATT_GROUP)], axis=0)
        m = jnp.maximum(jnp.maximum(jnp.max(s_prev, axis=1, keepdims=True), jnp.max(s_cur, axis=1, keepdims=True)),
                        jnp.maximum(jnp.max(s_next, axis=1, keepdims=True), jnp.max(s_ctx, axis=1, keepdims=True)))
        m = jnp.maximum(m, sink)
        p_prev = jnp.exp(s_prev - m)
        p_cur = jnp.exp(s_cur - m)
        p_next = jnp.exp(s_next - m)
        p_ctx = jnp.exp(s_ctx - m)
        denom = (jnp.sum(p_prev, axis=1, keepdims=True) + jnp.sum(p_cur, axis=1, keepdims=True)
                 + jnp.sum(p_next, axis=1, keepdims=True) + jnp.sum(p_ctx, axis=1, keepdims=True)
                 + jnp.exp(sink - m))
        pm = jnp.concatenate([p_prev, p_cur, p_next, p_ctx], axis=1).astype(BF16)
        o4 = _dot(pm, vv) / denom
        for pi in range(2):
            oa = o4[(2 * pi) * T:(2 * pi + 1) * T]
            ob = o4[(2 * pi + 1) * T:(2 * pi + 2) * T]
            p = 2 * kh + pi
            o_ref[0, :, p * LANES:(p + 1) * LANES] = jnp.where(low, oa, ob).astype(BF16)


def _attn_call(sink, q, k, v, L, C):
    B = q.shape[0]
    T = ATT_BLOCK
    nb = L // T
    ncb = C // T

    def kv_spec(off):
        return pl.BlockSpec((1, T, ATT_KV_W), lambda b, i: (b, ncb + jnp.clip(i + off, 0, nb - 1), 0))

    ctx_spec = pl.BlockSpec((1, C, ATT_KV_W), lambda b, i: (b, 0, 0))
    return pl.pallas_call(
        functools.partial(_attn_kernel, n_blocks=nb),
        grid=(B, nb),
        in_specs=[pl.BlockSpec(memory_space=pltpu.SMEM),
                  pl.BlockSpec((1, T, ATT_Q_W), lambda b, i: (b, ncb + i, 0)),
                  ctx_spec, kv_spec(-1), kv_spec(0), kv_spec(1),
                  ctx_spec, kv_spec(-1), kv_spec(0), kv_spec(1)],
        out_specs=pl.BlockSpec((1, T, ATT_Q_W), lambda b, i: (b, i, 0)),
        out_shape=jax.ShapeDtypeStruct((B, L, ATT_Q_W), BF16),
        compiler_params=_cparams(("arbitrary", "arbitrary")),
        name="attn",
    )(sink, q, k, k, k, k, v, v, v, v)


def _split3(x):
    x1 = x.astype(BF16)
    r = x - x1.astype(F32)
    x2 = r.astype(BF16)
    x3 = (r - x2.astype(F32)).astype(BF16)
    return x1, x2, x3


def _mlstm_kernel(qf_ref, ktf_ref, vf_ref, gf_ref, gtf_ref, qb_ref, ktb_ref, vb_ref, gb_ref, gtb_ref,
                  hf_ref, hb_ref, c_ref, n_ref, m_ref, *, n_ctx_chunks):
    j = pl.program_id(1)
    T = ML_CHUNK

    @pl.when(j == 0)
    def _():
        c_ref[...] = jnp.zeros_like(c_ref)
        n_ref[...] = jnp.zeros_like(n_ref)
        m_ref[...] = jnp.zeros_like(m_ref)

    row = lax.broadcasted_iota(jnp.int32, (T, T), 0)
    col = lax.broadcasted_iota(jnp.int32, (T, T), 1)
    lower = (col <= row)
    upper = (col >= row)
    lower_b = jnp.where(lower, 1.0, 0.0).astype(BF16)
    upper_b = jnp.where(upper, 1.0, 0.0).astype(BF16)
    ones8 = jnp.ones((8, T), BF16)

    dirs = (
        (qf_ref, ktf_ref, vf_ref, gf_ref, gtf_ref, hf_ref, lower, lower_b, upper_b, T - 1),
        (qb_ref, ktb_ref, vb_ref, gb_ref, gtb_ref, hb_ref, upper, upper_b, lower_b, 0),
    )
    for d, (q_ref, kt_ref, v_ref, g_ref, gt_ref, h_ref, mask, tri_c, tri_r, last) in enumerate(dirs):
        g = g_ref[0]
        gt = gt_ref[0]
        lf_c = jax.nn.log_sigmoid(g)
        lf_r = jax.nn.log_sigmoid(gt)
        bc = sum(_dot(tri_c, p) for p in _split3(lf_c))
        br = sum(_dot(p, tri_r) for p in _split3(lf_r))
        for h in range(ML_HEADS):
            ci = d * ML_HEADS + h
            gi = d * 2 * ML_HEADS + h
            fi = gi + ML_HEADS
            q = q_ref[0, :, h * ML_QK_DIM:(h + 1) * ML_QK_DIM]
            kt = kt_ref[0, h * ML_QK_DIM:(h + 1) * ML_QK_DIM, :]
            v = v_ref[0, :, h * ML_V_DIM:(h + 1) * ML_V_DIM]
            i_row = gt[gi:gi + 1, :]
            b_row = br[fi:fi + 1, :]
            b_col = bc[:, fi:fi + 1]
            m_old = m_ref[ci, 0:1, 0:1]
            n_old = n_ref[ci, 0:1, :]
            c_old = c_ref[ci]

            u_row = i_row - b_row
            umat = jnp.where(mask, u_row, -jnp.inf)
            m_col = jnp.maximum(jnp.max(umat, axis=1, keepdims=True), m_old)
            dexp = jnp.exp(umat - m_col)
            sd = _dot(q, kt) * dexp
            w_int = jnp.exp(m_old - m_col)
            e_col = jnp.exp(-(b_col + m_col))
            qn = jnp.sum(q.astype(F32) * n_old, axis=1, keepdims=True)
            nq = jnp.sum(sd, axis=1, keepdims=True) + w_int * qn
            den = jnp.maximum(jnp.abs(nq), e_col)
            num = _dot(sd.astype(BF16), v) + _dot(q, c_old.astype(BF16)) * w_int
            hval = num * (1.0 / den)

            @pl.when(j >= n_ctx_chunks)
            def _():
                h_ref[0, :, h * ML_V_DIM:(h + 1) * ML_V_DIM] = hval.astype(BF16)

            m_last = jnp.maximum(jnp.max(u_row, axis=1, keepdims=True), m_old)
            wk = jnp.exp(u_row - m_last)
            decay = jnp.exp(m_old - m_last)
            kw = (kt.astype(F32) * wk).astype(BF16)
            c_ref[ci] = decay * c_old + _dot(kw, v)
            n_ref[ci] = decay * n_ref[ci] + _dot_nt(ones8, kw)
            b_last = b_row[:, last:last + 1]
            m_ref[ci] = jnp.broadcast_to(b_last + m_last, (8, LANES))


def _mlstm_call(mq, mkt, mv, g, gt, L, C):
    B = mq.shape[0]
    T = ML_CHUNK
    ncc = C // T
    ncl = L // T
    nc = ncc + ncl

    def fwd(j):
        return j

    def bwd(j):
        return jnp.where(j < ncc, ncc - 1 - j, nc - 1 - (j - ncc))

    def specs(order):
        return [
            pl.BlockSpec((1, T, ML_QK_W), lambda b, j: (b, order(j), 0)),
            pl.BlockSpec((1, ML_QK_W, T), lambda b, j: (b, 0, order(j))),
            pl.BlockSpec((1, T, ML_V_W), lambda b, j: (b, order(j), 0)),
            pl.BlockSpec((1, T, ML_GATE_W), lambda b, j: (b, order(j), 0)),
            pl.BlockSpec((1, ML_GATE_W, T), lambda b, j: (b, 0, order(j))),
        ]

    out_specs = [
        pl.BlockSpec((1, T, ML_V_W), lambda b, j: (b, jnp.maximum(j - ncc, 0), 0)),
        pl.BlockSpec((1, T, ML_V_W), lambda b, j: (b, ncl - 1 - jnp.maximum(j - ncc, 0), 0)),
    ]
    n_chain = 2 * ML_HEADS
    return pl.pallas_call(
        functools.partial(_mlstm_kernel, n_ctx_chunks=ncc),
        grid=(B, nc),
        in_specs=specs(fwd) + specs(bwd),
        out_specs=out_specs,
        out_shape=[jax.ShapeDtypeStruct((B, L, ML_V_W), BF16)] * 2,
        scratch_shapes=[pltpu.VMEM((n_chain, ML_QK_DIM, ML_V_DIM), F32),
                        pltpu.VMEM((n_chain, 8, ML_QK_DIM), F32),
                        pltpu.VMEM((n_chain, 8, LANES), F32)],
        compiler_params=_cparams(("arbitrary", "arbitrary")),
        name="mlstm",
    )(mq, mkt, mv, g, gt, mq, mkt, mv, g, gt)


def _merge_kernel(att_ref, hf_ref, hb_ref, so_ref, sga_ref, sgm_ref, x_ref, mod_ref, mlw_ref, n2w_ref,
                  wa_ref, wm_ref, wo_ref, xmid_ref, h2_ref):
    ht = hf_ref[0].astype(F32) + hb_ref[0].astype(F32)
    mlw = mlw_ref[...]
    parts = []
    for h in range(ML_HEADS):
        seg = ht[:, h * ML_V_DIM:(h + 1) * ML_V_DIM]
        ms = jnp.mean(seg * seg, axis=-1, keepdims=True)
        parts.append(seg * lax.rsqrt(ms + EPS) * mlw[:, h * ML_V_DIM:(h + 1) * ML_V_DIM])
    ml = (jnp.concatenate(parts, axis=1) * so_ref[0].astype(F32)).astype(BF16)
    ya = _dot(att_ref[0], wa_ref[...])
    ym = _dot(ml, wm_ref[...])
    y = (sga_ref[0].astype(F32) * ya + sgm_ref[0].astype(F32) * ym).astype(BF16)
    y2 = _dot(y, wo_ref[...])
    g1 = mod_ref[0, 0:1, :]
    sh2 = mod_ref[0, 1:2, :]
    sc2 = mod_ref[0, 2:3, :]
    xm = x_ref[0] + g1 * y2
    xmid_ref[0] = xm
    ms = jnp.mean(xm * xm, axis=-1, keepdims=True)
    h2 = xm * lax.rsqrt(ms + EPS) * n2w_ref[...]
    h2_ref[0] = (h2 * sc2 + sh2).astype(BF16)


def _merge_call(att, hf, hb, so, sga, sgm, x, mod3, mlw, n2w, wa, wm, wo, C, tm):
    B, L, _ = x.shape
    nt = L // tm
    off = C // tm

    def lat(w):
        return pl.BlockSpec((1, tm, w), lambda b, i: (b, i, 0))

    def cat(w):
        return pl.BlockSpec((1, tm, w), lambda b, i: (b, i + off, 0))

    def const(shape):
        return pl.BlockSpec(shape, lambda b, i: (0,) * len(shape))

    return pl.pallas_call(
        _merge_kernel,
        grid=(B, nt),
        in_specs=[lat(D), lat(D), lat(D), cat(D), cat(D), cat(D), lat(D),
                  pl.BlockSpec((1, 3, D), lambda b, i: (b, 0, 0)),
                  const((1, D)), const((1, D)), const((D, D)), const((D, D)), const((D, D))],
        out_specs=[lat(D), lat(D)],
        out_shape=[jax.ShapeDtypeStruct((B, L, D), F32), jax.ShapeDtypeStruct((B, L, D), BF16)],
        compiler_params=_cparams(("arbitrary", "arbitrary")),
        name="merge",
    )(att, hf, hb, so, sga, sgm, x, mod3, mlw, n2w, wa, wm, wo)


HALO = 16


def _ffn_kernel(h_ref, hp_ref, hn_ref, xmid_ref, mod_ref, wup_ref, cw_ref, cb_ref, wdn_ref, o_ref,
                u_ref, act_ref, *, n_tiles, tn):
    i = pl.program_id(1)
    tm = h_ref.shape[1]
    h = h_ref[0]
    prev_row = jnp.where(i > 0, hp_ref[0].astype(F32)[HALO - 1:HALO, :], 0.0)
    next_row = jnp.where(i < n_tiles - 1, hn_ref[0].astype(F32)[0:1, :], 0.0)
    top = lax.broadcasted_iota(jnp.int32, (16, D), 0) < 8
    edge = jnp.where(top, prev_row, next_row).astype(BF16)

    def conv_part(c0):
        ue = _dot(edge, wup_ref[:, c0:c0 + tn])
        u_ref[0:8, :] = ue[0:8]
        u_ref[8 + tm:16 + tm, :] = ue[8:16]
        u_ref[8:8 + tm, :] = _dot(h, wup_ref[:, c0:c0 + tn])
        cw = cw_ref[:, c0:c0 + tn]
        return (cb_ref[:, c0:c0 + tn] + u_ref[7:7 + tm, :] * cw[0:1] + u_ref[8:8 + tm, :] * cw[1:2]
                + u_ref[9:9 + tm, :] * cw[2:3])

    for c in range(D_FF // tn):
        a = conv_part(c * tn)
        gte = conv_part(D_FF + c * tn)
        act_ref[:, c * tn:(c + 1) * tn] = (gte * jax.nn.sigmoid(gte) * a).astype(BF16)
    y = _dot(act_ref[...], wdn_ref[...])
    o_ref[0] = xmid_ref[0] + mod_ref[0] * y


def _ffn_call(h2, xmid, g2, wup, cw, cb, wdn, tm, tn):
    B, L, _ = xmid.shape
    nt = L // tm
    hb = tm // HALO
    nhb = L // HALO

    def lat(w):
        return pl.BlockSpec((1, tm, w), lambda b, i: (b, i, 0))

    def const(shape):
        return pl.BlockSpec(shape, lambda b, i: (0,) * len(shape))

    return pl.pallas_call(
        functools.partial(_ffn_kernel, n_tiles=nt, tn=tn),
        grid=(B, nt),
        in_specs=[lat(D),
                  pl.BlockSpec((1, HALO, D), lambda b, i: (b, jnp.maximum(i * hb - 1, 0), 0)),
                  pl.BlockSpec((1, HALO, D), lambda b, i: (b, jnp.minimum((i + 1) * hb, nhb - 1), 0)),
                  lat(D),
                  pl.BlockSpec((1, 1, D), lambda b, i: (b, 0, 0)),
                  const(wup.shape), const(cw.shape), const(cb.shape), const(wdn.shape)],
        out_specs=lat(D),
        out_shape=jax.ShapeDtypeStruct((B, L, D), F32),
        scratch_shapes=[pltpu.VMEM((tm + 16, tn), F32), pltpu.VMEM((tm, D_FF), BF16)],
        compiler_params=_cparams(("arbitrary", "arbitrary")),
        name="ffn",
    )(h2, h2, h2, xmid, g2, wup, cw, cb, wdn)


def _pair_perm(n_heads):
    half = ATT_HEAD_DIM // 2
    idx = []
    for p in range(n_heads // 2):
        for sub in range(4):
            head = 2 * p + (sub % 2)
            d0 = (sub // 2) * half
            idx.extend(head * ATT_HEAD_DIM + d0 + e for e in range(half))
    return np.asarray(idx, np.int32)


def _rope_tables(L, C):
    rows = L // GRID_W
    row = jnp.repeat(jnp.arange(rows, dtype=F32), GRID_W)
    col = jnp.tile(jnp.arange(GRID_W, dtype=F32), rows)
    n_freq = ATT_HEAD_DIM // 4
    inv_freq = ROPE_BASE ** (-jnp.arange(n_freq, dtype=F32) / n_freq)
    ang = jnp.concatenate([row[:, None] * inv_freq, col[:, None] * inv_freq], axis=-1)
    cos = jnp.tile(jnp.cos(ang), (1, 4))
    sin = jnp.tile(jnp.sin(ang), (1, 4))
    sign = jnp.where(jnp.arange(LANES) < LANES // 2, -1.0, 1.0).astype(F32)
    cos = jnp.concatenate([jnp.ones((C, LANES), F32), cos], axis=0)
    sin = jnp.concatenate([jnp.zeros((C, LANES), F32), sin * sign], axis=0)
    return cos, sin


def kernel(x, c, ctx, c_ctx, w_mod, b_mod, norm1_w, w_in, q_norm_w, k_norm_w, attn_sink, ml_gate_b, ml_norm_w,
           w_branch_att, w_branch_ml, w_out, norm2_w, w_up, conv_w, conv_b, w_down):
    B, L, _ = x.shape
    C = ctx.shape[1]
    assert L % 512 == 0 and C % 256 == 0 and L % GRID_W == 0
    l = 0

    n_rows = -(-(B + 1) // 8) * 8
    cc = jnp.concatenate([c, c_ctx[None, :], jnp.zeros((n_rows - B - 1, D), F32)], axis=0)
    mod = _mod_call(cc, w_mod[l], b_mod[l][None, :])
    sh1, sc1, g1, sh2, sc2, g2 = [mod[:, k * D:(k + 1) * D] for k in range(6)]
    lat_mod = jnp.stack([sh1[:B], 1.0 + sc1[:B]], axis=1)
    ctx_mod = jnp.broadcast_to(jnp.stack([sh1[B], 1.0 + sc1[B]], axis=0)[None], (B, 2, D))
    modsel = jnp.stack([ctx_mod, lat_mod], axis=1)
    mod3 = jnp.stack([g1[:B], sh2[:B], 1.0 + sc2[:B]], axis=1)
    g2b = g2[:B][:, None, :]

    w = w_in[l]
    qperm = _pair_perm(ATT_HEADS)
    kperm = _pair_perm(ATT_KV_HEADS)
    w_q = w[:, _O_AQ:_O_AQ + ATT_Q_W][:, qperm]
    w_k = w[:, _O_AK:_O_AK + ATT_KV_W][:, kperm]
    w_g = jnp.pad(w[:, _O_MG:_O_MG + ML_GATE_W], ((0, 0), (0, LANES - ML_GATE_W)))
    w_mk = w[:, _O_MK:_O_MK + ML_QK_W] * (ML_QK_DIM ** -0.5)
    w_p = jnp.concatenate([
        w_q, w_k, w[:, _O_AV:_O_AV + ATT_KV_W], w[:, _O_MQ:_O_MQ + ML_QK_W], w[:, _O_MV:_O_MV + ML_V_W],
        w[:, _O_MO:_O_MO + ML_V_W], w[:, _O_GA:_O_GA + D], w[:, _O_GM:_O_GM + D], w_g], axis=1).astype(BF16)
    wkt = w_mk.T.astype(BF16)
    wgt = w[:, _O_MG:_O_MG + ML_GATE_W].T.astype(BF16)

    head_of_col = np.concatenate([qperm // ATT_HEAD_DIM, ATT_HEADS + kperm // ATT_HEAD_DIM])
    e_np = (head_of_col[:, None] == np.arange(LANES)[None, :]).astype(np.float32)
    e_mat = jnp.asarray(e_np, BF16)
    et_mat = jnp.asarray(np.concatenate([e_np.T, e_np.T], axis=0), BF16)
    qkw = jnp.concatenate([q_norm_w[l][qperm % ATT_HEAD_DIM] * ATT_SCALE,
                           k_norm_w[l][kperm % ATT_HEAD_DIM]])[None, :]
    cos_t, sin_t = _rope_tables(L, C)
    gb = ml_gate_b[l].reshape(1, ML_GATE_W)
    gbt = ml_gate_b[l].reshape(ML_GATE_W, 1)

    q, k, v, mq, mkt, mv, so, sga, sgm, g, gt = _inproj_call(
        ctx, x, modsel, norm1_w[l][None, :], w_p, wkt, wgt, e_mat, et_mat, qkw, cos_t, sin_t, gb, gbt, tm=256)

    att = _attn_call(attn_sink[l], q, k, v, L, C)
    hf, hb = _mlstm_call(mq, mkt, mv, g, gt, L, C)

    xmid, h2 = _merge_call(att, hf, hb, so, sga, sgm, x, mod3, ml_norm_w[l][None, :], norm2_w[l][None, :],
                           w_branch_att[l].astype(BF16), w_branch_ml[l].astype(BF16), w_out[l].astype(BF16),
                           C, tm=256)
    out = _ffn_call(h2, xmid, g2b, w_up[l].astype(BF16), conv_w[l], conv_b[l][None, :],
                    w_down[l].astype(BF16), tm=512, tn=1408)
    return out
```

```python
import functools

import jax
import jax.numpy as jnp
import numpy as np
from jax import lax
from jax.experimental import pallas as pl
from jax.experimental.pallas import tpu as pltpu

D = 1024
GRID_W = 64
ATT_HEADS = 16
ATT_KV_HEADS = 4
ATT_HEAD_DIM = 64
ATT_GROUP = ATT_HEADS // ATT_KV_HEADS
ATT_BLOCK = 128
WINDOW = 128
ROPE_BASE = 10000.0
ATT_SCALE = ATT_HEAD_DIM ** -0.5
ML_HEADS = 4
ML_QK_DIM = 128
ML_V_DIM = 256
ML_CHUNK = 128
D_FF = 2816
EPS = 1e-6
NEG_INF = -1e30

ATT_Q_W = ATT_HEADS * ATT_HEAD_DIM
ATT_KV_W = ATT_KV_HEADS * ATT_HEAD_DIM
ML_QK_W = ML_HEADS * ML_QK_DIM
ML_V_W = ML_HEADS * ML_V_DIM
ML_GATE_W = 2 * 2 * ML_HEADS

LANES = 128
KVX_W = ATT_KV_HEADS * LANES
VMEM_LIMIT = 56 * 1024 * 1024

BF16 = jnp.bfloat16
F32 = jnp.float32

_O_AQ = 0
_O_AK = _O_AQ + ATT_Q_W
_O_AV = _O_AK + ATT_KV_W
_O_MQ = _O_AV + ATT_KV_W
_O_MK = _O_MQ + ML_QK_W
_O_MV = _O_MK + ML_QK_W
_O_MO = _O_MV + ML_V_W
_O_MG = _O_MO + ML_V_W
_O_GA = _O_MG + ML_GATE_W
_O_GM = _O_GA + D

QK_W = ATT_Q_W + ATT_KV_W
_P_QK = 0
_P_V = _P_QK + QK_W
_P_MQ = _P_V + ATT_KV_W
_P_MV = _P_MQ + ML_QK_W
_P_MO = _P_MV + ML_V_W
_P_GA = _P_MO + ML_V_W
_P_GM = _P_GA + D
_P_MG = _P_GM + D
_P_END = _P_MG + LANES


def _dot(a, b):
    return jnp.dot(a, b, preferred_element_type=F32)


def _dot_nt(a, b):
    return lax.dot_general(a, b, (((1,), (1,)), ((), ())), preferred_element_type=F32)


def _cparams(sem):
    return pltpu.CompilerParams(dimension_semantics=sem, vmem_limit_bytes=VMEM_LIMIT)


def _mod_kernel(c_ref, w_ref, b_ref, o_ref):
    c = c_ref[...]
    a = c * jax.nn.sigmoid(c)
    o_ref[...] = jnp.dot(a, w_ref[...], preferred_element_type=F32,
                         precision=lax.Precision.HIGHEST) + b_ref[...]


def _mod_call(cc, w_mod, b_mod):
    rows = cc.shape[0]
    n = w_mod.shape[1]
    tn = 1536
    return pl.pallas_call(
        _mod_kernel,
        grid=(n // tn,),
        in_specs=[pl.BlockSpec((rows, D), lambda j: (0, 0)),
                  pl.BlockSpec((D, tn), lambda j: (0, j)),
                  pl.BlockSpec((1, tn), lambda j: (0, j))],
        out_specs=pl.BlockSpec((rows, tn), lambda j: (0, j)),
        out_shape=jax.ShapeDtypeStruct((rows, n), F32),
        compiler_params=_cparams(("arbitrary",)),
        name="mod",
    )(cc, w_mod, b_mod)


def _inproj_kernel(ctx_ref, x_ref, mod_ref, n1w_ref, w_ref, wkt_ref, wgt_ref, e_ref, et_ref, qkw_ref,
                   cos_ref, sin_ref, gb_ref, gbt_ref,
                   q_ref, k_ref, v_ref, mq_ref, mkt_ref, mv_ref, so_ref, sga_ref, sgm_ref, g_ref, gt_ref,
                   hn_ref, *, n_ctx_tiles):
    i = pl.program_id(1)
    is_ctx = i < n_ctx_tiles
    xin = jnp.where(is_ctx, ctx_ref[0], x_ref[0])
    ms = jnp.mean(xin * xin, axis=-1, keepdims=True)
    y = xin * lax.rsqrt(ms + EPS) * n1w_ref[...]
    y = y * mod_ref[0, 0, 1:2, :] + mod_ref[0, 0, 0:1, :]
    hn_ref[...] = y.astype(BF16)
    hn = hn_ref[...]

    acc = _dot(hn, w_ref[:, _P_QK:_P_QK + QK_W])
    ss = _dot((acc * acc).astype(BF16), e_ref[...])
    r = lax.rsqrt(ss * (1.0 / ATT_HEAD_DIM) + EPS)
    r_hi = r.astype(BF16)
    r_lo = (r - r_hi.astype(F32)).astype(BF16)
    rb = _dot(jnp.concatenate([r_hi, r_lo], axis=1), et_ref[...])
    qn = acc * rb * qkw_ref[...]
    cos = cos_ref[...]
    sin = sin_ref[...]
    lane = lax.broadcasted_iota(jnp.int32, (1, LANES), 1)
    keep = ((lane // 32) % 2) == 0
    low = lane < (LANES // 2)
    for gi in range(QK_W // LANES):
        xs = qn[:, gi * LANES:(gi + 1) * LANES]
        o = xs * cos + pltpu.roll(xs, LANES // 2, 1) * sin
        if gi < ATT_Q_W // LANES:
            q_ref[0, :, gi * LANES:(gi + 1) * LANES] = o.astype(BF16)
        else:
            kh = 2 * (gi - ATT_Q_W // LANES)
            k_ref[0, :, kh * LANES:(kh + 1) * LANES] = jnp.where(keep, o, 0.0).astype(BF16)
            k_ref[0, :, (kh + 1) * LANES:(kh + 2) * LANES] = jnp.where(keep, pltpu.roll(o, 96, 1), 0.0).astype(BF16)

    vacc = _dot(hn, w_ref[:, _P_V:_P_V + ATT_KV_W])
    for jp in range(ATT_KV_W // LANES):
        vpair = vacc[:, jp * LANES:(jp + 1) * LANES]
        vrot = pltpu.roll(vpair, LANES // 2, 1)
        v_ref[0, :, (2 * jp) * LANES:(2 * jp + 1) * LANES] = jnp.where(low, vpair, vrot).astype(BF16)
        v_ref[0, :, (2 * jp + 1) * LANES:(2 * jp + 2) * LANES] = jnp.where(low, vrot, vpair).astype(BF16)
    mq_ref[0] = _dot(hn, w_ref[:, _P_MQ:_P_MQ + ML_QK_W]).astype(BF16)
    mv_ref[0] = _dot(hn, w_ref[:, _P_MV:_P_MV + ML_V_W]).astype(BF16)
    so_ref[0] = jax.nn.sigmoid(_dot(hn, w_ref[:, _P_MO:_P_MO + ML_V_W])).astype(BF16)
    sga_ref[0] = jax.nn.sigmoid(_dot(hn, w_ref[:, _P_GA:_P_GA + D])).astype(BF16)
    sgm_ref[0] = jax.nn.sigmoid(_dot(hn, w_ref[:, _P_GM:_P_GM + D])).astype(BF16)
    gfull = _dot(hn, w_ref[:, _P_MG:_P_MG + LANES])
    g_ref[0] = gfull[:, :ML_GATE_W] + gb_ref[...]
    mkt_ref[0] = _dot_nt(wkt_ref[...], hn).astype(BF16)
    gt_ref[0] = _dot_nt(wgt_ref[...], hn) + gbt_ref[...]


def _inproj_call(ctx, x, modsel, n1w, w_p, wkt, wgt, e_mat, et_mat, qkw, cos_t, sin_t, gb, gbt, tm):
    B, L, _ = x.shape
    C = ctx.shape[1]
    Lt = L + C
    nct = C // tm
    nt = Lt // tm

    def const(shape):
        return pl.BlockSpec(shape, lambda b, i: (0,) * len(shape))

    def rows(w):
        return pl.BlockSpec((1, tm, w), lambda b, i: (b, i, 0))

    in_specs = [
        pl.BlockSpec((1, tm, D), lambda b, i: (b, jnp.minimum(i, nct - 1), 0)),
        pl.BlockSpec((1, tm, D), lambda b, i: (b, jnp.maximum(i - nct, 0), 0)),
        pl.BlockSpec((1, 1, 2, D), lambda b, i: (b, jnp.where(i < nct, 0, 1), 0, 0)),
        const((1, D)),
        const(w_p.shape), const(wkt.shape), const(wgt.shape), const(e_mat.shape), const(et_mat.shape),
        const(qkw.shape),
        pl.BlockSpec((tm, LANES), lambda b, i: (i, 0)),
        pl.BlockSpec((tm, LANES), lambda b, i: (i, 0)),
        const(gb.shape), const(gbt.shape),
    ]
    out_specs = [
        rows(ATT_Q_W), rows(KVX_W), rows(KVX_W), rows(ML_QK_W),
        pl.BlockSpec((1, ML_QK_W, tm), lambda b, i: (b, 0, i)),
        rows(ML_V_W), rows(ML_V_W), rows(D), rows(D), rows(ML_GATE_W),
        pl.BlockSpec((1, ML_GATE_W, tm), lambda b, i: (b, 0, i)),
    ]
    out_shape = [
        jax.ShapeDtypeStruct((B, Lt, ATT_Q_W), BF16),
        jax.ShapeDtypeStruct((B, Lt, KVX_W), BF16),
        jax.ShapeDtypeStruct((B, Lt, KVX_W), BF16),
        jax.ShapeDtypeStruct((B, Lt, ML_QK_W), BF16),
        jax.ShapeDtypeStruct((B, ML_QK_W, Lt), BF16),
        jax.ShapeDtypeStruct((B, Lt, ML_V_W), BF16),
        jax.ShapeDtypeStruct((B, Lt, ML_V_W), BF16),
        jax.ShapeDtypeStruct((B, Lt, D), BF16),
        jax.ShapeDtypeStruct((B, Lt, D), BF16),
        jax.ShapeDtypeStruct((B, Lt, ML_GATE_W), F32),
        jax.ShapeDtypeStruct((B, ML_GATE_W, Lt), F32),
    ]
    return pl.pallas_call(
        functools.partial(_inproj_kernel, n_ctx_tiles=nct),
        grid=(B, nt),
        in_specs=in_specs,
        out_specs=out_specs,
        out_shape=out_shape,
        scratch_shapes=[pltpu.VMEM((tm, D), BF16)],
        compiler_params=_cparams(("arbitrary", "arbitrary")),
        name="inproj",
    )(ctx, x, modsel, n1w, w_p, wkt, wgt, e_mat, et_mat, qkw, cos_t, sin_t, gb, gbt)


def _attn_kernel(sink_ref, q_ref, kc_ref, kp_ref, k0_ref, kn_ref, vc_ref, vp_ref, v0_ref, vn_ref, o_ref,
                 *, n_blocks):
    i = pl.program_id(1)
    T = ATT_BLOCK
    C = kc_ref.shape[1]
    S = 3 * T + C
    lane = lax.broadcasted_iota(jnp.int32, (1, LANES), 1)
    low = lane < (LANES // 2)

    t_idx = lax.broadcasted_iota(jnp.int32, (ATT_GROUP * T, T), 0) % T
    s_idx = lax.broadcasted_iota(jnp.int32, (ATT_GROUP * T, T), 1)
    ok_prev = (s_idx >= t_idx) & (i > 0)
    ok_next = (s_idx <= t_idx) & (i < n_blocks - 1)

    k_all = jnp.concatenate([kp_ref[0], k0_ref[0], kn_ref[0], kc_ref[0]], axis=0)
    v_all = jnp.concatenate([vp_ref[0], v0_ref[0], vn_ref[0], vc_ref[0]], axis=0)
    qf = q_ref[0].astype(F32)

    for kh in range(ATT_KV_HEADS):
        kk = k_all[:, kh * LANES:(kh + 1) * LANES]
        vv = v_all[:, kh * LANES:(kh + 1) * LANES]
        parts = []
        for p in (2 * kh, 2 * kh + 1):
            qp = qf[:, p * LANES:(p + 1) * LANES]
            parts.append(qp)
            parts.append(pltpu.roll(qp, 96, 1))
        q4 = jnp.concatenate(parts, axis=0).astype(BF16)
        s = _dot_nt(q4, kk)
        s = jnp.concatenate([jnp.where(ok_prev, s[:, 0:T], NEG_INF), s[:, T:2 * T],
                             jnp.where(ok_next, s[:, 2 * T:3 * T], NEG_INF), s[:, 3 * T:S]], axis=1)
        sink = jnp.concatenate(
            [jnp.full((T, 1), sink_ref[kh * ATT_GROUP + g], F32) for g in range(ATT_GROUP)], axis=0)
        m = jnp.maximum(jnp.max(s, axis=1, keepdims=True), sink)
        pm = jnp.exp(s - m)
        denom = jnp.sum(pm, axis=1, keepdims=True) + jnp.exp(sink - m)
        o4 = _dot(pm.astype(BF16), vv) * (1.0 / denom)
        for pi in range(2):
            oa = o4[(2 * pi) * T:(2 * pi + 1) * T]
            ob = o4[(2 * pi + 1) * T:(2 * pi + 2) * T]
            p = 2 * kh + pi
            o_ref[0, :, p * LANES:(p + 1) * LANES] = jnp.where(low, oa, ob).astype(BF16)


def _attn_call(sink, q, k, v, L, C):
    B = q.shape[0]
    T = ATT_BLOCK
    nb = L // T
    ncb = C // T

    def kv_spec(off):
        return pl.BlockSpec((1, T, KVX_W), lambda b, i: (b, ncb + jnp.clip(i + off, 0, nb - 1), 0))

    ctx_spec = pl.BlockSpec((1, C, KVX_W), lambda b, i: (b, 0, 0))
    return pl.pallas_call(
        functools.partial(_attn_kernel, n_blocks=nb),
        grid=(B, nb),
        in_specs=[pl.BlockSpec(memory_space=pltpu.SMEM),
                  pl.BlockSpec((1, T, ATT_Q_W), lambda b, i: (b, ncb + i, 0)),
                  ctx_spec, kv_spec(-1), kv_spec(0), kv_spec(1),
                  ctx_spec, kv_spec(-1), kv_spec(0), kv_spec(1)],
        out_specs=pl.BlockSpec((1, T, ATT_Q_W), lambda b, i: (b, i, 0)),
        out_shape=jax.ShapeDtypeStruct((B, L, ATT_Q_W), BF16),
        compiler_params=_cparams(("arbitrary", "arbitrary")),
        name="attn",
    )(sink, q, k, k, k, k, v, v, v, v)


def _split3(x):
    x1 = x.astype(BF16)
    r = x - x1.astype(F32)
    x2 = r.astype(BF16)
    x3 = (r - x2.astype(F32)).astype(BF16)
    return x1, x2, x3


def _mlstm_kernel(qf_ref, ktf_ref, vf_ref, gf_ref, gtf_ref, qb_ref, ktb_ref, vb_ref, gb_ref, gtb_ref,
                  hf_ref, hb_ref, c_ref, n_ref, m_ref, *, n_ctx_chunks):
    j = pl.program_id(1)
    T = ML_CHUNK

    @pl.when(j == 0)
    def _():
        c_ref[...] = jnp.zeros_like(c_ref)
        n_ref[...] = jnp.zeros_like(n_ref)
        m_ref[...] = jnp.zeros_like(m_ref)

    row = lax.broadcasted_iota(jnp.int32, (T, T), 0)
    col = lax.broadcasted_iota(jnp.int32, (T, T), 1)
    lower = (col <= row)
    upper = (col >= row)
    lower_b = jnp.where(lower, 1.0, 0.0).astype(BF16)
    upper_b = jnp.where(upper, 1.0, 0.0).astype(BF16)
    ones8 = jnp.ones((8, T), BF16)

    dirs = (
        (qf_ref, ktf_ref, vf_ref, gf_ref, gtf_ref, hf_ref, lower, lower_b, upper_b, T - 1),
        (qb_ref, ktb_ref, vb_ref, gb_ref, gtb_ref, hb_ref, upper, upper_b, lower_b, 0),
    )
    for d, (q_ref, kt_ref, v_ref, g_ref, gt_ref, h_ref, mask, tri_c, tri_r, last) in enumerate(dirs):
        g = g_ref[0]
        gt = gt_ref[0]
        lf_c = jax.nn.log_sigmoid(g)
        lf_r = jax.nn.log_sigmoid(gt)
        bc = sum(_dot(tri_c, p) for p in _split3(lf_c))
        br = sum(_dot(p, tri_r) for p in _split3(lf_r))
        for h in range(ML_HEADS):
            ci = d * ML_HEADS + h
            gi = d * 2 * ML_HEADS + h
            fi = gi + ML_HEADS
            q = q_ref[0, :, h * ML_QK_DIM:(h + 1) * ML_QK_DIM]
            kt = kt_ref[0, h * ML_QK_DIM:(h + 1) * ML_QK_DIM, :]
            v = v_ref[0, :, h * ML_V_DIM:(h + 1) * ML_V_DIM]
            i_row = gt[gi:gi + 1, :]
            b_row = br[fi:fi + 1, :]
            b_col = bc[:, fi:fi + 1]
            m_old = m_ref[ci, 0:1, 0:1]
            n_old = n_ref[ci, 0:1, :]
            c_old = c_ref[ci]

            u_row = i_row - b_row
            umat = jnp.where(mask, u_row, -jnp.inf)
            m_col = jnp.maximum(jnp.max(umat, axis=1, keepdims=True), m_old)
            dexp = jnp.exp(umat - m_col)
            sd = _dot(q, kt) * dexp
            w_int = jnp.exp(m_old - m_col)
            e_col = jnp.exp(-(b_col + m_col))
            qn = jnp.sum(q.astype(F32) * n_old, axis=1, keepdims=True)
            nq = jnp.sum(sd, axis=1, keepdims=True) + w_int * qn
            den = jnp.maximum(jnp.abs(nq), e_col)
            num = _dot(sd.astype(BF16), v) + _dot(q, c_old.astype(BF16)) * w_int
            h_ref[0, :, h * ML_V_DIM:(h + 1) * ML_V_DIM] = (num * (1.0 / den)).astype(BF16)

            m_last = jnp.maximum(jnp.max(u_row, axis=1, keepdims=True), m_old)
            wk = jnp.exp(u_row - m_last)
            decay = jnp.exp(m_old - m_last)
            kw = (kt.astype(F32) * wk).astype(BF16)
            c_ref[ci] = decay * c_old + _dot(kw, v)
            n_ref[ci] = decay * n_ref[ci] + _dot_nt(ones8, kw)
            b_last = b_row[:, last:last + 1]
            m_ref[ci] = jnp.broadcast_to(b_last + m_last, (8, LANES))


def _mlstm_call(mq, mkt, mv, g, gt, L, C):
    B = mq.shape[0]
    T = ML_CHUNK
    ncc = C // T
    ncl = L // T
    nc = ncc + ncl

    def fwd(j):
        return j

    def bwd(j):
        return jnp.where(j < ncc, ncc - 1 - j, nc - 1 - (j - ncc))

    def specs(order):
        return [
            pl.BlockSpec((1, T, ML_QK_W), lambda b, j: (b, order(j), 0)),
            pl.BlockSpec((1, ML_QK_W, T), lambda b, j: (b, 0, order(j))),
            pl.BlockSpec((1, T, ML_V_W), lambda b, j: (b, order(j), 0)),
            pl.BlockSpec((1, T, ML_GATE_W), lambda b, j: (b, order(j), 0)),
            pl.BlockSpec((1, ML_GATE_W, T), lambda b, j: (b, 0, order(j))),
        ]

    out_specs = [
        pl.BlockSpec((1, T, ML_V_W), lambda b, j: (b, jnp.maximum(j - ncc, 0), 0)),
        pl.BlockSpec((1, T, ML_V_W), lambda b, j: (b, ncl - 1 - jnp.maximum(j - ncc, 0), 0)),
    ]
    n_chain = 2 * ML_HEADS
    return pl.pallas_call(
        functools.partial(_mlstm_kernel, n_ctx_chunks=ncc),
        grid=(B, nc),
        in_specs=specs(fwd) + specs(bwd),
        out_specs=out_specs,
        out_shape=[jax.ShapeDtypeStruct((B, L, ML_V_W), BF16)] * 2,
        scratch_shapes=[pltpu.VMEM((n_chain, ML_QK_DIM, ML_V_DIM), F32),
                        pltpu.VMEM((n_chain, 8, ML_QK_DIM), F32),
                        pltpu.VMEM((n_chain, 8, LANES), F32)],
        compiler_params=_cparams(("arbitrary", "arbitrary")),
        name="mlstm",
    )(mq, mkt, mv, g, gt, mq, mkt, mv, g, gt)


def _merge_kernel(att_ref, hf_ref, hb_ref, so_ref, sga_ref, sgm_ref, x_ref, mod_ref, mlw_ref, n2w_ref,
                  wa_ref, wm_ref, wo_ref, xmid_ref, h2_ref):
    ht = hf_ref[0].astype(F32) + hb_ref[0].astype(F32)
    mlw = mlw_ref[...]
    parts = []
    for h in range(ML_HEADS):
        seg = ht[:, h * ML_V_DIM:(h + 1) * ML_V_DIM]
        ms = jnp.mean(seg * seg, axis=-1, keepdims=True)
        parts.append(seg * lax.rsqrt(ms + EPS) * mlw[:, h * ML_V_DIM:(h + 1) * ML_V_DIM])
    ml = (jnp.concatenate(parts, axis=1) * so_ref[0].astype(F32)).astype(BF16)
    ya = _dot(att_ref[0], wa_ref[...])
    ym = _dot(ml, wm_ref[...])
    y = (sga_ref[0].astype(F32) * ya + sgm_ref[0].astype(F32) * ym).astype(BF16)
    y2 = _dot(y, wo_ref[...])
    g1 = mod_ref[0, 0:1, :]
    sh2 = mod_ref[0, 1:2, :]
    sc2 = mod_ref[0, 2:3, :]
    xm = x_ref[0] + g1 * y2
    xmid_ref[0] = xm
    ms = jnp.mean(xm * xm, axis=-1, keepdims=True)
    h2 = xm * lax.rsqrt(ms + EPS) * n2w_ref[...]
    h2_ref[0] = (h2 * sc2 + sh2).astype(BF16)


def _merge_call(att, hf, hb, so, sga, sgm, x, mod3, mlw, n2w, wa, wm, wo, C, tm):
    B, L, _ = x.shape
    nt = L // tm
    off = C // tm

    def lat(w):
        return pl.BlockSpec((1, tm, w), lambda b, i: (b, i, 0))

    def cat(w):
        return pl.BlockSpec((1, tm, w), lambda b, i: (b, i + off, 0))

    def const(shape):
        return pl.BlockSpec(shape, lambda b, i: (0,) * len(shape))

    return pl.pallas_call(
        _merge_kernel,
        grid=(B, nt),
        in_specs=[lat(D), lat(D), lat(D), cat(D), cat(D), cat(D), lat(D),
                  pl.BlockSpec((1, 3, D), lambda b, i: (b, 0, 0)),
                  const((1, D)), const((1, D)), const((D, D)), const((D, D)), const((D, D))],
        out_specs=[lat(D), lat(D)],
        out_shape=[jax.ShapeDtypeStruct((B, L, D), F32), jax.ShapeDtypeStruct((B, L, D), BF16)],
        compiler_params=_cparams(("arbitrary", "arbitrary")),
        name="merge",
    )(att, hf, hb, so, sga, sgm, x, mod3, mlw, n2w, wa, wm, wo)


HALO = 16


def _ffn_kernel(h_ref, hp_ref, hn_ref, xmid_ref, mod_ref, wup_ref, cw_ref, cb_ref, wdn_ref, o_ref,
                u_ref, act_ref, *, n_tiles, tn):
    i = pl.program_id(1)
    tm = h_ref.shape[1]
    h = h_ref[0]
    prev_row = jnp.where(i > 0, hp_ref[0].astype(F32)[HALO - 1:HALO, :], 0.0)
    next_row = jnp.where(i < n_tiles - 1, hn_ref[0].astype(F32)[0:1, :], 0.0)
    top = lax.broadcasted_iota(jnp.int32, (16, D), 0) < 8
    edge = jnp.where(top, prev_row, next_row).astype(BF16)

    def conv_part(c0):
        ue = _dot(edge, wup_ref[:, c0:c0 + tn])
        u_ref[0:8, :] = ue[0:8]
        u_ref[8 + tm:16 + tm, :] = ue[8:16]
        u_ref[8:8 + tm, :] = _dot(h, wup_ref[:, c0:c0 + tn])
        cw = cw_ref[:, c0:c0 + tn]
        return (cb_ref[:, c0:c0 + tn] + u_ref[7:7 + tm, :] * cw[0:1] + u_ref[8:8 + tm, :] * cw[1:2]
                + u_ref[9:9 + tm, :] * cw[2:3])

    for c in range(D_FF // tn):
        a = conv_part(c * tn)
        gte = conv_part(D_FF + c * tn)
        act_ref[:, c * tn:(c + 1) * tn] = (gte * jax.nn.sigmoid(gte) * a).astype(BF16)
    y = _dot(act_ref[...], wdn_ref[...])
    o_ref[0] = xmid_ref[0] + mod_ref[0] * y


def _ffn_call(h2, xmid, g2, wup, cw, cb, wdn, tm, tn):
    B, L, _ = xmid.shape
    nt = L // tm
    hb = tm // HALO
    nhb = L // HALO

    def lat(w):
        return pl.BlockSpec((1, tm, w), lambda b, i: (b, i, 0))

    def const(shape):
        return pl.BlockSpec(shape, lambda b, i: (0,) * len(shape))

    return pl.pallas_call(
        functools.partial(_ffn_kernel, n_tiles=nt, tn=tn),
        grid=(B, nt),
        in_specs=[lat(D),
                  pl.BlockSpec((1, HALO, D), lambda b, i: (b, jnp.maximum(i * hb - 1, 0), 0)),
                  pl.BlockSpec((1, HALO, D), lambda b, i: (b, jnp.minimum((i + 1) * hb, nhb - 1), 0)),
                  lat(D),
                  pl.BlockSpec((1, 1, D), lambda b, i: (b, 0, 0)),
                  const(wup.shape), const(cw.shape), const(cb.shape), const(wdn.shape)],
        out_specs=lat(D),
        out_shape=jax.ShapeDtypeStruct((B, L, D), F32),
        scratch_shapes=[pltpu.VMEM((tm + 16, tn), F32), pltpu.VMEM((tm, D_FF), BF16)],
        compiler_params=_cparams(("arbitrary", "arbitrary")),
        name="ffn",
    )(h2, h2, h2, xmid, g2, wup, cw, cb, wdn)


def _pair_perm(n_heads):
    half = ATT_HEAD_DIM // 2
    idx = []
    for p in range(n_heads // 2):
        for sub in range(4):
            head = 2 * p + (sub % 2)
            d0 = (sub // 2) * half
            idx.extend(head * ATT_HEAD_DIM + d0 + e for e in range(half))
    return np.asarray(idx, np.int32)


def _rope_tables(L, C):
    rows = L // GRID_W
    row = jnp.repeat(jnp.arange(rows, dtype=F32), GRID_W)
    col = jnp.tile(jnp.arange(GRID_W, dtype=F32), rows)
    n_freq = ATT_HEAD_DIM // 4
    inv_freq = ROPE_BASE ** (-jnp.arange(n_freq, dtype=F32) / n_freq)
    ang = jnp.concatenate([row[:, None] * inv_freq, col[:, None] * inv_freq], axis=-1)
    cos = jnp.tile(jnp.cos(ang), (1, 4))
    sin = jnp.tile(jnp.sin(ang), (1, 4))
    sign = jnp.where(jnp.arange(LANES) < LANES // 2, -1.0, 1.0).astype(F32)
    cos = jnp.concatenate([jnp.ones((C, LANES), F32), cos], axis=0)
    sin = jnp.concatenate([jnp.zeros((C, LANES), F32), sin * sign], axis=0)
    return cos, sin


def kernel(x, c, ctx, c_ctx, w_mod, b_mod, norm1_w, w_in, q_norm_w, k_norm_w, attn_sink, ml_gate_b, ml_norm_w,
           w_branch_att, w_branch_ml, w_out, norm2_w, w_up, conv_w, conv_b, w_down):
    B, L, _ = x.shape
    C = ctx.shape[1]
    assert L % 512 == 0 and C % 256 == 0 and L % GRID_W == 0
    l = 0

    n_rows = -(-(B + 1) // 8) * 8
    cc = jnp.concatenate([c, c_ctx[None, :], jnp.zeros((n_rows - B - 1, D), F32)], axis=0)
    mod = _mod_call(cc, w_mod[l], b_mod[l][None, :])
    sh1, sc1, g1, sh2, sc2, g2 = [mod[:, k * D:(k + 1) * D] for k in range(6)]
    lat_mod = jnp.stack([sh1[:B], 1.0 + sc1[:B]], axis=1)
    ctx_mod = jnp.broadcast_to(jnp.stack([sh1[B], 1.0 + sc1[B]], axis=0)[None], (B, 2, D))
    modsel = jnp.stack([ctx_mod, lat_mod], axis=1)
    mod3 = jnp.stack([g1[:B], sh2[:B], 1.0 + sc2[:B]], axis=1)
    g2b = g2[:B][:, None, :]

    w = w_in[l]
    qperm = _pair_perm(ATT_HEADS)
    kperm = _pair_perm(ATT_KV_HEADS)
    w_q = w[:, _O_AQ:_O_AQ + ATT_Q_W][:, qperm]
    w_k = w[:, _O_AK:_O_AK + ATT_KV_W][:, kperm]
    w_g = jnp.pad(w[:, _O_MG:_O_MG + ML_GATE_W], ((0, 0), (0, LANES - ML_GATE_W)))
    w_mk = w[:, _O_MK:_O_MK + ML_QK_W] * (ML_QK_DIM ** -0.5)
    w_p = jnp.concatenate([
        w_q, w_k, w[:, _O_AV:_O_AV + ATT_KV_W], w[:, _O_MQ:_O_MQ + ML_QK_W], w[:, _O_MV:_O_MV + ML_V_W],
        w[:, _O_MO:_O_MO + ML_V_W], w[:, _O_GA:_O_GA + D], w[:, _O_GM:_O_GM + D], w_g], axis=1).astype(BF16)
    wkt = w_mk.T.astype(BF16)
    wgt = w[:, _O_MG:_O_MG + ML_GATE_W].T.astype(BF16)

    head_of_col = np.concatenate([qperm // ATT_HEAD_DIM, ATT_HEADS + kperm // ATT_HEAD_DIM])
    e_np = (head_of_col[:, None] == np.arange(LANES)[None, :]).astype(np.float32)
    e_mat = jnp.asarray(e_np, BF16)
    et_mat = jnp.asarray(np.concatenate([e_np.T, e_np.T], axis=0), BF16)
    qkw = jnp.concatenate([q_norm_w[l][qperm % ATT_HEAD_DIM] * ATT_SCALE,
                           k_norm_w[l][kperm % ATT_HEAD_DIM]])[None, :]
    cos_t, sin_t = _rope_tables(L, C)
    gb = ml_gate_b[l].reshape(1, ML_GATE_W)
    gbt = ml_gate_b[l].reshape(ML_GATE_W, 1)

    q, k, v, mq, mkt, mv, so, sga, sgm, g, gt = _inproj_call(
        ctx, x, modsel, norm1_w[l][None, :], w_p, wkt, wgt, e_mat, et_mat, qkw, cos_t, sin_t, gb, gbt, tm=256)

    att = _attn_call(attn_sink[l], q, k, v, L, C)
    hf, hb = _mlstm_call(mq, mkt, mv, g, gt, L, C)

    xmid, h2 = _merge_call(att, hf, hb, so, sga, sgm, x, mod3, ml_norm_w[l][None, :], norm2_w[l][None, :],
                           w_branch_att[l].astype(BF16), w_branch_ml[l].astype(BF16), w_out[l].astype(BF16),
                           C, tm=256)
    out = _ffn_call(h2, xmid, g2b, w_up[l].astype(BF16), conv_w[l], conv_b[l][None, :],
                    w_down[l].astype(BF16), tm=512, tn=1408)
    return out
```

```python
import functools

import jax
import jax.numpy as jnp
import numpy as np
from jax import lax
from jax.experimental import pallas as pl
from jax.experimental.pallas import tpu as pltpu

D = 1024
GRID_W = 64
ATT_HEADS = 16
ATT_KV_HEADS = 4
ATT_HEAD_DIM = 64
ATT_GROUP = ATT_HEADS // ATT_KV_HEADS
ATT_BLOCK = 128
WINDOW = 128
ROPE_BASE = 10000.0
ATT_SCALE = ATT_HEAD_DIM ** -0.5
ML_HEADS = 4
ML_QK_DIM = 128
ML_V_DIM = 256
ML_CHUNK = 128
D_FF = 2816
EPS = 1e-6
NEG_INF = -1e30

ATT_Q_W = ATT_HEADS * ATT_HEAD_DIM
ATT_KV_W = ATT_KV_HEADS * ATT_HEAD_DIM
ML_QK_W = ML_HEADS * ML_QK_DIM
ML_V_W = ML_HEADS * ML_V_DIM
ML_GATE_W = 2 * 2 * ML_HEADS

LANES = 128
KX_W = ATT_KV_HEADS * 2 * LANES
VMEM_LIMIT = 56 * 1024 * 1024

BF16 = jnp.bfloat16
F32 = jnp.float32

_O_AQ = 0
_O_AK = _O_AQ + ATT_Q_W
_O_AV = _O_AK + ATT_KV_W
_O_MQ = _O_AV + ATT_KV_W
_O_MK = _O_MQ + ML_QK_W
_O_MV = _O_MK + ML_QK_W
_O_MO = _O_MV + ML_V_W
_O_MG = _O_MO + ML_V_W
_O_GA = _O_MG + ML_GATE_W
_O_GM = _O_GA + D

QK_W = ATT_Q_W + ATT_KV_W
_P_QK = 0
_P_MK = _P_QK + QK_W
_P_MG = _P_MK + ML_QK_W
_P_END = _P_MG + LANES
_R_V = 0
_R_MQ = _R_V + ATT_KV_W
_R_MV = _R_MQ + ML_QK_W
_R_MO = _R_MV + ML_V_W
_R_GA = _R_MO + ML_V_W
_R_GM = _R_GA + D
_R_END = _R_GM + D


def _dot(a, b):
    return jnp.dot(a, b, preferred_element_type=F32)


def _dot_nt(a, b):
    return lax.dot_general(a, b, (((1,), (1,)), ((), ())), preferred_element_type=F32)


def _dot_tn(a, b):
    return lax.dot_general(a, b, (((0,), (0,)), ((), ())), preferred_element_type=F32)


def _cparams(sem):
    return pltpu.CompilerParams(dimension_semantics=sem, vmem_limit_bytes=VMEM_LIMIT)


def _mod_kernel(c_ref, w_ref, b_ref, o_ref):
    c = c_ref[...]
    a = c * jax.nn.sigmoid(c)
    o_ref[...] = jnp.dot(a, w_ref[...], preferred_element_type=F32,
                         precision=lax.Precision.HIGHEST) + b_ref[...]


def _mod_call(cc, w_mod, b_mod):
    rows = cc.shape[0]
    n = w_mod.shape[1]
    tn = 1536
    return pl.pallas_call(
        _mod_kernel,
        grid=(n // tn,),
        in_specs=[pl.BlockSpec((rows, D), lambda j: (0, 0)),
                  pl.BlockSpec((D, tn), lambda j: (0, j)),
                  pl.BlockSpec((1, tn), lambda j: (0, j))],
        out_specs=pl.BlockSpec((rows, tn), lambda j: (0, j)),
        out_shape=jax.ShapeDtypeStruct((rows, n), F32),
        compiler_params=_cparams(("arbitrary",)),
        name="mod",
    )(cc, w_mod, b_mod)


def _split3(x):
    x1 = x.astype(BF16)
    r = x - x1.astype(F32)
    x2 = r.astype(BF16)
    x3 = (r - x2.astype(F32)).astype(BF16)
    return x1, x2, x3


def _inproj_kernel(ctx_ref, x_ref, mod_ref, n1w_ref, w_ref, wt_ref, wgt_ref, e_ref, et_ref, qkw_ref,
                   cos_ref, sin_ref, gb_ref, gbt_ref, tril_ref, triu_ref,
                   q_ref, k_ref, mk_ref, g_ref, bc_ref, vt_ref, mqt_ref, mvt_ref, sot_ref, sgat_ref, sgmt_ref, br_ref,
                   hn_ref, *, n_ctx_tiles):
    i = pl.program_id(1)
    is_ctx = i < n_ctx_tiles
    xin = jnp.where(is_ctx, ctx_ref[0], x_ref[0])
    ms = jnp.mean(xin * xin, axis=-1, keepdims=True)
    y = xin * lax.rsqrt(ms + EPS) * n1w_ref[...]
    y = y * mod_ref[0, 0, 1:2, :] + mod_ref[0, 0, 0:1, :]
    hn_ref[...] = y.astype(BF16)
    hn = hn_ref[...]

    acc = _dot(hn, w_ref[:, _P_QK:_P_QK + QK_W])
    ss = _dot((acc * acc).astype(BF16), e_ref[...])
    r = lax.rsqrt(ss * (1.0 / ATT_HEAD_DIM) + EPS)
    r_hi = r.astype(BF16)
    r_lo = (r - r_hi.astype(F32)).astype(BF16)
    rb = _dot(jnp.concatenate([r_hi, r_lo], axis=1), et_ref[...])
    qn = acc * rb * qkw_ref[...]
    cos = cos_ref[...]
    sin = sin_ref[...]
    lane = lax.broadcasted_iota(jnp.int32, (1, LANES), 1)
    keep = ((lane // 32) % 2) == 0
    for gi in range(QK_W // LANES):
        xs = qn[:, gi * LANES:(gi + 1) * LANES]
        o = xs * cos + pltpu.roll(xs, LANES // 2, 1) * sin
        if gi < ATT_Q_W // LANES:
            q_ref[0, :, gi * LANES:(gi + 1) * LANES] = o.astype(BF16)
        else:
            c0 = 4 * (gi - ATT_Q_W // LANES) * LANES
            o32 = pltpu.roll(o, 32, 1)
            o96 = pltpu.roll(o, 96, 1)
            k_ref[0, :, c0:c0 + LANES] = jnp.where(keep, o, 0.0).astype(BF16)
            k_ref[0, :, c0 + LANES:c0 + 2 * LANES] = jnp.where(keep, 0.0, o32).astype(BF16)
            k_ref[0, :, c0 + 2 * LANES:c0 + 3 * LANES] = jnp.where(keep, o96, 0.0).astype(BF16)
            k_ref[0, :, c0 + 3 * LANES:c0 + 4 * LANES] = jnp.where(keep, 0.0, o).astype(BF16)

    mk_ref[0] = _dot(hn, w_ref[:, _P_MK:_P_MK + ML_QK_W]).astype(BF16)

    g16 = _dot(hn, w_ref[:, _P_MG:_P_MG + LANES])[:, :ML_GATE_W] + gb_ref[...]
    g_ref[0] = g16
    tri_lo = tril_ref[...]
    tri_up = triu_ref[...]
    parts = _split3(jax.nn.log_sigmoid(g16))
    fwd_col = lax.broadcasted_iota(jnp.int32, (1, ML_GATE_W), 1) < ML_GATE_W // 2
    bc_ref[0] = jnp.where(fwd_col, sum(_dot(tri_lo, p) for p in parts), sum(_dot(tri_up, p) for p in parts))
    gt16 = _dot_nt(wgt_ref[...], hn) + gbt_ref[...]
    parts = _split3(jax.nn.log_sigmoid(gt16))
    fwd_row = lax.broadcasted_iota(jnp.int32, (ML_GATE_W, 1), 0) < ML_GATE_W // 2
    br_ref[0] = jnp.where(fwd_row, sum(_dot(p, tri_up) for p in parts), sum(_dot(p, tri_lo) for p in parts))

    vt_ref[0] = _dot_nt(wt_ref[_R_V:_R_V + ATT_KV_W, :], hn).astype(BF16)
    mqt_ref[0] = _dot_nt(wt_ref[_R_MQ:_R_MQ + ML_QK_W, :], hn).astype(BF16)
    mvt_ref[0] = _dot_nt(wt_ref[_R_MV:_R_MV + ML_V_W, :], hn).astype(BF16)
    sot_ref[0] = jax.nn.sigmoid(_dot_nt(wt_ref[_R_MO:_R_MO + ML_V_W, :], hn)).astype(BF16)
    sgat_ref[0] = jax.nn.sigmoid(_dot_nt(wt_ref[_R_GA:_R_GA + D, :], hn)).astype(BF16)
    sgmt_ref[0] = jax.nn.sigmoid(_dot_nt(wt_ref[_R_GM:_R_GM + D, :], hn)).astype(BF16)


def _block_tri(n, block, lower):
    r = np.arange(n)[:, None]
    c = np.arange(n)[None, :]
    same = (r // block) == (c // block)
    return jnp.asarray(same & ((c <= r) if lower else (c >= r)), BF16)


def _inproj_call(ctx, x, modsel, n1w, w_p, w_t, wgt, e_mat, et_mat, qkw, cos_t, sin_t, gb, gbt, tm):
    B, L, _ = x.shape
    C = ctx.shape[1]
    Lt = L + C
    nct = C // tm
    nt = Lt // tm
    tril = _block_tri(tm, ML_CHUNK, True)
    triu = _block_tri(tm, ML_CHUNK, False)

    def const(shape):
        return pl.BlockSpec(shape, lambda b, i: (0,) * len(shape))

    def rows(w):
        return pl.BlockSpec((1, tm, w), lambda b, i: (b, i, 0))

    def cols(h):
        return pl.BlockSpec((1, h, tm), lambda b, i: (b, 0, i))

    in_specs = [
        pl.BlockSpec((1, tm, D), lambda b, i: (b, jnp.minimum(i, nct - 1), 0)),
        pl.BlockSpec((1, tm, D), lambda b, i: (b, jnp.maximum(i - nct, 0), 0)),
        pl.BlockSpec((1, 1, 2, D), lambda b, i: (b, jnp.where(i < nct, 0, 1), 0, 0)),
        const((1, D)),
        const(w_p.shape), const(w_t.shape), const(wgt.shape), const(e_mat.shape), const(et_mat.shape),
        const(qkw.shape),
        pl.BlockSpec((tm, LANES), lambda b, i: (i, 0)),
        pl.BlockSpec((tm, LANES), lambda b, i: (i, 0)),
        const(gb.shape), const(gbt.shape), const(tril.shape), const(triu.shape),
    ]
    out_specs = [rows(ATT_Q_W), rows(KX_W), rows(ML_QK_W), rows(ML_GATE_W), rows(ML_GATE_W),
                 cols(ATT_KV_W), cols(ML_QK_W), cols(ML_V_W), cols(ML_V_W), cols(D), cols(D), cols(ML_GATE_W)]
    out_shape = [
        jax.ShapeDtypeStruct((B, Lt, ATT_Q_W), BF16),
        jax.ShapeDtypeStruct((B, Lt, KX_W), BF16),
        jax.ShapeDtypeStruct((B, Lt, ML_QK_W), BF16),
        jax.ShapeDtypeStruct((B, Lt, ML_GATE_W), F32),
        jax.ShapeDtypeStruct((B, Lt, ML_GATE_W), F32),
        jax.ShapeDtypeStruct((B, ATT_KV_W, Lt), BF16),
        jax.ShapeDtypeStruct((B, ML_QK_W, Lt), BF16),
        jax.ShapeDtypeStruct((B, ML_V_W, Lt), BF16),
        jax.ShapeDtypeStruct((B, ML_V_W, Lt), BF16),
        jax.ShapeDtypeStruct((B, D, Lt), BF16),
        jax.ShapeDtypeStruct((B, D, Lt), BF16),
        jax.ShapeDtypeStruct((B, ML_GATE_W, Lt), F32),
    ]
    return pl.pallas_call(
        functools.partial(_inproj_kernel, n_ctx_tiles=nct),
        grid=(B, nt),
        in_specs=in_specs,
        out_specs=out_specs,
        out_shape=out_shape,
        scratch_shapes=[pltpu.VMEM((tm, D), BF16)],
        compiler_params=_cparams(("arbitrary", "arbitrary")),
        name="inproj",
    )(ctx, x, modsel, n1w, w_p, w_t, wgt, e_mat, et_mat, qkw, cos_t, sin_t, gb, gbt, tril, triu)


def _attn_kernel(sink_ref, q_ref, kc_ref, kp_ref, k0_ref, kn_ref, vc_ref, vp_ref, v0_ref, vn_ref, o_ref,
                 *, n_blocks):
    i = pl.program_id(1)
    T = ATT_BLOCK
    hd = ATT_HEAD_DIM
    s_idx = lax.broadcasted_iota(jnp.int32, (T, 2 * T), 0)
    t_idx = lax.broadcasted_iota(jnp.int32, (T, 2 * T), 1) % T
    ok_prev = (s_idx >= t_idx) & (i > 0)
    ok_next = (s_idx <= t_idx) & (i < n_blocks - 1)
    first = lax.broadcasted_iota(jnp.int32, (1, 2 * T), 1) < T

    k_all = jnp.concatenate([kp_ref[0], k0_ref[0], kn_ref[0], kc_ref[0]], axis=0)
    vt_all = jnp.concatenate([vp_ref[0], v0_ref[0], vn_ref[0], vc_ref[0]], axis=1)
    q = q_ref[0]

    def scores(n):
        kh, var = divmod(n, 2)
        q2 = jnp.concatenate([q[:, (2 * kh) * LANES:(2 * kh + 1) * LANES],
                              q[:, (2 * kh + 1) * LANES:(2 * kh + 2) * LANES]], axis=0)
        kk = k_all[:, (2 * kh + var) * LANES:(2 * kh + var + 1) * LANES]
        return _dot_nt(kk, q2)

    n_iter = 2 * ATT_KV_HEADS
    st_next = scores(0)
    for n in range(n_iter):
        kh, var = divmod(n, 2)
        st = st_next
        if n + 1 < n_iter:
            st_next = scores(n + 1)
        vt = vt_all[kh * hd:(kh + 1) * hd, :]
        st = jnp.concatenate([jnp.where(ok_prev, st[0:T], NEG_INF), st[T:2 * T],
                              jnp.where(ok_next, st[2 * T:3 * T], NEG_INF), st[3 * T:]], axis=0)
        h0 = ATT_GROUP * kh + var
        h1 = h0 + 2
        sink = jnp.where(first, sink_ref[h0], sink_ref[h1])
        m = jnp.maximum(jnp.max(st, axis=0, keepdims=True), sink)
        p = jnp.exp(st - m)
        denom = jnp.sum(p, axis=0, keepdims=True) + jnp.exp(sink - m)
        ot = _dot(vt, p.astype(BF16)) * (1.0 / denom)
        o_ref[0, h0 * hd:(h0 + 1) * hd, :] = ot[:, 0:T].astype(BF16)
        o_ref[0, h1 * hd:(h1 + 1) * hd, :] = ot[:, T:2 * T].astype(BF16)


def _attn_call(sink, q, kx, vt, L, C):
    B = q.shape[0]
    T = ATT_BLOCK
    nb = L // T
    ncb = C // T

    def k_spec(off):
        return pl.BlockSpec((1, T, KX_W), lambda b, i: (b, ncb + jnp.clip(i + off, 0, nb - 1), 0))

    def v_spec(off):
        return pl.BlockSpec((1, ATT_KV_W, T), lambda b, i: (b, 0, ncb + jnp.clip(i + off, 0, nb - 1)))

    return pl.pallas_call(
        functools.partial(_attn_kernel, n_blocks=nb),
        grid=(B, nb),
        in_specs=[pl.BlockSpec(memory_space=pltpu.SMEM),
                  pl.BlockSpec((1, T, ATT_Q_W), lambda b, i: (b, ncb + i, 0)),
                  pl.BlockSpec((1, C, KX_W), lambda b, i: (b, 0, 0)), k_spec(-1), k_spec(0), k_spec(1),
                  pl.BlockSpec((1, ATT_KV_W, C), lambda b, i: (b, 0, 0)), v_spec(-1), v_spec(0), v_spec(1)],
        out_specs=pl.BlockSpec((1, ATT_Q_W, T), lambda b, i: (b, 0, i)),
        out_shape=jax.ShapeDtypeStruct((B, ATT_Q_W, L), BF16),
        compiler_params=_cparams(("arbitrary", "arbitrary")),
        name="attn",
    )(sink, q, kx, kx, kx, kx, vt, vt, vt, vt)


def _mlstm_kernel(qtf_ref, kf_ref, vtf_ref, gf_ref, bcf_ref, brf_ref, qtb_ref, kb_ref, vtb_ref, gb_ref, bcb_ref,
                  brb_ref, hf_ref, hb_ref, c_ref, n_ref, m_ref):
    j = pl.program_id(1)
    T = ML_CHUNK

    @pl.when(j == 0)
    def _():
        c_ref[...] = jnp.zeros_like(c_ref)
        n_ref[...] = jnp.zeros_like(n_ref)
        m_ref[...] = jnp.zeros_like(m_ref)

    row = lax.broadcasted_iota(jnp.int32, (T, T), 0)
    col = lax.broadcasted_iota(jnp.int32, (T, T), 1)
    ones8 = jnp.ones((8, T), BF16)
    top = lax.broadcasted_iota(jnp.int32, (8, ML_QK_DIM), 0) == 0

    dirs = (
        (qtf_ref, kf_ref, vtf_ref, gf_ref, bcf_ref, brf_ref, hf_ref, col >= row, T - 1),
        (qtb_ref, kb_ref, vtb_ref, gb_ref, bcb_ref, brb_ref, hb_ref, col <= row, 0),
    )
    chains = []
    for d, (qt_ref, k_ref, vt_ref, g_ref, bc_ref, br_ref, h_ref, mask, last) in enumerate(dirs):
        g = g_ref[0]
        bc = bc_ref[0]
        br = br_ref[0]
        for h in range(ML_HEADS):
            ci = d * ML_HEADS + h
            gi = d * 2 * ML_HEADS + h
            fi = gi + ML_HEADS
            qt = qt_ref[0, h * ML_QK_DIM:(h + 1) * ML_QK_DIM, :]
            k = k_ref[0, :, h * ML_QK_DIM:(h + 1) * ML_QK_DIM]
            vt = vt_ref[0, h * ML_V_DIM:(h + 1) * ML_V_DIM, :]
            u_col = g[:, gi:gi + 1] - bc[:, fi:fi + 1]
            b_row = br[fi:fi + 1, :]
            m_old = m_ref[ci, 0:1, 0:1]
            n_old = n_ref[ci]
            ct_old = c_ref[ci]
            st = _dot(k, qt)
            n_hi = n_old.astype(BF16)
            n_lo = (n_old - n_hi.astype(F32)).astype(BF16)
            qn2 = _dot(jnp.where(top, n_hi, n_lo), qt)
            m_last = jnp.maximum(jnp.max(u_col, axis=0, keepdims=True), m_old)
            decay = jnp.exp(m_old - m_last)
            kw = (k.astype(F32) * jnp.exp(u_col - m_last)).astype(BF16)
            c_new = decay * ct_old + _dot(vt, kw)
            n_new = decay * n_old + _dot(ones8, kw)
            m_new = b_row[:, last:last + 1] + m_last
            chains.append((ci, h, h_ref, mask, qt, vt, u_col, b_row, m_old, ct_old, st, qn2, c_new, n_new, m_new))

    for (ci, h, h_ref, mask, qt, vt, u_col, b_row, m_old, ct_old, st, qn2, c_new, n_new, m_new) in chains:
        umat = jnp.where(mask, u_col, -jnp.inf)
        m_row = jnp.maximum(jnp.max(umat, axis=0, keepdims=True), m_old)
        pt = st * jnp.exp(umat - m_row)
        w_int = jnp.exp(m_old - m_row)
        e_row = jnp.exp(-(b_row + m_row))
        nq = jnp.sum(pt, axis=0, keepdims=True) + w_int * (qn2[0:1, :] + qn2[1:2, :])
        den = jnp.maximum(jnp.abs(nq), e_row)
        lhs = jnp.concatenate([vt, ct_old.astype(BF16)], axis=1)
        rhs = jnp.concatenate([pt.astype(BF16), (qt.astype(F32) * w_int).astype(BF16)], axis=0)
        h_ref[0, h * ML_V_DIM:(h + 1) * ML_V_DIM, :] = (_dot(lhs, rhs) * (1.0 / den)).astype(BF16)

    for (ci, h, h_ref, mask, qt, vt, u_col, b_row, m_old, ct_old, st, qn2, c_new, n_new, m_new) in chains:
        c_ref[ci] = c_new
        n_ref[ci] = n_new
        m_ref[ci] = jnp.broadcast_to(m_new, (8, LANES))


def _mlstm_call(mqt, mk, mvt, g, bc, br, L, C):
    B = mk.shape[0]
    T = ML_CHUNK
    ncc = C // T
    ncl = L // T
    nc = ncc + ncl

    def fwd(j):
        return j

    def bwd(j):
        return jnp.where(j < ncc, ncc - 1 - j, nc - 1 - (j - ncc))

    def specs(order):
        return [
            pl.BlockSpec((1, ML_QK_W, T), lambda b, j: (b, 0, order(j))),
            pl.BlockSpec((1, T, ML_QK_W), lambda b, j: (b, order(j), 0)),
            pl.BlockSpec((1, ML_V_W, T), lambda b, j: (b, 0, order(j))),
            pl.BlockSpec((1, T, ML_GATE_W), lambda b, j: (b, order(j), 0)),
            pl.BlockSpec((1, T, ML_GATE_W), lambda b, j: (b, order(j), 0)),
            pl.BlockSpec((1, ML_GATE_W, T), lambda b, j: (b, 0, order(j))),
        ]

    out_specs = [
        pl.BlockSpec((1, ML_V_W, T), lambda b, j: (b, 0, jnp.maximum(j - ncc, 0))),
        pl.BlockSpec((1, ML_V_W, T), lambda b, j: (b, 0, ncl - 1 - jnp.maximum(j - ncc, 0))),
    ]
    n_chain = 2 * ML_HEADS
    return pl.pallas_call(
        _mlstm_kernel,
        grid=(B, nc),
        in_specs=specs(fwd) + specs(bwd),
        out_specs=out_specs,
        out_shape=[jax.ShapeDtypeStruct((B, ML_V_W, L), BF16)] * 2,
        scratch_shapes=[pltpu.VMEM((n_chain, ML_V_DIM, ML_QK_DIM), F32),
                        pltpu.VMEM((n_chain, 8, ML_QK_DIM), F32),
                        pltpu.VMEM((n_chain, 8, LANES), F32)],
        compiler_params=_cparams(("arbitrary", "arbitrary")),
        name="mlstm",
    )(mqt, mk, mvt, g, bc, br, mqt, mk, mvt, g, bc, br)


def _merge_kernel(att_ref, hf_ref, hb_ref, so_ref, sga_ref, sgm_ref, x_ref, mod_ref, mlw_ref, n2w_ref,
                  wat_ref, wmt_ref, wo_ref, xmid_ref, h2_ref):
    ht = hf_ref[0].astype(F32) + hb_ref[0].astype(F32)
    parts = []
    for h in range(ML_HEADS):
        seg = ht[h * ML_V_DIM:(h + 1) * ML_V_DIM, :]
        ms = jnp.mean(seg * seg, axis=0, keepdims=True)
        parts.append(seg * lax.rsqrt(ms + EPS))
    ml = (jnp.concatenate(parts, axis=0) * mlw_ref[...] * so_ref[0].astype(F32)).astype(BF16)
    ya = _dot(wat_ref[...], att_ref[0])
    ym = _dot(wmt_ref[...], ml)
    y = (sga_ref[0].astype(F32) * ya + sgm_ref[0].astype(F32) * ym).astype(BF16)
    y2 = _dot_tn(y, wo_ref[...])
    g1 = mod_ref[0, 0:1, :]
    sh2 = mod_ref[0, 1:2, :]
    sc2 = mod_ref[0, 2:3, :]
    xm = x_ref[0] + g1 * y2
    xmid_ref[0] = xm
    ms = jnp.mean(xm * xm, axis=-1, keepdims=True)
    h2 = xm * lax.rsqrt(ms + EPS) * n2w_ref[...]
    h2_ref[0] = (h2 * sc2 + sh2).astype(BF16)


def _merge_call(att_t, hf_t, hb_t, so_t, sga_t, sgm_t, x, mod3, mlw_b, n2w, wat, wmt, wo, C, tm):
    B, L, _ = x.shape
    nt = L // tm
    off = C // tm

    def lat(w):
        return pl.BlockSpec((1, tm, w), lambda b, i: (b, i, 0))

    def lat_t(h):
        return pl.BlockSpec((1, h, tm), lambda b, i: (b, 0, i))

    def cat_t(h):
        return pl.BlockSpec((1, h, tm), lambda b, i: (b, 0, i + off))

    def const(shape):
        return pl.BlockSpec(shape, lambda b, i: (0,) * len(shape))

    return pl.pallas_call(
        _merge_kernel,
        grid=(B, nt),
        in_specs=[lat_t(D), lat_t(D), lat_t(D), cat_t(D), cat_t(D), cat_t(D), lat(D),
                  pl.BlockSpec((1, 3, D), lambda b, i: (b, 0, 0)),
                  const((D, tm)), const((1, D)), const((D, D)), const((D, D)), const((D, D))],
        out_specs=[lat(D), lat(D)],
        out_shape=[jax.ShapeDtypeStruct((B, L, D), F32), jax.ShapeDtypeStruct((B, L, D), BF16)],
        compiler_params=_cparams(("arbitrary", "arbitrary")),
        name="merge",
    )(att_t, hf_t, hb_t, so_t, sga_t, sgm_t, x, mod3, mlw_b, n2w, wat, wmt, wo)


HALO = 16


def _ffn_kernel(h_ref, hp_ref, hn_ref, xmid_ref, mod_ref, wup_ref, cw_ref, cb_ref, wdn_ref, o_ref,
                u_ref, act_ref, *, n_tiles, tn):
    i = pl.program_id(1)
    tm = h_ref.shape[1]
    h = h_ref[0]
    prev_row = jnp.where(i > 0, hp_ref[0].astype(F32)[HALO - 1:HALO, :], 0.0)
    next_row = jnp.where(i < n_tiles - 1, hn_ref[0].astype(F32)[0:1, :], 0.0)
    top = lax.broadcasted_iota(jnp.int32, (16, D), 0) < 8
    edge = jnp.where(top, prev_row, next_row).astype(BF16)

    def conv_part(c0):
        ue = _dot(edge, wup_ref[:, c0:c0 + tn])
        u_ref[0:8, :] = ue[0:8]
        u_ref[8 + tm:16 + tm, :] = ue[8:16]
        u_ref[8:8 + tm, :] = _dot(h, wup_ref[:, c0:c0 + tn])
        cw = cw_ref[:, c0:c0 + tn]
        return (cb_ref[:, c0:c0 + tn] + u_ref[7:7 + tm, :] * cw[0:1] + u_ref[8:8 + tm, :] * cw[1:2]
                + u_ref[9:9 + tm, :] * cw[2:3])

    for c in range(D_FF // tn):
        a = conv_part(c * tn)
        gte = conv_part(D_FF + c * tn)
        act_ref[:, c * tn:(c + 1) * tn] = (gte * jax.nn.sigmoid(gte) * a).astype(BF16)
    y = _dot(act_ref[...], wdn_ref[...])
    o_ref[0] = xmid_ref[0] + mod_ref[0] * y


def _ffn_call(h2, xmid, g2, wup, cw, cb, wdn, tm, tn):
    B, L, _ = xmid.shape
    nt = L // tm
    hb = tm // HALO
    nhb = L // HALO

    def lat(w):
        return pl.BlockSpec((1, tm, w), lambda b, i: (b, i, 0))

    def const(shape):
        return pl.BlockSpec(shape, lambda b, i: (0,) * len(shape))

    return pl.pallas_call(
        functools.partial(_ffn_kernel, n_tiles=nt, tn=tn),
        grid=(B, nt),
        in_specs=[lat(D),
                  pl.BlockSpec((1, HALO, D), lambda b, i: (b, jnp.maximum(i * hb - 1, 0), 0)),
                  pl.BlockSpec((1, HALO, D), lambda b, i: (b, jnp.minimum((i + 1) * hb, nhb - 1), 0)),
                  lat(D),
                  pl.BlockSpec((1, 1, D), lambda b, i: (b, 0, 0)),
                  const(wup.shape), const(cw.shape), const(cb.shape), const(wdn.shape)],
        out_specs=lat(D),
        out_shape=jax.ShapeDtypeStruct((B, L, D), F32),
        scratch_shapes=[pltpu.VMEM((tm + 16, tn), F32), pltpu.VMEM((tm, D_FF), BF16)],
        compiler_params=_cparams(("arbitrary", "arbitrary")),
        name="ffn",
    )(h2, h2, h2, xmid, g2, wup, cw, cb, wdn)


def _pair_perm(n_heads):
    half = ATT_HEAD_DIM // 2
    idx = []
    for p in range(n_heads // 2):
        for sub in range(4):
            head = 2 * p + (sub % 2)
            d0 = (sub // 2) * half
            idx.extend(head * ATT_HEAD_DIM + d0 + e for e in range(half))
    return np.asarray(idx, np.int32)


def _rope_tables(L, C):
    rows = L // GRID_W
    row = jnp.repeat(jnp.arange(rows, dtype=F32), GRID_W)
    col = jnp.tile(jnp.arange(GRID_W, dtype=F32), rows)
    n_freq = ATT_HEAD_DIM // 4
    inv_freq = ROPE_BASE ** (-jnp.arange(n_freq, dtype=F32) / n_freq)
    ang = jnp.concatenate([row[:, None] * inv_freq, col[:, None] * inv_freq], axis=-1)
    cos = jnp.tile(jnp.cos(ang), (1, 4))
    sin = jnp.tile(jnp.sin(ang), (1, 4))
    sign = jnp.where(jnp.arange(LANES) < LANES // 2, -1.0, 1.0).astype(F32)
    cos = jnp.concatenate([jnp.ones((C, LANES), F32), cos], axis=0)
    sin = jnp.concatenate([jnp.zeros((C, LANES), F32), sin * sign], axis=0)
    return cos, sin


def kernel(x, c, ctx, c_ctx, w_mod, b_mod, norm1_w, w_in, q_norm_w, k_norm_w, attn_sink, ml_gate_b, ml_norm_w,
           w_branch_att, w_branch_ml, w_out, norm2_w, w_up, conv_w, conv_b, w_down):
    B, L, _ = x.shape
    C = ctx.shape[1]
    assert L % 512 == 0 and C % 256 == 0 and L % GRID_W == 0
    l = 0
    tm_merge = 256

    n_rows = -(-(B + 1) // 8) * 8
    cc = jnp.concatenate([c, c_ctx[None, :], jnp.zeros((n_rows - B - 1, D), F32)], axis=0)
    mod = _mod_call(cc, w_mod[l], b_mod[l][None, :])
    sh1, sc1, g1, sh2, sc2, g2 = [mod[:, k * D:(k + 1) * D] for k in range(6)]
    lat_mod = jnp.stack([sh1[:B], 1.0 + sc1[:B]], axis=1)
    ctx_mod = jnp.broadcast_to(jnp.stack([sh1[B], 1.0 + sc1[B]], axis=0)[None], (B, 2, D))
    modsel = jnp.stack([ctx_mod, lat_mod], axis=1)
    mod3 = jnp.stack([g1[:B], sh2[:B], 1.0 + sc2[:B]], axis=1)
    g2b = g2[:B][:, None, :]

    w = w_in[l]
    qperm = _pair_perm(ATT_HEADS)
    kperm = _pair_perm(ATT_KV_HEADS)
    w_q = w[:, _O_AQ:_O_AQ + ATT_Q_W][:, qperm]
    w_k = w[:, _O_AK:_O_AK + ATT_KV_W][:, kperm]
    w_g = jnp.pad(w[:, _O_MG:_O_MG + ML_GATE_W], ((0, 0), (0, LANES - ML_GATE_W)))
    w_mk = w[:, _O_MK:_O_MK + ML_QK_W] * (ML_QK_DIM ** -0.5)
    w_p = jnp.concatenate([w_q, w_k, w_mk, w_g], axis=1).astype(BF16)
    w_t = jnp.concatenate([w[:, _O_AV:_O_AV + ATT_KV_W], w[:, _O_MQ:_O_MQ + ML_QK_W], w[:, _O_MV:_O_MV + ML_V_W],
                           w[:, _O_MO:_O_MO + ML_V_W], w[:, _O_GA:_O_GA + D], w[:, _O_GM:_O_GM + D]],
                          axis=1).T.astype(BF16)
    wgt = w[:, _O_MG:_O_MG + ML_GATE_W].T.astype(BF16)

    head_of_col = np.concatenate([qperm // ATT_HEAD_DIM, ATT_HEADS + kperm // ATT_HEAD_DIM])
    e_np = (head_of_col[:, None] == np.arange(LANES)[None, :]).astype(np.float32)
    e_mat = jnp.asarray(e_np, BF16)
    et_mat = jnp.asarray(np.concatenate([e_np.T, e_np.T], axis=0), BF16)
    qkw = jnp.concatenate([q_norm_w[l][qperm % ATT_HEAD_DIM] * ATT_SCALE,
                           k_norm_w[l][kperm % ATT_HEAD_DIM]])[None, :]
    cos_t, sin_t = _rope_tables(L, C)
    gb = ml_gate_b[l].reshape(1, ML_GATE_W)
    gbt = ml_gate_b[l].reshape(ML_GATE_W, 1)

    q, kx, mk, g, bc, vt, mqt, mvt, sot, sgat, sgmt, br = _inproj_call(
        ctx, x, modsel, norm1_w[l][None, :], w_p, w_t, wgt, e_mat, et_mat, qkw, cos_t, sin_t, gb, gbt, tm=256)

    att_t = _attn_call(attn_sink[l], q, kx, vt, L, C)
    hf_t, hb_t = _mlstm_call(mqt, mk, mvt, g, bc, br, L, C)

    mlw_b = jnp.broadcast_to(ml_norm_w[l][:, None], (ML_V_W, tm_merge))
    xmid, h2 = _merge_call(att_t, hf_t, hb_t, sot, sgat, sgmt, x, mod3, mlw_b, norm2_w[l][None, :],
                           w_branch_att[l].T.astype(BF16), w_branch_ml[l].T.astype(BF16), w_out[l].astype(BF16),
                           C, tm=tm_merge)
    out = _ffn_call(h2, xmid, g2b, w_up[l].astype(BF16), conv_w[l], conv_b[l][None, :],
                    w_down[l].astype(BF16), tm=512, tn=1408)
    return out
```

```python
import functools

import jax
import jax.numpy as jnp
import numpy as np
from jax import lax
from jax.experimental import pallas as pl
from jax.experimental.pallas import tpu as pltpu

D = 1024
GRID_W = 64
ATT_HEADS = 16
ATT_KV_HEADS = 4
ATT_HEAD_DIM = 64
ATT_GROUP = ATT_HEADS // ATT_KV_HEADS
ATT_BLOCK = 128
WINDOW = 128
ROPE_BASE = 10000.0
ATT_SCALE = ATT_HEAD_DIM ** -0.5
LOG2E = 1.4426950408889634
ML_HEADS = 4
ML_QK_DIM = 128
ML_V_DIM = 256
ML_CHUNK = 128
D_FF = 2816
EPS = 1e-6
NEG_INF = -1e30

ATT_Q_W = ATT_HEADS * ATT_HEAD_DIM
ATT_KV_W = ATT_KV_HEADS * ATT_HEAD_DIM
ML_QK_W = ML_HEADS * ML_QK_DIM
ML_V_W = ML_HEADS * ML_V_DIM
ML_GATE_W = 2 * 2 * ML_HEADS

LANES = 128
KX_W = ATT_KV_HEADS * 2 * LANES
VMEM_LIMIT = 56 * 1024 * 1024

BF16 = jnp.bfloat16
F32 = jnp.float32

_O_AQ = 0
_O_AK = _O_AQ + ATT_Q_W
_O_AV = _O_AK + ATT_KV_W
_O_MQ = _O_AV + ATT_KV_W
_O_MK = _O_MQ + ML_QK_W
_O_MV = _O_MK + ML_QK_W
_O_MO = _O_MV + ML_V_W
_O_MG = _O_MO + ML_V_W
_O_GA = _O_MG + ML_GATE_W
_O_GM = _O_GA + D

QK_W = ATT_Q_W + ATT_KV_W
_P_QK = 0
_P_MK = _P_QK + QK_W
_P_MG = _P_MK + ML_QK_W
_P_END = _P_MG + LANES
_R_V = 0
_R_MQ = _R_V + ATT_KV_W
_R_MV = _R_MQ + ML_QK_W
_R_MO = _R_MV + ML_V_W
_R_GA = _R_MO + ML_V_W
_R_GM = _R_GA + D
_R_END = _R_GM + D


def _dot(a, b):
    return jnp.dot(a, b, preferred_element_type=F32)


def _dot_nt(a, b):
    return lax.dot_general(a, b, (((1,), (1,)), ((), ())), preferred_element_type=F32)


def _dot_tn(a, b):
    return lax.dot_general(a, b, (((0,), (0,)), ((), ())), preferred_element_type=F32)


def _cparams(sem):
    return pltpu.CompilerParams(dimension_semantics=sem, vmem_limit_bytes=VMEM_LIMIT)


def _mod_kernel(c_ref, w_ref, b_ref, o_ref):
    c = c_ref[...]
    a = c * jax.nn.sigmoid(c)
    o_ref[...] = jnp.dot(a, w_ref[...], preferred_element_type=F32,
                         precision=lax.Precision.HIGHEST) + b_ref[...]


def _mod_call(cc, w_mod, b_mod):
    rows = cc.shape[0]
    n = w_mod.shape[1]
    tn = 1536
    return pl.pallas_call(
        _mod_kernel,
        grid=(n // tn,),
        in_specs=[pl.BlockSpec((rows, D), lambda j: (0, 0)),
                  pl.BlockSpec((D, tn), lambda j: (0, j)),
                  pl.BlockSpec((1, tn), lambda j: (0, j))],
        out_specs=pl.BlockSpec((rows, tn), lambda j: (0, j)),
        out_shape=jax.ShapeDtypeStruct((rows, n), F32),
        compiler_params=_cparams(("arbitrary",)),
        name="mod",
    )(cc, w_mod, b_mod)


def _split3(x):
    x1 = x.astype(BF16)
    r = x - x1.astype(F32)
    x2 = r.astype(BF16)
    x3 = (r - x2.astype(F32)).astype(BF16)
    return x1, x2, x3


def _inproj_kernel(ctx_ref, x_ref, mod_ref, n1w_ref, w_ref, wt_ref, wgt_ref, e_ref, et_ref, qkw_ref,
                   cos_ref, sin_ref, gb_ref, gbt_ref, tril_ref, triu_ref,
                   q_ref, k_ref, mk_ref, g_ref, bc_ref, vt_ref, mqt_ref, mvt_ref, sot_ref, sgat_ref, sgmt_ref, br_ref,
                   hn_ref, *, n_ctx_tiles):
    i = pl.program_id(1)
    is_ctx = i < n_ctx_tiles
    xin = jnp.where(is_ctx, ctx_ref[0], x_ref[0])
    ms = jnp.mean(xin * xin, axis=-1, keepdims=True)
    y = xin * lax.rsqrt(ms + EPS) * n1w_ref[...]
    y = y * mod_ref[0, 0, 1:2, :] + mod_ref[0, 0, 0:1, :]
    hn_ref[...] = y.astype(BF16)
    hn = hn_ref[...]

    acc = _dot(hn, w_ref[:, _P_QK:_P_QK + QK_W])
    ss = _dot((acc * acc).astype(BF16), e_ref[...])
    r = lax.rsqrt(ss * (1.0 / ATT_HEAD_DIM) + EPS)
    r_hi = r.astype(BF16)
    r_lo = (r - r_hi.astype(F32)).astype(BF16)
    rb = _dot(jnp.concatenate([r_hi, r_lo], axis=1), et_ref[...])
    qn = acc * rb * qkw_ref[...]
    cos = cos_ref[...]
    sin = sin_ref[...]
    lane = lax.broadcasted_iota(jnp.int32, (1, LANES), 1)
    keep = ((lane // 32) % 2) == 0
    for gi in range(QK_W // LANES):
        xs = qn[:, gi * LANES:(gi + 1) * LANES]
        o = xs * cos + pltpu.roll(xs, LANES // 2, 1) * sin
        if gi < ATT_Q_W // LANES:
            q_ref[0, :, gi * LANES:(gi + 1) * LANES] = o.astype(BF16)
        else:
            c0 = 4 * (gi - ATT_Q_W // LANES) * LANES
            o32 = pltpu.roll(o, 32, 1)
            o96 = pltpu.roll(o, 96, 1)
            k_ref[0, :, c0:c0 + LANES] = jnp.where(keep, o, 0.0).astype(BF16)
            k_ref[0, :, c0 + LANES:c0 + 2 * LANES] = jnp.where(keep, 0.0, o32).astype(BF16)
            k_ref[0, :, c0 + 2 * LANES:c0 + 3 * LANES] = jnp.where(keep, o96, 0.0).astype(BF16)
            k_ref[0, :, c0 + 3 * LANES:c0 + 4 * LANES] = jnp.where(keep, 0.0, o).astype(BF16)

    mk_ref[0] = _dot(hn, w_ref[:, _P_MK:_P_MK + ML_QK_W]).astype(BF16)

    g16 = _dot(hn, w_ref[:, _P_MG:_P_MG + LANES])[:, :ML_GATE_W] + gb_ref[...]
    g_ref[0] = g16
    tri_lo = tril_ref[...]
    tri_up = triu_ref[...]
    parts = _split3(jax.nn.log_sigmoid(g16))
    fwd_col = lax.broadcasted_iota(jnp.int32, (1, ML_GATE_W), 1) < ML_GATE_W // 2
    bc_ref[0] = jnp.where(fwd_col, sum(_dot(tri_lo, p) for p in parts), sum(_dot(tri_up, p) for p in parts))
    gt16 = _dot_nt(wgt_ref[...], hn) + gbt_ref[...]
    parts = _split3(jax.nn.log_sigmoid(gt16))
    fwd_row = lax.broadcasted_iota(jnp.int32, (ML_GATE_W, 1), 0) < ML_GATE_W // 2
    br_ref[0] = jnp.where(fwd_row, sum(_dot(p, tri_up) for p in parts), sum(_dot(p, tri_lo) for p in parts))

    vt_ref[0] = _dot_nt(wt_ref[_R_V:_R_V + ATT_KV_W, :], hn).astype(BF16)
    mqt_ref[0] = _dot_nt(wt_ref[_R_MQ:_R_MQ + ML_QK_W, :], hn).astype(BF16)
    mvt_ref[0] = _dot_nt(wt_ref[_R_MV:_R_MV + ML_V_W, :], hn).astype(BF16)
    sot_ref[0] = jax.nn.sigmoid(_dot_nt(wt_ref[_R_MO:_R_MO + ML_V_W, :], hn)).astype(BF16)
    sgat_ref[0] = jax.nn.sigmoid(_dot_nt(wt_ref[_R_GA:_R_GA + D, :], hn)).astype(BF16)
    sgmt_ref[0] = jax.nn.sigmoid(_dot_nt(wt_ref[_R_GM:_R_GM + D, :], hn)).astype(BF16)


def _block_tri(n, block, lower):
    r = np.arange(n)[:, None]
    c = np.arange(n)[None, :]
    same = (r // block) == (c // block)
    return jnp.asarray(same & ((c <= r) if lower else (c >= r)), BF16)


def _inproj_call(ctx, x, modsel, n1w, w_p, w_t, wgt, e_mat, et_mat, qkw, cos_t, sin_t, gb, gbt, tm):
    B, L, _ = x.shape
    C = ctx.shape[1]
    Lt = L + C
    nct = C // tm
    nt = Lt // tm
    tril = _block_tri(tm, ML_CHUNK, True)
    triu = _block_tri(tm, ML_CHUNK, False)

    def const(shape):
        return pl.BlockSpec(shape, lambda b, i: (0,) * len(shape))

    def rows(w):
        return pl.BlockSpec((1, tm, w), lambda b, i: (b, i, 0))

    def cols(h):
        return pl.BlockSpec((1, h, tm), lambda b, i: (b, 0, i))

    in_specs = [
        pl.BlockSpec((1, tm, D), lambda b, i: (b, jnp.minimum(i, nct - 1), 0)),
        pl.BlockSpec((1, tm, D), lambda b, i: (b, jnp.maximum(i - nct, 0), 0)),
        pl.BlockSpec((1, 1, 2, D), lambda b, i: (b, jnp.where(i < nct, 0, 1), 0, 0)),
        const((1, D)),
        const(w_p.shape), const(w_t.shape), const(wgt.shape), const(e_mat.shape), const(et_mat.shape),
        const(qkw.shape),
        pl.BlockSpec((tm, LANES), lambda b, i: (i, 0)),
        pl.BlockSpec((tm, LANES), lambda b, i: (i, 0)),
        const(gb.shape), const(gbt.shape), const(tril.shape), const(triu.shape),
    ]
    out_specs = [rows(ATT_Q_W), rows(KX_W), rows(ML_QK_W), rows(ML_GATE_W), rows(ML_GATE_W),
                 cols(ATT_KV_W), cols(ML_QK_W), cols(ML_V_W), cols(ML_V_W), cols(D), cols(D), cols(ML_GATE_W)]
    out_shape = [
        jax.ShapeDtypeStruct((B, Lt, ATT_Q_W), BF16),
        jax.ShapeDtypeStruct((B, Lt, KX_W), BF16),
        jax.ShapeDtypeStruct((B, Lt, ML_QK_W), BF16),
        jax.ShapeDtypeStruct((B, Lt, ML_GATE_W), F32),
        jax.ShapeDtypeStruct((B, Lt, ML_GATE_W), F32),
        jax.ShapeDtypeStruct((B, ATT_KV_W, Lt), BF16),
        jax.ShapeDtypeStruct((B, ML_QK_W, Lt), BF16),
        jax.ShapeDtypeStruct((B, ML_V_W, Lt), BF16),
        jax.ShapeDtypeStruct((B, ML_V_W, Lt), BF16),
        jax.ShapeDtypeStruct((B, D, Lt), BF16),
        jax.ShapeDtypeStruct((B, D, Lt), BF16),
        jax.ShapeDtypeStruct((B, ML_GATE_W, Lt), F32),
    ]
    return pl.pallas_call(
        functools.partial(_inproj_kernel, n_ctx_tiles=nct),
        grid=(B, nt),
        in_specs=in_specs,
        out_specs=out_specs,
        out_shape=out_shape,
        scratch_shapes=[pltpu.VMEM((tm, D), BF16)],
        compiler_params=_cparams(("arbitrary", "arbitrary")),
        name="inproj",
    )(ctx, x, modsel, n1w, w_p, w_t, wgt, e_mat, et_mat, qkw, cos_t, sin_t, gb, gbt, tril, triu)


def _attn_kernel(sink_ref, q_ref, kc_ref, kp_ref, k0_ref, kn_ref, vc_ref, vp_ref, v0_ref, vn_ref, o_ref,
                 *, n_blocks):
    i = pl.program_id(1)
    T = ATT_BLOCK
    hd = ATT_HEAD_DIM
    s_idx = lax.broadcasted_iota(jnp.int32, (T, 2 * T), 0)
    t_idx = lax.broadcasted_iota(jnp.int32, (T, 2 * T), 1) % T
    ok_prev = (s_idx >= t_idx) & (i > 0)
    ok_next = (s_idx <= t_idx) & (i < n_blocks - 1)
    first = lax.broadcasted_iota(jnp.int32, (1, 2 * T), 1) < T

    k_all = jnp.concatenate([kp_ref[0], k0_ref[0], kn_ref[0], kc_ref[0]], axis=0)
    vt_all = jnp.concatenate([vp_ref[0], v0_ref[0], vn_ref[0], vc_ref[0]], axis=1)
    q = q_ref[0]

    def scores(n):
        kh, var = divmod(n, 2)
        q2 = jnp.concatenate([q[:, (2 * kh) * LANES:(2 * kh + 1) * LANES],
                              q[:, (2 * kh + 1) * LANES:(2 * kh + 2) * LANES]], axis=0)
        kk = k_all[:, (2 * kh + var) * LANES:(2 * kh + var + 1) * LANES]
        return _dot_nt(kk, q2)

    n_iter = 2 * ATT_KV_HEADS
    st_next = scores(0)
    for n in range(n_iter):
        kh, var = divmod(n, 2)
        st = st_next
        if n + 1 < n_iter:
            st_next = scores(n + 1)
        vt = vt_all[kh * hd:(kh + 1) * hd, :]
        st = jnp.concatenate([jnp.where(ok_prev, st[0:T], NEG_INF), st[T:2 * T],
                              jnp.where(ok_next, st[2 * T:3 * T], NEG_INF), st[3 * T:]], axis=0)
        h0 = ATT_GROUP * kh + var
        h1 = h0 + 2
        sink = jnp.where(first, sink_ref[h0], sink_ref[h1]) * LOG2E
        m = jnp.maximum(jnp.max(st, axis=0, keepdims=True), sink)
        p = jnp.exp2(st - m)
        denom = jnp.sum(p, axis=0, keepdims=True) + jnp.exp2(sink - m)
        ot = _dot(vt, p.astype(BF16)) * (1.0 / denom)
        o_ref[0, h0 * hd:(h0 + 1) * hd, :] = ot[:, 0:T].astype(BF16)
        o_ref[0, h1 * hd:(h1 + 1) * hd, :] = ot[:, T:2 * T].astype(BF16)


def _attn_call(sink, q, kx, vt, L, C):
    B = q.shape[0]
    T = ATT_BLOCK
    nb = L // T
    ncb = C // T

    def k_spec(off):
        return pl.BlockSpec((1, T, KX_W), lambda b, i: (b, ncb + jnp.clip(i + off, 0, nb - 1), 0))

    def v_spec(off):
        return pl.BlockSpec((1, ATT_KV_W, T), lambda b, i: (b, 0, ncb + jnp.clip(i + off, 0, nb - 1)))

    return pl.pallas_call(
        functools.partial(_attn_kernel, n_blocks=nb),
        grid=(B, nb),
        in_specs=[pl.BlockSpec(memory_space=pltpu.SMEM),
                  pl.BlockSpec((1, T, ATT_Q_W), lambda b, i: (b, ncb + i, 0)),
                  pl.BlockSpec((1, C, KX_W), lambda b, i: (b, 0, 0)), k_spec(-1), k_spec(0), k_spec(1),
                  pl.BlockSpec((1, ATT_KV_W, C), lambda b, i: (b, 0, 0)), v_spec(-1), v_spec(0), v_spec(1)],
        out_specs=pl.BlockSpec((1, ATT_Q_W, T), lambda b, i: (b, 0, i)),
        out_shape=jax.ShapeDtypeStruct((B, ATT_Q_W, L), BF16),
        compiler_params=_cparams(("arbitrary", "arbitrary")),
        name="attn",
    )(sink, q, kx, kx, kx, kx, vt, vt, vt, vt)


def _mlstm_kernel(qtf_ref, kf_ref, vtf_ref, gf_ref, bcf_ref, brf_ref, qtb_ref, kb_ref, vtb_ref, gb_ref, bcb_ref,
                  brb_ref, hf_ref, hb_ref, c_ref, n_ref, m_ref):
    j = pl.program_id(1)
    T = ML_CHUNK

    @pl.when(j == 0)
    def _():
        c_ref[...] = jnp.zeros_like(c_ref)
        n_ref[...] = jnp.zeros_like(n_ref)
        m_ref[...] = jnp.zeros_like(m_ref)

    row = lax.broadcasted_iota(jnp.int32, (T, T), 0)
    col = lax.broadcasted_iota(jnp.int32, (T, T), 1)
    ones8 = jnp.ones((8, T), BF16)
    top = lax.broadcasted_iota(jnp.int32, (8, ML_QK_DIM), 0) == 0

    dirs = (
        (qtf_ref, kf_ref, vtf_ref, gf_ref, bcf_ref, brf_ref, hf_ref, col >= row, T - 1),
        (qtb_ref, kb_ref, vtb_ref, gb_ref, bcb_ref, brb_ref, hb_ref, col <= row, 0),
    )
    chains = []
    for d, (qt_ref, k_ref, vt_ref, g_ref, bc_ref, br_ref, h_ref, mask, last) in enumerate(dirs):
        g = g_ref[0]
        bc = bc_ref[0]
        br = br_ref[0]
        for h in range(ML_HEADS):
            ci = d * ML_HEADS + h
            gi = d * 2 * ML_HEADS + h
            fi = gi + ML_HEADS
            qt = qt_ref[0, h * ML_QK_DIM:(h + 1) * ML_QK_DIM, :]
            k = k_ref[0, :, h * ML_QK_DIM:(h + 1) * ML_QK_DIM]
            vt = vt_ref[0, h * ML_V_DIM:(h + 1) * ML_V_DIM, :]
            u_col = g[:, gi:gi + 1] - bc[:, fi:fi + 1]
            b_row = br[fi:fi + 1, :]
            m_old = m_ref[ci, 0:1, 0:1]
            n_old = n_ref[ci]
            ct_old = c_ref[ci]
            st = _dot(k, qt)
            n_hi = n_old.astype(BF16)
            n_lo = (n_old - n_hi.astype(F32)).astype(BF16)
            qn2 = _dot(jnp.where(top, n_hi, n_lo), qt)
            m_last = jnp.maximum(jnp.max(u_col, axis=0, keepdims=True), m_old)
            decay = jnp.exp(m_old - m_last)
            kw = (k.astype(F32) * jnp.exp(u_col - m_last)).astype(BF16)
            c_new = decay * ct_old + _dot(vt, kw)
            n_new = decay * n_old + _dot(ones8, kw)
            m_new = b_row[:, last:last + 1] + m_last
            chains.append((ci, h, h_ref, mask, qt, vt, u_col, b_row, m_old, ct_old, st, qn2, c_new, n_new, m_new))

    for (ci, h, h_ref, mask, qt, vt, u_col, b_row, m_old, ct_old, st, qn2, c_new, n_new, m_new) in chains:
        umat = jnp.where(mask, u_col, -jnp.inf)
        m_row = jnp.maximum(jnp.max(umat, axis=0, keepdims=True), m_old)
        pt = st * jnp.exp(umat - m_row)
        w_int = jnp.exp(m_old - m_row)
        e_row = jnp.exp(-(b_row + m_row))
        nq = jnp.sum(pt, axis=0, keepdims=True) + w_int * (qn2[0:1, :] + qn2[1:2, :])
        den = jnp.maximum(jnp.abs(nq), e_row)
        lhs = jnp.concatenate([vt, ct_old.astype(BF16)], axis=1)
        rhs = jnp.concatenate([pt.astype(BF16), (qt.astype(F32) * w_int).astype(BF16)], axis=0)
        h_ref[0, h * ML_V_DIM:(h + 1) * ML_V_DIM, :] = (_dot(lhs, rhs) * (1.0 / den)).astype(BF16)

    for (ci, h, h_ref, mask, qt, vt, u_col, b_row, m_old, ct_old, st, qn2, c_new, n_new, m_new) in chains:
        c_ref[ci] = c_new
        n_ref[ci] = n_new
        m_ref[ci] = jnp.broadcast_to(m_new, (8, LANES))


def _mlstm_call(mqt, mk, mvt, g, bc, br, L, C):
    B = mk.shape[0]
    T = ML_CHUNK
    ncc = C // T
    ncl = L // T
    nc = ncc + ncl

    def fwd(j):
        return j

    def bwd(j):
        return jnp.where(j < ncc, ncc - 1 - j, nc - 1 - (j - ncc))

    def specs(order):
        return [
            pl.BlockSpec((1, ML_QK_W, T), lambda b, j: (b, 0, order(j))),
            pl.BlockSpec((1, T, ML_QK_W), lambda b, j: (b, order(j), 0)),
            pl.BlockSpec((1, ML_V_W, T), lambda b, j: (b, 0, order(j))),
            pl.BlockSpec((1, T, ML_GATE_W), lambda b, j: (b, order(j), 0)),
            pl.BlockSpec((1, T, ML_GATE_W), lambda b, j: (b, order(j), 0)),
            pl.BlockSpec((1, ML_GATE_W, T), lambda b, j: (b, 0, order(j))),
        ]

    out_specs = [
        pl.BlockSpec((1, ML_V_W, T), lambda b, j: (b, 0, jnp.maximum(j - ncc, 0))),
        pl.BlockSpec((1, ML_V_W, T), lambda b, j: (b, 0, ncl - 1 - jnp.maximum(j - ncc, 0))),
    ]
    n_chain = 2 * ML_HEADS
    return pl.pallas_call(
        _mlstm_kernel,
        grid=(B, nc),
        in_specs=specs(fwd) + specs(bwd),
        out_specs=out_specs,
        out_shape=[jax.ShapeDtypeStruct((B, ML_V_W, L), BF16)] * 2,
        scratch_shapes=[pltpu.VMEM((n_chain, ML_V_DIM, ML_QK_DIM), F32),
                        pltpu.VMEM((n_chain, 8, ML_QK_DIM), F32),
                        pltpu.VMEM((n_chain, 8, LANES), F32)],
        compiler_params=_cparams(("arbitrary", "arbitrary")),
        name="mlstm",
    )(mqt, mk, mvt, g, bc, br, mqt, mk, mvt, g, bc, br)


def _merge_kernel(att_ref, hf_ref, hb_ref, so_ref, sga_ref, sgm_ref, x_ref, mod_ref, mlw_ref, n2w_ref,
                  wat_ref, wmt_ref, wo_ref, xmid_ref, h2_ref):
    ht = hf_ref[0].astype(F32) + hb_ref[0].astype(F32)
    parts = []
    for h in range(ML_HEADS):
        seg = ht[h * ML_V_DIM:(h + 1) * ML_V_DIM, :]
        ms = jnp.mean(seg * seg, axis=0, keepdims=True)
        parts.append(seg * lax.rsqrt(ms + EPS))
    ml = (jnp.concatenate(parts, axis=0) * mlw_ref[...] * so_ref[0].astype(F32)).astype(BF16)
    ya = _dot(wat_ref[...], att_ref[0])
    ym = _dot(wmt_ref[...], ml)
    y = (sga_ref[0].astype(F32) * ya + sgm_ref[0].astype(F32) * ym).astype(BF16)
    y2 = _dot_tn(y, wo_ref[...])
    g1 = mod_ref[0, 0:1, :]
    sh2 = mod_ref[0, 1:2, :]
    sc2 = mod_ref[0, 2:3, :]
    xm = x_ref[0] + g1 * y2
    xmid_ref[0] = xm
    ms = jnp.mean(xm * xm, axis=-1, keepdims=True)
    h2 = xm * lax.rsqrt(ms + EPS) * n2w_ref[...]
    h2_ref[0] = (h2 * sc2 + sh2).astype(BF16)


def _merge_call(att_t, hf_t, hb_t, so_t, sga_t, sgm_t, x, mod3, mlw_b, n2w, wat, wmt, wo, C, tm):
    B, L, _ = x.shape
    nt = L // tm
    off = C // tm

    def lat(w):
        return pl.BlockSpec((1, tm, w), lambda b, i: (b, i, 0))

    def lat_t(h):
        return pl.BlockSpec((1, h, tm), lambda b, i: (b, 0, i))

    def cat_t(h):
        return pl.BlockSpec((1, h, tm), lambda b, i: (b, 0, i + off))

    def const(shape):
        return pl.BlockSpec(shape, lambda b, i: (0,) * len(shape))

    return pl.pallas_call(
        _merge_kernel,
        grid=(B, nt),
        in_specs=[lat_t(D), lat_t(D), lat_t(D), cat_t(D), cat_t(D), cat_t(D), lat(D),
                  pl.BlockSpec((1, 3, D), lambda b, i: (b, 0, 0)),
                  const((D, tm)), const((1, D)), const((D, D)), const((D, D)), const((D, D))],
        out_specs=[lat(D), lat(D)],
        out_shape=[jax.ShapeDtypeStruct((B, L, D), F32), jax.ShapeDtypeStruct((B, L, D), BF16)],
        compiler_params=_cparams(("arbitrary", "arbitrary")),
        name="merge",
    )(att_t, hf_t, hb_t, so_t, sga_t, sgm_t, x, mod3, mlw_b, n2w, wat, wmt, wo)


HALO = 16


def _ffn_kernel(h_ref, hp_ref, hn_ref, xmid_ref, mod_ref, wup_ref, cw_ref, cb_ref, wdn_ref, o_ref,
                act_ref, *, n_tiles, tn, dn):
    s = pl.program_id(0)
    i = jnp.minimum(s, pl.num_programs(0) - 2) % n_tiles
    cur = s % 2
    tm = h_ref.shape[1]

    @pl.when(s == 0)
    def _():
        act_ref[1] = jnp.zeros(act_ref.shape[1:], BF16)

    h = h_ref[0]
    prev_row = jnp.where(i > 0, hp_ref[0].astype(F32)[HALO - 1:HALO, :], 0.0)
    next_row = jnp.where(i < n_tiles - 1, hn_ref[0].astype(F32)[0:1, :], 0.0)
    top = lax.broadcasted_iota(jnp.int32, (16, D), 0) < 8
    edge = jnp.where(top, prev_row, next_row).astype(BF16)
    row8 = lax.broadcasted_iota(jnp.int32, (8, tn), 0)
    h_ext = jnp.concatenate([h, edge], axis=0)

    def up(c0):
        u_ext = _dot(h_ext, wup_ref[:, c0:c0 + tn])
        return u_ext[:tm], u_ext[tm:]

    def conv(u, ue, c0):
        below = pltpu.roll(u, 1, 0)
        above = pltpu.roll(u, tm - 1, 0)
        below = jnp.concatenate([jnp.where(row8 == 0, ue[0:8], below[0:8]), below[8:]], axis=0)
        above = jnp.concatenate([above[:tm - 8], jnp.where(row8 == 7, ue[8:16], above[tm - 8:])], axis=0)
        cw = cw_ref[:, c0:c0 + tn]
        return cb_ref[:, c0:c0 + tn] + below * cw[0:1] + u * cw[1:2] + above * cw[2:3]

    n_chunks = D_FF // tn
    n_dn = D // dn
    act_prev = act_ref[1 - cur]

    def down(k):
        cols = slice(k * dn, (k + 1) * dn)
        o_ref[0, :, cols] = xmid_ref[0, :, cols] + mod_ref[0, :, cols] * _dot(act_prev, wdn_ref[:, cols])

    nxt = (up(0), up(D_FF))
    done = 0
    for c in range(n_chunks):
        (ua, uae), (ug, uge) = nxt
        if c + 1 < n_chunks:
            nxt = (up((c + 1) * tn), up(D_FF + (c + 1) * tn))
        while done * n_chunks < (c + 1) * n_dn:
            down(done)
            done += 1
        a = conv(ua, uae, c * tn)
        gte = conv(ug, uge, D_FF + c * tn)
        act_ref[cur, :, c * tn:(c + 1) * tn] = (gte * jax.nn.sigmoid(gte) * a).astype(BF16)


def _ffn_call(h2, xmid, g2, wup, cw, cb, wdn, tm, tn, dn):
    B, L, _ = xmid.shape
    nt = L // tm
    n_all = B * nt
    hb = tm // HALO
    nhb = L // HALO

    def tile_in(s):
        t = jnp.minimum(s, n_all - 1)
        return t // nt, t % nt

    def tile_out(s):
        t = jnp.maximum(s - 1, 0)
        return t // nt, t % nt

    def in_spec():
        return pl.BlockSpec((1, tm, D), lambda s: (*tile_in(s), 0))

    def out_spec():
        return pl.BlockSpec((1, tm, D), lambda s: (*tile_out(s), 0))

    def prev_halo(s):
        b, i = tile_in(s)
        return b, jnp.maximum(i * hb - 1, 0), 0

    def next_halo(s):
        b, i = tile_in(s)
        return b, jnp.minimum((i + 1) * hb, nhb - 1), 0

    def const(shape):
        return pl.BlockSpec(shape, lambda s: (0,) * len(shape))

    return pl.pallas_call(
        functools.partial(_ffn_kernel, n_tiles=nt, tn=tn, dn=dn),
        grid=(n_all + 1,),
        in_specs=[in_spec(),
                  pl.BlockSpec((1, HALO, D), prev_halo),
                  pl.BlockSpec((1, HALO, D), next_halo),
                  out_spec(),
                  pl.BlockSpec((1, 1, D), lambda s: (tile_out(s)[0], 0, 0)),
                  const(wup.shape), const(cw.shape), const(cb.shape), const(wdn.shape)],
        out_specs=out_spec(),
        out_shape=jax.ShapeDtypeStruct((B, L, D), F32),
        scratch_shapes=[pltpu.VMEM((2, tm, D_FF), BF16)],
        compiler_params=_cparams(("arbitrary",)),
        name="ffn",
    )(h2, h2, h2, xmid, g2, wup, cw, cb, wdn)


def _pair_perm(n_heads):
    half = ATT_HEAD_DIM // 2
    idx = []
    for p in range(n_heads // 2):
        for sub in range(4):
            head = 2 * p + (sub % 2)
            d0 = (sub // 2) * half
            idx.extend(head * ATT_HEAD_DIM + d0 + e for e in range(half))
    return np.asarray(idx, np.int32)


def _rope_tables(L, C):
    rows = L // GRID_W
    row = jnp.repeat(jnp.arange(rows, dtype=F32), GRID_W)
    col = jnp.tile(jnp.arange(GRID_W, dtype=F32), rows)
    n_freq = ATT_HEAD_DIM // 4
    inv_freq = ROPE_BASE ** (-jnp.arange(n_freq, dtype=F32) / n_freq)
    ang = jnp.concatenate([row[:, None] * inv_freq, col[:, None] * inv_freq], axis=-1)
    cos = jnp.tile(jnp.cos(ang), (1, 4))
    sin = jnp.tile(jnp.sin(ang), (1, 4))
    sign = jnp.where(jnp.arange(LANES) < LANES // 2, -1.0, 1.0).astype(F32)
    cos = jnp.concatenate([jnp.ones((C, LANES), F32), cos], axis=0)
    sin = jnp.concatenate([jnp.zeros((C, LANES), F32), sin * sign], axis=0)
    return cos, sin


def kernel(x, c, ctx, c_ctx, w_mod, b_mod, norm1_w, w_in, q_norm_w, k_norm_w, attn_sink, ml_gate_b, ml_norm_w,
           w_branch_att, w_branch_ml, w_out, norm2_w, w_up, conv_w, conv_b, w_down):
    B, L, _ = x.shape
    C = ctx.shape[1]
    assert L % 512 == 0 and C % 256 == 0 and L % GRID_W == 0
    l = 0
    tm_merge = 256

    n_rows = -(-(B + 1) // 8) * 8
    cc = jnp.concatenate([c, c_ctx[None, :], jnp.zeros((n_rows - B - 1, D), F32)], axis=0)
    mod = _mod_call(cc, w_mod[l], b_mod[l][None, :])
    sh1, sc1, g1, sh2, sc2, g2 = [mod[:, k * D:(k + 1) * D] for k in range(6)]
    lat_mod = jnp.stack([sh1[:B], 1.0 + sc1[:B]], axis=1)
    ctx_mod = jnp.broadcast_to(jnp.stack([sh1[B], 1.0 + sc1[B]], axis=0)[None], (B, 2, D))
    modsel = jnp.stack([ctx_mod, lat_mod], axis=1)
    mod3 = jnp.stack([g1[:B], sh2[:B], 1.0 + sc2[:B]], axis=1)
    g2b = g2[:B][:, None, :]

    w = w_in[l]
    qperm = _pair_perm(ATT_HEADS)
    kperm = _pair_perm(ATT_KV_HEADS)
    w_q = w[:, _O_AQ:_O_AQ + ATT_Q_W][:, qperm]
    w_k = w[:, _O_AK:_O_AK + ATT_KV_W][:, kperm]
    w_g = jnp.pad(w[:, _O_MG:_O_MG + ML_GATE_W], ((0, 0), (0, LANES - ML_GATE_W)))
    w_mk = w[:, _O_MK:_O_MK + ML_QK_W] * (ML_QK_DIM ** -0.5)
    w_p = jnp.concatenate([w_q, w_k, w_mk, w_g], axis=1).astype(BF16)
    w_t = jnp.concatenate([w[:, _O_AV:_O_AV + ATT_KV_W], w[:, _O_MQ:_O_MQ + ML_QK_W], w[:, _O_MV:_O_MV + ML_V_W],
                           w[:, _O_MO:_O_MO + ML_V_W], w[:, _O_GA:_O_GA + D], w[:, _O_GM:_O_GM + D]],
                          axis=1).T.astype(BF16)
    wgt = w[:, _O_MG:_O_MG + ML_GATE_W].T.astype(BF16)

    head_of_col = np.concatenate([qperm // ATT_HEAD_DIM, ATT_HEADS + kperm // ATT_HEAD_DIM])
    e_np = (head_of_col[:, None] == np.arange(LANES)[None, :]).astype(np.float32)
    e_mat = jnp.asarray(e_np, BF16)
    et_mat = jnp.asarray(np.concatenate([e_np.T, e_np.T], axis=0), BF16)
    qkw = jnp.concatenate([q_norm_w[l][qperm % ATT_HEAD_DIM] * (ATT_SCALE * LOG2E),
                           k_norm_w[l][kperm % ATT_HEAD_DIM]])[None, :]
    cos_t, sin_t = _rope_tables(L, C)
    gb = ml_gate_b[l].reshape(1, ML_GATE_W)
    gbt = ml_gate_b[l].reshape(ML_GATE_W, 1)

    q, kx, mk, g, bc, vt, mqt, mvt, sot, sgat, sgmt, br = _inproj_call(
        ctx, x, modsel, norm1_w[l][None, :], w_p, w_t, wgt, e_mat, et_mat, qkw, cos_t, sin_t, gb, gbt, tm=256)

    att_t = _attn_call(attn_sink[l], q, kx, vt, L, C)
    hf_t, hb_t = _mlstm_call(mqt, mk, mvt, g, bc, br, L, C)

    mlw_b = jnp.broadcast_to(ml_norm_w[l][:, None], (ML_V_W, tm_merge))
    xmid, h2 = _merge_call(att_t, hf_t, hb_t, sot, sgat, sgmt, x, mod3, mlw_b, norm2_w[l][None, :],
                           w_branch_att[l].T.astype(BF16), w_branch_ml[l].T.astype(BF16), w_out[l].astype(BF16),
                           C, tm=tm_merge)
    out = _ffn_call(h2, xmid, g2b, w_up[l].astype(BF16), conv_w[l], conv_b[l][None, :],
                    w_down[l].astype(BF16), tm=512, tn=256, dn=256)
    return out
```

```python
import functools

import jax
import jax.numpy as jnp
import numpy as np
from jax import lax
from jax.experimental import pallas as pl
from jax.experimental.pallas import tpu as pltpu

D = 1024
GRID_W = 64
ATT_HEADS = 16
ATT_KV_HEADS = 4
ATT_HEAD_DIM = 64
ATT_GROUP = ATT_HEADS // ATT_KV_HEADS
ATT_BLOCK = 128
WINDOW = 128
ROPE_BASE = 10000.0
ATT_SCALE = ATT_HEAD_DIM ** -0.5
LOG2E = 1.4426950408889634
ML_HEADS = 4
ML_QK_DIM = 128
ML_V_DIM = 256
ML_CHUNK = 128
D_FF = 2816
EPS = 1e-6
NEG_INF = -1e30

ATT_Q_W = ATT_HEADS * ATT_HEAD_DIM
ATT_KV_W = ATT_KV_HEADS * ATT_HEAD_DIM
ML_QK_W = ML_HEADS * ML_QK_DIM
ML_V_W = ML_HEADS * ML_V_DIM
ML_GATE_W = 2 * 2 * ML_HEADS

LANES = 128
KX_W = ATT_KV_HEADS * 2 * LANES
VMEM_LIMIT = 56 * 1024 * 1024

BF16 = jnp.bfloat16
F32 = jnp.float32

_O_AQ = 0
_O_AK = _O_AQ + ATT_Q_W
_O_AV = _O_AK + ATT_KV_W
_O_MQ = _O_AV + ATT_KV_W
_O_MK = _O_MQ + ML_QK_W
_O_MV = _O_MK + ML_QK_W
_O_MO = _O_MV + ML_V_W
_O_MG = _O_MO + ML_V_W
_O_GA = _O_MG + ML_GATE_W
_O_GM = _O_GA + D

QK_W = ATT_Q_W + ATT_KV_W
_P_QK = 0
_P_MK = _P_QK + QK_W
_P_MG = _P_MK + ML_QK_W
_P_END = _P_MG + LANES
_R_V = 0
_R_MQ = _R_V + ATT_KV_W
_R_MV = _R_MQ + ML_QK_W
_R_MO = _R_MV + ML_V_W
_R_GA = _R_MO + ML_V_W
_R_GM = _R_GA + D
_R_END = _R_GM + D


def _dot(a, b):
    return jnp.dot(a, b, preferred_element_type=F32)


def _dot_nt(a, b):
    return lax.dot_general(a, b, (((1,), (1,)), ((), ())), preferred_element_type=F32)


def _dot_tn(a, b):
    return lax.dot_general(a, b, (((0,), (0,)), ((), ())), preferred_element_type=F32)


def _cparams(sem):
    return pltpu.CompilerParams(dimension_semantics=sem, vmem_limit_bytes=VMEM_LIMIT)


def _mod_kernel(c_ref, w_ref, b_ref, o_ref):
    c = c_ref[...]
    a = c * jax.nn.sigmoid(c)
    o_ref[...] = jnp.dot(a, w_ref[...], preferred_element_type=F32,
                         precision=lax.Precision.HIGHEST) + b_ref[...]


def _mod_call(cc, w_mod, b_mod):
    rows = cc.shape[0]
    n = w_mod.shape[1]
    tn = 1536
    return pl.pallas_call(
        _mod_kernel,
        grid=(n // tn,),
        in_specs=[pl.BlockSpec((rows, D), lambda j: (0, 0)),
                  pl.BlockSpec((D, tn), lambda j: (0, j)),
                  pl.BlockSpec((1, tn), lambda j: (0, j))],
        out_specs=pl.BlockSpec((rows, tn), lambda j: (0, j)),
        out_shape=jax.ShapeDtypeStruct((rows, n), F32),
        compiler_params=_cparams(("arbitrary",)),
        name="mod",
    )(cc, w_mod, b_mod)


def _split2(x):
    x1 = x.astype(BF16)
    x2 = (x - x1.astype(F32)).astype(BF16)
    return x1, x2


def _norm_modulate(x, n1w_ref, mod_ref):
    ms = jnp.mean(x * x, axis=-1, keepdims=True)
    y = x * lax.rsqrt(ms + EPS) * n1w_ref[...]
    return (y * mod_ref[0, 1:2, :] + mod_ref[0, 0:1, :]).astype(BF16)


def _head_rms_scale(ss):
    r = lax.rsqrt(ss * (1.0 / ATT_HEAD_DIM) + EPS)
    r_hi = r.astype(BF16)
    r_lo = (r - r_hi.astype(F32)).astype(BF16)
    return jnp.concatenate([r_hi, r_lo], axis=1)


def _store_k_variants(k_ref, pair, o):
    lane = lax.broadcasted_iota(jnp.int32, (1, LANES), 1)
    keep = ((lane // 32) % 2) == 0
    c0 = 4 * pair * LANES
    k_ref[0, :, c0:c0 + LANES] = jnp.where(keep, o, 0.0).astype(BF16)
    k_ref[0, :, c0 + LANES:c0 + 2 * LANES] = jnp.where(keep, 0.0, pltpu.roll(o, 32, 1)).astype(BF16)
    k_ref[0, :, c0 + 2 * LANES:c0 + 3 * LANES] = jnp.where(keep, pltpu.roll(o, 96, 1), 0.0).astype(BF16)
    k_ref[0, :, c0 + 3 * LANES:c0 + 4 * LANES] = jnp.where(keep, 0.0, o).astype(BF16)


def _cum_gates_cols(g16, tri_lo, tri_up):
    parts = _split2(jax.nn.log_sigmoid(g16))
    fwd_col = lax.broadcasted_iota(jnp.int32, (1, ML_GATE_W), 1) < ML_GATE_W // 2
    return jnp.where(fwd_col, sum(_dot(tri_lo, p) for p in parts), sum(_dot(tri_up, p) for p in parts))


def _cum_gates_rows(gt16, tri_lo, tri_up):
    parts = _split2(jax.nn.log_sigmoid(gt16))
    fwd_row = lax.broadcasted_iota(jnp.int32, (ML_GATE_W, 1), 0) < ML_GATE_W // 2
    return jnp.where(fwd_row, sum(_dot(p, tri_up) for p in parts), sum(_dot(p, tri_lo) for p in parts))


def _inproj_kernel(x_ref, mod_ref, n1w_ref, w_ref, wt_ref, wgt_ref, e_ref, et_ref, qkw_ref,
                   cos_ref, sin_ref, gb_ref, gbt_ref, tril_ref, triu_ref,
                   q_ref, k_ref, mk_ref, g_ref, bc_ref, vt_ref, mqt_ref, mvt_ref, sot_ref, sgat_ref, sgmt_ref, br_ref,
                   hn_ref):
    hn_ref[...] = _norm_modulate(x_ref[0], n1w_ref, mod_ref)
    hn = hn_ref[...]

    acc = _dot(hn, w_ref[:, _P_QK:_P_QK + QK_W])
    g16 = _dot(hn, w_ref[:, _P_MG:_P_MG + LANES])[:, :ML_GATE_W] + gb_ref[...]
    gt16 = _dot_nt(wgt_ref[...], hn) + gbt_ref[...]
    g_ref[0] = g16
    mk_ref[0] = _dot(hn, w_ref[:, _P_MK:_P_MK + ML_QK_W]).astype(BF16)
    ss = _dot((acc * acc).astype(BF16), e_ref[...])
    vt_ref[0] = _dot_nt(wt_ref[_R_V:_R_V + ATT_KV_W, :], hn).astype(BF16)
    mqt_ref[0] = _dot_nt(wt_ref[_R_MQ:_R_MQ + ML_QK_W, :], hn).astype(BF16)
    rb = _dot(_head_rms_scale(ss), et_ref[...])
    mvt_ref[0] = _dot_nt(wt_ref[_R_MV:_R_MV + ML_V_W, :], hn).astype(BF16)
    bc_ref[0] = _cum_gates_cols(g16, tril_ref[...], triu_ref[...])

    qn = acc * rb * qkw_ref[...]
    cos = cos_ref[...]
    sin = sin_ref[...]
    for gi in range(QK_W // LANES):
        xs = qn[:, gi * LANES:(gi + 1) * LANES]
        o = xs * cos + pltpu.roll(xs, LANES // 2, 1) * sin
        if gi < ATT_Q_W // LANES:
            q_ref[0, :, gi * LANES:(gi + 1) * LANES] = o.astype(BF16)
        else:
            _store_k_variants(k_ref, gi - ATT_Q_W // LANES, o)

    sot_ref[0] = jax.nn.sigmoid(_dot_nt(wt_ref[_R_MO:_R_MO + ML_V_W, :], hn)).astype(BF16)
    br_ref[0] = _cum_gates_rows(gt16, tril_ref[...], triu_ref[...])
    sgat_ref[0] = jax.nn.sigmoid(_dot_nt(wt_ref[_R_GA:_R_GA + D, :], hn)).astype(BF16)
    sgmt_ref[0] = jax.nn.sigmoid(_dot_nt(wt_ref[_R_GM:_R_GM + D, :], hn)).astype(BF16)


def _ctxproj_kernel(x_ref, mod_ref, n1w_ref, w_ref, wt_ref, wgt_ref, e_ref, et_ref, qkw_ref,
                    gb_ref, gbt_ref, tril_ref, triu_ref,
                    k_ref, mk_ref, g_ref, bc_ref, vt_ref, mvt_ref, br_ref):
    hn = _norm_modulate(x_ref[0], n1w_ref, mod_ref)
    acc = _dot(hn, w_ref[:, _P_QK + ATT_Q_W:_P_QK + QK_W])
    g16 = _dot(hn, w_ref[:, _P_MG:_P_MG + LANES])[:, :ML_GATE_W] + gb_ref[...]
    gt16 = _dot_nt(wgt_ref[...], hn) + gbt_ref[...]
    g_ref[0] = g16
    mk_ref[0] = _dot(hn, w_ref[:, _P_MK:_P_MK + ML_QK_W]).astype(BF16)
    ss = _dot((acc * acc).astype(BF16), e_ref[ATT_Q_W:QK_W, :])
    vt_ref[0] = _dot_nt(wt_ref[_R_V:_R_V + ATT_KV_W, :], hn).astype(BF16)
    rb = _dot(_head_rms_scale(ss), et_ref[:, ATT_Q_W:QK_W])
    mvt_ref[0] = _dot_nt(wt_ref[_R_MV:_R_MV + ML_V_W, :], hn).astype(BF16)
    bc_ref[0] = _cum_gates_cols(g16, tril_ref[...], triu_ref[...])
    kn = acc * rb * qkw_ref[:, ATT_Q_W:QK_W]
    for pair in range(ATT_KV_W // LANES):
        _store_k_variants(k_ref, pair, kn[:, pair * LANES:(pair + 1) * LANES])
    br_ref[0] = _cum_gates_rows(gt16, tril_ref[...], triu_ref[...])


def _block_tri(n, block, lower):
    r = np.arange(n)[:, None]
    c = np.arange(n)[None, :]
    same = (r // block) == (c // block)
    return jnp.asarray(same & ((c <= r) if lower else (c >= r)), BF16)


def _const_spec(shape):
    return pl.BlockSpec(shape, lambda b, i: (0,) * len(shape))


def _inproj_call(x, lat_mod, n1w, w_p, w_t, wgt, e_mat, et_mat, qkw, cos_t, sin_t, gb, gbt, tm):
    B, L, _ = x.shape
    tril = _block_tri(tm, ML_CHUNK, True)
    triu = _block_tri(tm, ML_CHUNK, False)

    def rows(w):
        return pl.BlockSpec((1, tm, w), lambda b, i: (b, i, 0))

    def cols(h):
        return pl.BlockSpec((1, h, tm), lambda b, i: (b, 0, i))

    consts = [n1w, w_p, w_t, wgt, e_mat, et_mat, qkw]
    tail = [gb, gbt, tril, triu]
    in_specs = ([rows(D), pl.BlockSpec((1, 2, D), lambda b, i: (b, 0, 0))]
                + [_const_spec(a.shape) for a in consts]
                + [pl.BlockSpec((tm, LANES), lambda b, i: (i, 0))] * 2
                + [_const_spec(a.shape) for a in tail])
    out_specs = [rows(ATT_Q_W), rows(KX_W), rows(ML_QK_W), rows(ML_GATE_W), rows(ML_GATE_W),
                 cols(ATT_KV_W), cols(ML_QK_W), cols(ML_V_W), cols(ML_V_W), cols(D), cols(D), cols(ML_GATE_W)]
    out_shape = [
        jax.ShapeDtypeStruct((B, L, ATT_Q_W), BF16),
        jax.ShapeDtypeStruct((B, L, KX_W), BF16),
        jax.ShapeDtypeStruct((B, L, ML_QK_W), BF16),
        jax.ShapeDtypeStruct((B, L, ML_GATE_W), F32),
        jax.ShapeDtypeStruct((B, L, ML_GATE_W), F32),
        jax.ShapeDtypeStruct((B, ATT_KV_W, L), BF16),
        jax.ShapeDtypeStruct((B, ML_QK_W, L), BF16),
        jax.ShapeDtypeStruct((B, ML_V_W, L), BF16),
        jax.ShapeDtypeStruct((B, ML_V_W, L), BF16),
        jax.ShapeDtypeStruct((B, D, L), BF16),
        jax.ShapeDtypeStruct((B, D, L), BF16),
        jax.ShapeDtypeStruct((B, ML_GATE_W, L), F32),
    ]
    return pl.pallas_call(
        _inproj_kernel,
        grid=(B, L // tm),
        in_specs=in_specs,
        out_specs=out_specs,
        out_shape=out_shape,
        scratch_shapes=[pltpu.VMEM((tm, D), BF16)],
        compiler_params=_cparams(("arbitrary", "arbitrary")),
        name="inproj",
    )(x, lat_mod, *consts, cos_t, sin_t, *tail)


def _ctxproj_call(ctx, ctx_mod, n1w, w_p, w_t, wgt, e_mat, et_mat, qkw, gb, gbt, tm):
    B, C, _ = ctx.shape
    tril = _block_tri(tm, ML_CHUNK, True)
    triu = _block_tri(tm, ML_CHUNK, False)

    def rows(w):
        return pl.BlockSpec((1, tm, w), lambda b, i: (b, i, 0))

    def cols(h):
        return pl.BlockSpec((1, h, tm), lambda b, i: (b, 0, i))

    consts = [ctx_mod, n1w, w_p, w_t, wgt, e_mat, et_mat, qkw, gb, gbt, tril, triu]
    out_specs = [rows(KX_W), rows(ML_QK_W), rows(ML_GATE_W), rows(ML_GATE_W),
                 cols(ATT_KV_W), cols(ML_V_W), cols(ML_GATE_W)]
    out_shape = [
        jax.ShapeDtypeStruct((B, C, KX_W), BF16),
        jax.ShapeDtypeStruct((B, C, ML_QK_W), BF16),
        jax.ShapeDtypeStruct((B, C, ML_GATE_W), F32),
        jax.ShapeDtypeStruct((B, C, ML_GATE_W), F32),
        jax.ShapeDtypeStruct((B, ATT_KV_W, C), BF16),
        jax.ShapeDtypeStruct((B, ML_V_W, C), BF16),
        jax.ShapeDtypeStruct((B, ML_GATE_W, C), F32),
    ]
    return pl.pallas_call(
        _ctxproj_kernel,
        grid=(B, C // tm),
        in_specs=[rows(D)] + [_const_spec(a.shape) for a in consts],
        out_specs=out_specs,
        out_shape=out_shape,
        compiler_params=_cparams(("arbitrary", "arbitrary")),
        name="ctxproj",
    )(ctx, *consts)


def _attn_kernel(sink_ref, q_ref, kc_ref, kp_ref, k0_ref, kn_ref, vc_ref, vp_ref, v0_ref, vn_ref, o_ref,
                 *, n_blocks):
    i = pl.program_id(1)
    T = ATT_BLOCK
    hd = ATT_HEAD_DIM
    s_idx = lax.broadcasted_iota(jnp.int32, (T, 2 * T), 0)
    t_idx = lax.broadcasted_iota(jnp.int32, (T, 2 * T), 1) % T
    ok_prev = (s_idx >= t_idx) & (i > 0)
    ok_next = (s_idx <= t_idx) & (i < n_blocks - 1)
    first = lax.broadcasted_iota(jnp.int32, (1, 2 * T), 1) < T

    k_all = jnp.concatenate([kp_ref[0], k0_ref[0], kn_ref[0], kc_ref[0]], axis=0)
    vt_all = jnp.concatenate([vp_ref[0], v0_ref[0], vn_ref[0], vc_ref[0]], axis=1)
    q = q_ref[0]

    def scores(n):
        kh, var = divmod(n, 2)
        q2 = jnp.concatenate([q[:, (2 * kh) * LANES:(2 * kh + 1) * LANES],
                              q[:, (2 * kh + 1) * LANES:(2 * kh + 2) * LANES]], axis=0)
        kk = k_all[:, (2 * kh + var) * LANES:(2 * kh + var + 1) * LANES]
        return _dot_nt(kk, q2)

    n_iter = 2 * ATT_KV_HEADS
    st_next = scores(0)
    for n in range(n_iter):
        kh, var = divmod(n, 2)
        st = st_next
        if n + 1 < n_iter:
            st_next = scores(n + 1)
        vt = vt_all[kh * hd:(kh + 1) * hd, :]
        st = jnp.concatenate([jnp.where(ok_prev, st[0:T], NEG_INF), st[T:2 * T],
                              jnp.where(ok_next, st[2 * T:3 * T], NEG_INF), st[3 * T:]], axis=0)
        h0 = ATT_GROUP * kh + var
        h1 = h0 + 2
        sink = jnp.where(first, sink_ref[h0], sink_ref[h1]) * LOG2E
        m = jnp.maximum(jnp.max(st, axis=0, keepdims=True), sink)
        p = jnp.exp2(st - m)
        denom = jnp.sum(p, axis=0, keepdims=True) + jnp.exp2(sink - m)
        ot = _dot(vt, p.astype(BF16)) * (1.0 / denom)
        o_ref[0, h0 * hd:(h0 + 1) * hd, :] = ot[:, 0:T].astype(BF16)
        o_ref[0, h1 * hd:(h1 + 1) * hd, :] = ot[:, T:2 * T].astype(BF16)


def _attn_call(sink, q, kx, vt, kx_c, vt_c):
    B, L, _ = q.shape
    C = kx_c.shape[1]
    T = ATT_BLOCK
    nb = L // T

    def k_spec(off):
        return pl.BlockSpec((1, T, KX_W), lambda b, i: (b, jnp.clip(i + off, 0, nb - 1), 0))

    def v_spec(off):
        return pl.BlockSpec((1, ATT_KV_W, T), lambda b, i: (b, 0, jnp.clip(i + off, 0, nb - 1)))

    return pl.pallas_call(
        functools.partial(_attn_kernel, n_blocks=nb),
        grid=(B, nb),
        in_specs=[pl.BlockSpec(memory_space=pltpu.SMEM),
                  pl.BlockSpec((1, T, ATT_Q_W), lambda b, i: (b, i, 0)),
                  pl.BlockSpec((1, C, KX_W), lambda b, i: (b, 0, 0)), k_spec(-1), k_spec(0), k_spec(1),
                  pl.BlockSpec((1, ATT_KV_W, C), lambda b, i: (b, 0, 0)), v_spec(-1), v_spec(0), v_spec(1)],
        out_specs=pl.BlockSpec((1, ATT_Q_W, T), lambda b, i: (b, 0, i)),
        out_shape=jax.ShapeDtypeStruct((B, ATT_Q_W, L), BF16),
        compiler_params=_cparams(("arbitrary", "arbitrary")),
        name="attn",
    )(sink, q, kx_c, kx, kx, kx, vt_c, vt, vt, vt)


N_CHAIN = 2 * ML_HEADS


def _mlstm_chunk(dirs, c_ref, n_ref, m_ref, with_h):
    T = ML_CHUNK
    row = lax.broadcasted_iota(jnp.int32, (T, T), 0)
    col = lax.broadcasted_iota(jnp.int32, (T, T), 1)
    ones8 = jnp.ones((8, T), BF16)
    top = lax.broadcasted_iota(jnp.int32, (8, ML_QK_DIM), 0) == 0
    masks = (col >= row, col <= row)
    lasts = (T - 1, 0)

    chains = []
    for d, (qt_ref, k_ref, vt_ref, g_ref, bc_ref, br_ref, h_ref) in enumerate(dirs):
        g = g_ref[0]
        bc = bc_ref[0]
        br = br_ref[0]
        for h in range(ML_HEADS):
            ci = d * ML_HEADS + h
            gi = d * 2 * ML_HEADS + h
            fi = gi + ML_HEADS
            k = k_ref[0, :, h * ML_QK_DIM:(h + 1) * ML_QK_DIM]
            vt = vt_ref[0, h * ML_V_DIM:(h + 1) * ML_V_DIM, :]
            u_col = g[:, gi:gi + 1] - bc[:, fi:fi + 1]
            b_row = br[fi:fi + 1, :]
            m_old = m_ref[ci, 0:1, 0:1]
            n_old = n_ref[ci]
            ct_old = c_ref[ci]
            qt = st = qn2 = None
            if with_h:
                qt = qt_ref[0, h * ML_QK_DIM:(h + 1) * ML_QK_DIM, :]
                st = _dot(k, qt)
                n_hi = n_old.astype(BF16)
                n_lo = (n_old - n_hi.astype(F32)).astype(BF16)
                qn2 = _dot(jnp.where(top, n_hi, n_lo), qt)
            m_last = jnp.maximum(jnp.max(u_col, axis=0, keepdims=True), m_old)
            decay = jnp.exp(m_old - m_last)
            kw = (k.astype(F32) * jnp.exp(u_col - m_last)).astype(BF16)
            c_new = decay * ct_old + _dot(vt, kw)
            n_new = decay * n_old + _dot(ones8, kw)
            m_new = b_row[:, lasts[d]:lasts[d] + 1] + m_last
            chains.append((ci, h, h_ref, masks[d], qt, vt, u_col, b_row, m_old, ct_old, st, qn2,
                           c_new, n_new, m_new))

    if with_h:
        for (ci, h, h_ref, mask, qt, vt, u_col, b_row, m_old, ct_old, st, qn2, c_new, n_new, m_new) in chains:
            umat = jnp.where(mask, u_col, -jnp.inf)
            m_row = jnp.maximum(jnp.max(umat, axis=0, keepdims=True), m_old)
            pt = st * jnp.exp(umat - m_row)
            w_int = jnp.exp(m_old - m_row)
            e_row = jnp.exp(-(b_row + m_row))
            nq = jnp.sum(pt, axis=0, keepdims=True) + w_int * (qn2[0:1, :] + qn2[1:2, :])
            den = jnp.maximum(jnp.abs(nq), e_row)
            lhs = jnp.concatenate([vt, ct_old.astype(BF16)], axis=1)
            rhs = jnp.concatenate([pt.astype(BF16), (qt.astype(F32) * w_int).astype(BF16)], axis=0)
            h_ref[0, h * ML_V_DIM:(h + 1) * ML_V_DIM, :] = (_dot(lhs, rhs) * (1.0 / den)).astype(BF16)

    for (ci, h, h_ref, mask, qt, vt, u_col, b_row, m_old, ct_old, st, qn2, c_new, n_new, m_new) in chains:
        c_ref[ci] = c_new
        n_ref[ci] = n_new
        m_ref[ci] = jnp.broadcast_to(m_new, (8, LANES))


def _mlstm_ctx_kernel(kf_ref, vtf_ref, gf_ref, bcf_ref, brf_ref, kb_ref, vtb_ref, gb_ref, bcb_ref, brb_ref,
                      c_ref, n_ref, m_ref):
    @pl.when(pl.program_id(1) == 0)
    def _():
        c_ref[...] = jnp.zeros_like(c_ref)
        n_ref[...] = jnp.zeros_like(n_ref)
        m_ref[...] = jnp.zeros_like(m_ref)

    dirs = ((None, kf_ref, vtf_ref, gf_ref, bcf_ref, brf_ref, None),
            (None, kb_ref, vtb_ref, gb_ref, bcb_ref, brb_ref, None))
    _mlstm_chunk(dirs, c_ref.at[0], n_ref.at[0], m_ref.at[0], with_h=False)


def _mlstm_kernel(c0_ref, n0_ref, m0_ref, qtf_ref, kf_ref, vtf_ref, gf_ref, bcf_ref, brf_ref,
                  qtb_ref, kb_ref, vtb_ref, gb_ref, bcb_ref, brb_ref, hf_ref, hb_ref, c_ref, n_ref, m_ref):
    @pl.when(pl.program_id(1) == 0)
    def _():
        c_ref[...] = c0_ref[0]
        n_ref[...] = n0_ref[0]
        m_ref[...] = m0_ref[0]

    dirs = ((qtf_ref, kf_ref, vtf_ref, gf_ref, bcf_ref, brf_ref, hf_ref),
            (qtb_ref, kb_ref, vtb_ref, gb_ref, bcb_ref, brb_ref, hb_ref))
    _mlstm_chunk(dirs, c_ref, n_ref, m_ref, with_h=True)


def _mlstm_specs(T, order, with_q):
    specs = [
        pl.BlockSpec((1, ML_QK_W, T), lambda b, j: (b, 0, order(j))),
        pl.BlockSpec((1, T, ML_QK_W), lambda b, j: (b, order(j), 0)),
        pl.BlockSpec((1, ML_V_W, T), lambda b, j: (b, 0, order(j))),
        pl.BlockSpec((1, T, ML_GATE_W), lambda b, j: (b, order(j), 0)),
        pl.BlockSpec((1, T, ML_GATE_W), lambda b, j: (b, order(j), 0)),
        pl.BlockSpec((1, ML_GATE_W, T), lambda b, j: (b, 0, order(j))),
    ]
    return specs if with_q else specs[1:]


_STATE_SHAPES = ((N_CHAIN, ML_V_DIM, ML_QK_DIM), (N_CHAIN, 8, ML_QK_DIM), (N_CHAIN, 8, LANES))


def _mlstm_ctx_call(mk, mvt, g, bc, br):
    B, C, _ = mk.shape
    T = ML_CHUNK
    nc = C // T
    state_specs = [pl.BlockSpec((1,) + s, lambda b, j: (b, 0, 0, 0)) for s in _STATE_SHAPES]
    return pl.pallas_call(
        _mlstm_ctx_kernel,
        grid=(B, nc),
        in_specs=_mlstm_specs(T, lambda j: j, False) + _mlstm_specs(T, lambda j: nc - 1 - j, False),
        out_specs=state_specs,
        out_shape=[jax.ShapeDtypeStruct((B,) + s, F32) for s in _STATE_SHAPES],
        compiler_params=_cparams(("arbitrary", "arbitrary")),
        name="mlstm_ctx",
    )(mk, mvt, g, bc, br, mk, mvt, g, bc, br)


def _mlstm_call(state, mqt, mk, mvt, g, bc, br):
    B, L, _ = mk.shape
    T = ML_CHUNK
    nc = L // T
    state_specs = [pl.BlockSpec((1,) + s, lambda b, j: (b, 0, 0, 0)) for s in _STATE_SHAPES]
    out_specs = [pl.BlockSpec((1, ML_V_W, T), lambda b, j: (b, 0, j)),
                 pl.BlockSpec((1, ML_V_W, T), lambda b, j: (b, 0, nc - 1 - j))]
    return pl.pallas_call(
        _mlstm_kernel,
        grid=(B, nc),
        in_specs=state_specs + _mlstm_specs(T, lambda j: j, True) + _mlstm_specs(T, lambda j: nc - 1 - j, True),
        out_specs=out_specs,
        out_shape=[jax.ShapeDtypeStruct((B, ML_V_W, L), BF16)] * 2,
        scratch_shapes=[pltpu.VMEM(s, F32) for s in _STATE_SHAPES],
        compiler_params=_cparams(("arbitrary", "arbitrary")),
        name="mlstm",
    )(*state, mqt, mk, mvt, g, bc, br, mqt, mk, mvt, g, bc, br)


def _merge_kernel(att_ref, hf_ref, hb_ref, so_ref, sga_ref, sgm_ref, x_ref, mod_ref, mlw_ref, n2w_ref,
                  wat_ref, wmt_ref, wo_ref, xmid_ref, h2_ref):
    ht = hf_ref[0].astype(F32) + hb_ref[0].astype(F32)
    parts = []
    for h in range(ML_HEADS):
        seg = ht[h * ML_V_DIM:(h + 1) * ML_V_DIM, :]
        ms = jnp.mean(seg * seg, axis=0, keepdims=True)
        parts.append(seg * lax.rsqrt(ms + EPS))
    ml = (jnp.concatenate(parts, axis=0) * mlw_ref[...] * so_ref[0].astype(F32)).astype(BF16)
    ya = _dot(wat_ref[...], att_ref[0])
    ym = _dot(wmt_ref[...], ml)
    y = (sga_ref[0].astype(F32) * ya + sgm_ref[0].astype(F32) * ym).astype(BF16)
    y2 = _dot_tn(y, wo_ref[...])
    g1 = mod_ref[0, 0:1, :]
    sh2 = mod_ref[0, 1:2, :]
    sc2 = mod_ref[0, 2:3, :]
    xm = x_ref[0] + g1 * y2
    xmid_ref[0] = xm
    ms = jnp.mean(xm * xm, axis=-1, keepdims=True)
    h2 = xm * lax.rsqrt(ms + EPS) * n2w_ref[...]
    h2_ref[0] = (h2 * sc2 + sh2).astype(BF16)


def _merge_call(att_t, hf_t, hb_t, so_t, sga_t, sgm_t, x, mod3, mlw_b, n2w, wat, wmt, wo, tm):
    B, L, _ = x.shape
    nt = L // tm

    def lat(w):
        return pl.BlockSpec((1, tm, w), lambda b, i: (b, i, 0))

    def lat_t(h):
        return pl.BlockSpec((1, h, tm), lambda b, i: (b, 0, i))

    def const(shape):
        return pl.BlockSpec(shape, lambda b, i: (0,) * len(shape))

    return pl.pallas_call(
        _merge_kernel,
        grid=(B, nt),
        in_specs=[lat_t(D), lat_t(D), lat_t(D), lat_t(D), lat_t(D), lat_t(D), lat(D),
                  pl.BlockSpec((1, 3, D), lambda b, i: (b, 0, 0)),
                  const((D, tm)), const((1, D)), const((D, D)), const((D, D)), const((D, D))],
        out_specs=[lat(D), lat(D)],
        out_shape=[jax.ShapeDtypeStruct((B, L, D), F32), jax.ShapeDtypeStruct((B, L, D), BF16)],
        compiler_params=_cparams(("arbitrary", "arbitrary")),
        name="merge",
    )(att_t, hf_t, hb_t, so_t, sga_t, sgm_t, x, mod3, mlw_b, n2w, wat, wmt, wo)


HALO = 16


def _ffn_kernel(h_ref, hp_ref, hn_ref, xmid_ref, mod_ref, wup_ref, cw_ref, cb_ref, wdn_ref, o_ref,
                act_ref, *, n_tiles, tn, dn):
    s = pl.program_id(0)
    i = jnp.minimum(s, pl.num_programs(0) - 2) % n_tiles
    cur = s % 2
    tm = h_ref.shape[1]

    @pl.when(s == 0)
    def _():
        act_ref[1] = jnp.zeros(act_ref.shape[1:], BF16)

    h = h_ref[0]
    prev_row = jnp.where(i > 0, hp_ref[0].astype(F32)[HALO - 1:HALO, :], 0.0)
    next_row = jnp.where(i < n_tiles - 1, hn_ref[0].astype(F32)[0:1, :], 0.0)
    top = lax.broadcasted_iota(jnp.int32, (16, D), 0) < 8
    edge = jnp.where(top, prev_row, next_row).astype(BF16)
    row8 = lax.broadcasted_iota(jnp.int32, (8, tn), 0)
    h_ext = jnp.concatenate([h, edge], axis=0)

    def up(c0):
        u_ext = _dot(h_ext, wup_ref[:, c0:c0 + tn])
        return u_ext[:tm], u_ext[tm:]

    def conv(u, ue, c0):
        below = pltpu.roll(u, 1, 0)
        above = pltpu.roll(u, tm - 1, 0)
        below = jnp.concatenate([jnp.where(row8 == 0, ue[0:8], below[0:8]), below[8:]], axis=0)
        above = jnp.concatenate([above[:tm - 8], jnp.where(row8 == 7, ue[8:16], above[tm - 8:])], axis=0)
        cw = cw_ref[:, c0:c0 + tn]
        return cb_ref[:, c0:c0 + tn] + below * cw[0:1] + u * cw[1:2] + above * cw[2:3]

    n_chunks = D_FF // tn
    n_dn = D // dn
    act_prev = act_ref[1 - cur]

    def down(k):
        cols = slice(k * dn, (k + 1) * dn)
        o_ref[0, :, cols] = xmid_ref[0, :, cols] + mod_ref[0, :, cols] * _dot(act_prev, wdn_ref[:, cols])

    nxt = (up(0), up(D_FF))
    done = 0
    for c in range(n_chunks):
        (ua, uae), (ug, uge) = nxt
        if c + 1 < n_chunks:
            nxt = (up((c + 1) * tn), up(D_FF + (c + 1) * tn))
        while done * n_chunks < (c + 1) * n_dn:
            down(done)
            done += 1
        a = conv(ua, uae, c * tn)
        gte = conv(ug, uge, D_FF + c * tn)
        act_ref[cur, :, c * tn:(c + 1) * tn] = (gte * jax.nn.sigmoid(gte) * a).astype(BF16)


def _ffn_call(h2, xmid, g2, wup, cw, cb, wdn, tm, tn, dn):
    B, L, _ = xmid.shape
    nt = L // tm
    n_all = B * nt
    hb = tm // HALO
    nhb = L // HALO

    def tile_in(s):
        t = jnp.minimum(s, n_all - 1)
        return t // nt, t % nt

    def tile_out(s):
        t = jnp.maximum(s - 1, 0)
        return t // nt, t % nt

    def in_spec():
        return pl.BlockSpec((1, tm, D), lambda s: (*tile_in(s), 0))

    def out_spec():
        return pl.BlockSpec((1, tm, D), lambda s: (*tile_out(s), 0))

    def prev_halo(s):
        b, i = tile_in(s)
        return b, jnp.maximum(i * hb - 1, 0), 0

    def next_halo(s):
        b, i = tile_in(s)
        return b, jnp.minimum((i + 1) * hb, nhb - 1), 0

    def const(shape):
        return pl.BlockSpec(shape, lambda s: (0,) * len(shape))

    return pl.pallas_call(
        functools.partial(_ffn_kernel, n_tiles=nt, tn=tn, dn=dn),
        grid=(n_all + 1,),
        in_specs=[in_spec(),
                  pl.BlockSpec((1, HALO, D), prev_halo),
                  pl.BlockSpec((1, HALO, D), next_halo),
                  out_spec(),
                  pl.BlockSpec((1, 1, D), lambda s: (tile_out(s)[0], 0, 0)),
                  const(wup.shape), const(cw.shape), const(cb.shape), const(wdn.shape)],
        out_specs=out_spec(),
        out_shape=jax.ShapeDtypeStruct((B, L, D), F32),
        scratch_shapes=[pltpu.VMEM((2, tm, D_FF), BF16)],
        compiler_params=_cparams(("arbitrary",)),
        name="ffn",
    )(h2, h2, h2, xmid, g2, wup, cw, cb, wdn)


def _pair_perm(n_heads):
    half = ATT_HEAD_DIM // 2
    idx = []
    for p in range(n_heads // 2):
        for sub in range(4):
            head = 2 * p + (sub % 2)
            d0 = (sub // 2) * half
            idx.extend(head * ATT_HEAD_DIM + d0 + e for e in range(half))
    return np.asarray(idx, np.int32)


def _rope_tables(L):
    rows = L // GRID_W
    row = jnp.repeat(jnp.arange(rows, dtype=F32), GRID_W)
    col = jnp.tile(jnp.arange(GRID_W, dtype=F32), rows)
    n_freq = ATT_HEAD_DIM // 4
    inv_freq = ROPE_BASE ** (-jnp.arange(n_freq, dtype=F32) / n_freq)
    ang = jnp.concatenate([row[:, None] * inv_freq, col[:, None] * inv_freq], axis=-1)
    cos = jnp.tile(jnp.cos(ang), (1, 4))
    sin = jnp.tile(jnp.sin(ang), (1, 4))
    sign = jnp.where(jnp.arange(LANES) < LANES // 2, -1.0, 1.0).astype(F32)
    return cos, sin * sign


def kernel(x, c, ctx, c_ctx, w_mod, b_mod, norm1_w, w_in, q_norm_w, k_norm_w, attn_sink, ml_gate_b, ml_norm_w,
           w_branch_att, w_branch_ml, w_out, norm2_w, w_up, conv_w, conv_b, w_down):
    B, L, _ = x.shape
    C = ctx.shape[1]
    assert L % 512 == 0 and C % 256 == 0 and L % GRID_W == 0
    l = 0
    tm_merge = 256

    n_rows = -(-(B + 1) // 8) * 8
    cc = jnp.concatenate([c, c_ctx[None, :], jnp.zeros((n_rows - B - 1, D), F32)], axis=0)
    mod = _mod_call(cc, w_mod[l], b_mod[l][None, :])
    sh1, sc1, g1, sh2, sc2, g2 = [mod[:, k * D:(k + 1) * D] for k in range(6)]
    lat_mod = jnp.stack([sh1[:B], 1.0 + sc1[:B]], axis=1)
    ctx_mod = jnp.stack([sh1[B], 1.0 + sc1[B]], axis=0)[None]
    mod3 = jnp.stack([g1[:B], sh2[:B], 1.0 + sc2[:B]], axis=1)
    g2b = g2[:B][:, None, :]

    w = w_in[l]
    qperm = _pair_perm(ATT_HEADS)
    kperm = _pair_perm(ATT_KV_HEADS)
    w_q = w[:, _O_AQ:_O_AQ + ATT_Q_W][:, qperm]
    w_k = w[:, _O_AK:_O_AK + ATT_KV_W][:, kperm]
    w_g = jnp.pad(w[:, _O_MG:_O_MG + ML_GATE_W], ((0, 0), (0, LANES - ML_GATE_W)))
    w_mk = w[:, _O_MK:_O_MK + ML_QK_W] * (ML_QK_DIM ** -0.5)
    w_p = jnp.concatenate([w_q, w_k, w_mk, w_g], axis=1).astype(BF16)
    w_t = jnp.concatenate([w[:, _O_AV:_O_AV + ATT_KV_W], w[:, _O_MQ:_O_MQ + ML_QK_W], w[:, _O_MV:_O_MV + ML_V_W],
                           w[:, _O_MO:_O_MO + ML_V_W], w[:, _O_GA:_O_GA + D], w[:, _O_GM:_O_GM + D]],
                          axis=1).T.astype(BF16)
    wgt = w[:, _O_MG:_O_MG + ML_GATE_W].T.astype(BF16)

    head_of_col = np.concatenate([qperm // ATT_HEAD_DIM, ATT_HEADS + kperm // ATT_HEAD_DIM])
    e_np = (head_of_col[:, None] == np.arange(LANES)[None, :]).astype(np.float32)
    e_mat = jnp.asarray(e_np, BF16)
    et_mat = jnp.asarray(np.concatenate([e_np.T, e_np.T], axis=0), BF16)
    qkw = jnp.concatenate([q_norm_w[l][qperm % ATT_HEAD_DIM] * (ATT_SCALE * LOG2E),
                           k_norm_w[l][kperm % ATT_HEAD_DIM]])[None, :]
    cos_t, sin_t = _rope_tables(L)
    gb = ml_gate_b[l].reshape(1, ML_GATE_W)
    gbt = ml_gate_b[l].reshape(ML_GATE_W, 1)
    n1w = norm1_w[l][None, :]

    kx_c, mk_c, g_c, bc_c, vt_c, mvt_c, br_c = _ctxproj_call(
        ctx, ctx_mod, n1w, w_p, w_t, wgt, e_mat, et_mat, qkw, gb, gbt, tm=256)
    q, kx, mk, g, bc, vt, mqt, mvt, sot, sgat, sgmt, br = _inproj_call(
        x, lat_mod, n1w, w_p, w_t, wgt, e_mat, et_mat, qkw, cos_t, sin_t, gb, gbt, tm=512)

    att_t = _attn_call(attn_sink[l], q, kx, vt, kx_c, vt_c)
    state = _mlstm_ctx_call(mk_c, mvt_c, g_c, bc_c, br_c)
    hf_t, hb_t = _mlstm_call(state, mqt, mk, mvt, g, bc, br)

    mlw_b = jnp.broadcast_to(ml_norm_w[l][:, None], (ML_V_W, tm_merge))
    xmid, h2 = _merge_call(att_t, hf_t, hb_t, sot, sgat, sgmt, x, mod3, mlw_b, norm2_w[l][None, :],
                           w_branch_att[l].T.astype(BF16), w_branch_ml[l].T.astype(BF16), w_out[l].astype(BF16),
                           tm=tm_merge)
    out = _ffn_call(h2, xmid, g2b, w_up[l].astype(BF16), conv_w[l], conv_b[l][None, :],
                    w_down[l].astype(BF16), tm=512, tn=256, dn=256)
    return out
```

```python
import functools

import jax
import jax.numpy as jnp
import numpy as np
from jax import lax
from jax.experimental import pallas as pl
from jax.experimental.pallas import tpu as pltpu

D = 1024
GRID_W = 64
ATT_HEADS = 16
ATT_KV_HEADS = 4
ATT_HEAD_DIM = 64
ATT_GROUP = ATT_HEADS // ATT_KV_HEADS
ATT_BLOCK = 128
WINDOW = 128
ROPE_BASE = 10000.0
ATT_SCALE = ATT_HEAD_DIM ** -0.5
LOG2E = 1.4426950408889634
ML_HEADS = 4
ML_QK_DIM = 128
ML_V_DIM = 256
ML_CHUNK = 128
D_FF = 2816
EPS = 1e-6
NEG_INF = -1e30

ATT_Q_W = ATT_HEADS * ATT_HEAD_DIM
ATT_KV_W = ATT_KV_HEADS * ATT_HEAD_DIM
ML_QK_W = ML_HEADS * ML_QK_DIM
ML_V_W = ML_HEADS * ML_V_DIM
ML_GATE_W = 2 * 2 * ML_HEADS

LANES = 128
KX_W = ATT_KV_HEADS * 2 * LANES
VMEM_LIMIT = 56 * 1024 * 1024

BF16 = jnp.bfloat16
F32 = jnp.float32

_O_AQ = 0
_O_AK = _O_AQ + ATT_Q_W
_O_AV = _O_AK + ATT_KV_W
_O_MQ = _O_AV + ATT_KV_W
_O_MK = _O_MQ + ML_QK_W
_O_MV = _O_MK + ML_QK_W
_O_MO = _O_MV + ML_V_W
_O_MG = _O_MO + ML_V_W
_O_GA = _O_MG + ML_GATE_W
_O_GM = _O_GA + D

QK_W = ATT_Q_W + ATT_KV_W
_P_QK = 0
_P_MK = _P_QK + QK_W
_P_MG = _P_MK + ML_QK_W
_P_END = _P_MG + LANES
_R_V = 0
_R_MQ = _R_V + ATT_KV_W
_R_MV = _R_MQ + ML_QK_W
_R_MO = _R_MV + ML_V_W
_R_GA = _R_MO + ML_V_W
_R_GM = _R_GA + D
_R_END = _R_GM + D


def _dot(a, b):
    return jnp.dot(a, b, preferred_element_type=F32)


def _dot_nt(a, b):
    return lax.dot_general(a, b, (((1,), (1,)), ((), ())), preferred_element_type=F32)


def _dot_tn(a, b):
    return lax.dot_general(a, b, (((0,), (0,)), ((), ())), preferred_element_type=F32)


def _cparams(sem):
    return pltpu.CompilerParams(dimension_semantics=sem, vmem_limit_bytes=VMEM_LIMIT)


def _mod_kernel(c_ref, w_ref, b_ref, o_ref):
    c = c_ref[...]
    a = c * jax.nn.sigmoid(c)
    o_ref[...] = jnp.dot(a, w_ref[...], preferred_element_type=F32,
                         precision=lax.Precision.HIGHEST) + b_ref[...]


def _mod_call(cc, w_mod, b_mod):
    rows = cc.shape[0]
    n = w_mod.shape[1]
    tn = 1536
    return pl.pallas_call(
        _mod_kernel,
        grid=(n // tn,),
        in_specs=[pl.BlockSpec((rows, D), lambda j: (0, 0)),
                  pl.BlockSpec((D, tn), lambda j: (0, j)),
                  pl.BlockSpec((1, tn), lambda j: (0, j))],
        out_specs=pl.BlockSpec((rows, tn), lambda j: (0, j)),
        out_shape=jax.ShapeDtypeStruct((rows, n), F32),
        compiler_params=_cparams(("arbitrary",)),
        name="mod",
    )(cc, w_mod, b_mod)


def _split2(x):
    x1 = x.astype(BF16)
    x2 = (x - x1.astype(F32)).astype(BF16)
    return x1, x2


def _norm_modulate(x, n1w_ref, mod_ref):
    ms = jnp.mean(x * x, axis=-1, keepdims=True)
    y = x * lax.rsqrt(ms + EPS) * n1w_ref[...]
    return (y * mod_ref[0, 1:2, :] + mod_ref[0, 0:1, :]).astype(BF16)


def _head_rms_scale(ss):
    r = lax.rsqrt(ss * (1.0 / ATT_HEAD_DIM) + EPS)
    r_hi = r.astype(BF16)
    r_lo = (r - r_hi.astype(F32)).astype(BF16)
    return jnp.concatenate([r_hi, r_lo], axis=1)


def _store_k_variants(k_ref, pair, o):
    lane = lax.broadcasted_iota(jnp.int32, (1, LANES), 1)
    keep = ((lane // 32) % 2) == 0
    c0 = 4 * pair * LANES
    k_ref[0, :, c0:c0 + LANES] = jnp.where(keep, o, 0.0).astype(BF16)
    k_ref[0, :, c0 + LANES:c0 + 2 * LANES] = jnp.where(keep, 0.0, pltpu.roll(o, 32, 1)).astype(BF16)
    k_ref[0, :, c0 + 2 * LANES:c0 + 3 * LANES] = jnp.where(keep, pltpu.roll(o, 96, 1), 0.0).astype(BF16)
    k_ref[0, :, c0 + 3 * LANES:c0 + 4 * LANES] = jnp.where(keep, 0.0, o).astype(BF16)


def _cum_gates_cols(g16, tri_lo, tri_up):
    parts = _split2(jax.nn.log_sigmoid(g16))
    fwd_col = lax.broadcasted_iota(jnp.int32, (1, ML_GATE_W), 1) < ML_GATE_W // 2
    return jnp.where(fwd_col, sum(_dot(tri_lo, p) for p in parts), sum(_dot(tri_up, p) for p in parts))


def _cum_gates_rows(gt16, tri_lo, tri_up):
    parts = _split2(jax.nn.log_sigmoid(gt16))
    fwd_row = lax.broadcasted_iota(jnp.int32, (ML_GATE_W, 1), 0) < ML_GATE_W // 2
    return jnp.where(fwd_row, sum(_dot(p, tri_up) for p in parts), sum(_dot(p, tri_lo) for p in parts))


def _inproj_kernel(x_ref, mod_ref, n1w_ref, w_ref, wt_ref, wgt_ref, e_ref, et_ref, qkw_ref,
                   cos_ref, sin_ref, gb_ref, gbt_ref, tril_ref, triu_ref,
                   q_ref, k_ref, mk_ref, g_ref, bc_ref, vt_ref, mqt_ref, mvt_ref, sot_ref, sgat_ref, sgmt_ref, br_ref,
                   hn_ref):
    hn_ref[...] = _norm_modulate(x_ref[0], n1w_ref, mod_ref)
    hn = hn_ref[...]

    acc = _dot(hn, w_ref[:, _P_QK:_P_QK + QK_W])
    g16 = _dot(hn, w_ref[:, _P_MG:_P_MG + LANES])[:, :ML_GATE_W] + gb_ref[...]
    gt16 = _dot_nt(wgt_ref[...], hn) + gbt_ref[...]
    g_ref[0] = g16
    mk_ref[0] = _dot(hn, w_ref[:, _P_MK:_P_MK + ML_QK_W]).astype(BF16)
    ss = _dot((acc * acc).astype(BF16), e_ref[...])
    vt_ref[0] = _dot_nt(wt_ref[_R_V:_R_V + ATT_KV_W, :], hn).astype(BF16)
    mqt_ref[0] = _dot_nt(wt_ref[_R_MQ:_R_MQ + ML_QK_W, :], hn).astype(BF16)
    rb = _dot(_head_rms_scale(ss), et_ref[...])
    mvt_ref[0] = _dot_nt(wt_ref[_R_MV:_R_MV + ML_V_W, :], hn).astype(BF16)
    bc_ref[0] = _cum_gates_cols(g16, tril_ref[...], triu_ref[...])

    qn = acc * rb * qkw_ref[...]
    cos = cos_ref[...]
    sin = sin_ref[...]
    for gi in range(QK_W // LANES):
        xs = qn[:, gi * LANES:(gi + 1) * LANES]
        o = xs * cos + pltpu.roll(xs, LANES // 2, 1) * sin
        if gi < ATT_Q_W // LANES:
            q_ref[0, :, gi * LANES:(gi + 1) * LANES] = o.astype(BF16)
        else:
            _store_k_variants(k_ref, gi - ATT_Q_W // LANES, o)

    sot_ref[0] = jax.nn.sigmoid(_dot_nt(wt_ref[_R_MO:_R_MO + ML_V_W, :], hn)).astype(BF16)
    br_ref[0] = _cum_gates_rows(gt16, tril_ref[...], triu_ref[...])
    sgat_ref[0] = jax.nn.sigmoid(_dot_nt(wt_ref[_R_GA:_R_GA + D, :], hn)).astype(BF16)
    sgmt_ref[0] = jax.nn.sigmoid(_dot_nt(wt_ref[_R_GM:_R_GM + D, :], hn)).astype(BF16)


def _ctxproj_kernel(x_ref, mod_ref, n1w_ref, w_ref, wt_ref, wgt_ref, e_ref, et_ref, qkw_ref,
                    gb_ref, gbt_ref, tril_ref, triu_ref,
                    k_ref, mk_ref, g_ref, bc_ref, vt_ref, mvt_ref, br_ref):
    hn = _norm_modulate(x_ref[0], n1w_ref, mod_ref)
    acc = _dot(hn, w_ref[:, _P_QK + ATT_Q_W:_P_QK + QK_W])
    g16 = _dot(hn, w_ref[:, _P_MG:_P_MG + LANES])[:, :ML_GATE_W] + gb_ref[...]
    gt16 = _dot_nt(wgt_ref[...], hn) + gbt_ref[...]
    g_ref[0] = g16
    mk_ref[0] = _dot(hn, w_ref[:, _P_MK:_P_MK + ML_QK_W]).astype(BF16)
    ss = _dot((acc * acc).astype(BF16), e_ref[ATT_Q_W:QK_W, :])
    vt_ref[0] = _dot_nt(wt_ref[_R_V:_R_V + ATT_KV_W, :], hn).astype(BF16)
    rb = _dot(_head_rms_scale(ss), et_ref[:, ATT_Q_W:QK_W])
    mvt_ref[0] = _dot_nt(wt_ref[_R_MV:_R_MV + ML_V_W, :], hn).astype(BF16)
    bc_ref[0] = _cum_gates_cols(g16, tril_ref[...], triu_ref[...])
    kn = acc * rb * qkw_ref[:, ATT_Q_W:QK_W]
    for pair in range(ATT_KV_W // LANES):
        _store_k_variants(k_ref, pair, kn[:, pair * LANES:(pair + 1) * LANES])
    br_ref[0] = _cum_gates_rows(gt16, tril_ref[...], triu_ref[...])


def _block_tri(n, block, lower):
    r = np.arange(n)[:, None]
    c = np.arange(n)[None, :]
    same = (r // block) == (c // block)
    return jnp.asarray(same & ((c <= r) if lower else (c >= r)), BF16)


def _const_spec(shape):
    return pl.BlockSpec(shape, lambda b, i: (0,) * len(shape))


def _inproj_call(x, lat_mod, n1w, w_p, w_t, wgt, e_mat, et_mat, qkw, cos_t, sin_t, gb, gbt, tm):
    B, L, _ = x.shape
    tril = _block_tri(tm, ML_CHUNK, True)
    triu = _block_tri(tm, ML_CHUNK, False)

    def rows(w):
        return pl.BlockSpec((1, tm, w), lambda b, i: (b, i, 0))

    def cols(h):
        return pl.BlockSpec((1, h, tm), lambda b, i: (b, 0, i))

    consts = [n1w, w_p, w_t, wgt, e_mat, et_mat, qkw]
    tail = [gb, gbt, tril, triu]
    in_specs = ([rows(D), pl.BlockSpec((1, 2, D), lambda b, i: (b, 0, 0))]
                + [_const_spec(a.shape) for a in consts]
                + [pl.BlockSpec((tm, LANES), lambda b, i: (i, 0))] * 2
                + [_const_spec(a.shape) for a in tail])
    out_specs = [rows(ATT_Q_W), rows(KX_W), rows(ML_QK_W), rows(ML_GATE_W), rows(ML_GATE_W),
                 cols(ATT_KV_W), cols(ML_QK_W), cols(ML_V_W), cols(ML_V_W), cols(D), cols(D), cols(ML_GATE_W)]
    out_shape = [
        jax.ShapeDtypeStruct((B, L, ATT_Q_W), BF16),
        jax.ShapeDtypeStruct((B, L, KX_W), BF16),
        jax.ShapeDtypeStruct((B, L, ML_QK_W), BF16),
        jax.ShapeDtypeStruct((B, L, ML_GATE_W), F32),
        jax.ShapeDtypeStruct((B, L, ML_GATE_W), F32),
        jax.ShapeDtypeStruct((B, ATT_KV_W, L), BF16),
        jax.ShapeDtypeStruct((B, ML_QK_W, L), BF16),
        jax.ShapeDtypeStruct((B, ML_V_W, L), BF16),
        jax.ShapeDtypeStruct((B, ML_V_W, L), BF16),
        jax.ShapeDtypeStruct((B, D, L), BF16),
        jax.ShapeDtypeStruct((B, D, L), BF16),
        jax.ShapeDtypeStruct((B, ML_GATE_W, L), F32),
    ]
    return pl.pallas_call(
        _inproj_kernel,
        grid=(B, L // tm),
        in_specs=in_specs,
        out_specs=out_specs,
        out_shape=out_shape,
        scratch_shapes=[pltpu.VMEM((tm, D), BF16)],
        compiler_params=_cparams(("arbitrary", "arbitrary")),
        name="inproj",
    )(x, lat_mod, *consts, cos_t, sin_t, *tail)


def _ctxproj_call(ctx, ctx_mod, n1w, w_p, w_t, wgt, e_mat, et_mat, qkw, gb, gbt, tm):
    B, C, _ = ctx.shape
    tril = _block_tri(tm, ML_CHUNK, True)
    triu = _block_tri(tm, ML_CHUNK, False)

    def rows(w):
        return pl.BlockSpec((1, tm, w), lambda b, i: (b, i, 0))

    def cols(h):
        return pl.BlockSpec((1, h, tm), lambda b, i: (b, 0, i))

    consts = [ctx_mod, n1w, w_p, w_t, wgt, e_mat, et_mat, qkw, gb, gbt, tril, triu]
    out_specs = [rows(KX_W), rows(ML_QK_W), rows(ML_GATE_W), rows(ML_GATE_W),
                 cols(ATT_KV_W), cols(ML_V_W), cols(ML_GATE_W)]
    out_shape = [
        jax.ShapeDtypeStruct((B, C, KX_W), BF16),
        jax.ShapeDtypeStruct((B, C, ML_QK_W), BF16),
        jax.ShapeDtypeStruct((B, C, ML_GATE_W), F32),
        jax.ShapeDtypeStruct((B, C, ML_GATE_W), F32),
        jax.ShapeDtypeStruct((B, ATT_KV_W, C), BF16),
        jax.ShapeDtypeStruct((B, ML_V_W, C), BF16),
        jax.ShapeDtypeStruct((B, ML_GATE_W, C), F32),
    ]
    return pl.pallas_call(
        _ctxproj_kernel,
        grid=(B, C // tm),
        in_specs=[rows(D)] + [_const_spec(a.shape) for a in consts],
        out_specs=out_specs,
        out_shape=out_shape,
        compiler_params=_cparams(("arbitrary", "arbitrary")),
        name="ctxproj",
    )(ctx, *consts)


ATT_AHEAD = 5

def _attn_kernel(sink_ref, q_ref, kc_ref, kp_ref, k0_ref, kn_ref, vc_ref, vp_ref, v0_ref, vn_ref, o_ref,
                 *, n_blocks):
    i = pl.program_id(1)
    T = ATT_BLOCK
    hd = ATT_HEAD_DIM
    s_idx = lax.broadcasted_iota(jnp.int32, (T, 2 * T), 0)
    t_idx = lax.broadcasted_iota(jnp.int32, (T, 2 * T), 1) % T
    ok_prev = (s_idx >= t_idx) & (i > 0)
    ok_next = (s_idx <= t_idx) & (i < n_blocks - 1)
    first = lax.broadcasted_iota(jnp.int32, (1, 2 * T), 1) < T

    k_all = jnp.concatenate([kp_ref[0], k0_ref[0], kn_ref[0], kc_ref[0]], axis=0)
    vt_all = jnp.concatenate([vp_ref[0], v0_ref[0], vn_ref[0], vc_ref[0]], axis=1)
    q = q_ref[0]
    ones_rows = jnp.ones((16, vt_all.shape[1]), BF16)

    def scores(n):
        kh, var = divmod(n, 2)
        q2 = jnp.concatenate([q[:, (2 * kh) * LANES:(2 * kh + 1) * LANES],
                              q[:, (2 * kh + 1) * LANES:(2 * kh + 2) * LANES]], axis=0)
        kk = k_all[:, (2 * kh + var) * LANES:(2 * kh + var + 1) * LANES]
        return _dot_nt(kk, q2)

    n_iter = 2 * ATT_KV_HEADS
    pending = [scores(n) for n in range(ATT_AHEAD)]
    for n in range(n_iter):
        kh, var = divmod(n, 2)
        st = pending.pop(0)
        if n + ATT_AHEAD < n_iter:
            pending.append(scores(n + ATT_AHEAD))
        vt = vt_all[kh * hd:(kh + 1) * hd, :]
        st = jnp.concatenate([jnp.where(ok_prev, st[0:T], NEG_INF), st[T:2 * T],
                              jnp.where(ok_next, st[2 * T:3 * T], NEG_INF), st[3 * T:]], axis=0)
        h0 = ATT_GROUP * kh + var
        h1 = h0 + 2
        sink = jnp.where(first, sink_ref[h0], sink_ref[h1]) * LOG2E
        m = jnp.maximum(jnp.max(st, axis=0, keepdims=True), sink)
        p = jnp.exp2(st - m)
        ot = _dot(jnp.concatenate([vt, ones_rows], axis=0), p.astype(BF16))
        denom = ot[hd:hd + 1, :] + jnp.exp2(sink - m)
        ot = ot[0:hd, :] * (1.0 / denom)
        o_ref[0, h0 * hd:(h0 + 1) * hd, :] = ot[:, 0:T].astype(BF16)
        o_ref[0, h1 * hd:(h1 + 1) * hd, :] = ot[:, T:2 * T].astype(BF16)


def _attn_call(sink, q, kx, vt, kx_c, vt_c):
    B, L, _ = q.shape
    C = kx_c.shape[1]
    T = ATT_BLOCK
    nb = L // T

    def k_spec(off):
        return pl.BlockSpec((1, T, KX_W), lambda b, i: (b, jnp.clip(i + off, 0, nb - 1), 0))

    def v_spec(off):
        return pl.BlockSpec((1, ATT_KV_W, T), lambda b, i: (b, 0, jnp.clip(i + off, 0, nb - 1)))

    return pl.pallas_call(
        functools.partial(_attn_kernel, n_blocks=nb),
        grid=(B, nb),
        in_specs=[pl.BlockSpec(memory_space=pltpu.SMEM),
                  pl.BlockSpec((1, T, ATT_Q_W), lambda b, i: (b, i, 0)),
                  pl.BlockSpec((1, C, KX_W), lambda b, i: (b, 0, 0)), k_spec(-1), k_spec(0), k_spec(1),
                  pl.BlockSpec((1, ATT_KV_W, C), lambda b, i: (b, 0, 0)), v_spec(-1), v_spec(0), v_spec(1)],
        out_specs=pl.BlockSpec((1, ATT_Q_W, T), lambda b, i: (b, 0, i)),
        out_shape=jax.ShapeDtypeStruct((B, ATT_Q_W, L), BF16),
        compiler_params=_cparams(("arbitrary", "arbitrary")),
        name="attn",
    )(sink, q, kx_c, kx, kx, kx, vt_c, vt, vt, vt)


N_CHAIN = 2 * ML_HEADS


def _mlstm_chunk(dirs, c_ref, n_ref, m_ref, with_h):
    T = ML_CHUNK
    row = lax.broadcasted_iota(jnp.int32, (T, T), 0)
    col = lax.broadcasted_iota(jnp.int32, (T, T), 1)
    ones8 = jnp.ones((8, T), BF16)
    top = lax.broadcasted_iota(jnp.int32, (8, ML_QK_DIM), 0) == 0
    masks = (col >= row, col <= row)
    lasts = (T - 1, 0)

    chains = []
    for d, (qt_ref, k_ref, vt_ref, g_ref, bc_ref, br_ref, h_ref) in enumerate(dirs):
        g = g_ref[0]
        bc = bc_ref[0]
        br = br_ref[0]
        for h in range(ML_HEADS):
            ci = d * ML_HEADS + h
            gi = d * 2 * ML_HEADS + h
            fi = gi + ML_HEADS
            k = k_ref[0, :, h * ML_QK_DIM:(h + 1) * ML_QK_DIM]
            vt = vt_ref[0, h * ML_V_DIM:(h + 1) * ML_V_DIM, :]
            u_col = g[:, gi:gi + 1] - bc[:, fi:fi + 1]
            b_row = br[fi:fi + 1, :]
            m_old = m_ref[ci, 0:1, 0:1]
            n_old = n_ref[ci]
            ct_old = c_ref[ci]
            qt = st = qn2 = None
            if with_h:
                qt = qt_ref[0, h * ML_QK_DIM:(h + 1) * ML_QK_DIM, :]
                st = _dot(k, qt)
                n_hi = n_old.astype(BF16)
                n_lo = (n_old - n_hi.astype(F32)).astype(BF16)
                qn2 = _dot(jnp.where(top, n_hi, n_lo), qt)
            m_last = jnp.maximum(jnp.max(u_col, axis=0, keepdims=True), m_old)
            decay = jnp.exp(m_old - m_last)
            kw = (k.astype(F32) * jnp.exp(u_col - m_last)).astype(BF16)
            c_new = decay * ct_old + _dot(vt, kw)
            n_new = decay * n_old + _dot(ones8, kw)
            m_new = b_row[:, lasts[d]:lasts[d] + 1] + m_last
            chains.append((ci, h, h_ref, masks[d], qt, vt, u_col, b_row, m_old, ct_old, st, qn2,
                           c_new, n_new, m_new))

    if with_h:
        for (ci, h, h_ref, mask, qt, vt, u_col, b_row, m_old, ct_old, st, qn2, c_new, n_new, m_new) in chains:
            umat = jnp.where(mask, u_col, -jnp.inf)
            m_row = jnp.maximum(jnp.max(umat, axis=0, keepdims=True), m_old)
            pt = st * jnp.exp(umat - m_row)
            w_int = jnp.exp(m_old - m_row)
            e_row = jnp.exp(-(b_row + m_row))
            nq = jnp.sum(pt, axis=0, keepdims=True) + w_int * (qn2[0:1, :] + qn2[1:2, :])
            den = jnp.maximum(jnp.abs(nq), e_row)
            lhs = jnp.concatenate([vt, ct_old.astype(BF16)], axis=1)
            rhs = jnp.concatenate([pt.astype(BF16), (qt.astype(F32) * w_int).astype(BF16)], axis=0)
            h_ref[0, h * ML_V_DIM:(h + 1) * ML_V_DIM, :] = (_dot(lhs, rhs) * (1.0 / den)).astype(BF16)

    for (ci, h, h_ref, mask, qt, vt, u_col, b_row, m_old, ct_old, st, qn2, c_new, n_new, m_new) in chains:
        c_ref[ci] = c_new
        n_ref[ci] = n_new
        m_ref[ci] = jnp.broadcast_to(m_new, (8, LANES))


def _mlstm_ctx_kernel(kf_ref, vtf_ref, gf_ref, bcf_ref, brf_ref, kb_ref, vtb_ref, gb_ref, bcb_ref, brb_ref,
                      c_ref, n_ref, m_ref):
    @pl.when(pl.program_id(1) == 0)
    def _():
        c_ref[...] = jnp.zeros_like(c_ref)
        n_ref[...] = jnp.zeros_like(n_ref)
        m_ref[...] = jnp.zeros_like(m_ref)

    dirs = ((None, kf_ref, vtf_ref, gf_ref, bcf_ref, brf_ref, None),
            (None, kb_ref, vtb_ref, gb_ref, bcb_ref, brb_ref, None))
    _mlstm_chunk(dirs, c_ref.at[0], n_ref.at[0], m_ref.at[0], with_h=False)


def _mlstm_kernel(c0_ref, n0_ref, m0_ref, qtf_ref, kf_ref, vtf_ref, gf_ref, bcf_ref, brf_ref,
                  qtb_ref, kb_ref, vtb_ref, gb_ref, bcb_ref, brb_ref, hf_ref, hb_ref, c_ref, n_ref, m_ref):
    @pl.when(pl.program_id(1) == 0)
    def _():
        c_ref[...] = c0_ref[0]
        n_ref[...] = n0_ref[0]
        m_ref[...] = m0_ref[0]

    dirs = ((qtf_ref, kf_ref, vtf_ref, gf_ref, bcf_ref, brf_ref, hf_ref),
            (qtb_ref, kb_ref, vtb_ref, gb_ref, bcb_ref, brb_ref, hb_ref))
    _mlstm_chunk(dirs, c_ref, n_ref, m_ref, with_h=True)


def _mlstm_specs(T, order, with_q):
    specs = [
        pl.BlockSpec((1, ML_QK_W, T), lambda b, j: (b, 0, order(j))),
        pl.BlockSpec((1, T, ML_QK_W), lambda b, j: (b, order(j), 0)),
        pl.BlockSpec((1, ML_V_W, T), lambda b, j: (b, 0, order(j))),
        pl.BlockSpec((1, T, ML_GATE_W), lambda b, j: (b, order(j), 0)),
        pl.BlockSpec((1, T, ML_GATE_W), lambda b, j: (b, order(j), 0)),
        pl.BlockSpec((1, ML_GATE_W, T), lambda b, j: (b, 0, order(j))),
    ]
    return specs if with_q else specs[1:]


_STATE_SHAPES = ((N_CHAIN, ML_V_DIM, ML_QK_DIM), (N_CHAIN, 8, ML_QK_DIM), (N_CHAIN, 8, LANES))


def _mlstm_ctx_call(mk, mvt, g, bc, br):
    B, C, _ = mk.shape
    T = ML_CHUNK
    nc = C // T
    state_specs = [pl.BlockSpec((1,) + s, lambda b, j: (b, 0, 0, 0)) for s in _STATE_SHAPES]
    return pl.pallas_call(
        _mlstm_ctx_kernel,
        grid=(B, nc),
        in_specs=_mlstm_specs(T, lambda j: j, False) + _mlstm_specs(T, lambda j: nc - 1 - j, False),
        out_specs=state_specs,
        out_shape=[jax.ShapeDtypeStruct((B,) + s, F32) for s in _STATE_SHAPES],
        compiler_params=_cparams(("arbitrary", "arbitrary")),
        name="mlstm_ctx",
    )(mk, mvt, g, bc, br, mk, mvt, g, bc, br)


def _mlstm_call(state, mqt, mk, mvt, g, bc, br):
    B, L, _ = mk.shape
    T = ML_CHUNK
    nc = L // T
    state_specs = [pl.BlockSpec((1,) + s, lambda b, j: (b, 0, 0, 0)) for s in _STATE_SHAPES]
    out_specs = [pl.BlockSpec((1, ML_V_W, T), lambda b, j: (b, 0, j)),
                 pl.BlockSpec((1, ML_V_W, T), lambda b, j: (b, 0, nc - 1 - j))]
    return pl.pallas_call(
        _mlstm_kernel,
        grid=(B, nc),
        in_specs=state_specs + _mlstm_specs(T, lambda j: j, True) + _mlstm_specs(T, lambda j: nc - 1 - j, True),
        out_specs=out_specs,
        out_shape=[jax.ShapeDtypeStruct((B, ML_V_W, L), BF16)] * 2,
        scratch_shapes=[pltpu.VMEM(s, F32) for s in _STATE_SHAPES],
        compiler_params=_cparams(("arbitrary", "arbitrary")),
        name="mlstm",
    )(*state, mqt, mk, mvt, g, bc, br, mqt, mk, mvt, g, bc, br)


def _merge_kernel(att_ref, hf_ref, hb_ref, so_ref, sga_ref, sgm_ref, x_ref, mod_ref, mlw_ref, n2w_ref,
                  wat_ref, wmt_ref, wo_ref, xmid_ref, h2_ref):
    ht = hf_ref[0].astype(F32) + hb_ref[0].astype(F32)
    parts = []
    for h in range(ML_HEADS):
        seg = ht[h * ML_V_DIM:(h + 1) * ML_V_DIM, :]
        ms = jnp.mean(seg * seg, axis=0, keepdims=True)
        parts.append(seg * lax.rsqrt(ms + EPS))
    ml = (jnp.concatenate(parts, axis=0) * mlw_ref[...] * so_ref[0].astype(F32)).astype(BF16)
    ya = _dot(wat_ref[...], att_ref[0])
    ym = _dot(wmt_ref[...], ml)
    y = (sga_ref[0].astype(F32) * ya + sgm_ref[0].astype(F32) * ym).astype(BF16)
    y2 = _dot_tn(y, wo_ref[...])
    g1 = mod_ref[0, 0:1, :]
    sh2 = mod_ref[0, 1:2, :]
    sc2 = mod_ref[0, 2:3, :]
    xm = x_ref[0] + g1 * y2
    xmid_ref[0] = xm
    ms = jnp.mean(xm * xm, axis=-1, keepdims=True)
    h2 = xm * lax.rsqrt(ms + EPS) * n2w_ref[...]
    h2_ref[0] = (h2 * sc2 + sh2).astype(BF16)


def _merge_call(att_t, hf_t, hb_t, so_t, sga_t, sgm_t, x, mod3, mlw_b, n2w, wat, wmt, wo, tm):
    B, L, _ = x.shape
    nt = L // tm

    def lat(w):
        return pl.BlockSpec((1, tm, w), lambda b, i: (b, i, 0))

    def lat_t(h):
        return pl.BlockSpec((1, h, tm), lambda b, i: (b, 0, i))

    def const(shape):
        return pl.BlockSpec(shape, lambda b, i: (0,) * len(shape))

    return pl.pallas_call(
        _merge_kernel,
        grid=(B, nt),
        in_specs=[lat_t(D), lat_t(D), lat_t(D), lat_t(D), lat_t(D), lat_t(D), lat(D),
                  pl.BlockSpec((1, 3, D), lambda b, i: (b, 0, 0)),
                  const((D, tm)), const((1, D)), const((D, D)), const((D, D)), const((D, D))],
        out_specs=[lat(D), lat(D)],
        out_shape=[jax.ShapeDtypeStruct((B, L, D), F32), jax.ShapeDtypeStruct((B, L, D), BF16)],
        compiler_params=_cparams(("arbitrary", "arbitrary")),
        name="merge",
    )(att_t, hf_t, hb_t, so_t, sga_t, sgm_t, x, mod3, mlw_b, n2w, wat, wmt, wo)


HALO = 16
FFN_AHEAD = 1


def _ffn_kernel(h_ref, hp_ref, hn_ref, xmid_ref, mod_ref, wup_ref, cw_ref, cb_ref, wdn_ref, o_ref,
                act_ref, *, n_tiles, tn, dn):
    s = pl.program_id(0)
    i = jnp.minimum(s, pl.num_programs(0) - 2) % n_tiles
    cur = s % 2
    tm = h_ref.shape[1]

    @pl.when(s == 0)
    def _():
        act_ref[1] = jnp.zeros(act_ref.shape[1:], BF16)

    h = h_ref[0]
    prev_row = jnp.where(i > 0, hp_ref[0].astype(F32)[HALO - 1:HALO, :], 0.0)
    next_row = jnp.where(i < n_tiles - 1, hn_ref[0].astype(F32)[0:1, :], 0.0)
    top = lax.broadcasted_iota(jnp.int32, (16, D), 0) < 8
    edge = jnp.where(top, prev_row, next_row).astype(BF16)
    row8 = lax.broadcasted_iota(jnp.int32, (8, tn), 0)
    h_ext = jnp.concatenate([h, edge], axis=0)

    def up(c0):
        u_ext = _dot(h_ext, wup_ref[:, c0:c0 + tn])
        return u_ext[:tm], u_ext[tm:]

    def conv(u, ue, c0):
        below = pltpu.roll(u, 1, 0)
        above = pltpu.roll(u, tm - 1, 0)
        below = jnp.concatenate([jnp.where(row8 == 0, ue[0:8], below[0:8]), below[8:]], axis=0)
        above = jnp.concatenate([above[:tm - 8], jnp.where(row8 == 7, ue[8:16], above[tm - 8:])], axis=0)
        cw = cw_ref[:, c0:c0 + tn]
        return cb_ref[:, c0:c0 + tn] + below * cw[0:1] + u * cw[1:2] + above * cw[2:3]

    n_chunks = D_FF // tn
    n_dn = D // dn
    act_prev = act_ref[1 - cur]

    def down(k):
        cols = slice(k * dn, (k + 1) * dn)
        o_ref[0, :, cols] = xmid_ref[0, :, cols] + mod_ref[0, :, cols] * _dot(act_prev, wdn_ref[:, cols])

    pending = [(up(c * tn), up(D_FF + c * tn)) for c in range(FFN_AHEAD)]
    done = 0
    for c in range(n_chunks):
        (ua, uae), (ug, uge) = pending.pop(0)
        if c + FFN_AHEAD < n_chunks:
            pending.append((up((c + FFN_AHEAD) * tn), up(D_FF + (c + FFN_AHEAD) * tn)))
        while done * n_chunks < (c + 1) * n_dn:
            down(done)
            done += 1
        a = conv(ua, uae, c * tn)
        gte = conv(ug, uge, D_FF + c * tn)
        act_ref[cur, :, c * tn:(c + 1) * tn] = (gte * jax.nn.sigmoid(gte) * a).astype(BF16)


def _ffn_call(h2, xmid, g2, wup, cw, cb, wdn, tm, tn, dn):
    B, L, _ = xmid.shape
    nt = L // tm
    n_all = B * nt
    hb = tm // HALO
    nhb = L // HALO

    def tile_in(s):
        t = jnp.minimum(s, n_all - 1)
        return t // nt, t % nt

    def tile_out(s):
        t = jnp.maximum(s - 1, 0)
        return t // nt, t % nt

    def in_spec():
        return pl.BlockSpec((1, tm, D), lambda s: (*tile_in(s), 0))

    def out_spec():
        return pl.BlockSpec((1, tm, D), lambda s: (*tile_out(s), 0))

    def prev_halo(s):
        b, i = tile_in(s)
        return b, jnp.maximum(i * hb - 1, 0), 0

    def next_halo(s):
        b, i = tile_in(s)
        return b, jnp.minimum((i + 1) * hb, nhb - 1), 0

    def const(shape):
        return pl.BlockSpec(shape, lambda s: (0,) * len(shape))

    return pl.pallas_call(
        functools.partial(_ffn_kernel, n_tiles=nt, tn=tn, dn=dn),
        grid=(n_all + 1,),
        in_specs=[in_spec(),
                  pl.BlockSpec((1, HALO, D), prev_halo),
                  pl.BlockSpec((1, HALO, D), next_halo),
                  out_spec(),
                  pl.BlockSpec((1, 1, D), lambda s: (tile_out(s)[0], 0, 0)),
                  const(wup.shape), const(cw.shape), const(cb.shape), const(wdn.shape)],
        out_specs=out_spec(),
        out_shape=jax.ShapeDtypeStruct((B, L, D), F32),
        scratch_shapes=[pltpu.VMEM((2, tm, D_FF), BF16)],
        compiler_params=_cparams(("arbitrary",)),
        name="ffn",
    )(h2, h2, h2, xmid, g2, wup, cw, cb, wdn)


def _pair_perm(n_heads):
    half = ATT_HEAD_DIM // 2
    idx = []
    for p in range(n_heads // 2):
        for sub in range(4):
            head = 2 * p + (sub % 2)
            d0 = (sub // 2) * half
            idx.extend(head * ATT_HEAD_DIM + d0 + e for e in range(half))
    return np.asarray(idx, np.int32)


def _rope_tables(L):
    rows = L // GRID_W
    row = jnp.repeat(jnp.arange(rows, dtype=F32), GRID_W)
    col = jnp.tile(jnp.arange(GRID_W, dtype=F32), rows)
    n_freq = ATT_HEAD_DIM // 4
    inv_freq = ROPE_BASE ** (-jnp.arange(n_freq, dtype=F32) / n_freq)
    ang = jnp.concatenate([row[:, None] * inv_freq, col[:, None] * inv_freq], axis=-1)
    cos = jnp.tile(jnp.cos(ang), (1, 4))
    sin = jnp.tile(jnp.sin(ang), (1, 4))
    sign = jnp.where(jnp.arange(LANES) < LANES // 2, -1.0, 1.0).astype(F32)
    return cos, sin * sign


def kernel(x, c, ctx, c_ctx, w_mod, b_mod, norm1_w, w_in, q_norm_w, k_norm_w, attn_sink, ml_gate_b, ml_norm_w,
           w_branch_att, w_branch_ml, w_out, norm2_w, w_up, conv_w, conv_b, w_down):
    B, L, _ = x.shape
    C = ctx.shape[1]
    assert L % 512 == 0 and C % 256 == 0 and L % GRID_W == 0
    l = 0
    tm_merge = 256

    n_rows = -(-(B + 1) // 8) * 8
    cc = jnp.concatenate([c, c_ctx[None, :], jnp.zeros((n_rows - B - 1, D), F32)], axis=0)
    mod = _mod_call(cc, w_mod[l], b_mod[l][None, :])
    sh1, sc1, g1, sh2, sc2, g2 = [mod[:, k * D:(k + 1) * D] for k in range(6)]
    lat_mod = jnp.stack([sh1[:B], 1.0 + sc1[:B]], axis=1)
    ctx_mod = jnp.stack([sh1[B], 1.0 + sc1[B]], axis=0)[None]
    mod3 = jnp.stack([g1[:B], sh2[:B], 1.0 + sc2[:B]], axis=1)
    g2b = g2[:B][:, None, :]

    w = w_in[l]
    qperm = _pair_perm(ATT_HEADS)
    kperm = _pair_perm(ATT_KV_HEADS)
    def pair_cols(wc, n_heads):
        half = ATT_HEAD_DIM // 2
        wc = wc.reshape(D, n_heads // 2, 2, 2, half).transpose(0, 1, 3, 2, 4)
        return wc.reshape(D, n_heads * ATT_HEAD_DIM)

    w_q = pair_cols(w[:, _O_AQ:_O_AQ + ATT_Q_W], ATT_HEADS)
    w_k = pair_cols(w[:, _O_AK:_O_AK + ATT_KV_W], ATT_KV_HEADS)
    w_g = jnp.pad(w[:, _O_MG:_O_MG + ML_GATE_W], ((0, 0), (0, LANES - ML_GATE_W)))
    w_mk = w[:, _O_MK:_O_MK + ML_QK_W] * (ML_QK_DIM ** -0.5)
    w_p = jnp.concatenate([w_q, w_k, w_mk, w_g], axis=1).astype(BF16)
    w_t = jnp.concatenate([w[:, _O_AV:_O_AV + ATT_KV_W], w[:, _O_MQ:_O_MQ + ML_QK_W], w[:, _O_MV:_O_MV + ML_V_W],
                           w[:, _O_MO:_O_MO + ML_V_W], w[:, _O_GA:_O_GA + D], w[:, _O_GM:_O_GM + D]],
                          axis=1).T.astype(BF16)
    wgt = w[:, _O_MG:_O_MG + ML_GATE_W].T.astype(BF16)

    head_of_col = np.concatenate([qperm // ATT_HEAD_DIM, ATT_HEADS + kperm // ATT_HEAD_DIM])
    e_np = (head_of_col[:, None] == np.arange(LANES)[None, :]).astype(np.float32)
    e_mat = jnp.asarray(e_np, BF16)
    et_mat = jnp.asarray(np.concatenate([e_np.T, e_np.T], axis=0), BF16)
    def pair_tiled(wn, n_heads):
        half = ATT_HEAD_DIM // 2
        return jnp.tile(jnp.concatenate([wn[:half], wn[:half], wn[half:], wn[half:]]), n_heads // 2)

    qkw = jnp.concatenate([pair_tiled(q_norm_w[l], ATT_HEADS) * (ATT_SCALE * LOG2E),
                           pair_tiled(k_norm_w[l], ATT_KV_HEADS)])[None, :]
    cos_t, sin_t = _rope_tables(L)
    gb = ml_gate_b[l].reshape(1, ML_GATE_W)
    gbt = ml_gate_b[l].reshape(ML_GATE_W, 1)
    n1w = norm1_w[l][None, :]

    kx_c, mk_c, g_c, bc_c, vt_c, mvt_c, br_c = _ctxproj_call(
        ctx, ctx_mod, n1w, w_p, w_t, wgt, e_mat, et_mat, qkw, gb, gbt, tm=256)
    q, kx, mk, g, bc, vt, mqt, mvt, sot, sgat, sgmt, br = _inproj_call(
        x, lat_mod, n1w, w_p, w_t, wgt, e_mat, et_mat, qkw, cos_t, sin_t, gb, gbt, tm=512)

    att_t = _attn_call(attn_sink[l], q, kx, vt, kx_c, vt_c)
    state = _mlstm_ctx_call(mk_c, mvt_c, g_c, bc_c, br_c)
    hf_t, hb_t = _mlstm_call(state, mqt, mk, mvt, g, bc, br)

    mlw_b = jnp.broadcast_to(ml_norm_w[l][:, None], (ML_V_W, tm_merge))
    xmid, h2 = _merge_call(att_t, hf_t, hb_t, sot, sgat, sgmt, x, mod3, mlw_b, norm2_w[l][None, :],
                           w_branch_att[l].T.astype(BF16), w_branch_ml[l].T.astype(BF16), w_out[l].astype(BF16),
                           tm=tm_merge)
    out = _ffn_call(h2, xmid, g2b, w_up[l].astype(BF16), conv_w[l], conv_b[l][None, :],
                    w_down[l].astype(BF16), tm=512, tn=256, dn=256)
    return out
```

```python
import functools

import jax
import jax.numpy as jnp
import numpy as np
from jax import lax
from jax.experimental import pallas as pl
from jax.experimental.pallas import tpu as pltpu

D = 1024
GRID_W = 64
ATT_HEADS = 16
ATT_KV_HEADS = 4
ATT_HEAD_DIM = 64
ATT_GROUP = ATT_HEADS // ATT_KV_HEADS
ATT_BLOCK = 128
WINDOW = 128
ROPE_BASE = 10000.0
ATT_SCALE = ATT_HEAD_DIM ** -0.5
LOG2E = 1.4426950408889634
ML_HEADS = 4
ML_QK_DIM = 128
ML_V_DIM = 256
ML_CHUNK = 128
D_FF = 2816
EPS = 1e-6
NEG_INF = -1e30

ATT_Q_W = ATT_HEADS * ATT_HEAD_DIM
ATT_KV_W = ATT_KV_HEADS * ATT_HEAD_DIM
ML_QK_W = ML_HEADS * ML_QK_DIM
ML_V_W = ML_HEADS * ML_V_DIM
ML_GATE_W = 2 * 2 * ML_HEADS

LANES = 128
KX_W = ATT_KV_HEADS * 2 * LANES
VMEM_LIMIT = 56 * 1024 * 1024

BF16 = jnp.bfloat16
F32 = jnp.float32

_O_AQ = 0
_O_AK = _O_AQ + ATT_Q_W
_O_AV = _O_AK + ATT_KV_W
_O_MQ = _O_AV + ATT_KV_W
_O_MK = _O_MQ + ML_QK_W
_O_MV = _O_MK + ML_QK_W
_O_MO = _O_MV + ML_V_W
_O_MG = _O_MO + ML_V_W
_O_GA = _O_MG + ML_GATE_W
_O_GM = _O_GA + D

QK_W = ATT_Q_W + ATT_KV_W
_P_QK = 0
_P_MK = _P_QK + QK_W
_P_MG = _P_MK + ML_QK_W
_P_END = _P_MG + LANES
_R_V = 0
_R_MQ = _R_V + ATT_KV_W
_R_MV = _R_MQ + ML_QK_W
_R_MO = _R_MV + ML_V_W
_R_GA = _R_MO + ML_V_W
_R_GM = _R_GA + D
_R_END = _R_GM + D


def _dot(a, b):
    return jnp.dot(a, b, preferred_element_type=F32)


def _dot_nt(a, b):
    return lax.dot_general(a, b, (((1,), (1,)), ((), ())), preferred_element_type=F32)


def _dot_tn(a, b):
    return lax.dot_general(a, b, (((0,), (0,)), ((), ())), preferred_element_type=F32)


def _cparams(sem):
    return pltpu.CompilerParams(dimension_semantics=sem, vmem_limit_bytes=VMEM_LIMIT)


def _mod_kernel(c_ref, w_ref, b_ref, o_ref):
    c = c_ref[...]
    a = c * jax.nn.sigmoid(c)
    o_ref[...] = jnp.dot(a, w_ref[...], preferred_element_type=F32,
                         precision=lax.Precision.HIGHEST) + b_ref[...]


def _mod_call(cc, w_mod, b_mod):
    rows = cc.shape[0]
    n = w_mod.shape[1]
    tn = 1536
    return pl.pallas_call(
        _mod_kernel,
        grid=(n // tn,),
        in_specs=[pl.BlockSpec((rows, D), lambda j: (0, 0)),
                  pl.BlockSpec((D, tn), lambda j: (0, j)),
                  pl.BlockSpec((1, tn), lambda j: (0, j))],
        out_specs=pl.BlockSpec((rows, tn), lambda j: (0, j)),
        out_shape=jax.ShapeDtypeStruct((rows, n), F32),
        compiler_params=_cparams(("arbitrary",)),
        name="mod",
    )(cc, w_mod, b_mod)


def _split2(x):
    x1 = x.astype(BF16)
    x2 = (x - x1.astype(F32)).astype(BF16)
    return x1, x2


def _norm_modulate(x, n1w_ref, mod_ref):
    ms = jnp.mean(x * x, axis=-1, keepdims=True)
    y = x * lax.rsqrt(ms + EPS) * n1w_ref[...]
    return (y * mod_ref[0, 1:2, :] + mod_ref[0, 0:1, :]).astype(BF16)


def _head_rms_scale(ss):
    r = lax.rsqrt(ss * (1.0 / ATT_HEAD_DIM) + EPS)
    r_hi = r.astype(BF16)
    r_lo = (r - r_hi.astype(F32)).astype(BF16)
    return jnp.concatenate([r_hi, r_lo], axis=1)


def _store_k_variants(k_ref, pair, o):
    lane = lax.broadcasted_iota(jnp.int32, (1, LANES), 1)
    keep = ((lane // 32) % 2) == 0
    c0 = 4 * pair * LANES
    k_ref[0, :, c0:c0 + LANES] = jnp.where(keep, o, 0.0).astype(BF16)
    k_ref[0, :, c0 + LANES:c0 + 2 * LANES] = jnp.where(keep, 0.0, pltpu.roll(o, 32, 1)).astype(BF16)
    k_ref[0, :, c0 + 2 * LANES:c0 + 3 * LANES] = jnp.where(keep, pltpu.roll(o, 96, 1), 0.0).astype(BF16)
    k_ref[0, :, c0 + 3 * LANES:c0 + 4 * LANES] = jnp.where(keep, 0.0, o).astype(BF16)


def _cum_gates_cols(g16, tri_lo, tri_up):
    parts = _split2(jax.nn.log_sigmoid(g16))
    fwd_col = lax.broadcasted_iota(jnp.int32, (1, ML_GATE_W), 1) < ML_GATE_W // 2
    return jnp.where(fwd_col, sum(_dot(tri_lo, p) for p in parts), sum(_dot(tri_up, p) for p in parts))


def _cum_gates_rows(gt16, tri_lo, tri_up):
    parts = _split2(jax.nn.log_sigmoid(gt16))
    fwd_row = lax.broadcasted_iota(jnp.int32, (ML_GATE_W, 1), 0) < ML_GATE_W // 2
    return jnp.where(fwd_row, sum(_dot(p, tri_up) for p in parts), sum(_dot(p, tri_lo) for p in parts))


def _inproj_kernel(x_ref, mod_ref, n1w_ref, w_ref, wt_ref, wgt_ref, e_ref, et_ref, qkw_ref,
                   cos_ref, sin_ref, gb_ref, gbt_ref, tril_ref, triu_ref,
                   q_ref, k_ref, mk_ref, g_ref, bc_ref, vt_ref, mqt_ref, mvt_ref, sot_ref, sgat_ref, sgmt_ref, br_ref,
                   hn_ref):
    hn_ref[...] = _norm_modulate(x_ref[0], n1w_ref, mod_ref)
    hn = hn_ref[...]

    acc = _dot(hn, w_ref[:, _P_QK:_P_QK + QK_W])
    g16 = _dot(hn, w_ref[:, _P_MG:_P_MG + LANES])[:, :ML_GATE_W] + gb_ref[...]
    gt16 = _dot_nt(wgt_ref[...], hn) + gbt_ref[...]
    g_ref[0] = g16
    mk_ref[0] = _dot(hn, w_ref[:, _P_MK:_P_MK + ML_QK_W]).astype(BF16)
    ss = _dot((acc * acc).astype(BF16), e_ref[...])
    vt_ref[0] = _dot_nt(wt_ref[_R_V:_R_V + ATT_KV_W, :], hn).astype(BF16)
    mqt_ref[0] = _dot_nt(wt_ref[_R_MQ:_R_MQ + ML_QK_W, :], hn).astype(BF16)
    rb = _dot(_head_rms_scale(ss), et_ref[...])
    mvt_ref[0] = _dot_nt(wt_ref[_R_MV:_R_MV + ML_V_W, :], hn).astype(BF16)
    bc_ref[0] = _cum_gates_cols(g16, tril_ref[...], triu_ref[...])

    qn = acc * rb * qkw_ref[...]
    cos = cos_ref[...]
    sin = sin_ref[...]
    for gi in range(QK_W // LANES):
        xs = qn[:, gi * LANES:(gi + 1) * LANES]
        o = xs * cos + pltpu.roll(xs, LANES // 2, 1) * sin
        if gi < ATT_Q_W // LANES:
            q_ref[0, :, gi * LANES:(gi + 1) * LANES] = o.astype(BF16)
        else:
            _store_k_variants(k_ref, gi - ATT_Q_W // LANES, o)

    sot_ref[0] = jax.nn.sigmoid(_dot_nt(wt_ref[_R_MO:_R_MO + ML_V_W, :], hn)).astype(BF16)
    br_ref[0] = _cum_gates_rows(gt16, tril_ref[...], triu_ref[...])
    sgat_ref[0] = jax.nn.sigmoid(_dot_nt(wt_ref[_R_GA:_R_GA + D, :], hn)).astype(BF16)
    sgmt_ref[0] = jax.nn.sigmoid(_dot_nt(wt_ref[_R_GM:_R_GM + D, :], hn)).astype(BF16)


def _ctxproj_kernel(x_ref, mod_ref, n1w_ref, w_ref, wt_ref, wgt_ref, e_ref, et_ref, qkw_ref,
                    gb_ref, gbt_ref, tril_ref, triu_ref,
                    k_ref, mk_ref, g_ref, bc_ref, vt_ref, mvt_ref, br_ref):
    hn = _norm_modulate(x_ref[0], n1w_ref, mod_ref)
    acc = _dot(hn, w_ref[:, _P_QK + ATT_Q_W:_P_QK + QK_W])
    g16 = _dot(hn, w_ref[:, _P_MG:_P_MG + LANES])[:, :ML_GATE_W] + gb_ref[...]
    gt16 = _dot_nt(wgt_ref[...], hn) + gbt_ref[...]
    g_ref[0] = g16
    mk_ref[0] = _dot(hn, w_ref[:, _P_MK:_P_MK + ML_QK_W]).astype(BF16)
    ss = _dot((acc * acc).astype(BF16), e_ref[ATT_Q_W:QK_W, :])
    vt_ref[0] = _dot_nt(wt_ref[_R_V:_R_V + ATT_KV_W, :], hn).astype(BF16)
    rb = _dot(_head_rms_scale(ss), et_ref[:, ATT_Q_W:QK_W])
    mvt_ref[0] = _dot_nt(wt_ref[_R_MV:_R_MV + ML_V_W, :], hn).astype(BF16)
    bc_ref[0] = _cum_gates_cols(g16, tril_ref[...], triu_ref[...])
    kn = acc * rb * qkw_ref[:, ATT_Q_W:QK_W]
    for pair in range(ATT_KV_W // LANES):
        _store_k_variants(k_ref, pair, kn[:, pair * LANES:(pair + 1) * LANES])
    br_ref[0] = _cum_gates_rows(gt16, tril_ref[...], triu_ref[...])


def _block_tri(n, block, lower):
    r = np.arange(n)[:, None]
    c = np.arange(n)[None, :]
    same = (r // block) == (c // block)
    return jnp.asarray(same & ((c <= r) if lower else (c >= r)), BF16)


def _const_spec(shape):
    return pl.BlockSpec(shape, lambda b, i: (0,) * len(shape))


def _inproj_call(x, lat_mod, n1w, w_p, w_t, wgt, e_mat, et_mat, qkw, cos_t, sin_t, gb, gbt, tm):
    B, L, _ = x.shape
    tril = _block_tri(tm, ML_CHUNK, True)
    triu = _block_tri(tm, ML_CHUNK, False)

    def rows(w):
        return pl.BlockSpec((1, tm, w), lambda b, i: (b, i, 0))

    def cols(h):
        return pl.BlockSpec((1, h, tm), lambda b, i: (b, 0, i))

    consts = [n1w, w_p, w_t, wgt, e_mat, et_mat, qkw]
    tail = [gb, gbt, tril, triu]
    in_specs = ([rows(D), pl.BlockSpec((1, 2, D), lambda b, i: (b, 0, 0))]
                + [_const_spec(a.shape) for a in consts]
                + [pl.BlockSpec((tm, LANES), lambda b, i: (i, 0))] * 2
                + [_const_spec(a.shape) for a in tail])
    out_specs = [rows(ATT_Q_W), rows(KX_W), rows(ML_QK_W), rows(ML_GATE_W), rows(ML_GATE_W),
                 cols(ATT_KV_W), cols(ML_QK_W), cols(ML_V_W), cols(ML_V_W), cols(D), cols(D), cols(ML_GATE_W)]
    out_shape = [
        jax.ShapeDtypeStruct((B, L, ATT_Q_W), BF16),
        jax.ShapeDtypeStruct((B, L, KX_W), BF16),
        jax.ShapeDtypeStruct((B, L, ML_QK_W), BF16),
        jax.ShapeDtypeStruct((B, L, ML_GATE_W), F32),
        jax.ShapeDtypeStruct((B, L, ML_GATE_W), F32),
        jax.ShapeDtypeStruct((B, ATT_KV_W, L), BF16),
        jax.ShapeDtypeStruct((B, ML_QK_W, L), BF16),
        jax.ShapeDtypeStruct((B, ML_V_W, L), BF16),
        jax.ShapeDtypeStruct((B, ML_V_W, L), BF16),
        jax.ShapeDtypeStruct((B, D, L), BF16),
        jax.ShapeDtypeStruct((B, D, L), BF16),
        jax.ShapeDtypeStruct((B, ML_GATE_W, L), F32),
    ]
    return pl.pallas_call(
        _inproj_kernel,
        grid=(B, L // tm),
        in_specs=in_specs,
        out_specs=out_specs,
        out_shape=out_shape,
        scratch_shapes=[pltpu.VMEM((tm, D), BF16)],
        compiler_params=_cparams(("arbitrary", "arbitrary")),
        name="inproj",
    )(x, lat_mod, *consts, cos_t, sin_t, *tail)


def _ctxproj_call(ctx, ctx_mod, n1w, w_p, w_t, wgt, e_mat, et_mat, qkw, gb, gbt, tm):
    B, C, _ = ctx.shape
    tril = _block_tri(tm, ML_CHUNK, True)
    triu = _block_tri(tm, ML_CHUNK, False)

    def rows(w):
        return pl.BlockSpec((1, tm, w), lambda b, i: (b, i, 0))

    def cols(h):
        return pl.BlockSpec((1, h, tm), lambda b, i: (b, 0, i))

    consts = [ctx_mod, n1w, w_p, w_t, wgt, e_mat, et_mat, qkw, gb, gbt, tril, triu]
    out_specs = [rows(KX_W), rows(ML_QK_W), rows(ML_GATE_W), rows(ML_GATE_W),
                 cols(ATT_KV_W), cols(ML_V_W), cols(ML_GATE_W)]
    out_shape = [
        jax.ShapeDtypeStruct((B, C, KX_W), BF16),
        jax.ShapeDtypeStruct((B, C, ML_QK_W), BF16),
        jax.ShapeDtypeStruct((B, C, ML_GATE_W), F32),
        jax.ShapeDtypeStruct((B, C, ML_GATE_W), F32),
        jax.ShapeDtypeStruct((B, ATT_KV_W, C), BF16),
        jax.ShapeDtypeStruct((B, ML_V_W, C), BF16),
        jax.ShapeDtypeStruct((B, ML_GATE_W, C), F32),
    ]
    return pl.pallas_call(
        _ctxproj_kernel,
        grid=(B, C // tm),
        in_specs=[rows(D)] + [_const_spec(a.shape) for a in consts],
        out_specs=out_specs,
        out_shape=out_shape,
        compiler_params=_cparams(("arbitrary", "arbitrary")),
        name="ctxproj",
    )(ctx, *consts)


ATT_AHEAD = 5

ATT_QB = 4


def _attn_kernel(sink_ref, q_ref, kc_ref, kp_ref, k0_ref, kn_ref, vc_ref, vp_ref, v0_ref, vn_ref, o_ref,
                 *, n_steps):
    i = pl.program_id(1)
    T = ATT_BLOCK
    hd = ATT_HEAD_DIM
    s_idx = lax.broadcasted_iota(jnp.int32, (T, 2 * T), 0)
    t_idx = lax.broadcasted_iota(jnp.int32, (T, 2 * T), 1) % T
    first = lax.broadcasted_iota(jnp.int32, (1, 2 * T), 1) < T

    k_own = k0_ref[0]
    v_own = v0_ref[0]
    k_blk = [kp_ref[0]] + [k_own[b * T:(b + 1) * T] for b in range(ATT_QB)] + [kn_ref[0]]
    v_blk = [vp_ref[0]] + [v_own[:, b * T:(b + 1) * T] for b in range(ATT_QB)] + [vn_ref[0]]
    ones_rows = jnp.ones((16, 3 * T + kc_ref.shape[1]), BF16)

    def window(qb):
        k_all = jnp.concatenate(k_blk[qb:qb + 3] + [kc_ref[0]], axis=0)
        vt_all = jnp.concatenate(v_blk[qb:qb + 3] + [vc_ref[0]], axis=1)
        ok_prev = (s_idx >= t_idx) & ((i > 0) if qb == 0 else True)
        ok_next = (s_idx <= t_idx) & ((i < n_steps - 1) if qb == ATT_QB - 1 else True)
        return k_all, vt_all, ok_prev, ok_next

    windows = [window(qb) for qb in range(ATT_QB)]
    per_qb = 2 * ATT_KV_HEADS

    def scores(n):
        qb, r = divmod(n, per_qb)
        kh, var = divmod(r, 2)
        q = q_ref[0, qb * T:(qb + 1) * T, :]
        q2 = jnp.concatenate([q[:, (2 * kh) * LANES:(2 * kh + 1) * LANES],
                              q[:, (2 * kh + 1) * LANES:(2 * kh + 2) * LANES]], axis=0)
        kk = windows[qb][0][:, (2 * kh + var) * LANES:(2 * kh + var + 1) * LANES]
        return _dot_nt(kk, q2)

    n_iter = ATT_QB * per_qb
    pending = [scores(n) for n in range(ATT_AHEAD)]
    for n in range(n_iter):
        qb, r = divmod(n, per_qb)
        kh, var = divmod(r, 2)
        _, vt_all, ok_prev, ok_next = windows[qb]
        st = pending.pop(0)
        if n + ATT_AHEAD < n_iter:
            pending.append(scores(n + ATT_AHEAD))
        vt = vt_all[kh * hd:(kh + 1) * hd, :]
        st = jnp.concatenate([jnp.where(ok_prev, st[0:T], NEG_INF), st[T:2 * T],
                              jnp.where(ok_next, st[2 * T:3 * T], NEG_INF), st[3 * T:]], axis=0)
        h0 = ATT_GROUP * kh + var
        h1 = h0 + 2
        sink = jnp.where(first, sink_ref[h0], sink_ref[h1]) * LOG2E
        m = jnp.maximum(jnp.max(st, axis=0, keepdims=True), sink)
        p = jnp.exp2(st - m)
        ot = _dot(jnp.concatenate([vt, ones_rows], axis=0), p.astype(BF16))
        denom = ot[hd:hd + 1, :] + jnp.exp2(sink - m)
        ot = ot[0:hd, :] * (1.0 / denom)
        o_ref[0, h0 * hd:(h0 + 1) * hd, qb * T:(qb + 1) * T] = ot[:, 0:T].astype(BF16)
        o_ref[0, h1 * hd:(h1 + 1) * hd, qb * T:(qb + 1) * T] = ot[:, T:2 * T].astype(BF16)


def _attn_call(sink, q, kx, vt, kx_c, vt_c):
    B, L, _ = q.shape
    C = kx_c.shape[1]
    T = ATT_BLOCK
    nb = L // T
    TQ = ATT_QB * T
    ns = L // TQ

    def edge(i, off):
        return jnp.clip(i * ATT_QB + (off if off < 0 else ATT_QB), 0, nb - 1)

    return pl.pallas_call(
        functools.partial(_attn_kernel, n_steps=ns),
        grid=(B, ns),
        in_specs=[pl.BlockSpec(memory_space=pltpu.SMEM),
                  pl.BlockSpec((1, TQ, ATT_Q_W), lambda b, i: (b, i, 0)),
                  pl.BlockSpec((1, C, KX_W), lambda b, i: (b, 0, 0)),
                  pl.BlockSpec((1, T, KX_W), lambda b, i: (b, edge(i, -1), 0)),
                  pl.BlockSpec((1, TQ, KX_W), lambda b, i: (b, i, 0)),
                  pl.BlockSpec((1, T, KX_W), lambda b, i: (b, edge(i, 1), 0)),
                  pl.BlockSpec((1, ATT_KV_W, C), lambda b, i: (b, 0, 0)),
                  pl.BlockSpec((1, ATT_KV_W, T), lambda b, i: (b, 0, edge(i, -1))),
                  pl.BlockSpec((1, ATT_KV_W, TQ), lambda b, i: (b, 0, i)),
                  pl.BlockSpec((1, ATT_KV_W, T), lambda b, i: (b, 0, edge(i, 1)))],
        out_specs=pl.BlockSpec((1, ATT_Q_W, TQ), lambda b, i: (b, 0, i)),
        out_shape=jax.ShapeDtypeStruct((B, ATT_Q_W, L), BF16),
        compiler_params=_cparams(("arbitrary", "arbitrary")),
        name="attn",
    )(sink, q, kx_c, kx, kx, kx, vt_c, vt, vt, vt)


N_CHAIN = 2 * ML_HEADS


def _mlstm_chunk(dirs, c_ref, n_ref, m_ref, with_h, cps):
    T = ML_CHUNK
    row = lax.broadcasted_iota(jnp.int32, (T, T), 0)
    col = lax.broadcasted_iota(jnp.int32, (T, T), 1)
    ones8 = jnp.ones((8, T), BF16)
    top = lax.broadcasted_iota(jnp.int32, (8, ML_QK_DIM), 0) == 0
    masks = (col >= row, col <= row)
    lasts = (T - 1, 0)

    state = [(c_ref[ci], n_ref[ci], m_ref[ci, 0:1, 0:1]) for ci in range(N_CHAIN)]
    chains = []
    for sub in range(cps):
        for d, (qt_ref, k_ref, vt_ref, g_ref, bc_ref, br_ref, h_ref) in enumerate(dirs):
            sc = sub if d == 0 else cps - 1 - sub
            tok = slice(sc * T, (sc + 1) * T)
            g = g_ref[0, tok, :]
            bc = bc_ref[0, tok, :]
            br = br_ref[0, :, tok]
            for h in range(ML_HEADS):
                ci = d * ML_HEADS + h
                gi = d * 2 * ML_HEADS + h
                fi = gi + ML_HEADS
                k = k_ref[0, tok, h * ML_QK_DIM:(h + 1) * ML_QK_DIM]
                vt = vt_ref[0, h * ML_V_DIM:(h + 1) * ML_V_DIM, tok]
                u_col = g[:, gi:gi + 1] - bc[:, fi:fi + 1]
                b_row = br[fi:fi + 1, :]
                ct_old, n_old, m_old = state[ci]
                qt = st = qn2 = None
                if with_h:
                    qt = qt_ref[0, h * ML_QK_DIM:(h + 1) * ML_QK_DIM, tok]
                    st = _dot(k, qt)
                    n_hi = n_old.astype(BF16)
                    n_lo = (n_old - n_hi.astype(F32)).astype(BF16)
                    qn2 = _dot(jnp.where(top, n_hi, n_lo), qt)
                m_last = jnp.maximum(jnp.max(u_col, axis=0, keepdims=True), m_old)
                decay = jnp.exp(m_old - m_last)
                kw = (k.astype(F32) * jnp.exp(u_col - m_last)).astype(BF16)
                state[ci] = (decay * ct_old + _dot(vt, kw), decay * n_old + _dot(ones8, kw),
                             b_row[:, lasts[d]:lasts[d] + 1] + m_last)
                chains.append((h, h_ref, tok, masks[d], qt, vt, u_col, b_row, m_old, ct_old, st, qn2))

    if with_h:
        for (h, h_ref, tok, mask, qt, vt, u_col, b_row, m_old, ct_old, st, qn2) in chains:
            umat = jnp.where(mask, u_col, -jnp.inf)
            m_row = jnp.maximum(jnp.max(umat, axis=0, keepdims=True), m_old)
            pt = st * jnp.exp(umat - m_row)
            w_int = jnp.exp(m_old - m_row)
            e_row = jnp.exp(-(b_row + m_row))
            nq = jnp.sum(pt, axis=0, keepdims=True) + w_int * (qn2[0:1, :] + qn2[1:2, :])
            den = jnp.maximum(jnp.abs(nq), e_row)
            lhs = jnp.concatenate([vt, ct_old.astype(BF16)], axis=1)
            rhs = jnp.concatenate([pt.astype(BF16), (qt.astype(F32) * w_int).astype(BF16)], axis=0)
            h_ref[0, h * ML_V_DIM:(h + 1) * ML_V_DIM, tok] = (_dot(lhs, rhs) * (1.0 / den)).astype(BF16)

    for ci, (c_new, n_new, m_new) in enumerate(state):
        c_ref[ci] = c_new
        n_ref[ci] = n_new
        m_ref[ci] = jnp.broadcast_to(m_new, (8, LANES))


def _mlstm_ctx_kernel(kf_ref, vtf_ref, gf_ref, bcf_ref, brf_ref, kb_ref, vtb_ref, gb_ref, bcb_ref, brb_ref,
                      c_ref, n_ref, m_ref, *, cps):
    @pl.when(pl.program_id(1) == 0)
    def _():
        c_ref[...] = jnp.zeros_like(c_ref)
        n_ref[...] = jnp.zeros_like(n_ref)
        m_ref[...] = jnp.zeros_like(m_ref)

    dirs = ((None, kf_ref, vtf_ref, gf_ref, bcf_ref, brf_ref, None),
            (None, kb_ref, vtb_ref, gb_ref, bcb_ref, brb_ref, None))
    _mlstm_chunk(dirs, c_ref.at[0], n_ref.at[0], m_ref.at[0], with_h=False, cps=cps)


def _mlstm_kernel(c0_ref, n0_ref, m0_ref, qtf_ref, kf_ref, vtf_ref, gf_ref, bcf_ref, brf_ref,
                  qtb_ref, kb_ref, vtb_ref, gb_ref, bcb_ref, brb_ref, hf_ref, hb_ref, c_ref, n_ref, m_ref, *, cps):
    @pl.when(pl.program_id(1) == 0)
    def _():
        c_ref[...] = c0_ref[0]
        n_ref[...] = n0_ref[0]
        m_ref[...] = m0_ref[0]

    dirs = ((qtf_ref, kf_ref, vtf_ref, gf_ref, bcf_ref, brf_ref, hf_ref),
            (qtb_ref, kb_ref, vtb_ref, gb_ref, bcb_ref, brb_ref, hb_ref))
    _mlstm_chunk(dirs, c_ref, n_ref, m_ref, with_h=True, cps=cps)


def _mlstm_specs(T, order, with_q):
    specs = [
        pl.BlockSpec((1, ML_QK_W, T), lambda b, j: (b, 0, order(j))),
        pl.BlockSpec((1, T, ML_QK_W), lambda b, j: (b, order(j), 0)),
        pl.BlockSpec((1, ML_V_W, T), lambda b, j: (b, 0, order(j))),
        pl.BlockSpec((1, T, ML_GATE_W), lambda b, j: (b, order(j), 0)),
        pl.BlockSpec((1, T, ML_GATE_W), lambda b, j: (b, order(j), 0)),
        pl.BlockSpec((1, ML_GATE_W, T), lambda b, j: (b, 0, order(j))),
    ]
    return specs if with_q else specs[1:]


_STATE_SHAPES = ((N_CHAIN, ML_V_DIM, ML_QK_DIM), (N_CHAIN, 8, ML_QK_DIM), (N_CHAIN, 8, LANES))


ML_CPS = 4
ML_CTX_CPS = 2


def _mlstm_ctx_call(mk, mvt, g, bc, br):
    B, C, _ = mk.shape
    T = ML_CTX_CPS * ML_CHUNK
    nc = C // T
    state_specs = [pl.BlockSpec((1,) + s, lambda b, j: (b, 0, 0, 0)) for s in _STATE_SHAPES]
    return pl.pallas_call(
        functools.partial(_mlstm_ctx_kernel, cps=ML_CTX_CPS),
        grid=(B, nc),
        in_specs=_mlstm_specs(T, lambda j: j, False) + _mlstm_specs(T, lambda j: nc - 1 - j, False),
        out_specs=state_specs,
        out_shape=[jax.ShapeDtypeStruct((B,) + s, F32) for s in _STATE_SHAPES],
        compiler_params=_cparams(("arbitrary", "arbitrary")),
        name="mlstm_ctx",
    )(mk, mvt, g, bc, br, mk, mvt, g, bc, br)


def _mlstm_call(state, mqt, mk, mvt, g, bc, br):
    B, L, _ = mk.shape
    T = ML_CPS * ML_CHUNK
    nc = L // T
    state_specs = [pl.BlockSpec((1,) + s, lambda b, j: (b, 0, 0, 0)) for s in _STATE_SHAPES]
    out_specs = [pl.BlockSpec((1, ML_V_W, T), lambda b, j: (b, 0, j)),
                 pl.BlockSpec((1, ML_V_W, T), lambda b, j: (b, 0, nc - 1 - j))]
    return pl.pallas_call(
        functools.partial(_mlstm_kernel, cps=ML_CPS),
        grid=(B, nc),
        in_specs=state_specs + _mlstm_specs(T, lambda j: j, True) + _mlstm_specs(T, lambda j: nc - 1 - j, True),
        out_specs=out_specs,
        out_shape=[jax.ShapeDtypeStruct((B, ML_V_W, L), BF16)] * 2,
        scratch_shapes=[pltpu.VMEM(s, F32) for s in _STATE_SHAPES],
        compiler_params=_cparams(("arbitrary", "arbitrary")),
        name="mlstm",
    )(*state, mqt, mk, mvt, g, bc, br, mqt, mk, mvt, g, bc, br)


def _merge_kernel(att_ref, hf_ref, hb_ref, so_ref, sga_ref, sgm_ref, x_ref, mod_ref, mlw_ref, n2w_ref,
                  wat_ref, wmt_ref, wo_ref, xmid_ref, h2_ref):
    ht = hf_ref[0].astype(F32) + hb_ref[0].astype(F32)
    parts = []
    for h in range(ML_HEADS):
        seg = ht[h * ML_V_DIM:(h + 1) * ML_V_DIM, :]
        ms = jnp.mean(seg * seg, axis=0, keepdims=True)
        parts.append(seg * lax.rsqrt(ms + EPS))
    ml = (jnp.concatenate(parts, axis=0) * mlw_ref[...] * so_ref[0].astype(F32)).astype(BF16)
    ya = _dot(wat_ref[...], att_ref[0])
    ym = _dot(wmt_ref[...], ml)
    y = (sga_ref[0].astype(F32) * ya + sgm_ref[0].astype(F32) * ym).astype(BF16)
    y2 = _dot_tn(y, wo_ref[...])
    g1 = mod_ref[0, 0:1, :]
    sh2 = mod_ref[0, 1:2, :]
    sc2 = mod_ref[0, 2:3, :]
    xm = x_ref[0] + g1 * y2
    xmid_ref[0] = xm
    ms = jnp.mean(xm * xm, axis=-1, keepdims=True)
    h2 = xm * lax.rsqrt(ms + EPS) * n2w_ref[...]
    h2_ref[0] = (h2 * sc2 + sh2).astype(BF16)


def _merge_call(att_t, hf_t, hb_t, so_t, sga_t, sgm_t, x, mod3, mlw_b, n2w, wat, wmt, wo, tm):
    B, L, _ = x.shape
    nt = L // tm

    def lat(w):
        return pl.BlockSpec((1, tm, w), lambda b, i: (b, i, 0))

    def lat_t(h):
        return pl.BlockSpec((1, h, tm), lambda b, i: (b, 0, i))

    def const(shape):
        return pl.BlockSpec(shape, lambda b, i: (0,) * len(shape))

    return pl.pallas_call(
        _merge_kernel,
        grid=(B, nt),
        in_specs=[lat_t(D), lat_t(D), lat_t(D), lat_t(D), lat_t(D), lat_t(D), lat(D),
                  pl.BlockSpec((1, 3, D), lambda b, i: (b, 0, 0)),
                  const((D, tm)), const((1, D)), const((D, D)), const((D, D)), const((D, D))],
        out_specs=[lat(D), lat(D)],
        out_shape=[jax.ShapeDtypeStruct((B, L, D), F32), jax.ShapeDtypeStruct((B, L, D), BF16)],
        compiler_params=_cparams(("arbitrary", "arbitrary")),
        name="merge",
    )(att_t, hf_t, hb_t, so_t, sga_t, sgm_t, x, mod3, mlw_b, n2w, wat, wmt, wo)


HALO = 16
FFN_AHEAD = 1


def _ffn_kernel(h_ref, hp_ref, hn_ref, xmid_ref, mod_ref, wup_ref, cw_ref, cb_ref, wdn_ref, o_ref,
                act_ref, *, n_tiles, tn, dn):
    s = pl.program_id(0)
    i = jnp.minimum(s, pl.num_programs(0) - 2) % n_tiles
    cur = s % 2
    tm = h_ref.shape[1]

    @pl.when(s == 0)
    def _():
        act_ref[1] = jnp.zeros(act_ref.shape[1:], BF16)

    h = h_ref[0]
    prev_row = jnp.where(i > 0, hp_ref[0].astype(F32)[HALO - 1:HALO, :], 0.0)
    next_row = jnp.where(i < n_tiles - 1, hn_ref[0].astype(F32)[0:1, :], 0.0)
    top = lax.broadcasted_iota(jnp.int32, (16, D), 0) < 8
    edge = jnp.where(top, prev_row, next_row).astype(BF16)
    row8 = lax.broadcasted_iota(jnp.int32, (8, tn), 0)
    h_ext = jnp.concatenate([h, edge], axis=0)

    def up(c0):
        u_ext = _dot(h_ext, wup_ref[:, c0:c0 + tn])
        return u_ext[:tm], u_ext[tm:]

    def conv(u, ue, c0):
        below = pltpu.roll(u, 1, 0)
        above = pltpu.roll(u, tm - 1, 0)
        below = jnp.concatenate([jnp.where(row8 == 0, ue[0:8], below[0:8]), below[8:]], axis=0)
        above = jnp.concatenate([above[:tm - 8], jnp.where(row8 == 7, ue[8:16], above[tm - 8:])], axis=0)
        cw = cw_ref[:, c0:c0 + tn]
        return cb_ref[:, c0:c0 + tn] + below * cw[0:1] + u * cw[1:2] + above * cw[2:3]

    n_chunks = D_FF // tn
    n_dn = D // dn
    act_prev = act_ref[1 - cur]

    def down(k):
        cols = slice(k * dn, (k + 1) * dn)
        o_ref[0, :, cols] = xmid_ref[0, :, cols] + mod_ref[0, :, cols] * _dot(act_prev, wdn_ref[:, cols])

    pending = [(up(c * tn), up(D_FF + c * tn)) for c in range(FFN_AHEAD)]
    done = 0
    for c in range(n_chunks):
        (ua, uae), (ug, uge) = pending.pop(0)
        if c + FFN_AHEAD < n_chunks:
            pending.append((up((c + FFN_AHEAD) * tn), up(D_FF + (c + FFN_AHEAD) * tn)))
        while done * n_chunks < (c + 1) * n_dn:
            down(done)
            done += 1
        a = conv(ua, uae, c * tn)
        gte = conv(ug, uge, D_FF + c * tn)
        act_ref[cur, :, c * tn:(c + 1) * tn] = (gte * jax.nn.sigmoid(gte) * a).astype(BF16)


def _ffn_call(h2, xmid, g2, wup, cw, cb, wdn, tm, tn, dn):
    B, L, _ = xmid.shape
    nt = L // tm
    n_all = B * nt
    hb = tm // HALO
    nhb = L // HALO

    def tile_in(s):
        t = jnp.minimum(s, n_all - 1)
        return t // nt, t % nt

    def tile_out(s):
        t = jnp.maximum(s - 1, 0)
        return t // nt, t % nt

    def in_spec():
        return pl.BlockSpec((1, tm, D), lambda s: (*tile_in(s), 0))

    def out_spec():
        return pl.BlockSpec((1, tm, D), lambda s: (*tile_out(s), 0))

    def prev_halo(s):
        b, i = tile_in(s)
        return b, jnp.maximum(i * hb - 1, 0), 0

    def next_halo(s):
        b, i = tile_in(s)
        return b, jnp.minimum((i + 1) * hb, nhb - 1), 0

    def const(shape):
        return pl.BlockSpec(shape, lambda s: (0,) * len(shape))

    return pl.pallas_call(
        functools.partial(_ffn_kernel, n_tiles=nt, tn=tn, dn=dn),
        grid=(n_all + 1,),
        in_specs=[in_spec(),
                  pl.BlockSpec((1, HALO, D), prev_halo),
                  pl.BlockSpec((1, HALO, D), next_halo),
                  out_spec(),
                  pl.BlockSpec((1, 1, D), lambda s: (tile_out(s)[0], 0, 0)),
                  const(wup.shape), const(cw.shape), const(cb.shape), const(wdn.shape)],
        out_specs=out_spec(),
        out_shape=jax.ShapeDtypeStruct((B, L, D), F32),
        scratch_shapes=[pltpu.VMEM((2, tm, D_FF), BF16)],
        compiler_params=_cparams(("arbitrary",)),
        name="ffn",
    )(h2, h2, h2, xmid, g2, wup, cw, cb, wdn)


def _pair_perm(n_heads):
    half = ATT_HEAD_DIM // 2
    idx = []
    for p in range(n_heads // 2):
        for sub in range(4):
            head = 2 * p + (sub % 2)
            d0 = (sub // 2) * half
            idx.extend(head * ATT_HEAD_DIM + d0 + e for e in range(half))
    return np.asarray(idx, np.int32)


def _rope_tables(L):
    rows = L // GRID_W
    row = jnp.repeat(jnp.arange(rows, dtype=F32), GRID_W)
    col = jnp.tile(jnp.arange(GRID_W, dtype=F32), rows)
    n_freq = ATT_HEAD_DIM // 4
    inv_freq = ROPE_BASE ** (-jnp.arange(n_freq, dtype=F32) / n_freq)
    ang = jnp.concatenate([row[:, None] * inv_freq, col[:, None] * inv_freq], axis=-1)
    cos = jnp.tile(jnp.cos(ang), (1, 4))
    sin = jnp.tile(jnp.sin(ang), (1, 4))
    sign = jnp.where(jnp.arange(LANES) < LANES // 2, -1.0, 1.0).astype(F32)
    return cos, sin * sign


def kernel(x, c, ctx, c_ctx, w_mod, b_mod, norm1_w, w_in, q_norm_w, k_norm_w, attn_sink, ml_gate_b, ml_norm_w,
           w_branch_att, w_branch_ml, w_out, norm2_w, w_up, conv_w, conv_b, w_down):
    B, L, _ = x.shape
    C = ctx.shape[1]
    assert L % 512 == 0 and C % 256 == 0 and L % GRID_W == 0
    l = 0
    tm_merge = 256

    n_rows = -(-(B + 1) // 8) * 8
    cc = jnp.concatenate([c, c_ctx[None, :], jnp.zeros((n_rows - B - 1, D), F32)], axis=0)
    mod = _mod_call(cc, w_mod[l], b_mod[l][None, :])
    sh1, sc1, g1, sh2, sc2, g2 = [mod[:, k * D:(k + 1) * D] for k in range(6)]
    lat_mod = jnp.stack([sh1[:B], 1.0 + sc1[:B]], axis=1)
    ctx_mod = jnp.stack([sh1[B], 1.0 + sc1[B]], axis=0)[None]
    mod3 = jnp.stack([g1[:B], sh2[:B], 1.0 + sc2[:B]], axis=1)
    g2b = g2[:B][:, None, :]

    w = w_in[l]
    qperm = _pair_perm(ATT_HEADS)
    kperm = _pair_perm(ATT_KV_HEADS)
    def pair_cols(wc, n_heads):
        half = ATT_HEAD_DIM // 2
        wc = wc.reshape(D, n_heads // 2, 2, 2, half).transpose(0, 1, 3, 2, 4)
        return wc.reshape(D, n_heads * ATT_HEAD_DIM)

    w_q = pair_cols(w[:, _O_AQ:_O_AQ + ATT_Q_W], ATT_HEADS)
    w_k = pair_cols(w[:, _O_AK:_O_AK + ATT_KV_W], ATT_KV_HEADS)
    w_g = jnp.pad(w[:, _O_MG:_O_MG + ML_GATE_W], ((0, 0), (0, LANES - ML_GATE_W)))
    w_mk = w[:, _O_MK:_O_MK + ML_QK_W] * (ML_QK_DIM ** -0.5)
    w_p = jnp.concatenate([w_q, w_k, w_mk, w_g], axis=1).astype(BF16)
    w_t = jnp.concatenate([w[:, _O_AV:_O_AV + ATT_KV_W], w[:, _O_MQ:_O_MQ + ML_QK_W], w[:, _O_MV:_O_MV + ML_V_W],
                           w[:, _O_MO:_O_MO + ML_V_W], w[:, _O_GA:_O_GA + D], w[:, _O_GM:_O_GM + D]],
                          axis=1).T.astype(BF16)
    wgt = w[:, _O_MG:_O_MG + ML_GATE_W].T.astype(BF16)

    head_of_col = np.concatenate([qperm // ATT_HEAD_DIM, ATT_HEADS + kperm // ATT_HEAD_DIM])
    e_np = (head_of_col[:, None] == np.arange(LANES)[None, :]).astype(np.float32)
    e_mat = jnp.asarray(e_np, BF16)
    et_mat = jnp.asarray(np.concatenate([e_np.T, e_np.T], axis=0), BF16)
    def pair_tiled(wn, n_heads):
        half = ATT_HEAD_DIM // 2
        return jnp.tile(jnp.concatenate([wn[:half], wn[:half], wn[half:], wn[half:]]), n_heads // 2)

    qkw = jnp.concatenate([pair_tiled(q_norm_w[l], ATT_HEADS) * (ATT_SCALE * LOG2E),
                           pair_tiled(k_norm_w[l], ATT_KV_HEADS)])[None, :]
    cos_t, sin_t = _rope_tables(L)
    gb = ml_gate_b[l].reshape(1, ML_GATE_W)
    gbt = ml_gate_b[l].reshape(ML_GATE_W, 1)
    n1w = norm1_w[l][None, :]

    kx_c, mk_c, g_c, bc_c, vt_c, mvt_c, br_c = _ctxproj_call(
        ctx, ctx_mod, n1w, w_p, w_t, wgt, e_mat, et_mat, qkw, gb, gbt, tm=256)
    q, kx, mk, g, bc, vt, mqt, mvt, sot, sgat, sgmt, br = _inproj_call(
        x, lat_mod, n1w, w_p, w_t, wgt, e_mat, et_mat, qkw, cos_t, sin_t, gb, gbt, tm=512)

    att_t = _attn_call(attn_sink[l], q, kx, vt, kx_c, vt_c)
    state = _mlstm_ctx_call(mk_c, mvt_c, g_c, bc_c, br_c)
    hf_t, hb_t = _mlstm_call(state, mqt, mk, mvt, g, bc, br)

    mlw_b = jnp.broadcast_to(ml_norm_w[l][:, None], (ML_V_W, tm_merge))
    xmid, h2 = _merge_call(att_t, hf_t, hb_t, sot, sgat, sgmt, x, mod3, mlw_b, norm2_w[l][None, :],
                           w_branch_att[l].T.astype(BF16), w_branch_ml[l].T.astype(BF16), w_out[l].astype(BF16),
                           tm=tm_merge)
    out = _ffn_call(h2, xmid, g2b, w_up[l].astype(BF16), conv_w[l], conv_b[l][None, :],
                    w_down[l].astype(BF16), tm=512, tn=256, dn=256)
    return out
```

```python
import functools

import jax
import jax.numpy as jnp
import numpy as np
from jax import lax
from jax.experimental import pallas as pl
from jax.experimental.pallas import tpu as pltpu

D = 1024
GRID_W = 64
ATT_HEADS = 16
ATT_KV_HEADS = 4
ATT_HEAD_DIM = 64
ATT_GROUP = ATT_HEADS // ATT_KV_HEADS
ATT_BLOCK = 128
WINDOW = 128
ROPE_BASE = 10000.0
ATT_SCALE = ATT_HEAD_DIM ** -0.5
LOG2E = 1.4426950408889634
ML_HEADS = 4
ML_QK_DIM = 128
ML_V_DIM = 256
ML_CHUNK = 128
D_FF = 2816
EPS = 1e-6
NEG_INF = -1e30

ATT_Q_W = ATT_HEADS * ATT_HEAD_DIM
ATT_KV_W = ATT_KV_HEADS * ATT_HEAD_DIM
ML_QK_W = ML_HEADS * ML_QK_DIM
ML_V_W = ML_HEADS * ML_V_DIM
ML_GATE_W = 2 * 2 * ML_HEADS

LANES = 128
KX_W = ATT_KV_HEADS * 2 * LANES
VMEM_LIMIT = 56 * 1024 * 1024

BF16 = jnp.bfloat16
F32 = jnp.float32

_O_AQ = 0
_O_AK = _O_AQ + ATT_Q_W
_O_AV = _O_AK + ATT_KV_W
_O_MQ = _O_AV + ATT_KV_W
_O_MK = _O_MQ + ML_QK_W
_O_MV = _O_MK + ML_QK_W
_O_MO = _O_MV + ML_V_W
_O_MG = _O_MO + ML_V_W
_O_GA = _O_MG + ML_GATE_W
_O_GM = _O_GA + D

QK_W = ATT_Q_W + ATT_KV_W
_P_QK = 0
_P_MK = _P_QK + QK_W
_P_MG = _P_MK + ML_QK_W
_P_END = _P_MG + LANES
_R_V = 0
_R_MQ = _R_V + ATT_KV_W
_R_MV = _R_MQ + ML_QK_W
_R_MO = _R_MV + ML_V_W
_R_GA = _R_MO + ML_V_W
_R_GM = _R_GA + D
_R_END = _R_GM + D


def _dot(a, b):
    return jnp.dot(a, b, preferred_element_type=F32)


def _dot_nt(a, b):
    return lax.dot_general(a, b, (((1,), (1,)), ((), ())), preferred_element_type=F32)


def _dot_tn(a, b):
    return lax.dot_general(a, b, (((0,), (0,)), ((), ())), preferred_element_type=F32)


def _cparams(sem):
    return pltpu.CompilerParams(dimension_semantics=sem, vmem_limit_bytes=VMEM_LIMIT)


def _mod_kernel(c_ref, w_ref, b_ref, o_ref):
    c = c_ref[...]
    a = c * jax.nn.sigmoid(c)
    o_ref[...] = jnp.dot(a, w_ref[...], preferred_element_type=F32,
                         precision=lax.Precision.HIGHEST) + b_ref[...]


def _mod_call(cc, w_mod, b_mod):
    rows = cc.shape[0]
    n = w_mod.shape[1]
    tn = 1536
    return pl.pallas_call(
        _mod_kernel,
        grid=(n // tn,),
        in_specs=[pl.BlockSpec((rows, D), lambda j: (0, 0)),
                  pl.BlockSpec((D, tn), lambda j: (0, j)),
                  pl.BlockSpec((1, tn), lambda j: (0, j))],
        out_specs=pl.BlockSpec((rows, tn), lambda j: (0, j)),
        out_shape=jax.ShapeDtypeStruct((rows, n), F32),
        compiler_params=_cparams(("arbitrary",)),
        name="mod",
    )(cc, w_mod, b_mod)


def _split2(x):
    x1 = x.astype(BF16)
    x2 = (x - x1.astype(F32)).astype(BF16)
    return x1, x2


def _norm_modulate(x, n1w_ref, mod_ref):
    ms = jnp.mean(x * x, axis=-1, keepdims=True)
    y = x * lax.rsqrt(ms + EPS) * n1w_ref[...]
    return (y * mod_ref[0, 1:2, :] + mod_ref[0, 0:1, :]).astype(BF16)


def _half_sigmoid(half_x):
    return 0.5 * jnp.tanh(half_x) + 0.5


def _head_rms_scale(ss):
    r = lax.rsqrt(ss * (1.0 / ATT_HEAD_DIM) + EPS)
    r_hi = r.astype(BF16)
    r_lo = (r - r_hi.astype(F32)).astype(BF16)
    return jnp.concatenate([r_hi, r_lo], axis=1)


def _store_k_variants(k_ref, pair, o):
    lane = lax.broadcasted_iota(jnp.int32, (1, LANES), 1)
    keep = ((lane // 32) % 2) == 0
    c0 = 4 * pair * LANES
    k_ref[0, :, c0:c0 + LANES] = jnp.where(keep, o, 0.0).astype(BF16)
    k_ref[0, :, c0 + LANES:c0 + 2 * LANES] = jnp.where(keep, 0.0, pltpu.roll(o, 32, 1)).astype(BF16)
    k_ref[0, :, c0 + 2 * LANES:c0 + 3 * LANES] = jnp.where(keep, pltpu.roll(o, 96, 1), 0.0).astype(BF16)
    k_ref[0, :, c0 + 3 * LANES:c0 + 4 * LANES] = jnp.where(keep, 0.0, o).astype(BF16)


def _cum_gates_cols(g16, tri_lo, tri_up):
    parts = _split2(jax.nn.log_sigmoid(g16))
    fwd_col = lax.broadcasted_iota(jnp.int32, (1, ML_GATE_W), 1) < ML_GATE_W // 2
    return jnp.where(fwd_col, sum(_dot(tri_lo, p) for p in parts), sum(_dot(tri_up, p) for p in parts))


def _cum_gates_rows(gt16, tri_lo, tri_up):
    parts = _split2(jax.nn.log_sigmoid(gt16))
    fwd_row = lax.broadcasted_iota(jnp.int32, (ML_GATE_W, 1), 0) < ML_GATE_W // 2
    return jnp.where(fwd_row, sum(_dot(p, tri_up) for p in parts), sum(_dot(p, tri_lo) for p in parts))


def _inproj_kernel(x_ref, mod_ref, n1w_ref, w_ref, wt_ref, wgt_ref, e_ref, et_ref, qkw_ref,
                   cos_ref, sin_ref, gb_ref, gbt_ref, tril_ref, triu_ref,
                   q_ref, k_ref, mk_ref, g_ref, bc_ref, vt_ref, mqt_ref, mvt_ref, sot_ref, sgat_ref, sgmt_ref, br_ref,
                   hn_ref):
    hn_ref[...] = _norm_modulate(x_ref[0], n1w_ref, mod_ref)
    hn = hn_ref[...]

    acc = _dot(hn, w_ref[:, _P_QK:_P_QK + QK_W])
    g16 = _dot(hn, w_ref[:, _P_MG:_P_MG + LANES])[:, :ML_GATE_W] + gb_ref[...]
    gt16 = _dot_nt(wgt_ref[...], hn) + gbt_ref[...]
    g_ref[0] = g16
    mk_ref[0] = _dot(hn, w_ref[:, _P_MK:_P_MK + ML_QK_W]).astype(BF16)
    ss = _dot((acc * acc).astype(BF16), e_ref[...])
    vt_ref[0] = _dot_nt(wt_ref[_R_V:_R_V + ATT_KV_W, :], hn).astype(BF16)
    sot_ref[0] = _half_sigmoid(_dot_nt(wt_ref[_R_MO:_R_MO + ML_V_W, :], hn)).astype(BF16)
    rb = _dot(_head_rms_scale(ss), et_ref[...])
    sgat_ref[0] = _half_sigmoid(_dot_nt(wt_ref[_R_GA:_R_GA + D, :], hn)).astype(BF16)
    bc_ref[0] = _cum_gates_cols(g16, tril_ref[...], triu_ref[...])

    qn = acc * rb * qkw_ref[...]
    cos = cos_ref[...]
    sin = sin_ref[...]
    for gi in range(QK_W // LANES):
        xs = qn[:, gi * LANES:(gi + 1) * LANES]
        o = xs * cos + pltpu.roll(xs, LANES // 2, 1) * sin
        if gi < ATT_Q_W // LANES:
            q_ref[0, :, gi * LANES:(gi + 1) * LANES] = o.astype(BF16)
        else:
            _store_k_variants(k_ref, gi - ATT_Q_W // LANES, o)

    sgmt_ref[0] = _half_sigmoid(_dot_nt(wt_ref[_R_GM:_R_GM + D, :], hn)).astype(BF16)
    br_ref[0] = _cum_gates_rows(gt16, tril_ref[...], triu_ref[...])
    mqt_ref[0] = _dot_nt(wt_ref[_R_MQ:_R_MQ + ML_QK_W, :], hn).astype(BF16)
    mvt_ref[0] = _dot_nt(wt_ref[_R_MV:_R_MV + ML_V_W, :], hn).astype(BF16)


def _ctxproj_kernel(x_ref, mod_ref, n1w_ref, w_ref, wt_ref, wgt_ref, e_ref, et_ref, qkw_ref,
                    gb_ref, gbt_ref, tril_ref, triu_ref,
                    k_ref, mk_ref, g_ref, bc_ref, vt_ref, mvt_ref, br_ref):
    hn = _norm_modulate(x_ref[0], n1w_ref, mod_ref)
    acc = _dot(hn, w_ref[:, _P_QK + ATT_Q_W:_P_QK + QK_W])
    g16 = _dot(hn, w_ref[:, _P_MG:_P_MG + LANES])[:, :ML_GATE_W] + gb_ref[...]
    gt16 = _dot_nt(wgt_ref[...], hn) + gbt_ref[...]
    g_ref[0] = g16
    mk_ref[0] = _dot(hn, w_ref[:, _P_MK:_P_MK + ML_QK_W]).astype(BF16)
    ss = _dot((acc * acc).astype(BF16), e_ref[ATT_Q_W:QK_W, :])
    vt_ref[0] = _dot_nt(wt_ref[_R_V:_R_V + ATT_KV_W, :], hn).astype(BF16)
    rb = _dot(_head_rms_scale(ss), et_ref[:, ATT_Q_W:QK_W])
    mvt_ref[0] = _dot_nt(wt_ref[_R_MV:_R_MV + ML_V_W, :], hn).astype(BF16)
    bc_ref[0] = _cum_gates_cols(g16, tril_ref[...], triu_ref[...])
    kn = acc * rb * qkw_ref[:, ATT_Q_W:QK_W]
    for pair in range(ATT_KV_W // LANES):
        _store_k_variants(k_ref, pair, kn[:, pair * LANES:(pair + 1) * LANES])
    br_ref[0] = _cum_gates_rows(gt16, tril_ref[...], triu_ref[...])


def _block_tri(n, block, lower):
    r = np.arange(n)[:, None]
    c = np.arange(n)[None, :]
    same = (r // block) == (c // block)
    return jnp.asarray(same & ((c <= r) if lower else (c >= r)), BF16)


def _const_spec(shape):
    return pl.BlockSpec(shape, lambda b, i: (0,) * len(shape))


def _inproj_call(x, lat_mod, n1w, w_p, w_t, wgt, e_mat, et_mat, qkw, cos_t, sin_t, gb, gbt, tm):
    B, L, _ = x.shape
    tril = _block_tri(tm, ML_CHUNK, True)
    triu = _block_tri(tm, ML_CHUNK, False)

    def rows(w):
        return pl.BlockSpec((1, tm, w), lambda b, i: (b, i, 0))

    def cols(h):
        return pl.BlockSpec((1, h, tm), lambda b, i: (b, 0, i))

    consts = [n1w, w_p, w_t, wgt, e_mat, et_mat, qkw]
    tail = [gb, gbt, tril, triu]
    in_specs = ([rows(D), pl.BlockSpec((1, 2, D), lambda b, i: (b, 0, 0))]
                + [_const_spec(a.shape) for a in consts]
                + [pl.BlockSpec((tm, LANES), lambda b, i: (i, 0))] * 2
                + [_const_spec(a.shape) for a in tail])
    out_specs = [rows(ATT_Q_W), rows(KX_W), rows(ML_QK_W), rows(ML_GATE_W), rows(ML_GATE_W),
                 cols(ATT_KV_W), cols(ML_QK_W), cols(ML_V_W), cols(ML_V_W), cols(D), cols(D), cols(ML_GATE_W)]
    out_shape = [
        jax.ShapeDtypeStruct((B, L, ATT_Q_W), BF16),
        jax.ShapeDtypeStruct((B, L, KX_W), BF16),
        jax.ShapeDtypeStruct((B, L, ML_QK_W), BF16),
        jax.ShapeDtypeStruct((B, L, ML_GATE_W), F32),
        jax.ShapeDtypeStruct((B, L, ML_GATE_W), F32),
        jax.ShapeDtypeStruct((B, ATT_KV_W, L), BF16),
        jax.ShapeDtypeStruct((B, ML_QK_W, L), BF16),
        jax.ShapeDtypeStruct((B, ML_V_W, L), BF16),
        jax.ShapeDtypeStruct((B, ML_V_W, L), BF16),
        jax.ShapeDtypeStruct((B, D, L), BF16),
        jax.ShapeDtypeStruct((B, D, L), BF16),
        jax.ShapeDtypeStruct((B, ML_GATE_W, L), F32),
    ]
    return pl.pallas_call(
        _inproj_kernel,
        grid=(B, L // tm),
        in_specs=in_specs,
        out_specs=out_specs,
        out_shape=out_shape,
        scratch_shapes=[pltpu.VMEM((tm, D), BF16)],
        compiler_params=_cparams(("arbitrary", "arbitrary")),
        name="inproj",
    )(x, lat_mod, *consts, cos_t, sin_t, *tail)


def _ctxproj_call(ctx, ctx_mod, n1w, w_p, w_t, wgt, e_mat, et_mat, qkw, gb, gbt, tm):
    B, C, _ = ctx.shape
    tril = _block_tri(tm, ML_CHUNK, True)
    triu = _block_tri(tm, ML_CHUNK, False)

    def rows(w):
        return pl.BlockSpec((1, tm, w), lambda b, i: (b, i, 0))

    def cols(h):
        return pl.BlockSpec((1, h, tm), lambda b, i: (b, 0, i))

    consts = [ctx_mod, n1w, w_p, w_t, wgt, e_mat, et_mat, qkw, gb, gbt, tril, triu]
    out_specs = [rows(KX_W), rows(ML_QK_W), rows(ML_GATE_W), rows(ML_GATE_W),
                 cols(ATT_KV_W), cols(ML_V_W), cols(ML_GATE_W)]
    out_shape = [
        jax.ShapeDtypeStruct((B, C, KX_W), BF16),
        jax.ShapeDtypeStruct((B, C, ML_QK_W), BF16),
        jax.ShapeDtypeStruct((B, C, ML_GATE_W), F32),
        jax.ShapeDtypeStruct((B, C, ML_GATE_W), F32),
        jax.ShapeDtypeStruct((B, ATT_KV_W, C), BF16),
        jax.ShapeDtypeStruct((B, ML_V_W, C), BF16),
        jax.ShapeDtypeStruct((B, ML_GATE_W, C), F32),
    ]
    return pl.pallas_call(
        _ctxproj_kernel,
        grid=(B, C // tm),
        in_specs=[rows(D)] + [_const_spec(a.shape) for a in consts],
        out_specs=out_specs,
        out_shape=out_shape,
        compiler_params=_cparams(("arbitrary", "arbitrary")),
        name="ctxproj",
    )(ctx, *consts)


ATT_AHEAD = 5

ATT_QB = 4


def _attn_kernel(sink_ref, q_ref, kc_ref, kp_ref, k0_ref, kn_ref, vc_ref, vp_ref, v0_ref, vn_ref, o_ref,
                 *, n_steps):
    i = pl.program_id(1)
    T = ATT_BLOCK
    hd = ATT_HEAD_DIM
    s_idx = lax.broadcasted_iota(jnp.int32, (T, 2 * T), 0)
    t_idx = lax.broadcasted_iota(jnp.int32, (T, 2 * T), 1) % T
    first = lax.broadcasted_iota(jnp.int32, (1, 2 * T), 1) < T

    k_own = k0_ref[0]
    v_own = v0_ref[0]
    k_blk = [kp_ref[0]] + [k_own[b * T:(b + 1) * T] for b in range(ATT_QB)] + [kn_ref[0]]
    v_blk = [vp_ref[0]] + [v_own[:, b * T:(b + 1) * T] for b in range(ATT_QB)] + [vn_ref[0]]
    ones_rows = jnp.ones((16, 3 * T + kc_ref.shape[1]), BF16)

    def window(qb):
        k_all = jnp.concatenate(k_blk[qb:qb + 3] + [kc_ref[0]], axis=0)
        vt_all = jnp.concatenate(v_blk[qb:qb + 3] + [vc_ref[0]], axis=1)
        ok_prev = (s_idx >= t_idx) & ((i > 0) if qb == 0 else True)
        ok_next = (s_idx <= t_idx) & ((i < n_steps - 1) if qb == ATT_QB - 1 else True)
        return k_all, vt_all, ok_prev, ok_next

    windows = [window(qb) for qb in range(ATT_QB)]
    per_qb = 2 * ATT_KV_HEADS

    def scores(n):
        qb, r = divmod(n, per_qb)
        kh, var = divmod(r, 2)
        q = q_ref[0, qb * T:(qb + 1) * T, :]
        q2 = jnp.concatenate([q[:, (2 * kh) * LANES:(2 * kh + 1) * LANES],
                              q[:, (2 * kh + 1) * LANES:(2 * kh + 2) * LANES]], axis=0)
        kk = windows[qb][0][:, (2 * kh + var) * LANES:(2 * kh + var + 1) * LANES]
        return _dot_nt(kk, q2)

    n_iter = ATT_QB * per_qb
    pending = [scores(n) for n in range(ATT_AHEAD)]
    for n in range(n_iter):
        qb, r = divmod(n, per_qb)
        kh, var = divmod(r, 2)
        _, vt_all, ok_prev, ok_next = windows[qb]
        st = pending.pop(0)
        if n + ATT_AHEAD < n_iter:
            pending.append(scores(n + ATT_AHEAD))
        vt = vt_all[kh * hd:(kh + 1) * hd, :]
        st = jnp.concatenate([jnp.where(ok_prev, st[0:T], NEG_INF), st[T:2 * T],
                              jnp.where(ok_next, st[2 * T:3 * T], NEG_INF), st[3 * T:]], axis=0)
        h0 = ATT_GROUP * kh + var
        h1 = h0 + 2
        sink = jnp.where(first, sink_ref[h0], sink_ref[h1]) * LOG2E
        m = jnp.maximum(jnp.max(st, axis=0, keepdims=True), sink)
        p = jnp.exp2(st - m)
        ot = _dot(jnp.concatenate([vt, ones_rows], axis=0), p.astype(BF16))
        denom = ot[hd:hd + 1, :] + jnp.exp2(sink - m)
        ot = ot[0:hd, :] * (1.0 / denom)
        o_ref[0, h0 * hd:(h0 + 1) * hd, qb * T:(qb + 1) * T] = ot[:, 0:T].astype(BF16)
        o_ref[0, h1 * hd:(h1 + 1) * hd, qb * T:(qb + 1) * T] = ot[:, T:2 * T].astype(BF16)


def _attn_call(sink, q, kx, vt, kx_c, vt_c):
    B, L, _ = q.shape
    C = kx_c.shape[1]
    T = ATT_BLOCK
    nb = L // T
    TQ = ATT_QB * T
    ns = L // TQ

    def edge(i, off):
        return jnp.clip(i * ATT_QB + (off if off < 0 else ATT_QB), 0, nb - 1)

    return pl.pallas_call(
        functools.partial(_attn_kernel, n_steps=ns),
        grid=(B, ns),
        in_specs=[pl.BlockSpec(memory_space=pltpu.SMEM),
                  pl.BlockSpec((1, TQ, ATT_Q_W), lambda b, i: (b, i, 0)),
                  pl.BlockSpec((1, C, KX_W), lambda b, i: (b, 0, 0)),
                  pl.BlockSpec((1, T, KX_W), lambda b, i: (b, edge(i, -1), 0)),
                  pl.BlockSpec((1, TQ, KX_W), lambda b, i: (b, i, 0)),
                  pl.BlockSpec((1, T, KX_W), lambda b, i: (b, edge(i, 1), 0)),
                  pl.BlockSpec((1, ATT_KV_W, C), lambda b, i: (b, 0, 0)),
                  pl.BlockSpec((1, ATT_KV_W, T), lambda b, i: (b, 0, edge(i, -1))),
                  pl.BlockSpec((1, ATT_KV_W, TQ), lambda b, i: (b, 0, i)),
                  pl.BlockSpec((1, ATT_KV_W, T), lambda b, i: (b, 0, edge(i, 1)))],
        out_specs=pl.BlockSpec((1, ATT_Q_W, TQ), lambda b, i: (b, 0, i)),
        out_shape=jax.ShapeDtypeStruct((B, ATT_Q_W, L), BF16),
        compiler_params=_cparams(("arbitrary", "arbitrary")),
        name="attn",
    )(sink, q, kx_c, kx, kx, kx, vt_c, vt, vt, vt)


N_CHAIN = 2 * ML_HEADS


def _mlstm_chunk(dirs, c_ref, n_ref, m_ref, with_h, cps):
    T = ML_CHUNK
    row = lax.broadcasted_iota(jnp.int32, (T, T), 0)
    col = lax.broadcasted_iota(jnp.int32, (T, T), 1)
    ones8 = jnp.ones((8, T), BF16)
    top = lax.broadcasted_iota(jnp.int32, (8, ML_QK_DIM), 0) == 0
    masks = (col >= row, col <= row)
    lasts = (T - 1, 0)

    state = [(c_ref[ci], n_ref[ci], m_ref[ci, 0:1, 0:1]) for ci in range(N_CHAIN)]
    chains = []
    for sub in range(cps):
        for d, (qt_ref, k_ref, vt_ref, g_ref, bc_ref, br_ref, h_ref) in enumerate(dirs):
            sc = sub if d == 0 else cps - 1 - sub
            tok = slice(sc * T, (sc + 1) * T)
            g = g_ref[0, tok, :]
            bc = bc_ref[0, tok, :]
            br = br_ref[0, :, tok]
            for h in range(ML_HEADS):
                ci = d * ML_HEADS + h
                gi = d * 2 * ML_HEADS + h
                fi = gi + ML_HEADS
                k = k_ref[0, tok, h * ML_QK_DIM:(h + 1) * ML_QK_DIM]
                vt = vt_ref[0, h * ML_V_DIM:(h + 1) * ML_V_DIM, tok]
                u_col = g[:, gi:gi + 1] - bc[:, fi:fi + 1]
                b_row = br[fi:fi + 1, :]
                ct_old, n_old, m_old = state[ci]
                qt = st = qn2 = None
                if with_h:
                    qt = qt_ref[0, h * ML_QK_DIM:(h + 1) * ML_QK_DIM, tok]
                    st = _dot(k, qt)
                    n_hi = n_old.astype(BF16)
                    n_lo = (n_old - n_hi.astype(F32)).astype(BF16)
                    qn2 = _dot(jnp.where(top, n_hi, n_lo), qt)
                m_last = jnp.maximum(jnp.max(u_col, axis=0, keepdims=True), m_old)
                decay = jnp.exp(m_old - m_last)
                kw = (k.astype(F32) * jnp.exp(u_col - m_last)).astype(BF16)
                state[ci] = (decay * ct_old + _dot(vt, kw), decay * n_old + _dot(ones8, kw),
                             b_row[:, lasts[d]:lasts[d] + 1] + m_last)
                chains.append((h, h_ref, tok, masks[d], qt, vt, u_col, b_row, m_old, ct_old, st, qn2))

    if with_h:
        for (h, h_ref, tok, mask, qt, vt, u_col, b_row, m_old, ct_old, st, qn2) in chains:
            umat = jnp.where(mask, u_col, -jnp.inf)
            m_row = jnp.maximum(jnp.max(umat, axis=0, keepdims=True), m_old)
            pt = st * jnp.exp(umat - m_row)
            w_int = jnp.exp(m_old - m_row)
            e_row = jnp.exp(-(b_row + m_row))
            nq = jnp.sum(pt, axis=0, keepdims=True) + w_int * (qn2[0:1, :] + qn2[1:2, :])
            den = jnp.maximum(jnp.abs(nq), e_row)
            lhs = jnp.concatenate([vt, ct_old.astype(BF16)], axis=1)
            rhs = jnp.concatenate([pt.astype(BF16), (qt.astype(F32) * w_int).astype(BF16)], axis=0)
            h_ref[0, h * ML_V_DIM:(h + 1) * ML_V_DIM, tok] = (_dot(lhs, rhs) * (1.0 / den)).astype(BF16)

    for ci, (c_new, n_new, m_new) in enumerate(state):
        c_ref[ci] = c_new
        n_ref[ci] = n_new
        m_ref[ci] = jnp.broadcast_to(m_new, (8, LANES))


def _mlstm_ctx_kernel(kf_ref, vtf_ref, gf_ref, bcf_ref, brf_ref, kb_ref, vtb_ref, gb_ref, bcb_ref, brb_ref,
                      c_ref, n_ref, m_ref, *, cps):
    @pl.when(pl.program_id(1) == 0)
    def _():
        c_ref[...] = jnp.zeros_like(c_ref)
        n_ref[...] = jnp.zeros_like(n_ref)
        m_ref[...] = jnp.zeros_like(m_ref)

    dirs = ((None, kf_ref, vtf_ref, gf_ref, bcf_ref, brf_ref, None),
            (None, kb_ref, vtb_ref, gb_ref, bcb_ref, brb_ref, None))
    _mlstm_chunk(dirs, c_ref.at[0], n_ref.at[0], m_ref.at[0], with_h=False, cps=cps)


def _mlstm_kernel(c0_ref, n0_ref, m0_ref, qtf_ref, kf_ref, vtf_ref, gf_ref, bcf_ref, brf_ref,
                  qtb_ref, kb_ref, vtb_ref, gb_ref, bcb_ref, brb_ref, hf_ref, hb_ref, c_ref, n_ref, m_ref, *, cps):
    @pl.when(pl.program_id(1) == 0)
    def _():
        c_ref[...] = c0_ref[0]
        n_ref[...] = n0_ref[0]
        m_ref[...] = m0_ref[0]

    dirs = ((qtf_ref, kf_ref, vtf_ref, gf_ref, bcf_ref, brf_ref, hf_ref),
            (qtb_ref, kb_ref, vtb_ref, gb_ref, bcb_ref, brb_ref, hb_ref))
    _mlstm_chunk(dirs, c_ref, n_ref, m_ref, with_h=True, cps=cps)


def _mlstm_specs(T, order, with_q):
    specs = [
        pl.BlockSpec((1, ML_QK_W, T), lambda b, j: (b, 0, order(j))),
        pl.BlockSpec((1, T, ML_QK_W), lambda b, j: (b, order(j), 0)),
        pl.BlockSpec((1, ML_V_W, T), lambda b, j: (b, 0, order(j))),
        pl.BlockSpec((1, T, ML_GATE_W), lambda b, j: (b, order(j), 0)),
        pl.BlockSpec((1, T, ML_GATE_W), lambda b, j: (b, order(j), 0)),
        pl.BlockSpec((1, ML_GATE_W, T), lambda b, j: (b, 0, order(j))),
    ]
    return specs if with_q else specs[1:]


_STATE_SHAPES = ((N_CHAIN, ML_V_DIM, ML_QK_DIM), (N_CHAIN, 8, ML_QK_DIM), (N_CHAIN, 8, LANES))


ML_CPS = 4
ML_CTX_CPS = 2


def _mlstm_ctx_call(mk, mvt, g, bc, br):
    B, C, _ = mk.shape
    T = ML_CTX_CPS * ML_CHUNK
    nc = C // T
    state_specs = [pl.BlockSpec((1,) + s, lambda b, j: (b, 0, 0, 0)) for s in _STATE_SHAPES]
    return pl.pallas_call(
        functools.partial(_mlstm_ctx_kernel, cps=ML_CTX_CPS),
        grid=(B, nc),
        in_specs=_mlstm_specs(T, lambda j: j, False) + _mlstm_specs(T, lambda j: nc - 1 - j, False),
        out_specs=state_specs,
        out_shape=[jax.ShapeDtypeStruct((B,) + s, F32) for s in _STATE_SHAPES],
        compiler_params=_cparams(("arbitrary", "arbitrary")),
        name="mlstm_ctx",
    )(mk, mvt, g, bc, br, mk, mvt, g, bc, br)


def _mlstm_call(state, mqt, mk, mvt, g, bc, br):
    B, L, _ = mk.shape
    T = ML_CPS * ML_CHUNK
    nc = L // T
    state_specs = [pl.BlockSpec((1,) + s, lambda b, j: (b, 0, 0, 0)) for s in _STATE_SHAPES]
    out_specs = [pl.BlockSpec((1, ML_V_W, T), lambda b, j: (b, 0, j)),
                 pl.BlockSpec((1, ML_V_W, T), lambda b, j: (b, 0, nc - 1 - j))]
    return pl.pallas_call(
        functools.partial(_mlstm_kernel, cps=ML_CPS),
        grid=(B, nc),
        in_specs=state_specs + _mlstm_specs(T, lambda j: j, True) + _mlstm_specs(T, lambda j: nc - 1 - j, True),
        out_specs=out_specs,
        out_shape=[jax.ShapeDtypeStruct((B, ML_V_W, L), BF16)] * 2,
        scratch_shapes=[pltpu.VMEM(s, F32) for s in _STATE_SHAPES],
        compiler_params=_cparams(("arbitrary", "arbitrary")),
        name="mlstm",
    )(*state, mqt, mk, mvt, g, bc, br, mqt, mk, mvt, g, bc, br)


def _merge_kernel(att_ref, hf_ref, hb_ref, so_ref, sga_ref, sgm_ref, x_ref, mod_ref, mlw_ref, n2w_ref,
                  wat_ref, wmt_ref, wo_ref, xmid_ref, h2_ref, y_ref):
    s = pl.program_id(0)
    cur = s % 2

    @pl.when(s == 0)
    def _():
        y_ref[1] = jnp.zeros(y_ref.shape[1:], BF16)

    y2 = _dot_tn(y_ref[1 - cur], wo_ref[...])
    ya = _dot(wat_ref[...], att_ref[0])

    ht = hf_ref[0].astype(F32) + hb_ref[0].astype(F32)
    parts = []
    for h in range(ML_HEADS):
        seg = ht[h * ML_V_DIM:(h + 1) * ML_V_DIM, :]
        ms = jnp.mean(seg * seg, axis=0, keepdims=True)
        parts.append(seg * lax.rsqrt(ms + EPS))
    ml = (jnp.concatenate(parts, axis=0) * mlw_ref[...] * so_ref[0].astype(F32)).astype(BF16)
    ym = _dot(wmt_ref[...], ml)

    g1 = mod_ref[0, 0:1, :]
    sh2 = mod_ref[0, 1:2, :]
    sc2 = mod_ref[0, 2:3, :]
    xm = x_ref[0] + g1 * y2
    xmid_ref[0] = xm
    ms = jnp.mean(xm * xm, axis=-1, keepdims=True)
    h2 = xm * lax.rsqrt(ms + EPS) * n2w_ref[...]
    h2_ref[0] = (h2 * sc2 + sh2).astype(BF16)

    y_ref[cur] = (sga_ref[0].astype(F32) * ya + sgm_ref[0].astype(F32) * ym).astype(BF16)


def _merge_call(att_t, hf_t, hb_t, so_t, sga_t, sgm_t, x, mod3, mlw_b, n2w, wat, wmt, wo, tm):
    B, L, _ = x.shape
    nt = L // tm
    n_all = B * nt

    def tile_in(s):
        t = jnp.minimum(s, n_all - 1)
        return t // nt, t % nt

    def tile_out(s):
        t = jnp.maximum(s - 1, 0)
        return t // nt, t % nt

    def in_t(h):
        return pl.BlockSpec((1, h, tm), lambda s: (tile_in(s)[0], 0, tile_in(s)[1]))

    def out_rows(w):
        return pl.BlockSpec((1, tm, w), lambda s: (*tile_out(s), 0))

    def const(shape):
        return pl.BlockSpec(shape, lambda s: (0,) * len(shape))

    return pl.pallas_call(
        _merge_kernel,
        grid=(n_all + 1,),
        in_specs=[in_t(D), in_t(D), in_t(D), in_t(D), in_t(D), in_t(D), out_rows(D),
                  pl.BlockSpec((1, 3, D), lambda s: (tile_out(s)[0], 0, 0)),
                  const((D, tm)), const((1, D)), const((D, D)), const((D, D)), const((D, D))],
        out_specs=[out_rows(D), out_rows(D)],
        out_shape=[jax.ShapeDtypeStruct((B, L, D), F32), jax.ShapeDtypeStruct((B, L, D), BF16)],
        scratch_shapes=[pltpu.VMEM((2, D, tm), BF16)],
        compiler_params=_cparams(("arbitrary",)),
        name="merge",
    )(att_t, hf_t, hb_t, so_t, sga_t, sgm_t, x, mod3, mlw_b, n2w, wat, wmt, wo)


HALO = 16
FFN_AHEAD = 1


def _ffn_kernel(h_ref, hp_ref, hn_ref, xmid_ref, mod_ref, wup_ref, cw_ref, cb_ref, wdn_ref, o_ref,
                act_ref, *, n_tiles, tn, dn):
    s = pl.program_id(0)
    i = jnp.minimum(s, pl.num_programs(0) - 2) % n_tiles
    cur = s % 2
    tm = h_ref.shape[1]

    @pl.when(s == 0)
    def _():
        act_ref[1] = jnp.zeros(act_ref.shape[1:], BF16)

    h = h_ref[0]
    prev_row = jnp.where(i > 0, hp_ref[0].astype(F32)[HALO - 1:HALO, :], 0.0)
    next_row = jnp.where(i < n_tiles - 1, hn_ref[0].astype(F32)[0:1, :], 0.0)
    top = lax.broadcasted_iota(jnp.int32, (16, D), 0) < 8
    edge = jnp.where(top, prev_row, next_row).astype(BF16)
    row8 = lax.broadcasted_iota(jnp.int32, (8, tn), 0)
    h_ext = jnp.concatenate([h, edge], axis=0)

    def up(c0):
        u_ext = _dot(h_ext, wup_ref[:, c0:c0 + tn])
        return u_ext[:tm], u_ext[tm:]

    def conv(u, ue, c0):
        below = pltpu.roll(u, 1, 0)
        above = pltpu.roll(u, tm - 1, 0)
        below = jnp.concatenate([jnp.where(row8 == 0, ue[0:8], below[0:8]), below[8:]], axis=0)
        above = jnp.concatenate([above[:tm - 8], jnp.where(row8 == 7, ue[8:16], above[tm - 8:])], axis=0)
        cw = cw_ref[:, c0:c0 + tn]
        return cb_ref[:, c0:c0 + tn] + below * cw[0:1] + u * cw[1:2] + above * cw[2:3]

    n_chunks = D_FF // tn
    n_dn = D // dn
    act_prev = act_ref[1 - cur]

    def down(k):
        cols = slice(k * dn, (k + 1) * dn)
        o_ref[0, :, cols] = xmid_ref[0, :, cols] + mod_ref[0, :, cols] * _dot(act_prev, wdn_ref[:, cols])

    pending = [(up(c * tn), up(D_FF + c * tn)) for c in range(FFN_AHEAD)]
    done = 0
    for c in range(n_chunks):
        (ua, uae), (ug, uge) = pending.pop(0)
        if c + FFN_AHEAD < n_chunks:
            pending.append((up((c + FFN_AHEAD) * tn), up(D_FF + (c + FFN_AHEAD) * tn)))
        while done * n_chunks < (c + 1) * n_dn:
            down(done)
            done += 1
        a = conv(ua, uae, c * tn)
        hg = conv(ug, uge, D_FF + c * tn)
        act_ref[cur, :, c * tn:(c + 1) * tn] = ((hg + hg * jnp.tanh(hg)) * a).astype(BF16)


def _ffn_call(h2, xmid, g2, wup, cw, cb, wdn, tm, tn, dn):
    B, L, _ = xmid.shape
    nt = L // tm
    n_all = B * nt
    hb = tm // HALO
    nhb = L // HALO

    def tile_in(s):
        t = jnp.minimum(s, n_all - 1)
        return t // nt, t % nt

    def tile_out(s):
        t = jnp.maximum(s - 1, 0)
        return t // nt, t % nt

    def in_spec():
        return pl.BlockSpec((1, tm, D), lambda s: (*tile_in(s), 0))

    def out_spec():
        return pl.BlockSpec((1, tm, D), lambda s: (*tile_out(s), 0))

    def prev_halo(s):
        b, i = tile_in(s)
        return b, jnp.maximum(i * hb - 1, 0), 0

    def next_halo(s):
        b, i = tile_in(s)
        return b, jnp.minimum((i + 1) * hb, nhb - 1), 0

    def const(shape):
        return pl.BlockSpec(shape, lambda s: (0,) * len(shape))

    return pl.pallas_call(
        functools.partial(_ffn_kernel, n_tiles=nt, tn=tn, dn=dn),
        grid=(n_all + 1,),
        in_specs=[in_spec(),
                  pl.BlockSpec((1, HALO, D), prev_halo),
                  pl.BlockSpec((1, HALO, D), next_halo),
                  out_spec(),
                  pl.BlockSpec((1, 1, D), lambda s: (tile_out(s)[0], 0, 0)),
                  const(wup.shape), const(cw.shape), const(cb.shape), const(wdn.shape)],
        out_specs=out_spec(),
        out_shape=jax.ShapeDtypeStruct((B, L, D), F32),
        scratch_shapes=[pltpu.VMEM((2, tm, D_FF), BF16)],
        compiler_params=_cparams(("arbitrary",)),
        name="ffn",
    )(h2, h2, h2, xmid, g2, wup, cw, cb, wdn)


def _pair_perm(n_heads):
    half = ATT_HEAD_DIM // 2
    idx = []
    for p in range(n_heads // 2):
        for sub in range(4):
            head = 2 * p + (sub % 2)
            d0 = (sub // 2) * half
            idx.extend(head * ATT_HEAD_DIM + d0 + e for e in range(half))
    return np.asarray(idx, np.int32)


def _rope_tables(L):
    rows = L // GRID_W
    row = jnp.repeat(jnp.arange(rows, dtype=F32), GRID_W)
    col = jnp.tile(jnp.arange(GRID_W, dtype=F32), rows)
    n_freq = ATT_HEAD_DIM // 4
    inv_freq = ROPE_BASE ** (-jnp.arange(n_freq, dtype=F32) / n_freq)
    ang = jnp.concatenate([row[:, None] * inv_freq, col[:, None] * inv_freq], axis=-1)
    cos = jnp.tile(jnp.cos(ang), (1, 4))
    sin = jnp.tile(jnp.sin(ang), (1, 4))
    sign = jnp.where(jnp.arange(LANES) < LANES // 2, -1.0, 1.0).astype(F32)
    return cos, sin * sign


def kernel(x, c, ctx, c_ctx, w_mod, b_mod, norm1_w, w_in, q_norm_w, k_norm_w, attn_sink, ml_gate_b, ml_norm_w,
           w_branch_att, w_branch_ml, w_out, norm2_w, w_up, conv_w, conv_b, w_down):
    B, L, _ = x.shape
    C = ctx.shape[1]
    assert L % 512 == 0 and C % 256 == 0 and L % GRID_W == 0
    l = 0
    tm_merge = 512

    n_rows = -(-(B + 1) // 8) * 8
    cc = jnp.concatenate([c, c_ctx[None, :], jnp.zeros((n_rows - B - 1, D), F32)], axis=0)
    mod = _mod_call(cc, w_mod[l], b_mod[l][None, :])
    sh1, sc1, g1, sh2, sc2, g2 = [mod[:, k * D:(k + 1) * D] for k in range(6)]
    lat_mod = jnp.stack([sh1[:B], 1.0 + sc1[:B]], axis=1)
    ctx_mod = jnp.stack([sh1[B], 1.0 + sc1[B]], axis=0)[None]
    mod3 = jnp.stack([g1[:B], sh2[:B], 1.0 + sc2[:B]], axis=1)
    g2b = g2[:B][:, None, :]

    w = w_in[l]
    qperm = _pair_perm(ATT_HEADS)
    kperm = _pair_perm(ATT_KV_HEADS)
    def pair_cols(wc, n_heads):
        half = ATT_HEAD_DIM // 2
        wc = wc.reshape(D, n_heads // 2, 2, 2, half).transpose(0, 1, 3, 2, 4)
        return wc.reshape(D, n_heads * ATT_HEAD_DIM)

    w_q = pair_cols(w[:, _O_AQ:_O_AQ + ATT_Q_W], ATT_HEADS)
    w_k = pair_cols(w[:, _O_AK:_O_AK + ATT_KV_W], ATT_KV_HEADS)
    w_g = jnp.pad(w[:, _O_MG:_O_MG + ML_GATE_W], ((0, 0), (0, LANES - ML_GATE_W)))
    w_mk = w[:, _O_MK:_O_MK + ML_QK_W] * (ML_QK_DIM ** -0.5)
    w_p = jnp.concatenate([w_q, w_k, w_mk, w_g], axis=1).astype(BF16)
    w_t = jnp.concatenate([w[:, _O_AV:_O_AV + ATT_KV_W], w[:, _O_MQ:_O_MQ + ML_QK_W], w[:, _O_MV:_O_MV + ML_V_W],
                           0.5 * w[:, _O_MO:_O_MO + ML_V_W], 0.5 * w[:, _O_GA:_O_GA + D],
                           0.5 * w[:, _O_GM:_O_GM + D]], axis=1).T.astype(BF16)
    wgt = w[:, _O_MG:_O_MG + ML_GATE_W].T.astype(BF16)

    head_of_col = np.concatenate([qperm // ATT_HEAD_DIM, ATT_HEADS + kperm // ATT_HEAD_DIM])
    e_np = (head_of_col[:, None] == np.arange(LANES)[None, :]).astype(np.float32)
    e_mat = jnp.asarray(e_np, BF16)
    et_mat = jnp.asarray(np.concatenate([e_np.T, e_np.T], axis=0), BF16)
    def pair_tiled(wn, n_heads):
        half = ATT_HEAD_DIM // 2
        return jnp.tile(jnp.concatenate([wn[:half], wn[:half], wn[half:], wn[half:]]), n_heads // 2)

    qkw = jnp.concatenate([pair_tiled(q_norm_w[l], ATT_HEADS) * (ATT_SCALE * LOG2E),
                           pair_tiled(k_norm_w[l], ATT_KV_HEADS)])[None, :]
    cos_t, sin_t = _rope_tables(L)
    gb = ml_gate_b[l].reshape(1, ML_GATE_W)
    gbt = ml_gate_b[l].reshape(ML_GATE_W, 1)
    n1w = norm1_w[l][None, :]

    kx_c, mk_c, g_c, bc_c, vt_c, mvt_c, br_c = _ctxproj_call(
        ctx, ctx_mod, n1w, w_p, w_t, wgt, e_mat, et_mat, qkw, gb, gbt, tm=256)
    q, kx, mk, g, bc, vt, mqt, mvt, sot, sgat, sgmt, br = _inproj_call(
        x, lat_mod, n1w, w_p, w_t, wgt, e_mat, et_mat, qkw, cos_t, sin_t, gb, gbt, tm=512)

    att_t = _attn_call(attn_sink[l], q, kx, vt, kx_c, vt_c)
    state = _mlstm_ctx_call(mk_c, mvt_c, g_c, bc_c, br_c)
    hf_t, hb_t = _mlstm_call(state, mqt, mk, mvt, g, bc, br)

    mlw_b = jnp.broadcast_to(ml_norm_w[l][:, None], (ML_V_W, tm_merge))
    xmid, h2 = _merge_call(att_t, hf_t, hb_t, sot, sgat, sgmt, x, mod3, mlw_b, norm2_w[l][None, :],
                           w_branch_att[l].T.astype(BF16), w_branch_ml[l].T.astype(BF16), w_out[l].astype(BF16),
                           tm=tm_merge)
    gate_half = jnp.where(jnp.arange(2 * D_FF) < D_FF, 1.0, 0.5).astype(F32)
    out = _ffn_call(h2, xmid, g2b, w_up[l].astype(BF16), conv_w[l] * gate_half, (conv_b[l] * gate_half)[None, :],
                    w_down[l].astype(BF16), tm=512, tn=256, dn=256)
    return out
```

```python
import functools

import jax
import jax.numpy as jnp
import numpy as np
from jax import lax
from jax.experimental import pallas as pl
from jax.experimental.pallas import tpu as pltpu

D = 1024
GRID_W = 64
ATT_HEADS = 16
ATT_KV_HEADS = 4
ATT_HEAD_DIM = 64
ATT_GROUP = ATT_HEADS // ATT_KV_HEADS
ATT_BLOCK = 128
WINDOW = 128
ROPE_BASE = 10000.0
ATT_SCALE = ATT_HEAD_DIM ** -0.5
LOG2E = 1.4426950408889634
ML_HEADS = 4
ML_QK_DIM = 128
ML_V_DIM = 256
ML_CHUNK = 128
D_FF = 2816
EPS = 1e-6
NEG_INF = -1e30

ATT_Q_W = ATT_HEADS * ATT_HEAD_DIM
ATT_KV_W = ATT_KV_HEADS * ATT_HEAD_DIM
ML_QK_W = ML_HEADS * ML_QK_DIM
ML_V_W = ML_HEADS * ML_V_DIM
ML_GATE_W = 2 * 2 * ML_HEADS

LANES = 128
KX_W = ATT_KV_HEADS * 2 * LANES
VMEM_LIMIT = 56 * 1024 * 1024

BF16 = jnp.bfloat16
F32 = jnp.float32

_O_AQ = 0
_O_AK = _O_AQ + ATT_Q_W
_O_AV = _O_AK + ATT_KV_W
_O_MQ = _O_AV + ATT_KV_W
_O_MK = _O_MQ + ML_QK_W
_O_MV = _O_MK + ML_QK_W
_O_MO = _O_MV + ML_V_W
_O_MG = _O_MO + ML_V_W
_O_GA = _O_MG + ML_GATE_W
_O_GM = _O_GA + D

QK_W = ATT_Q_W + ATT_KV_W
_P_QK = 0
_P_MK = _P_QK + QK_W
_P_MG = _P_MK + ML_QK_W
_P_END = _P_MG + LANES
_R_V = 0
_R_MQ = _R_V + ATT_KV_W
_R_MV = _R_MQ + ML_QK_W
_R_MO = _R_MV + ML_V_W
_R_GA = _R_MO + ML_V_W
_R_GM = _R_GA + D
_R_END = _R_GM + D


def _dot(a, b):
    return jnp.dot(a, b, preferred_element_type=F32)


def _dot_nt(a, b):
    return lax.dot_general(a, b, (((1,), (1,)), ((), ())), preferred_element_type=F32)


def _dot_tn(a, b):
    return lax.dot_general(a, b, (((0,), (0,)), ((), ())), preferred_element_type=F32)


def _cparams(sem):
    return pltpu.CompilerParams(dimension_semantics=sem, vmem_limit_bytes=VMEM_LIMIT)


def _mod_kernel(c_ref, w_ref, b_ref, o_ref):
    c = c_ref[...]
    a = c * jax.nn.sigmoid(c)
    a_hi = a.astype(BF16)
    a_lo = (a - a_hi.astype(F32)).astype(BF16)
    w = w_ref[...]
    w_hi = w.astype(BF16)
    w_lo = (w - w_hi.astype(F32)).astype(BF16)
    rows = a.shape[0]
    both = _dot(jnp.concatenate([a_hi, a_lo], axis=0), w_hi)
    o_ref[0, :, 0, :] = both[:rows] + both[rows:] + _dot(a_hi, w_lo) + b_ref[...]


MOD_SH1, MOD_SC1, MOD_SH2, MOD_SC2, MOD_G1, MOD_G2 = range(6)


def _mod_call(cc, w_mod, b_mod):
    rows = cc.shape[0]
    n_seg = w_mod.shape[1] // D
    assert n_seg == 6

    def out_pos(j):
        return jnp.where(j == 2, MOD_G1, jnp.where((j == 3) | (j == 4), j - 1, j))

    return pl.pallas_call(
        _mod_kernel,
        grid=(n_seg,),
        in_specs=[pl.BlockSpec((rows, D), lambda j: (0, 0)),
                  pl.BlockSpec((D, D), lambda j: (0, j)),
                  pl.BlockSpec((1, D), lambda j: (0, j))],
        out_specs=pl.BlockSpec((1, rows, 1, D), lambda j: (out_pos(j), 0, 0, 0)),
        out_shape=jax.ShapeDtypeStruct((n_seg, rows, 1, D), F32),
        compiler_params=_cparams(("arbitrary",)),
        name="mod",
    )(cc, w_mod, b_mod)


def _mod_spec(seg, n_seg, row_of):
    return pl.BlockSpec((n_seg, 1, 1, D), lambda *ids: (seg // n_seg, row_of(*ids), 0, 0))


def _split2(x):
    x1 = x.astype(BF16)
    x2 = (x - x1.astype(F32)).astype(BF16)
    return x1, x2


def _norm_modulate(x, n1w_ref, mod_ref):
    ms = jnp.mean(x * x, axis=-1, keepdims=True)
    y = x * lax.rsqrt(ms + EPS) * n1w_ref[...]
    return (y * (1.0 + mod_ref[1, 0]) + mod_ref[0, 0]).astype(BF16)


def _half_sigmoid(half_x):
    return 0.5 * jnp.tanh(half_x) + 0.5


def _head_rms_scale(ss):
    r = lax.rsqrt(ss * (1.0 / ATT_HEAD_DIM) + EPS)
    r_hi = r.astype(BF16)
    r_lo = (r - r_hi.astype(F32)).astype(BF16)
    return jnp.concatenate([r_hi, r_lo], axis=1)


def _store_k_variants(k_ref, pair, o):
    lane = lax.broadcasted_iota(jnp.int32, (1, LANES), 1)
    keep = ((lane // 32) % 2) == 0
    c0 = 4 * pair * LANES
    k_ref[0, :, c0:c0 + LANES] = jnp.where(keep, o, 0.0).astype(BF16)
    k_ref[0, :, c0 + LANES:c0 + 2 * LANES] = jnp.where(keep, 0.0, pltpu.roll(o, 32, 1)).astype(BF16)
    k_ref[0, :, c0 + 2 * LANES:c0 + 3 * LANES] = jnp.where(keep, pltpu.roll(o, 96, 1), 0.0).astype(BF16)
    k_ref[0, :, c0 + 3 * LANES:c0 + 4 * LANES] = jnp.where(keep, 0.0, o).astype(BF16)


def _cum_gates_cols(g16, tri_lo, tri_up):
    parts = _split2(jax.nn.log_sigmoid(g16))
    fwd_col = lax.broadcasted_iota(jnp.int32, (1, ML_GATE_W), 1) < ML_GATE_W // 2
    return jnp.where(fwd_col, sum(_dot(tri_lo, p) for p in parts), sum(_dot(tri_up, p) for p in parts))


def _cum_gates_rows(gt16, tri_lo, tri_up):
    parts = _split2(jax.nn.log_sigmoid(gt16))
    fwd_row = lax.broadcasted_iota(jnp.int32, (ML_GATE_W, 1), 0) < ML_GATE_W // 2
    return jnp.where(fwd_row, sum(_dot(p, tri_up) for p in parts), sum(_dot(p, tri_lo) for p in parts))


def _inproj_kernel(x_ref, mod_ref, n1w_ref, w_ref, wt_ref, wgt_ref, e_ref, et_ref, qkw_ref,
                   cos_ref, sin_ref, gb_ref, gbt_ref, tril_ref, triu_ref,
                   q_ref, k_ref, mk_ref, g_ref, bc_ref, vt_ref, mqt_ref, mvt_ref, sot_ref, sgat_ref, sgmt_ref, br_ref,
                   hn_ref):
    hn_ref[...] = _norm_modulate(x_ref[0], n1w_ref, mod_ref)
    hn = hn_ref[...]

    acc = _dot(hn, w_ref[:, _P_QK:_P_QK + QK_W])
    g16 = _dot(hn, w_ref[:, _P_MG:_P_MG + LANES])[:, :ML_GATE_W] + gb_ref[...]
    gt16 = _dot_nt(wgt_ref[...], hn) + gbt_ref[...]
    g_ref[0] = g16
    mk_ref[0] = _dot(hn, w_ref[:, _P_MK:_P_MK + ML_QK_W]).astype(BF16)
    ss = _dot((acc * acc).astype(BF16), e_ref[...])
    vt_ref[0] = _dot_nt(wt_ref[_R_V:_R_V + ATT_KV_W, :], hn).astype(BF16)
    sot_ref[0] = _half_sigmoid(_dot_nt(wt_ref[_R_MO:_R_MO + ML_V_W, :], hn)).astype(BF16)
    rb = _dot(_head_rms_scale(ss), et_ref[...])
    sgat_ref[0] = _half_sigmoid(_dot_nt(wt_ref[_R_GA:_R_GA + D, :], hn)).astype(BF16)
    bc_ref[0] = _cum_gates_cols(g16, tril_ref[...], triu_ref[...])

    qn = acc * rb * qkw_ref[...]
    cos = cos_ref[...]
    sin = sin_ref[...]
    for gi in range(QK_W // LANES):
        xs = qn[:, gi * LANES:(gi + 1) * LANES]
        o = xs * cos + pltpu.roll(xs, LANES // 2, 1) * sin
        if gi < ATT_Q_W // LANES:
            q_ref[0, :, gi * LANES:(gi + 1) * LANES] = o.astype(BF16)
        else:
            _store_k_variants(k_ref, gi - ATT_Q_W // LANES, o)

    sgmt_ref[0] = _half_sigmoid(_dot_nt(wt_ref[_R_GM:_R_GM + D, :], hn)).astype(BF16)
    br_ref[0] = _cum_gates_rows(gt16, tril_ref[...], triu_ref[...])
    mqt_ref[0] = _dot_nt(wt_ref[_R_MQ:_R_MQ + ML_QK_W, :], hn).astype(BF16)
    mvt_ref[0] = _dot_nt(wt_ref[_R_MV:_R_MV + ML_V_W, :], hn).astype(BF16)


def _ctxproj_kernel(x_ref, mod_ref, n1w_ref, w_ref, wt_ref, wgt_ref, e_ref, et_ref, qkw_ref,
                    gb_ref, gbt_ref, tril_ref, triu_ref,
                    k_ref, mk_ref, g_ref, bc_ref, vt_ref, mvt_ref, br_ref):
    hn = _norm_modulate(x_ref[0], n1w_ref, mod_ref)
    acc = _dot(hn, w_ref[:, _P_QK + ATT_Q_W:_P_QK + QK_W])
    g16 = _dot(hn, w_ref[:, _P_MG:_P_MG + LANES])[:, :ML_GATE_W] + gb_ref[...]
    gt16 = _dot_nt(wgt_ref[...], hn) + gbt_ref[...]
    g_ref[0] = g16
    mk_ref[0] = _dot(hn, w_ref[:, _P_MK:_P_MK + ML_QK_W]).astype(BF16)
    ss = _dot((acc * acc).astype(BF16), e_ref[ATT_Q_W:QK_W, :])
    vt_ref[0] = _dot_nt(wt_ref[_R_V:_R_V + ATT_KV_W, :], hn).astype(BF16)
    rb = _dot(_head_rms_scale(ss), et_ref[:, ATT_Q_W:QK_W])
    mvt_ref[0] = _dot_nt(wt_ref[_R_MV:_R_MV + ML_V_W, :], hn).astype(BF16)
    bc_ref[0] = _cum_gates_cols(g16, tril_ref[...], triu_ref[...])
    kn = acc * rb * qkw_ref[:, ATT_Q_W:QK_W]
    for pair in range(ATT_KV_W // LANES):
        _store_k_variants(k_ref, pair, kn[:, pair * LANES:(pair + 1) * LANES])
    br_ref[0] = _cum_gates_rows(gt16, tril_ref[...], triu_ref[...])


def _block_tri(n, block, lower):
    r = np.arange(n)[:, None]
    c = np.arange(n)[None, :]
    same = (r // block) == (c // block)
    return jnp.asarray(same & ((c <= r) if lower else (c >= r)), BF16)


def _const_spec(shape):
    return pl.BlockSpec(shape, lambda b, i: (0,) * len(shape))


def _inproj_call(x, mod6, n1w, w_p, w_t, wgt, e_mat, et_mat, qkw, cos_t, sin_t, gb, gbt, tm):
    B, L, _ = x.shape
    tril = _block_tri(tm, ML_CHUNK, True)
    triu = _block_tri(tm, ML_CHUNK, False)

    def rows(w):
        return pl.BlockSpec((1, tm, w), lambda b, i: (b, i, 0))

    def cols(h):
        return pl.BlockSpec((1, h, tm), lambda b, i: (b, 0, i))

    consts = [n1w, w_p, w_t, wgt, e_mat, et_mat, qkw]
    tail = [gb, gbt, tril, triu]
    in_specs = ([rows(D), _mod_spec(MOD_SH1, 2, lambda b, i: b)]
                + [_const_spec(a.shape) for a in consts]
                + [pl.BlockSpec((tm, LANES), lambda b, i: (i, 0))] * 2
                + [_const_spec(a.shape) for a in tail])
    out_specs = [rows(ATT_Q_W), rows(KX_W), rows(ML_QK_W), rows(ML_GATE_W), rows(ML_GATE_W),
                 cols(ATT_KV_W), cols(ML_QK_W), cols(ML_V_W), cols(ML_V_W), cols(D), cols(D), cols(ML_GATE_W)]
    out_shape = [
        jax.ShapeDtypeStruct((B, L, ATT_Q_W), BF16),
        jax.ShapeDtypeStruct((B, L, KX_W), BF16),
        jax.ShapeDtypeStruct((B, L, ML_QK_W), BF16),
        jax.ShapeDtypeStruct((B, L, ML_GATE_W), F32),
        jax.ShapeDtypeStruct((B, L, ML_GATE_W), F32),
        jax.ShapeDtypeStruct((B, ATT_KV_W, L), BF16),
        jax.ShapeDtypeStruct((B, ML_QK_W, L), BF16),
        jax.ShapeDtypeStruct((B, ML_V_W, L), BF16),
        jax.ShapeDtypeStruct((B, ML_V_W, L), BF16),
        jax.ShapeDtypeStruct((B, D, L), BF16),
        jax.ShapeDtypeStruct((B, D, L), BF16),
        jax.ShapeDtypeStruct((B, ML_GATE_W, L), F32),
    ]
    return pl.pallas_call(
        _inproj_kernel,
        grid=(B, L // tm),
        in_specs=in_specs,
        out_specs=out_specs,
        out_shape=out_shape,
        scratch_shapes=[pltpu.VMEM((tm, D), BF16)],
        compiler_params=_cparams(("arbitrary", "arbitrary")),
        name="inproj",
    )(x, mod6, *consts, cos_t, sin_t, *tail)


def _ctxproj_call(ctx, mod6, ctx_row, n1w, w_p, w_t, wgt, e_mat, et_mat, qkw, gb, gbt, tm):
    B, C, _ = ctx.shape
    tril = _block_tri(tm, ML_CHUNK, True)
    triu = _block_tri(tm, ML_CHUNK, False)

    def rows(w):
        return pl.BlockSpec((1, tm, w), lambda b, i: (b, i, 0))

    def cols(h):
        return pl.BlockSpec((1, h, tm), lambda b, i: (b, 0, i))

    consts = [n1w, w_p, w_t, wgt, e_mat, et_mat, qkw, gb, gbt, tril, triu]
    out_specs = [rows(KX_W), rows(ML_QK_W), rows(ML_GATE_W), rows(ML_GATE_W),
                 cols(ATT_KV_W), cols(ML_V_W), cols(ML_GATE_W)]
    out_shape = [
        jax.ShapeDtypeStruct((B, C, KX_W), BF16),
        jax.ShapeDtypeStruct((B, C, ML_QK_W), BF16),
        jax.ShapeDtypeStruct((B, C, ML_GATE_W), F32),
        jax.ShapeDtypeStruct((B, C, ML_GATE_W), F32),
        jax.ShapeDtypeStruct((B, ATT_KV_W, C), BF16),
        jax.ShapeDtypeStruct((B, ML_V_W, C), BF16),
        jax.ShapeDtypeStruct((B, ML_GATE_W, C), F32),
    ]
    return pl.pallas_call(
        _ctxproj_kernel,
        grid=(B, C // tm),
        in_specs=([rows(D), _mod_spec(MOD_SH1, 2, lambda b, i: ctx_row)]
                  + [_const_spec(a.shape) for a in consts]),
        out_specs=out_specs,
        out_shape=out_shape,
        compiler_params=_cparams(("arbitrary", "arbitrary")),
        name="ctxproj",
    )(ctx, mod6, *consts)


ATT_AHEAD = 5

ATT_QB = 4


def _attn_kernel(sink_ref, q_ref, kc_ref, kp_ref, k0_ref, kn_ref, vc_ref, vp_ref, v0_ref, vn_ref, o_ref,
                 *, n_steps):
    i = pl.program_id(1)
    T = ATT_BLOCK
    hd = ATT_HEAD_DIM
    s_idx = lax.broadcasted_iota(jnp.int32, (T, 2 * T), 0)
    t_idx = lax.broadcasted_iota(jnp.int32, (T, 2 * T), 1) % T
    first = lax.broadcasted_iota(jnp.int32, (1, 2 * T), 1) < T

    k_own = k0_ref[0]
    v_own = v0_ref[0]
    k_blk = [kp_ref[0]] + [k_own[b * T:(b + 1) * T] for b in range(ATT_QB)] + [kn_ref[0]]
    v_blk = [vp_ref[0]] + [v_own[:, b * T:(b + 1) * T] for b in range(ATT_QB)] + [vn_ref[0]]
    ones_rows = jnp.ones((16, 3 * T + kc_ref.shape[1]), BF16)

    def window(qb):
        k_all = jnp.concatenate(k_blk[qb:qb + 3] + [kc_ref[0]], axis=0)
        vt_all = jnp.concatenate(v_blk[qb:qb + 3] + [vc_ref[0]], axis=1)
        ok_prev = (s_idx >= t_idx) & ((i > 0) if qb == 0 else True)
        ok_next = (s_idx <= t_idx) & ((i < n_steps - 1) if qb == ATT_QB - 1 else True)
        return k_all, vt_all, ok_prev, ok_next

    windows = [window(qb) for qb in range(ATT_QB)]
    per_qb = 2 * ATT_KV_HEADS

    def scores(n):
        qb, r = divmod(n, per_qb)
        kh, var = divmod(r, 2)
        q = q_ref[0, qb * T:(qb + 1) * T, :]
        q2 = jnp.concatenate([q[:, (2 * kh) * LANES:(2 * kh + 1) * LANES],
                              q[:, (2 * kh + 1) * LANES:(2 * kh + 2) * LANES]], axis=0)
        kk = windows[qb][0][:, (2 * kh + var) * LANES:(2 * kh + var + 1) * LANES]
        return _dot_nt(kk, q2)

    n_iter = ATT_QB * per_qb
    pending = [scores(n) for n in range(ATT_AHEAD)]
    for n in range(n_iter):
        qb, r = divmod(n, per_qb)
        kh, var = divmod(r, 2)
        _, vt_all, ok_prev, ok_next = windows[qb]
        st = pending.pop(0)
        if n + ATT_AHEAD < n_iter:
            pending.append(scores(n + ATT_AHEAD))
        vt = vt_all[kh * hd:(kh + 1) * hd, :]
        st = jnp.concatenate([jnp.where(ok_prev, st[0:T], NEG_INF), st[T:2 * T],
                              jnp.where(ok_next, st[2 * T:3 * T], NEG_INF), st[3 * T:]], axis=0)
        h0 = ATT_GROUP * kh + var
        h1 = h0 + 2
        sink = jnp.where(first, sink_ref[h0], sink_ref[h1]) * LOG2E
        m = jnp.maximum(jnp.max(st, axis=0, keepdims=True), sink)
        p = jnp.exp2(st - m)
        ot = _dot(jnp.concatenate([vt, ones_rows], axis=0), p.astype(BF16))
        denom = ot[hd:hd + 1, :] + jnp.exp2(sink - m)
        ot = ot[0:hd, :] * (1.0 / denom)
        o_ref[0, h0 * hd:(h0 + 1) * hd, qb * T:(qb + 1) * T] = ot[:, 0:T].astype(BF16)
        o_ref[0, h1 * hd:(h1 + 1) * hd, qb * T:(qb + 1) * T] = ot[:, T:2 * T].astype(BF16)


def _attn_call(sink, q, kx, vt, kx_c, vt_c):
    B, L, _ = q.shape
    C = kx_c.shape[1]
    T = ATT_BLOCK
    nb = L // T
    TQ = ATT_QB * T
    ns = L // TQ

    def edge(i, off):
        return jnp.clip(i * ATT_QB + (off if off < 0 else ATT_QB), 0, nb - 1)

    return pl.pallas_call(
        functools.partial(_attn_kernel, n_steps=ns),
        grid=(B, ns),
        in_specs=[pl.BlockSpec(memory_space=pltpu.SMEM),
                  pl.BlockSpec((1, TQ, ATT_Q_W), lambda b, i: (b, i, 0)),
                  pl.BlockSpec((1, C, KX_W), lambda b, i: (b, 0, 0)),
                  pl.BlockSpec((1, T, KX_W), lambda b, i: (b, edge(i, -1), 0)),
                  pl.BlockSpec((1, TQ, KX_W), lambda b, i: (b, i, 0)),
                  pl.BlockSpec((1, T, KX_W), lambda b, i: (b, edge(i, 1), 0)),
                  pl.BlockSpec((1, ATT_KV_W, C), lambda b, i: (b, 0, 0)),
                  pl.BlockSpec((1, ATT_KV_W, T), lambda b, i: (b, 0, edge(i, -1))),
                  pl.BlockSpec((1, ATT_KV_W, TQ), lambda b, i: (b, 0, i)),
                  pl.BlockSpec((1, ATT_KV_W, T), lambda b, i: (b, 0, edge(i, 1)))],
        out_specs=pl.BlockSpec((1, ATT_Q_W, TQ), lambda b, i: (b, 0, i)),
        out_shape=jax.ShapeDtypeStruct((B, ATT_Q_W, L), BF16),
        compiler_params=_cparams(("arbitrary", "arbitrary")),
        name="attn",
    )(sink, q, kx_c, kx, kx, kx, vt_c, vt, vt, vt)


N_CHAIN = 2 * ML_HEADS


def _mlstm_chunk(dirs, c_ref, n_ref, m_ref, with_h, cps):
    T = ML_CHUNK
    row = lax.broadcasted_iota(jnp.int32, (T, T), 0)
    col = lax.broadcasted_iota(jnp.int32, (T, T), 1)
    ones8 = jnp.ones((8, T), BF16)
    top = lax.broadcasted_iota(jnp.int32, (8, ML_QK_DIM), 0) == 0
    masks = (col >= row, col <= row)
    lasts = (T - 1, 0)

    state = [(c_ref[ci], n_ref[ci], m_ref[ci, 0:1, 0:1]) for ci in range(N_CHAIN)]
    chains = []
    for sub in range(cps):
        for d, (qt_ref, k_ref, vt_ref, g_ref, bc_ref, br_ref, h_ref) in enumerate(dirs):
            sc = sub if d == 0 else cps - 1 - sub
            tok = slice(sc * T, (sc + 1) * T)
            g = g_ref[0, tok, :]
            bc = bc_ref[0, tok, :]
            br = br_ref[0, :, tok]
            for h in range(ML_HEADS):
                ci = d * ML_HEADS + h
                gi = d * 2 * ML_HEADS + h
                fi = gi + ML_HEADS
                k = k_ref[0, tok, h * ML_QK_DIM:(h + 1) * ML_QK_DIM]
                vt = vt_ref[0, h * ML_V_DIM:(h + 1) * ML_V_DIM, tok]
                u_col = g[:, gi:gi + 1] - bc[:, fi:fi + 1]
                b_row = br[fi:fi + 1, :]
                ct_old, n_old, m_old = state[ci]
                qt = st = qn2 = None
                if with_h:
                    qt = qt_ref[0, h * ML_QK_DIM:(h + 1) * ML_QK_DIM, tok]
                    st = _dot(k, qt)
                    n_hi = n_old.astype(BF16)
                    n_lo = (n_old - n_hi.astype(F32)).astype(BF16)
                    qn2 = _dot(jnp.where(top, n_hi, n_lo), qt)
                m_last = jnp.maximum(jnp.max(u_col, axis=0, keepdims=True), m_old)
                decay = jnp.exp(m_old - m_last)
                kw = (k.astype(F32) * jnp.exp(u_col - m_last)).astype(BF16)
                state[ci] = (decay * ct_old + _dot(vt, kw), decay * n_old + _dot(ones8, kw),
                             b_row[:, lasts[d]:lasts[d] + 1] + m_last)
                chains.append((h, h_ref, tok, masks[d], qt, vt, u_col, b_row, m_old, ct_old, st, qn2))

    if with_h:
        for (h, h_ref, tok, mask, qt, vt, u_col, b_row, m_old, ct_old, st, qn2) in chains:
            umat = jnp.where(mask, u_col, -jnp.inf)
            m_row = jnp.maximum(jnp.max(umat, axis=0, keepdims=True), m_old)
            pt = st * jnp.exp(umat - m_row)
            w_int = jnp.exp(m_old - m_row)
            e_row = jnp.exp(-(b_row + m_row))
            nq = jnp.sum(pt, axis=0, keepdims=True) + w_int * (qn2[0:1, :] + qn2[1:2, :])
            den = jnp.maximum(jnp.abs(nq), e_row)
            lhs = jnp.concatenate([vt, ct_old.astype(BF16)], axis=1)
            rhs = jnp.concatenate([pt.astype(BF16), (qt.astype(F32) * w_int).astype(BF16)], axis=0)
            h_ref[0, h * ML_V_DIM:(h + 1) * ML_V_DIM, tok] = (_dot(lhs, rhs) * (1.0 / den)).astype(BF16)

    for ci, (c_new, n_new, m_new) in enumerate(state):
        c_ref[ci] = c_new
        n_ref[ci] = n_new
        m_ref[ci] = jnp.broadcast_to(m_new, (8, LANES))


def _mlstm_ctx_kernel(kf_ref, vtf_ref, gf_ref, bcf_ref, brf_ref, kb_ref, vtb_ref, gb_ref, bcb_ref, brb_ref,
                      c_ref, n_ref, m_ref, *, cps):
    @pl.when(pl.program_id(1) == 0)
    def _():
        c_ref[...] = jnp.zeros_like(c_ref)
        n_ref[...] = jnp.zeros_like(n_ref)
        m_ref[...] = jnp.zeros_like(m_ref)

    dirs = ((None, kf_ref, vtf_ref, gf_ref, bcf_ref, brf_ref, None),
            (None, kb_ref, vtb_ref, gb_ref, bcb_ref, brb_ref, None))
    _mlstm_chunk(dirs, c_ref.at[0], n_ref.at[0], m_ref.at[0], with_h=False, cps=cps)


def _mlstm_kernel(c0_ref, n0_ref, m0_ref, qtf_ref, kf_ref, vtf_ref, gf_ref, bcf_ref, brf_ref,
                  qtb_ref, kb_ref, vtb_ref, gb_ref, bcb_ref, brb_ref, hf_ref, hb_ref, c_ref, n_ref, m_ref, *, cps):
    @pl.when(pl.program_id(1) == 0)
    def _():
        c_ref[...] = c0_ref[0]
        n_ref[...] = n0_ref[0]
        m_ref[...] = m0_ref[0]

    dirs = ((qtf_ref, kf_ref, vtf_ref, gf_ref, bcf_ref, brf_ref, hf_ref),
            (qtb_ref, kb_ref, vtb_ref, gb_ref, bcb_ref, brb_ref, hb_ref))
    _mlstm_chunk(dirs, c_ref, n_ref, m_ref, with_h=True, cps=cps)


def _mlstm_specs(T, order, with_q):
    specs = [
        pl.BlockSpec((1, ML_QK_W, T), lambda b, j: (b, 0, order(j))),
        pl.BlockSpec((1, T, ML_QK_W), lambda b, j: (b, order(j), 0)),
        pl.BlockSpec((1, ML_V_W, T), lambda b, j: (b, 0, order(j))),
        pl.BlockSpec((1, T, ML_GATE_W), lambda b, j: (b, order(j), 0)),
        pl.BlockSpec((1, T, ML_GATE_W), lambda b, j: (b, order(j), 0)),
        pl.BlockSpec((1, ML_GATE_W, T), lambda b, j: (b, 0, order(j))),
    ]
    return specs if with_q else specs[1:]


_STATE_SHAPES = ((N_CHAIN, ML_V_DIM, ML_QK_DIM), (N_CHAIN, 8, ML_QK_DIM), (N_CHAIN, 8, LANES))


ML_CPS = 4
ML_CTX_CPS = 2


def _mlstm_ctx_call(mk, mvt, g, bc, br):
    B, C, _ = mk.shape
    T = ML_CTX_CPS * ML_CHUNK
    nc = C // T
    state_specs = [pl.BlockSpec((1,) + s, lambda b, j: (b, 0, 0, 0)) for s in _STATE_SHAPES]
    return pl.pallas_call(
        functools.partial(_mlstm_ctx_kernel, cps=ML_CTX_CPS),
        grid=(B, nc),
        in_specs=_mlstm_specs(T, lambda j: j, False) + _mlstm_specs(T, lambda j: nc - 1 - j, False),
        out_specs=state_specs,
        out_shape=[jax.ShapeDtypeStruct((B,) + s, F32) for s in _STATE_SHAPES],
        compiler_params=_cparams(("arbitrary", "arbitrary")),
        name="mlstm_ctx",
    )(mk, mvt, g, bc, br, mk, mvt, g, bc, br)


def _mlstm_call(state, mqt, mk, mvt, g, bc, br):
    B, L, _ = mk.shape
    T = ML_CPS * ML_CHUNK
    nc = L // T
    state_specs = [pl.BlockSpec((1,) + s, lambda b, j: (b, 0, 0, 0)) for s in _STATE_SHAPES]
    out_specs = [pl.BlockSpec((1, ML_V_W, T), lambda b, j: (b, 0, j)),
                 pl.BlockSpec((1, ML_V_W, T), lambda b, j: (b, 0, nc - 1 - j))]
    return pl.pallas_call(
        functools.partial(_mlstm_kernel, cps=ML_CPS),
        grid=(B, nc),
        in_specs=state_specs + _mlstm_specs(T, lambda j: j, True) + _mlstm_specs(T, lambda j: nc - 1 - j, True),
        out_specs=out_specs,
        out_shape=[jax.ShapeDtypeStruct((B, ML_V_W, L), BF16)] * 2,
        scratch_shapes=[pltpu.VMEM(s, F32) for s in _STATE_SHAPES],
        compiler_params=_cparams(("arbitrary", "arbitrary")),
        name="mlstm",
    )(*state, mqt, mk, mvt, g, bc, br, mqt, mk, mvt, g, bc, br)


def _merge_kernel(att_ref, hf_ref, hb_ref, so_ref, sga_ref, sgm_ref, x_ref, g1_ref, mod2_ref, mlw_ref, n2w_ref,
                  wat_ref, wmt_ref, wo_ref, xmid_ref, h2_ref, y_ref):
    s = pl.program_id(0)
    last = pl.num_programs(0) - 1
    cur = s % 2

    def body(do_branch, do_out):
        if do_out:
            y2 = _dot_tn(y_ref[1 - cur], wo_ref[...])
        if do_branch:
            ya = _dot(wat_ref[...], att_ref[0])
            ht = hf_ref[0].astype(F32) + hb_ref[0].astype(F32)
            parts = []
            for h in range(ML_HEADS):
                seg = ht[h * ML_V_DIM:(h + 1) * ML_V_DIM, :]
                ms = jnp.mean(seg * seg, axis=0, keepdims=True)
                parts.append(seg * lax.rsqrt(ms + EPS))
            ml = (jnp.concatenate(parts, axis=0) * mlw_ref[...] * so_ref[0].astype(F32)).astype(BF16)
            ym = _dot(wmt_ref[...], ml)
        if do_out:
            xm = x_ref[0] + g1_ref[0, 0] * y2
            xmid_ref[0] = xm
            ms = jnp.mean(xm * xm, axis=-1, keepdims=True)
            h2 = xm * lax.rsqrt(ms + EPS) * n2w_ref[...]
            h2_ref[0] = (h2 * (1.0 + mod2_ref[1, 0]) + mod2_ref[0, 0]).astype(BF16)
        if do_branch:
            y_ref[cur] = (sga_ref[0].astype(F32) * ya + sgm_ref[0].astype(F32) * ym).astype(BF16)

    pl.when(s == 0)(lambda: body(True, False))
    pl.when((s > 0) & (s < last))(lambda: body(True, True))
    pl.when(s == last)(lambda: body(False, True))


def _merge_call(att_t, hf_t, hb_t, so_t, sga_t, sgm_t, x, mod6, mlw_b, n2w, wat, wmt, wo, tm):
    B, L, _ = x.shape
    nt = L // tm
    n_all = B * nt

    def tile_in(s):
        t = jnp.minimum(s, n_all - 1)
        return t // nt, t % nt

    def tile_out(s):
        t = jnp.maximum(s - 1, 0)
        return t // nt, t % nt

    def in_t(h):
        return pl.BlockSpec((1, h, tm), lambda s: (tile_in(s)[0], 0, tile_in(s)[1]))

    def out_rows(w):
        return pl.BlockSpec((1, tm, w), lambda s: (*tile_out(s), 0))

    def const(shape):
        return pl.BlockSpec(shape, lambda s: (0,) * len(shape))

    return pl.pallas_call(
        _merge_kernel,
        grid=(n_all + 1,),
        in_specs=[in_t(D), in_t(D), in_t(D), in_t(D), in_t(D), in_t(D), out_rows(D),
                  _mod_spec(MOD_G1, 1, lambda s: tile_out(s)[0]),
                  _mod_spec(MOD_SH2, 2, lambda s: tile_out(s)[0]),
                  const((D, tm)), const((1, D)), const((D, D)), const((D, D)), const((D, D))],
        out_specs=[out_rows(D), out_rows(D)],
        out_shape=[jax.ShapeDtypeStruct((B, L, D), F32), jax.ShapeDtypeStruct((B, L, D), BF16)],
        scratch_shapes=[pltpu.VMEM((2, D, tm), BF16)],
        compiler_params=_cparams(("arbitrary",)),
        name="merge",
    )(att_t, hf_t, hb_t, so_t, sga_t, sgm_t, x, mod6, mod6, mlw_b, n2w, wat, wmt, wo)


HALO = 16
FFN_AHEAD = 1


def _ffn_kernel(h_ref, hp_ref, hn_ref, xmid_ref, mod_ref, wup_ref, cw_ref, cb_ref, wdn_ref, o_ref,
                act_ref, *, n_tiles, tn, dn):
    s = pl.program_id(0)
    last = pl.num_programs(0) - 1

    @pl.when(s == 0)
    def _():
        _ffn_body(h_ref, hp_ref, hn_ref, xmid_ref, mod_ref, wup_ref, cw_ref, cb_ref, wdn_ref, o_ref, act_ref,
                  n_tiles=n_tiles, tn=tn, dn=dn, do_up=True, do_down=False)

    @pl.when((s > 0) & (s < last))
    def _():
        _ffn_body(h_ref, hp_ref, hn_ref, xmid_ref, mod_ref, wup_ref, cw_ref, cb_ref, wdn_ref, o_ref, act_ref,
                  n_tiles=n_tiles, tn=tn, dn=dn, do_up=True, do_down=True)

    @pl.when(s == last)
    def _():
        _ffn_body(h_ref, hp_ref, hn_ref, xmid_ref, mod_ref, wup_ref, cw_ref, cb_ref, wdn_ref, o_ref, act_ref,
                  n_tiles=n_tiles, tn=tn, dn=dn, do_up=False, do_down=True)


def _ffn_body(h_ref, hp_ref, hn_ref, xmid_ref, mod_ref, wup_ref, cw_ref, cb_ref, wdn_ref, o_ref, act_ref,
              *, n_tiles, tn, dn, do_up, do_down):
    s = pl.program_id(0)
    i = s % n_tiles
    cur = s % 2
    tm = h_ref.shape[1]
    n_chunks = D_FF // tn
    n_dn = D // dn
    act_prev = act_ref[1 - cur] if do_down else None

    def down(k):
        cols = slice(k * dn, (k + 1) * dn)
        o_ref[0, :, cols] = xmid_ref[0, :, cols] + mod_ref[0, 0, :, cols] * _dot(act_prev, wdn_ref[:, cols])

    if not do_up:
        for k in range(n_dn):
            down(k)
        return

    h = h_ref[0]
    prev_row = jnp.where(i > 0, hp_ref[0].astype(F32)[HALO - 1:HALO, :], 0.0)
    next_row = jnp.where(i < n_tiles - 1, hn_ref[0].astype(F32)[0:1, :], 0.0)
    top = lax.broadcasted_iota(jnp.int32, (16, D), 0) < 8
    edge = jnp.where(top, prev_row, next_row).astype(BF16)
    row8 = lax.broadcasted_iota(jnp.int32, (8, tn), 0)
    h_ext = jnp.concatenate([h, edge], axis=0)

    def up(c0):
        u_ext = _dot(h_ext, wup_ref[:, c0:c0 + tn])
        return u_ext[:tm], u_ext[tm:]

    def conv(u, ue, c0):
        below = pltpu.roll(u, 1, 0)
        above = pltpu.roll(u, tm - 1, 0)
        below = jnp.concatenate([jnp.where(row8 == 0, ue[0:8], below[0:8]), below[8:]], axis=0)
        above = jnp.concatenate([above[:tm - 8], jnp.where(row8 == 7, ue[8:16], above[tm - 8:])], axis=0)
        cw = cw_ref[:, c0:c0 + tn]
        return cb_ref[:, c0:c0 + tn] + below * cw[0:1] + u * cw[1:2] + above * cw[2:3]

    pending = [(up(c * tn), up(D_FF + c * tn)) for c in range(FFN_AHEAD)]
    done = 0
    for c in range(n_chunks):
        (ua, uae), (ug, uge) = pending.pop(0)
        if c + FFN_AHEAD < n_chunks:
            pending.append((up((c + FFN_AHEAD) * tn), up(D_FF + (c + FFN_AHEAD) * tn)))
        while do_down and done * n_chunks < (c + 1) * n_dn:
            down(done)
            done += 1
        a = conv(ua, uae, c * tn)
        hg = conv(ug, uge, D_FF + c * tn)
        act_ref[cur, :, c * tn:(c + 1) * tn] = ((hg + hg * jnp.tanh(hg)) * a).astype(BF16)


def _ffn_call(h2, xmid, g2, wup, cw, cb, wdn, tm, tn, dn):
    B, L, _ = xmid.shape
    nt = L // tm
    n_all = B * nt
    hb = tm // HALO
    nhb = L // HALO

    def tile_in(s):
        t = jnp.minimum(s, n_all - 1)
        return t // nt, t % nt

    def tile_out(s):
        t = jnp.maximum(s - 1, 0)
        return t // nt, t % nt

    def in_spec():
        return pl.BlockSpec((1, tm, D), lambda s: (*tile_in(s), 0))

    def out_spec():
        return pl.BlockSpec((1, tm, D), lambda s: (*tile_out(s), 0))

    def prev_halo(s):
        b, i = tile_in(s)
        return b, jnp.maximum(i * hb - 1, 0), 0

    def next_halo(s):
        b, i = tile_in(s)
        return b, jnp.minimum((i + 1) * hb, nhb - 1), 0

    def const(shape):
        return pl.BlockSpec(shape, lambda s: (0,) * len(shape))

    return pl.pallas_call(
        functools.partial(_ffn_kernel, n_tiles=nt, tn=tn, dn=dn),
        grid=(n_all + 1,),
        in_specs=[in_spec(),
                  pl.BlockSpec((1, HALO, D), prev_halo),
                  pl.BlockSpec((1, HALO, D), next_halo),
                  out_spec(),
                  _mod_spec(MOD_G2, 1, lambda s: tile_out(s)[0]),
                  const(wup.shape), const(cw.shape), const(cb.shape), const(wdn.shape)],
        out_specs=out_spec(),
        out_shape=jax.ShapeDtypeStruct((B, L, D), F32),
        scratch_shapes=[pltpu.VMEM((2, tm, D_FF), BF16)],
        compiler_params=_cparams(("arbitrary",)),
        name="ffn",
    )(h2, h2, h2, xmid, g2, wup, cw, cb, wdn)


def _pair_perm(n_heads):
    half = ATT_HEAD_DIM // 2
    idx = []
    for p in range(n_heads // 2):
        for sub in range(4):
            head = 2 * p + (sub % 2)
            d0 = (sub // 2) * half
            idx.extend(head * ATT_HEAD_DIM + d0 + e for e in range(half))
    return np.asarray(idx, np.int32)


def _rope_tables(L):
    f32 = np.float32
    rows = L // GRID_W
    row = np.repeat(np.arange(rows, dtype=f32), GRID_W)
    col = np.tile(np.arange(GRID_W, dtype=f32), rows)
    n_freq = ATT_HEAD_DIM // 4
    inv_freq = (f32(ROPE_BASE) ** (-np.arange(n_freq, dtype=f32) / f32(n_freq))).astype(f32)
    ang = np.concatenate([row[:, None] * inv_freq, col[:, None] * inv_freq], axis=-1).astype(f32)
    cos = np.tile(np.cos(ang).astype(f32), (1, 4))
    sin = np.tile(np.sin(ang).astype(f32), (1, 4))
    sign = np.where(np.arange(LANES) < LANES // 2, -1.0, 1.0).astype(f32)
    return jnp.asarray(cos), jnp.asarray(sin * sign)


def kernel(x, c, ctx, c_ctx, w_mod, b_mod, norm1_w, w_in, q_norm_w, k_norm_w, attn_sink, ml_gate_b, ml_norm_w,
           w_branch_att, w_branch_ml, w_out, norm2_w, w_up, conv_w, conv_b, w_down):
    B, L, _ = x.shape
    C = ctx.shape[1]
    assert L % 512 == 0 and C % 256 == 0 and L % GRID_W == 0
    l = 0
    tm_merge = 512

    n_rows = -(-(B + 1) // 16) * 16
    cc = jnp.concatenate([c, c_ctx[None, :], jnp.zeros((n_rows - B - 1, D), F32)], axis=0)
    mod6 = _mod_call(cc, w_mod[l], b_mod[l][None, :])

    w = w_in[l]
    qperm = _pair_perm(ATT_HEADS)
    kperm = _pair_perm(ATT_KV_HEADS)
    def pair_cols(wc, n_heads):
        half = ATT_HEAD_DIM // 2
        wc = wc.reshape(D, n_heads // 2, 2, 2, half).transpose(0, 1, 3, 2, 4)
        return wc.reshape(D, n_heads * ATT_HEAD_DIM)

    w_q = pair_cols(w[:, _O_AQ:_O_AQ + ATT_Q_W], ATT_HEADS)
    w_k = pair_cols(w[:, _O_AK:_O_AK + ATT_KV_W], ATT_KV_HEADS)
    w_g = jnp.pad(w[:, _O_MG:_O_MG + ML_GATE_W], ((0, 0), (0, LANES - ML_GATE_W)))
    w_mk = w[:, _O_MK:_O_MK + ML_QK_W] * (ML_QK_DIM ** -0.5)
    w_p = jnp.concatenate([w_q, w_k, w_mk, w_g], axis=1).astype(BF16)
    w_t = jnp.concatenate([w[:, _O_AV:_O_AV + ATT_KV_W], w[:, _O_MQ:_O_MQ + ML_QK_W], w[:, _O_MV:_O_MV + ML_V_W],
                           0.5 * w[:, _O_MO:_O_MO + ML_V_W], 0.5 * w[:, _O_GA:_O_GA + D],
                           0.5 * w[:, _O_GM:_O_GM + D]], axis=1).T.astype(BF16)
    wgt = w[:, _O_MG:_O_MG + ML_GATE_W].T.astype(BF16)

    head_of_col = np.concatenate([qperm // ATT_HEAD_DIM, ATT_HEADS + kperm // ATT_HEAD_DIM])
    e_np = (head_of_col[:, None] == np.arange(LANES)[None, :]).astype(np.float32)
    e_mat = jnp.asarray(e_np, BF16)
    et_mat = jnp.asarray(np.concatenate([e_np.T, e_np.T], axis=0), BF16)
    def pair_tiled(wn, n_heads):
        half = ATT_HEAD_DIM // 2
        return jnp.tile(jnp.concatenate([wn[:half], wn[:half], wn[half:], wn[half:]]), n_heads // 2)

    qkw = jnp.concatenate([pair_tiled(q_norm_w[l], ATT_HEADS) * (ATT_SCALE * LOG2E),
                           pair_tiled(k_norm_w[l], ATT_KV_HEADS)])[None, :]
    cos_t, sin_t = _rope_tables(L)
    gb = ml_gate_b[l].reshape(1, ML_GATE_W)
    gbt = ml_gate_b[l].reshape(ML_GATE_W, 1)
    n1w = norm1_w[l][None, :]

    kx_c, mk_c, g_c, bc_c, vt_c, mvt_c, br_c = _ctxproj_call(
        ctx, mod6, B, n1w, w_p, w_t, wgt, e_mat, et_mat, qkw, gb, gbt, tm=256)
    q, kx, mk, g, bc, vt, mqt, mvt, sot, sgat, sgmt, br = _inproj_call(
        x, mod6, n1w, w_p, w_t, wgt, e_mat, et_mat, qkw, cos_t, sin_t, gb, gbt, tm=512)

    att_t = _attn_call(attn_sink[l], q, kx, vt, kx_c, vt_c)
    state = _mlstm_ctx_call(mk_c, mvt_c, g_c, bc_c, br_c)
    hf_t, hb_t = _mlstm_call(state, mqt, mk, mvt, g, bc, br)

    mlw_b = jnp.broadcast_to(ml_norm_w[l][:, None], (ML_V_W, tm_merge))
    xmid, h2 = _merge_call(att_t, hf_t, hb_t, sot, sgat, sgmt, x, mod6, mlw_b, norm2_w[l][None, :],
                           w_branch_att[l].T.astype(BF16), w_branch_ml[l].T.astype(BF16), w_out[l].astype(BF16),
                           tm=tm_merge)
    gate_half = jnp.where(jnp.arange(2 * D_FF) < D_FF, 1.0, 0.5).astype(F32)
    out = _ffn_call(h2, xmid, mod6, w_up[l].astype(BF16), conv_w[l] * gate_half, (conv_b[l] * gate_half)[None, :],
                    w_down[l].astype(BF16), tm=512, tn=256, dn=256)
    return out
```

```python
import functools

import jax
import jax.numpy as jnp
import numpy as np
from jax import lax
from jax.experimental import pallas as pl
from jax.experimental.pallas import tpu as pltpu

D = 1024
GRID_W = 64
ATT_HEADS = 16
ATT_KV_HEADS = 4
ATT_HEAD_DIM = 64
ATT_GROUP = ATT_HEADS // ATT_KV_HEADS
ATT_BLOCK = 128
WINDOW = 128
ROPE_BASE = 10000.0
ATT_SCALE = ATT_HEAD_DIM ** -0.5
LOG2E = 1.4426950408889634
ML_HEADS = 4
ML_QK_DIM = 128
ML_V_DIM = 256
ML_CHUNK = 128
D_FF = 2816
EPS = 1e-6
NEG_INF = -1e30

ATT_Q_W = ATT_HEADS * ATT_HEAD_DIM
ATT_KV_W = ATT_KV_HEADS * ATT_HEAD_DIM
ML_QK_W = ML_HEADS * ML_QK_DIM
ML_V_W = ML_HEADS * ML_V_DIM
ML_GATE_W = 2 * 2 * ML_HEADS

LANES = 128
KX_W = ATT_KV_HEADS * 2 * LANES
VMEM_LIMIT = 56 * 1024 * 1024

BF16 = jnp.bfloat16
F32 = jnp.float32

_O_AQ = 0
_O_AK = _O_AQ + ATT_Q_W
_O_AV = _O_AK + ATT_KV_W
_O_MQ = _O_AV + ATT_KV_W
_O_MK = _O_MQ + ML_QK_W
_O_MV = _O_MK + ML_QK_W
_O_MO = _O_MV + ML_V_W
_O_MG = _O_MO + ML_V_W
_O_GA = _O_MG + ML_GATE_W
_O_GM = _O_GA + D

QK_W = ATT_Q_W + ATT_KV_W
_P_QK = 0
_P_MK = _P_QK + QK_W
_P_MG = _P_MK + ML_QK_W
_P_END = _P_MG + LANES
_R_V = 0
_R_MQ = _R_V + ATT_KV_W
_R_MV = _R_MQ + ML_QK_W
_R_MO = _R_MV + ML_V_W
_R_GA = _R_MO + ML_V_W
_R_GM = _R_GA + D
_R_END = _R_GM + D


def _dot(a, b):
    return jnp.dot(a, b, preferred_element_type=F32)


def _dot_nt(a, b):
    return lax.dot_general(a, b, (((1,), (1,)), ((), ())), preferred_element_type=F32)


def _dot_tn(a, b):
    return lax.dot_general(a, b, (((0,), (0,)), ((), ())), preferred_element_type=F32)


def _cparams(sem):
    return pltpu.CompilerParams(dimension_semantics=sem, vmem_limit_bytes=VMEM_LIMIT)


def _mod_kernel(c_ref, w_ref, b_ref, o_ref):
    c = c_ref[...]
    a = c * jax.nn.sigmoid(c)
    a_hi = a.astype(BF16)
    a_lo = (a - a_hi.astype(F32)).astype(BF16)
    w = w_ref[...]
    w_hi = w.astype(BF16)
    w_lo = (w - w_hi.astype(F32)).astype(BF16)
    rows = a.shape[0]
    both = _dot(jnp.concatenate([a_hi, a_lo], axis=0), w_hi)
    o_ref[0, :, 0, :] = both[:rows] + both[rows:] + _dot(a_hi, w_lo) + b_ref[...]


MOD_SH1, MOD_SC1, MOD_SH2, MOD_SC2, MOD_G1, MOD_G2 = range(6)


def _mod_call(cc, w_mod, b_mod):
    rows = cc.shape[0]
    n_seg = w_mod.shape[1] // D
    assert n_seg == 6

    def out_pos(j):
        return jnp.where(j == 2, MOD_G1, jnp.where((j == 3) | (j == 4), j - 1, j))

    return pl.pallas_call(
        _mod_kernel,
        grid=(n_seg,),
        in_specs=[pl.BlockSpec((rows, D), lambda j: (0, 0)),
                  pl.BlockSpec((D, D), lambda j: (0, j)),
                  pl.BlockSpec((1, D), lambda j: (0, j))],
        out_specs=pl.BlockSpec((1, rows, 1, D), lambda j: (out_pos(j), 0, 0, 0)),
        out_shape=jax.ShapeDtypeStruct((n_seg, rows, 1, D), F32),
        compiler_params=_cparams(("arbitrary",)),
        name="mod",
    )(cc, w_mod, b_mod)


def _mod_spec(seg, n_seg, row_of):
    return pl.BlockSpec((n_seg, 1, 1, D), lambda *ids: (seg // n_seg, row_of(*ids), 0, 0))


def _split2(x):
    x1 = x.astype(BF16)
    x2 = (x - x1.astype(F32)).astype(BF16)
    return x1, x2


def _norm_modulate(x, n1w_ref, mod_ref):
    ms = jnp.mean(x * x, axis=-1, keepdims=True)
    y = x * lax.rsqrt(ms + EPS) * n1w_ref[...]
    return (y * (1.0 + mod_ref[1, 0]) + mod_ref[0, 0]).astype(BF16)


def _half_sigmoid(half_x):
    return 0.5 * jnp.tanh(half_x) + 0.5


def _head_rms_scale(ss):
    r = lax.rsqrt(ss * (1.0 / ATT_HEAD_DIM) + EPS)
    r_hi = r.astype(BF16)
    r_lo = (r - r_hi.astype(F32)).astype(BF16)
    return jnp.concatenate([r_hi, r_lo], axis=1)


def _store_k_variants(k_ref, pair, o):
    lane = lax.broadcasted_iota(jnp.int32, (1, LANES), 1)
    keep = ((lane // 32) % 2) == 0
    c0 = 4 * pair * LANES
    k_ref[0, :, c0:c0 + LANES] = jnp.where(keep, o, 0.0).astype(BF16)
    k_ref[0, :, c0 + LANES:c0 + 2 * LANES] = jnp.where(keep, 0.0, pltpu.roll(o, 32, 1)).astype(BF16)
    k_ref[0, :, c0 + 2 * LANES:c0 + 3 * LANES] = jnp.where(keep, pltpu.roll(o, 96, 1), 0.0).astype(BF16)
    k_ref[0, :, c0 + 3 * LANES:c0 + 4 * LANES] = jnp.where(keep, 0.0, o).astype(BF16)


def _cum_gates_cols(g16, tri_lo, tri_up):
    parts = _split2(jax.nn.log_sigmoid(g16))
    fwd_col = lax.broadcasted_iota(jnp.int32, (1, ML_GATE_W), 1) < ML_GATE_W // 2
    return jnp.where(fwd_col, sum(_dot(tri_lo, p) for p in parts), sum(_dot(tri_up, p) for p in parts))


def _cum_gates_rows(gt16, tri_lo, tri_up):
    parts = _split2(jax.nn.log_sigmoid(gt16))
    fwd_row = lax.broadcasted_iota(jnp.int32, (ML_GATE_W, 1), 0) < ML_GATE_W // 2
    return jnp.where(fwd_row, sum(_dot(p, tri_up) for p in parts), sum(_dot(p, tri_lo) for p in parts))


def _inproj_kernel(x_ref, mod_ref, n1w_ref, w_ref, wt_ref, wgt_ref, e_ref, et_ref, qkw_ref,
                   cos_ref, sin_ref, gb_ref, gbt_ref, tril_ref, triu_ref,
                   q_ref, k_ref, mk_ref, g_ref, bc_ref, vt_ref, mqt_ref, mvt_ref, sot_ref, sgat_ref, sgmt_ref, br_ref,
                   hn_ref):
    hn_ref[...] = _norm_modulate(x_ref[0], n1w_ref, mod_ref)
    hn = hn_ref[...]

    acc = _dot(hn, w_ref[:, _P_QK:_P_QK + QK_W])
    g16 = _dot(hn, w_ref[:, _P_MG:_P_MG + LANES])[:, :ML_GATE_W] + gb_ref[...]
    gt16 = _dot_nt(wgt_ref[...], hn) + gbt_ref[...]
    g_ref[0] = g16
    mk_ref[0] = _dot(hn, w_ref[:, _P_MK:_P_MK + ML_QK_W]).astype(BF16)
    ss = _dot((acc * acc).astype(BF16), e_ref[...])
    vt_ref[0] = _dot_nt(wt_ref[_R_V:_R_V + ATT_KV_W, :], hn).astype(BF16)
    sot_ref[0] = _half_sigmoid(_dot_nt(wt_ref[_R_MO:_R_MO + ML_V_W, :], hn)).astype(BF16)
    rb = _dot(_head_rms_scale(ss), et_ref[...])
    sgat_ref[0] = _half_sigmoid(_dot_nt(wt_ref[_R_GA:_R_GA + D, :], hn)).astype(BF16)
    bc_ref[0] = _cum_gates_cols(g16, tril_ref[...], triu_ref[...])

    qn = acc * rb * qkw_ref[...]
    cos = cos_ref[...]
    sin = sin_ref[...]
    for gi in range(QK_W // LANES):
        xs = qn[:, gi * LANES:(gi + 1) * LANES]
        o = xs * cos + pltpu.roll(xs, LANES // 2, 1) * sin
        if gi < ATT_Q_W // LANES:
            q_ref[0, :, gi * LANES:(gi + 1) * LANES] = o.astype(BF16)
        else:
            _store_k_variants(k_ref, gi - ATT_Q_W // LANES, o)

    sgmt_ref[0] = _half_sigmoid(_dot_nt(wt_ref[_R_GM:_R_GM + D, :], hn)).astype(BF16)
    br_ref[0] = _cum_gates_rows(gt16, tril_ref[...], triu_ref[...])
    mqt_ref[0] = _dot_nt(wt_ref[_R_MQ:_R_MQ + ML_QK_W, :], hn).astype(BF16)
    mvt_ref[0] = _dot_nt(wt_ref[_R_MV:_R_MV + ML_V_W, :], hn).astype(BF16)


def _ctxproj_kernel(x_ref, mod_ref, n1w_ref, w_ref, wt_ref, wgt_ref, e_ref, et_ref, qkw_ref,
                    gb_ref, gbt_ref, tril_ref, triu_ref,
                    k_ref, mk_ref, g_ref, bc_ref, vt_ref, mvt_ref, br_ref):
    hn = _norm_modulate(x_ref[0], n1w_ref, mod_ref)
    acc = _dot(hn, w_ref[:, _P_QK + ATT_Q_W:_P_QK + QK_W])
    g16 = _dot(hn, w_ref[:, _P_MG:_P_MG + LANES])[:, :ML_GATE_W] + gb_ref[...]
    gt16 = _dot_nt(wgt_ref[...], hn) + gbt_ref[...]
    g_ref[0] = g16
    mk_ref[0] = _dot(hn, w_ref[:, _P_MK:_P_MK + ML_QK_W]).astype(BF16)
    ss = _dot((acc * acc).astype(BF16), e_ref[ATT_Q_W:QK_W, :])
    vt_ref[0] = _dot_nt(wt_ref[_R_V:_R_V + ATT_KV_W, :], hn).astype(BF16)
    rb = _dot(_head_rms_scale(ss), et_ref[:, ATT_Q_W:QK_W])
    mvt_ref[0] = _dot_nt(wt_ref[_R_MV:_R_MV + ML_V_W, :], hn).astype(BF16)
    bc_ref[0] = _cum_gates_cols(g16, tril_ref[...], triu_ref[...])
    kn = acc * rb * qkw_ref[:, ATT_Q_W:QK_W]
    for pair in range(ATT_KV_W // LANES):
        _store_k_variants(k_ref, pair, kn[:, pair * LANES:(pair + 1) * LANES])
    br_ref[0] = _cum_gates_rows(gt16, tril_ref[...], triu_ref[...])


def _block_tri(n, block, lower):
    r = np.arange(n)[:, None]
    c = np.arange(n)[None, :]
    same = (r // block) == (c // block)
    return jnp.asarray(same & ((c <= r) if lower else (c >= r)), BF16)


def _const_spec(shape):
    return pl.BlockSpec(shape, lambda b, i: (0,) * len(shape))


def _inproj_call(x, mod6, n1w, w_p, w_t, wgt, e_mat, et_mat, qkw, cos_t, sin_t, gb, gbt, tm):
    B, L, _ = x.shape
    tril = _block_tri(tm, ML_CHUNK, True)
    triu = _block_tri(tm, ML_CHUNK, False)

    def rows(w):
        return pl.BlockSpec((1, tm, w), lambda b, i: (b, i, 0))

    def cols(h):
        return pl.BlockSpec((1, h, tm), lambda b, i: (b, 0, i))

    consts = [n1w, w_p, w_t, wgt, e_mat, et_mat, qkw]
    tail = [gb, gbt, tril, triu]
    in_specs = ([rows(D), _mod_spec(MOD_SH1, 2, lambda b, i: b)]
                + [_const_spec(a.shape) for a in consts]
                + [pl.BlockSpec((tm, LANES), lambda b, i: (i, 0))] * 2
                + [_const_spec(a.shape) for a in tail])
    out_specs = [rows(ATT_Q_W), rows(KX_W), rows(ML_QK_W), rows(ML_GATE_W), rows(ML_GATE_W),
                 cols(ATT_KV_W), cols(ML_QK_W), cols(ML_V_W), cols(ML_V_W), cols(D), cols(D), cols(ML_GATE_W)]
    out_shape = [
        jax.ShapeDtypeStruct((B, L, ATT_Q_W), BF16),
        jax.ShapeDtypeStruct((B, L, KX_W), BF16),
        jax.ShapeDtypeStruct((B, L, ML_QK_W), BF16),
        jax.ShapeDtypeStruct((B, L, ML_GATE_W), F32),
        jax.ShapeDtypeStruct((B, L, ML_GATE_W), F32),
        jax.ShapeDtypeStruct((B, ATT_KV_W, L), BF16),
        jax.ShapeDtypeStruct((B, ML_QK_W, L), BF16),
        jax.ShapeDtypeStruct((B, ML_V_W, L), BF16),
        jax.ShapeDtypeStruct((B, ML_V_W, L), BF16),
        jax.ShapeDtypeStruct((B, D, L), BF16),
        jax.ShapeDtypeStruct((B, D, L), BF16),
        jax.ShapeDtypeStruct((B, ML_GATE_W, L), F32),
    ]
    return pl.pallas_call(
        _inproj_kernel,
        grid=(B, L // tm),
        in_specs=in_specs,
        out_specs=out_specs,
        out_shape=out_shape,
        scratch_shapes=[pltpu.VMEM((tm, D), BF16)],
        compiler_params=_cparams(("arbitrary", "arbitrary")),
        name="inproj",
    )(x, mod6, *consts, cos_t, sin_t, *tail)


def _ctxproj_call(ctx, mod6, ctx_row, n1w, w_p, w_t, wgt, e_mat, et_mat, qkw, gb, gbt, tm):
    B, C, _ = ctx.shape
    tril = _block_tri(tm, ML_CHUNK, True)
    triu = _block_tri(tm, ML_CHUNK, False)

    def rows(w):
        return pl.BlockSpec((1, tm, w), lambda b, i: (b, i, 0))

    def cols(h):
        return pl.BlockSpec((1, h, tm), lambda b, i: (b, 0, i))

    consts = [n1w, w_p, w_t, wgt, e_mat, et_mat, qkw, gb, gbt, tril, triu]
    out_specs = [rows(KX_W), rows(ML_QK_W), rows(ML_GATE_W), rows(ML_GATE_W),
                 cols(ATT_KV_W), cols(ML_V_W), cols(ML_GATE_W)]
    out_shape = [
        jax.ShapeDtypeStruct((B, C, KX_W), BF16),
        jax.ShapeDtypeStruct((B, C, ML_QK_W), BF16),
        jax.ShapeDtypeStruct((B, C, ML_GATE_W), F32),
        jax.ShapeDtypeStruct((B, C, ML_GATE_W), F32),
        jax.ShapeDtypeStruct((B, ATT_KV_W, C), BF16),
        jax.ShapeDtypeStruct((B, ML_V_W, C), BF16),
        jax.ShapeDtypeStruct((B, ML_GATE_W, C), F32),
    ]
    return pl.pallas_call(
        _ctxproj_kernel,
        grid=(B, C // tm),
        in_specs=([rows(D), _mod_spec(MOD_SH1, 2, lambda b, i: ctx_row)]
                  + [_const_spec(a.shape) for a in consts]),
        out_specs=out_specs,
        out_shape=out_shape,
        compiler_params=_cparams(("arbitrary", "arbitrary")),
        name="ctxproj",
    )(ctx, mod6, *consts)


ATT_AHEAD = 12

ATT_QB = 4


def _attn_stream(sink_ref, q_ref, kc_ref, kp_ref, k0_ref, kn_ref, vc_ref, vp_ref, v0_ref, vn_ref, o_ref,
                 n_steps):
    i = pl.program_id(1)
    T = ATT_BLOCK
    hd = ATT_HEAD_DIM
    s_idx = lax.broadcasted_iota(jnp.int32, (T, 2 * T), 0)
    t_idx = lax.broadcasted_iota(jnp.int32, (T, 2 * T), 1) % T
    first = lax.broadcasted_iota(jnp.int32, (1, 2 * T), 1) < T

    k_own = k0_ref[0]
    v_own = v0_ref[0]
    k_blk = [kp_ref[0]] + [k_own[b * T:(b + 1) * T] for b in range(ATT_QB)] + [kn_ref[0]]
    v_blk = [vp_ref[0]] + [v_own[:, b * T:(b + 1) * T] for b in range(ATT_QB)] + [vn_ref[0]]
    ones_rows = jnp.ones((16, 3 * T + kc_ref.shape[1]), BF16)

    def window(qb):
        k_all = jnp.concatenate(k_blk[qb:qb + 3] + [kc_ref[0]], axis=0)
        vt_all = jnp.concatenate(v_blk[qb:qb + 3] + [vc_ref[0]], axis=1)
        ok_prev = (s_idx >= t_idx) & ((i > 0) if qb == 0 else True)
        ok_next = (s_idx <= t_idx) & ((i < n_steps - 1) if qb == ATT_QB - 1 else True)
        return k_all, vt_all, ok_prev, ok_next

    windows = [window(qb) for qb in range(ATT_QB)]
    per_qb = 2 * ATT_KV_HEADS

    def scores(n):
        qb, r = divmod(n, per_qb)
        kh, var = divmod(r, 2)
        q = q_ref[0, qb * T:(qb + 1) * T, :]
        q2 = jnp.concatenate([q[:, (2 * kh) * LANES:(2 * kh + 1) * LANES],
                              q[:, (2 * kh + 1) * LANES:(2 * kh + 2) * LANES]], axis=0)
        kk = windows[qb][0][:, (2 * kh + var) * LANES:(2 * kh + var + 1) * LANES]
        return _dot_nt(kk, q2)

    def finish(n, st):
        qb, r = divmod(n, per_qb)
        kh, var = divmod(r, 2)
        _, vt_all, ok_prev, ok_next = windows[qb]
        vt = vt_all[kh * hd:(kh + 1) * hd, :]
        st = jnp.concatenate([jnp.where(ok_prev, st[0:T], NEG_INF), st[T:2 * T],
                              jnp.where(ok_next, st[2 * T:3 * T], NEG_INF), st[3 * T:]], axis=0)
        h0 = ATT_GROUP * kh + var
        h1 = h0 + 2
        sink = jnp.where(first, sink_ref[h0], sink_ref[h1]) * LOG2E
        m = jnp.maximum(jnp.max(st, axis=0, keepdims=True), sink)
        p = jnp.exp2(st - m)
        ot = _dot(jnp.concatenate([vt, ones_rows], axis=0), p.astype(BF16))
        denom = ot[hd:hd + 1, :] + jnp.exp2(sink - m)
        ot = ot[0:hd, :] * (1.0 / denom)
        o_ref[0, h0 * hd:(h0 + 1) * hd, qb * T:(qb + 1) * T] = ot[:, 0:T].astype(BF16)
        o_ref[0, h1 * hd:(h1 + 1) * hd, qb * T:(qb + 1) * T] = ot[:, T:2 * T].astype(BF16)

    return ATT_QB * per_qb, scores, finish


def _attn_specs(L, C):
    T = ATT_BLOCK
    nb = L // T
    TQ = ATT_QB * T

    def edge(i, off):
        return jnp.clip(i * ATT_QB + (off if off < 0 else ATT_QB), 0, nb - 1)

    in_specs = [pl.BlockSpec(memory_space=pltpu.SMEM),
                pl.BlockSpec((1, TQ, ATT_Q_W), lambda b, i: (b, i, 0)),
                pl.BlockSpec((1, C, KX_W), lambda b, i: (b, 0, 0)),
                pl.BlockSpec((1, T, KX_W), lambda b, i: (b, edge(i, -1), 0)),
                pl.BlockSpec((1, TQ, KX_W), lambda b, i: (b, i, 0)),
                pl.BlockSpec((1, T, KX_W), lambda b, i: (b, edge(i, 1), 0)),
                pl.BlockSpec((1, ATT_KV_W, C), lambda b, i: (b, 0, 0)),
                pl.BlockSpec((1, ATT_KV_W, T), lambda b, i: (b, 0, edge(i, -1))),
                pl.BlockSpec((1, ATT_KV_W, TQ), lambda b, i: (b, 0, i)),
                pl.BlockSpec((1, ATT_KV_W, T), lambda b, i: (b, 0, edge(i, 1)))]
    return in_specs, pl.BlockSpec((1, ATT_Q_W, TQ), lambda b, i: (b, 0, i))


N_CHAIN = 2 * ML_HEADS


def _mlstm_load_state(c_ref, n_ref, m_ref):
    return [(c_ref[ci], n_ref[ci], m_ref[ci, 0:1, 0:1]) for ci in range(N_CHAIN)]


def _mlstm_phase1(dirs, state, item, with_h, cps):
    T = ML_CHUNK
    sub, rest = divmod(item, N_CHAIN)
    d, h = divmod(rest, ML_HEADS)
    qt_ref, k_ref, vt_ref, g_ref, bc_ref, br_ref, h_ref = dirs[d]
    row = lax.broadcasted_iota(jnp.int32, (T, T), 0)
    col = lax.broadcasted_iota(jnp.int32, (T, T), 1)
    mask = (col >= row) if d == 0 else (col <= row)
    last = T - 1 if d == 0 else 0
    sc = sub if d == 0 else cps - 1 - sub
    tok = slice(sc * T, (sc + 1) * T)
    ci = d * ML_HEADS + h
    gi = d * 2 * ML_HEADS + h
    fi = gi + ML_HEADS
    k = k_ref[0, tok, h * ML_QK_DIM:(h + 1) * ML_QK_DIM]
    vt = vt_ref[0, h * ML_V_DIM:(h + 1) * ML_V_DIM, tok]
    u_col = g_ref[0, tok, gi:gi + 1] - bc_ref[0, tok, fi:fi + 1]
    b_row = br_ref[0, fi:fi + 1, tok]
    ct_old, n_old, m_old = state[ci]
    qt = st = qn2 = None
    if with_h:
        qt = qt_ref[0, h * ML_QK_DIM:(h + 1) * ML_QK_DIM, tok]
        st = _dot(k, qt)
        top = lax.broadcasted_iota(jnp.int32, (8, ML_QK_DIM), 0) == 0
        n_hi = n_old.astype(BF16)
        n_lo = (n_old - n_hi.astype(F32)).astype(BF16)
        qn2 = _dot(jnp.where(top, n_hi, n_lo), qt)
    m_last = jnp.maximum(jnp.max(u_col, axis=0, keepdims=True), m_old)
    decay = jnp.exp(m_old - m_last)
    kw = (k.astype(F32) * jnp.exp(u_col - m_last)).astype(BF16)
    state[ci] = (decay * ct_old + _dot(vt, kw), decay * n_old + _dot(jnp.ones((8, T), BF16), kw),
                 b_row[:, last:last + 1] + m_last)
    return (h, h_ref, tok, mask, qt, vt, u_col, b_row, m_old, ct_old, st, qn2)


def _mlstm_phase2(chain):
    h, h_ref, tok, mask, qt, vt, u_col, b_row, m_old, ct_old, st, qn2 = chain
    umat = jnp.where(mask, u_col, -jnp.inf)
    m_row = jnp.maximum(jnp.max(umat, axis=0, keepdims=True), m_old)
    pt = st * jnp.exp(umat - m_row)
    w_int = jnp.exp(m_old - m_row)
    e_row = jnp.exp(-(b_row + m_row))
    nq = jnp.sum(pt, axis=0, keepdims=True) + w_int * (qn2[0:1, :] + qn2[1:2, :])
    den = jnp.maximum(jnp.abs(nq), e_row)
    lhs = jnp.concatenate([vt, ct_old.astype(BF16)], axis=1)
    rhs = jnp.concatenate([pt.astype(BF16), (qt.astype(F32) * w_int).astype(BF16)], axis=0)
    h_ref[0, h * ML_V_DIM:(h + 1) * ML_V_DIM, tok] = (_dot(lhs, rhs) * (1.0 / den)).astype(BF16)


def _mlstm_commit(state, c_ref, n_ref, m_ref):
    for ci, (c_new, n_new, m_new) in enumerate(state):
        c_ref[ci] = c_new
        n_ref[ci] = n_new
        m_ref[ci] = jnp.broadcast_to(m_new, (8, LANES))


def _mlstm_ctx_kernel(kf_ref, vtf_ref, gf_ref, bcf_ref, brf_ref, kb_ref, vtb_ref, gb_ref, bcb_ref, brb_ref,
                      c_ref, n_ref, m_ref, *, cps):
    @pl.when(pl.program_id(1) == 0)
    def _():
        c_ref[...] = jnp.zeros_like(c_ref)
        n_ref[...] = jnp.zeros_like(n_ref)
        m_ref[...] = jnp.zeros_like(m_ref)

    dirs = ((None, kf_ref, vtf_ref, gf_ref, bcf_ref, brf_ref, None),
            (None, kb_ref, vtb_ref, gb_ref, bcb_ref, brb_ref, None))
    state_refs = (c_ref.at[0], n_ref.at[0], m_ref.at[0])
    state = _mlstm_load_state(*state_refs)
    for item in range(cps * N_CHAIN):
        _mlstm_phase1(dirs, state, item, with_h=False, cps=cps)
    _mlstm_commit(state, *state_refs)


def _mixer_kernel(sink_ref, q_ref, kc_ref, kp_ref, k0_ref, kn_ref, vc_ref, vp_ref, v0_ref, vn_ref,
                  c0_ref, n0_ref, m0_ref, qtf_ref, kf_ref, vtf_ref, gf_ref, bcf_ref, brf_ref,
                  qtb_ref, kb_ref, vtb_ref, gb_ref, bcb_ref, brb_ref,
                  att_ref, hf_ref, hb_ref, c_ref, n_ref, m_ref, *, n_steps, cps):
    @pl.when(pl.program_id(1) == 0)
    def _():
        c_ref[...] = c0_ref[0]
        n_ref[...] = n0_ref[0]
        m_ref[...] = m0_ref[0]

    n_iter, scores, finish = _attn_stream(sink_ref, q_ref, kc_ref, kp_ref, k0_ref, kn_ref,
                                          vc_ref, vp_ref, v0_ref, vn_ref, att_ref, n_steps)
    pending = [scores(n) for n in range(ATT_AHEAD)]
    dirs = ((qtf_ref, kf_ref, vtf_ref, gf_ref, bcf_ref, brf_ref, hf_ref),
            (qtb_ref, kb_ref, vtb_ref, gb_ref, bcb_ref, brb_ref, hb_ref))
    state = _mlstm_load_state(c_ref, n_ref, m_ref)
    n_items = cps * N_CHAIN
    items = [_mlstm_phase1(dirs, state, it, with_h=True, cps=cps) for it in range(ML_AHEAD)]
    for n in range(max(n_iter, n_items)):
        if n < n_iter:
            st = pending.pop(0)
            if n + ATT_AHEAD < n_iter:
                pending.append(scores(n + ATT_AHEAD))
            finish(n, st)
        if n < n_items:
            if n + ML_AHEAD < n_items:
                items.append(_mlstm_phase1(dirs, state, n + ML_AHEAD, with_h=True, cps=cps))
            _mlstm_phase2(items.pop(0))
    _mlstm_commit(state, c_ref, n_ref, m_ref)


def _mlstm_specs(T, order, with_q):
    specs = [
        pl.BlockSpec((1, ML_QK_W, T), lambda b, j: (b, 0, order(j))),
        pl.BlockSpec((1, T, ML_QK_W), lambda b, j: (b, order(j), 0)),
        pl.BlockSpec((1, ML_V_W, T), lambda b, j: (b, 0, order(j))),
        pl.BlockSpec((1, T, ML_GATE_W), lambda b, j: (b, order(j), 0)),
        pl.BlockSpec((1, T, ML_GATE_W), lambda b, j: (b, order(j), 0)),
        pl.BlockSpec((1, ML_GATE_W, T), lambda b, j: (b, 0, order(j))),
    ]
    return specs if with_q else specs[1:]


_STATE_SHAPES = ((N_CHAIN, ML_V_DIM, ML_QK_DIM), (N_CHAIN, 8, ML_QK_DIM), (N_CHAIN, 8, LANES))


ML_AHEAD = 32
ML_CPS = 4
ML_CTX_CPS = 2


def _mlstm_ctx_call(mk, mvt, g, bc, br):
    B, C, _ = mk.shape
    T = ML_CTX_CPS * ML_CHUNK
    nc = C // T
    state_specs = [pl.BlockSpec((1,) + s, lambda b, j: (b, 0, 0, 0)) for s in _STATE_SHAPES]
    return pl.pallas_call(
        functools.partial(_mlstm_ctx_kernel, cps=ML_CTX_CPS),
        grid=(B, nc),
        in_specs=_mlstm_specs(T, lambda j: j, False) + _mlstm_specs(T, lambda j: nc - 1 - j, False),
        out_specs=state_specs,
        out_shape=[jax.ShapeDtypeStruct((B,) + s, F32) for s in _STATE_SHAPES],
        compiler_params=_cparams(("arbitrary", "arbitrary")),
        name="mlstm_ctx",
    )(mk, mvt, g, bc, br, mk, mvt, g, bc, br)


def _mixer_call(sink, q, kx, vt, kx_c, vt_c, state, mqt, mk, mvt, g, bc, br):
    B, L, _ = mk.shape
    C = kx_c.shape[1]
    T = ML_CPS * ML_CHUNK
    assert T == ATT_QB * ATT_BLOCK
    nc = L // T
    att_in, att_out = _attn_specs(L, C)
    state_specs = [pl.BlockSpec((1,) + s, lambda b, j: (b, 0, 0, 0)) for s in _STATE_SHAPES]
    out_specs = [att_out,
                 pl.BlockSpec((1, ML_V_W, T), lambda b, j: (b, 0, j)),
                 pl.BlockSpec((1, ML_V_W, T), lambda b, j: (b, 0, nc - 1 - j))]
    return pl.pallas_call(
        functools.partial(_mixer_kernel, n_steps=nc, cps=ML_CPS),
        grid=(B, nc),
        in_specs=(att_in + state_specs + _mlstm_specs(T, lambda j: j, True)
                  + _mlstm_specs(T, lambda j: nc - 1 - j, True)),
        out_specs=out_specs,
        out_shape=[jax.ShapeDtypeStruct((B, ATT_Q_W, L), BF16)] + [jax.ShapeDtypeStruct((B, ML_V_W, L), BF16)] * 2,
        scratch_shapes=[pltpu.VMEM(s, F32) for s in _STATE_SHAPES],
        compiler_params=_cparams(("arbitrary", "arbitrary")),
        name="mixer",
    )(sink, q, kx_c, kx, kx, kx, vt_c, vt, vt, vt, *state, mqt, mk, mvt, g, bc, br, mqt, mk, mvt, g, bc, br)


def _merge_kernel(att_ref, hf_ref, hb_ref, so_ref, sga_ref, sgm_ref, x_ref, g1_ref, mod2_ref, mlw_ref, n2w_ref,
                  wat_ref, wmt_ref, wo_ref, xmid_ref, h2_ref, y_ref):
    s = pl.program_id(0)
    last = pl.num_programs(0) - 1
    cur = s % 2

    def body(do_branch, do_out):
        if do_out:
            y2 = _dot_tn(y_ref[1 - cur], wo_ref[...])
        if do_branch:
            ya = _dot(wat_ref[...], att_ref[0])
            ht = hf_ref[0].astype(F32) + hb_ref[0].astype(F32)
            parts = []
            for h in range(ML_HEADS):
                seg = ht[h * ML_V_DIM:(h + 1) * ML_V_DIM, :]
                ms = jnp.mean(seg * seg, axis=0, keepdims=True)
                parts.append(seg * lax.rsqrt(ms + EPS))
            ml = (jnp.concatenate(parts, axis=0) * mlw_ref[...] * so_ref[0].astype(F32)).astype(BF16)
            ym = _dot(wmt_ref[...], ml)
        if do_out:
            xm = x_ref[0] + g1_ref[0, 0] * y2
            xmid_ref[0] = xm
            ms = jnp.mean(xm * xm, axis=-1, keepdims=True)
            h2 = xm * lax.rsqrt(ms + EPS) * n2w_ref[...]
            h2_ref[0] = (h2 * (1.0 + mod2_ref[1, 0]) + mod2_ref[0, 0]).astype(BF16)
        if do_branch:
            y_ref[cur] = (sga_ref[0].astype(F32) * ya + sgm_ref[0].astype(F32) * ym).astype(BF16)

    pl.when(s == 0)(lambda: body(True, False))
    pl.when((s > 0) & (s < last))(lambda: body(True, True))
    pl.when(s == last)(lambda: body(False, True))


def _merge_call(att_t, hf_t, hb_t, so_t, sga_t, sgm_t, x, mod6, mlw_b, n2w, wat, wmt, wo, tm):
    B, L, _ = x.shape
    nt = L // tm
    n_all = B * nt

    def tile_in(s):
        t = jnp.minimum(s, n_all - 1)
        return t // nt, t % nt

    def tile_out(s):
        t = jnp.maximum(s - 1, 0)
        return t // nt, t % nt

    def in_t(h):
        return pl.BlockSpec((1, h, tm), lambda s: (tile_in(s)[0], 0, tile_in(s)[1]))

    def out_rows(w):
        return pl.BlockSpec((1, tm, w), lambda s: (*tile_out(s), 0))

    def const(shape):
        return pl.BlockSpec(shape, lambda s: (0,) * len(shape))

    return pl.pallas_call(
        _merge_kernel,
        grid=(n_all + 1,),
        in_specs=[in_t(D), in_t(D), in_t(D), in_t(D), in_t(D), in_t(D), out_rows(D),
                  _mod_spec(MOD_G1, 1, lambda s: tile_out(s)[0]),
                  _mod_spec(MOD_SH2, 2, lambda s: tile_out(s)[0]),
                  const((D, tm)), const((1, D)), const((D, D)), const((D, D)), const((D, D))],
        out_specs=[out_rows(D), out_rows(D)],
        out_shape=[jax.ShapeDtypeStruct((B, L, D), F32), jax.ShapeDtypeStruct((B, L, D), BF16)],
        scratch_shapes=[pltpu.VMEM((2, D, tm), BF16)],
        compiler_params=_cparams(("arbitrary",)),
        name="merge",
    )(att_t, hf_t, hb_t, so_t, sga_t, sgm_t, x, mod6, mod6, mlw_b, n2w, wat, wmt, wo)


HALO = 16
FFN_AHEAD = 1


def _ffn_kernel(h_ref, hp_ref, hn_ref, xmid_ref, mod_ref, wup_ref, cw_ref, cb_ref, wdn_ref, o_ref,
                act_ref, *, n_tiles, tn, dn):
    s = pl.program_id(0)
    last = pl.num_programs(0) - 1

    @pl.when(s == 0)
    def _():
        _ffn_body(h_ref, hp_ref, hn_ref, xmid_ref, mod_ref, wup_ref, cw_ref, cb_ref, wdn_ref, o_ref, act_ref,
                  n_tiles=n_tiles, tn=tn, dn=dn, do_up=True, do_down=False)

    @pl.when((s > 0) & (s < last))
    def _():
        _ffn_body(h_ref, hp_ref, hn_ref, xmid_ref, mod_ref, wup_ref, cw_ref, cb_ref, wdn_ref, o_ref, act_ref,
                  n_tiles=n_tiles, tn=tn, dn=dn, do_up=True, do_down=True)

    @pl.when(s == last)
    def _():
        _ffn_body(h_ref, hp_ref, hn_ref, xmid_ref, mod_ref, wup_ref, cw_ref, cb_ref, wdn_ref, o_ref, act_ref,
                  n_tiles=n_tiles, tn=tn, dn=dn, do_up=False, do_down=True)


def _ffn_body(h_ref, hp_ref, hn_ref, xmid_ref, mod_ref, wup_ref, cw_ref, cb_ref, wdn_ref, o_ref, act_ref,
              *, n_tiles, tn, dn, do_up, do_down):
    s = pl.program_id(0)
    i = s % n_tiles
    cur = s % 2
    tm = h_ref.shape[1]
    n_chunks = D_FF // tn
    n_dn = D // dn
    act_prev = act_ref[1 - cur] if do_down else None

    def down(k):
        cols = slice(k * dn, (k + 1) * dn)
        o_ref[0, :, cols] = xmid_ref[0, :, cols] + mod_ref[0, 0, :, cols] * _dot(act_prev, wdn_ref[:, cols])

    if not do_up:
        for k in range(n_dn):
            down(k)
        return

    h = h_ref[0]
    prev_row = jnp.where(i > 0, hp_ref[0].astype(F32)[HALO - 1:HALO, :], 0.0)
    next_row = jnp.where(i < n_tiles - 1, hn_ref[0].astype(F32)[0:1, :], 0.0)
    top = lax.broadcasted_iota(jnp.int32, (16, D), 0) < 8
    edge = jnp.where(top, prev_row, next_row).astype(BF16)
    row8 = lax.broadcasted_iota(jnp.int32, (8, tn), 0)
    h_ext = jnp.concatenate([h, edge], axis=0)

    def up(c0):
        u_ext = _dot(h_ext, wup_ref[:, c0:c0 + tn])
        return u_ext[:tm], u_ext[tm:]

    def conv(u, ue, c0):
        below = pltpu.roll(u, 1, 0)
        above = pltpu.roll(u, tm - 1, 0)
        below = jnp.concatenate([jnp.where(row8 == 0, ue[0:8], below[0:8]), below[8:]], axis=0)
        above = jnp.concatenate([above[:tm - 8], jnp.where(row8 == 7, ue[8:16], above[tm - 8:])], axis=0)
        cw = cw_ref[:, c0:c0 + tn]
        return cb_ref[:, c0:c0 + tn] + below * cw[0:1] + u * cw[1:2] + above * cw[2:3]

    pending = [(up(c * tn), up(D_FF + c * tn)) for c in range(FFN_AHEAD)]
    done = 0
    for c in range(n_chunks):
        (ua, uae), (ug, uge) = pending.pop(0)
        if c + FFN_AHEAD < n_chunks:
            pending.append((up((c + FFN_AHEAD) * tn), up(D_FF + (c + FFN_AHEAD) * tn)))
        while do_down and done * n_chunks < (c + 1) * n_dn:
            down(done)
            done += 1
        a = conv(ua, uae, c * tn)
        hg = conv(ug, uge, D_FF + c * tn)
        act_ref[cur, :, c * tn:(c + 1) * tn] = ((hg + hg * jnp.tanh(hg)) * a).astype(BF16)


def _ffn_call(h2, xmid, g2, wup, cw, cb, wdn, tm, tn, dn):
    B, L, _ = xmid.shape
    nt = L // tm
    n_all = B * nt
    hb = tm // HALO
    nhb = L // HALO

    def tile_in(s):
        t = jnp.minimum(s, n_all - 1)
        return t // nt, t % nt

    def tile_out(s):
        t = jnp.maximum(s - 1, 0)
        return t // nt, t % nt

    def in_spec():
        return pl.BlockSpec((1, tm, D), lambda s: (*tile_in(s), 0))

    def out_spec():
        return pl.BlockSpec((1, tm, D), lambda s: (*tile_out(s), 0))

    def prev_halo(s):
        b, i = tile_in(s)
        return b, jnp.maximum(i * hb - 1, 0), 0

    def next_halo(s):
        b, i = tile_in(s)
        return b, jnp.minimum((i + 1) * hb, nhb - 1), 0

    def const(shape):
        return pl.BlockSpec(shape, lambda s: (0,) * len(shape))

    return pl.pallas_call(
        functools.partial(_ffn_kernel, n_tiles=nt, tn=tn, dn=dn),
        grid=(n_all + 1,),
        in_specs=[in_spec(),
                  pl.BlockSpec((1, HALO, D), prev_halo),
                  pl.BlockSpec((1, HALO, D), next_halo),
                  out_spec(),
                  _mod_spec(MOD_G2, 1, lambda s: tile_out(s)[0]),
                  const(wup.shape), const(cw.shape), const(cb.shape), const(wdn.shape)],
        out_specs=out_spec(),
        out_shape=jax.ShapeDtypeStruct((B, L, D), F32),
        scratch_shapes=[pltpu.VMEM((2, tm, D_FF), BF16)],
        compiler_params=_cparams(("arbitrary",)),
        name="ffn",
    )(h2, h2, h2, xmid, g2, wup, cw, cb, wdn)


def _pair_perm(n_heads):
    half = ATT_HEAD_DIM // 2
    idx = []
    for p in range(n_heads // 2):
        for sub in range(4):
            head = 2 * p + (sub % 2)
            d0 = (sub // 2) * half
            idx.extend(head * ATT_HEAD_DIM + d0 + e for e in range(half))
    return np.asarray(idx, np.int32)


def _rope_tables(L):
    f32 = np.float32
    rows = L // GRID_W
    row = np.repeat(np.arange(rows, dtype=f32), GRID_W)
    col = np.tile(np.arange(GRID_W, dtype=f32), rows)
    n_freq = ATT_HEAD_DIM // 4
    inv_freq = (f32(ROPE_BASE) ** (-np.arange(n_freq, dtype=f32) / f32(n_freq))).astype(f32)
    ang = np.concatenate([row[:, None] * inv_freq, col[:, None] * inv_freq], axis=-1).astype(f32)
    cos = np.tile(np.cos(ang).astype(f32), (1, 4))
    sin = np.tile(np.sin(ang).astype(f32), (1, 4))
    sign = np.where(np.arange(LANES) < LANES // 2, -1.0, 1.0).astype(f32)
    return jnp.asarray(cos), jnp.asarray(sin * sign)


def kernel(x, c, ctx, c_ctx, w_mod, b_mod, norm1_w, w_in, q_norm_w, k_norm_w, attn_sink, ml_gate_b, ml_norm_w,
           w_branch_att, w_branch_ml, w_out, norm2_w, w_up, conv_w, conv_b, w_down):
    B, L, _ = x.shape
    C = ctx.shape[1]
    assert L % 512 == 0 and C % 256 == 0 and L % GRID_W == 0
    l = 0
    tm_merge = 512

    n_rows = -(-(B + 1) // 16) * 16
    cc = jnp.concatenate([c, c_ctx[None, :], jnp.zeros((n_rows - B - 1, D), F32)], axis=0)
    mod6 = _mod_call(cc, w_mod[l], b_mod[l][None, :])

    w = w_in[l]
    qperm = _pair_perm(ATT_HEADS)
    kperm = _pair_perm(ATT_KV_HEADS)
    def pair_cols(wc, n_heads):
        half = ATT_HEAD_DIM // 2
        wc = wc.reshape(D, n_heads // 2, 2, 2, half).transpose(0, 1, 3, 2, 4)
        return wc.reshape(D, n_heads * ATT_HEAD_DIM)

    w_q = pair_cols(w[:, _O_AQ:_O_AQ + ATT_Q_W], ATT_HEADS)
    w_k = pair_cols(w[:, _O_AK:_O_AK + ATT_KV_W], ATT_KV_HEADS)
    w_g = jnp.pad(w[:, _O_MG:_O_MG + ML_GATE_W], ((0, 0), (0, LANES - ML_GATE_W)))
    w_mk = w[:, _O_MK:_O_MK + ML_QK_W] * (ML_QK_DIM ** -0.5)
    w_p = jnp.concatenate([w_q, w_k, w_mk, w_g], axis=1).astype(BF16)
    w_t = jnp.concatenate([w[:, _O_AV:_O_AV + ATT_KV_W], w[:, _O_MQ:_O_MQ + ML_QK_W], w[:, _O_MV:_O_MV + ML_V_W],
                           0.5 * w[:, _O_MO:_O_MO + ML_V_W], 0.5 * w[:, _O_GA:_O_GA + D],
                           0.5 * w[:, _O_GM:_O_GM + D]], axis=1).T.astype(BF16)
    wgt = w[:, _O_MG:_O_MG + ML_GATE_W].T.astype(BF16)

    head_of_col = np.concatenate([qperm // ATT_HEAD_DIM, ATT_HEADS + kperm // ATT_HEAD_DIM])
    e_np = (head_of_col[:, None] == np.arange(LANES)[None, :]).astype(np.float32)
    e_mat = jnp.asarray(e_np, BF16)
    et_mat = jnp.asarray(np.concatenate([e_np.T, e_np.T], axis=0), BF16)
    def pair_tiled(wn, n_heads):
        half = ATT_HEAD_DIM // 2
        return jnp.tile(jnp.concatenate([wn[:half], wn[:half], wn[half:], wn[half:]]), n_heads // 2)

    qkw = jnp.concatenate([pair_tiled(q_norm_w[l], ATT_HEADS) * (ATT_SCALE * LOG2E),
                           pair_tiled(k_norm_w[l], ATT_KV_HEADS)])[None, :]
    cos_t, sin_t = _rope_tables(L)
    gb = ml_gate_b[l].reshape(1, ML_GATE_W)
    gbt = ml_gate_b[l].reshape(ML_GATE_W, 1)
    n1w = norm1_w[l][None, :]

    kx_c, mk_c, g_c, bc_c, vt_c, mvt_c, br_c = _ctxproj_call(
        ctx, mod6, B, n1w, w_p, w_t, wgt, e_mat, et_mat, qkw, gb, gbt, tm=256)
    q, kx, mk, g, bc, vt, mqt, mvt, sot, sgat, sgmt, br = _inproj_call(
        x, mod6, n1w, w_p, w_t, wgt, e_mat, et_mat, qkw, cos_t, sin_t, gb, gbt, tm=512)

    state = _mlstm_ctx_call(mk_c, mvt_c, g_c, bc_c, br_c)
    att_t, hf_t, hb_t = _mixer_call(attn_sink[l], q, kx, vt, kx_c, vt_c, state, mqt, mk, mvt, g, bc, br)

    mlw_b = jnp.broadcast_to(ml_norm_w[l][:, None], (ML_V_W, tm_merge))
    xmid, h2 = _merge_call(att_t, hf_t, hb_t, sot, sgat, sgmt, x, mod6, mlw_b, norm2_w[l][None, :],
                           w_branch_att[l].T.astype(BF16), w_branch_ml[l].T.astype(BF16), w_out[l].astype(BF16),
                           tm=tm_merge)
    gate_half = jnp.where(jnp.arange(2 * D_FF) < D_FF, 1.0, 0.5).astype(F32)
    out = _ffn_call(h2, xmid, mod6, w_up[l].astype(BF16), conv_w[l] * gate_half, (conv_b[l] * gate_half)[None, :],
                    w_down[l].astype(BF16), tm=512, tn=256, dn=256)
    return out
```

```python
import functools

import jax
import jax.numpy as jnp
import numpy as np
from jax import lax
from jax.experimental import pallas as pl
from jax.experimental.pallas import tpu as pltpu

D = 1024
GRID_W = 64
ATT_HEADS = 16
ATT_KV_HEADS = 4
ATT_HEAD_DIM = 64
ATT_GROUP = ATT_HEADS // ATT_KV_HEADS
ATT_BLOCK = 128
WINDOW = 128
ROPE_BASE = 10000.0
ATT_SCALE = ATT_HEAD_DIM ** -0.5
LOG2E = 1.4426950408889634
ML_HEADS = 4
ML_QK_DIM = 128
ML_V_DIM = 256
ML_CHUNK = 128
D_FF = 2816
EPS = 1e-6
NEG_INF = -1e30

ATT_Q_W = ATT_HEADS * ATT_HEAD_DIM
ATT_KV_W = ATT_KV_HEADS * ATT_HEAD_DIM
ML_QK_W = ML_HEADS * ML_QK_DIM
ML_V_W = ML_HEADS * ML_V_DIM
ML_GATE_W = 2 * 2 * ML_HEADS

LANES = 128
KX_W = ATT_KV_HEADS * 2 * LANES
VMEM_LIMIT = 56 * 1024 * 1024

BF16 = jnp.bfloat16
F32 = jnp.float32

_O_AQ = 0
_O_AK = _O_AQ + ATT_Q_W
_O_AV = _O_AK + ATT_KV_W
_O_MQ = _O_AV + ATT_KV_W
_O_MK = _O_MQ + ML_QK_W
_O_MV = _O_MK + ML_QK_W
_O_MO = _O_MV + ML_V_W
_O_MG = _O_MO + ML_V_W
_O_GA = _O_MG + ML_GATE_W
_O_GM = _O_GA + D

QK_W = ATT_Q_W + ATT_KV_W
_P_QK = 0
_P_MK = _P_QK + QK_W
_P_MG = _P_MK + ML_QK_W
_P_END = _P_MG + LANES
_R_V = _P_END
_R_MQ = _R_V + ATT_KV_W
_R_MV = _R_MQ + ML_QK_W
_R_MO = _R_MV + ML_V_W
_R_GA = _R_MO + ML_V_W
_R_GM = _R_GA + D
_R_END = _R_GM + D


def _dot(a, b):
    return jnp.dot(a, b, preferred_element_type=F32)


def _dot_nt(a, b):
    return lax.dot_general(a, b, (((1,), (1,)), ((), ())), preferred_element_type=F32)


def _dot_tn(a, b):
    return lax.dot_general(a, b, (((0,), (0,)), ((), ())), preferred_element_type=F32)


def _dot_ft(w, x):
    return lax.dot_general(w, x, (((0,), (1,)), ((), ())), preferred_element_type=F32)


def _cparams(sem):
    return pltpu.CompilerParams(dimension_semantics=sem, vmem_limit_bytes=VMEM_LIMIT)


def _mod_kernel(c_ref, w_ref, b_ref, o_ref):
    c = c_ref[...]
    a = c * jax.nn.sigmoid(c)
    a_hi = a.astype(BF16)
    a_lo = (a - a_hi.astype(F32)).astype(BF16)
    w = w_ref[...]
    w_hi = w.astype(BF16)
    w_lo = (w - w_hi.astype(F32)).astype(BF16)
    rows = a.shape[0]
    both = _dot(jnp.concatenate([a_hi, a_lo], axis=0), w_hi)
    o_ref[0, :, 0, :] = both[:rows] + both[rows:] + _dot(a_hi, w_lo) + b_ref[...]


MOD_SH1, MOD_SC1, MOD_SH2, MOD_SC2, MOD_G1, MOD_G2 = range(6)


def _mod_call(cc, w_mod, b_mod):
    rows = cc.shape[0]
    n_seg = w_mod.shape[1] // D
    assert n_seg == 6

    def out_pos(j):
        return jnp.where(j == 2, MOD_G1, jnp.where((j == 3) | (j == 4), j - 1, j))

    return pl.pallas_call(
        _mod_kernel,
        grid=(n_seg,),
        in_specs=[pl.BlockSpec((rows, D), lambda j: (0, 0)),
                  pl.BlockSpec((D, D), lambda j: (0, j)),
                  pl.BlockSpec((1, D), lambda j: (0, j))],
        out_specs=pl.BlockSpec((1, rows, 1, D), lambda j: (out_pos(j), 0, 0, 0)),
        out_shape=jax.ShapeDtypeStruct((n_seg, rows, 1, D), F32),
        compiler_params=_cparams(("arbitrary",)),
        name="mod",
    )(cc, w_mod, b_mod)


def _mod_spec(seg, n_seg, row_of):
    return pl.BlockSpec((n_seg, 1, 1, D), lambda *ids: (seg // n_seg, row_of(*ids), 0, 0))


def _split2(x):
    x1 = x.astype(BF16)
    x2 = (x - x1.astype(F32)).astype(BF16)
    return x1, x2


def _norm_modulate(x, n1w_ref, mod_ref):
    ms = jnp.mean(x * x, axis=-1, keepdims=True)
    y = x * lax.rsqrt(ms + EPS) * n1w_ref[...]
    return (y * (1.0 + mod_ref[1, 0]) + mod_ref[0, 0]).astype(BF16)


def _half_sigmoid(half_x):
    return 0.5 * jnp.tanh(half_x) + 0.5


def _head_rms_scale(ss):
    r = lax.rsqrt(ss * (1.0 / ATT_HEAD_DIM) + EPS)
    r_hi = r.astype(BF16)
    r_lo = (r - r_hi.astype(F32)).astype(BF16)
    return jnp.concatenate([r_hi, r_lo], axis=1)


def _store_k_variants(k_ref, pair, o):
    lane = lax.broadcasted_iota(jnp.int32, (1, LANES), 1)
    keep = ((lane // 32) % 2) == 0
    c0 = 4 * pair * LANES
    k_ref[0, :, c0:c0 + LANES] = jnp.where(keep, o, 0.0).astype(BF16)
    k_ref[0, :, c0 + LANES:c0 + 2 * LANES] = jnp.where(keep, 0.0, pltpu.roll(o, 32, 1)).astype(BF16)
    k_ref[0, :, c0 + 2 * LANES:c0 + 3 * LANES] = jnp.where(keep, pltpu.roll(o, 96, 1), 0.0).astype(BF16)
    k_ref[0, :, c0 + 3 * LANES:c0 + 4 * LANES] = jnp.where(keep, 0.0, o).astype(BF16)


def _cum_gates_cols(g16, tri_lo, tri_up):
    parts = _split2(jax.nn.log_sigmoid(g16))
    fwd_col = lax.broadcasted_iota(jnp.int32, (1, ML_GATE_W), 1) < ML_GATE_W // 2
    return jnp.where(fwd_col, sum(_dot(tri_lo, p) for p in parts), sum(_dot(tri_up, p) for p in parts))


def _cum_gates_rows(gt16, tri_lo, tri_up):
    parts = _split2(jax.nn.log_sigmoid(gt16))
    fwd_row = lax.broadcasted_iota(jnp.int32, (ML_GATE_W, 1), 0) < ML_GATE_W // 2
    return jnp.where(fwd_row, sum(_dot(p, tri_up) for p in parts), sum(_dot(p, tri_lo) for p in parts))


def _inproj_kernel(x_ref, mod_ref, n1w_ref, w_ref, e_ref, et_ref, qkw_ref,
                   cos_ref, sin_ref, gb_ref, gbt_ref, tril_ref, triu_ref,
                   q_ref, k_ref, mk_ref, g_ref, bc_ref, vt_ref, mqt_ref, mvt_ref, sot_ref, sgat_ref, sgmt_ref, br_ref,
                   hn_ref):
    hn_ref[...] = _norm_modulate(x_ref[0], n1w_ref, mod_ref)
    hn = hn_ref[...]

    def ft(c0, width):
        return _dot_ft(w_ref[:, c0:c0 + width], hn)

    acc = _dot(hn, w_ref[:, _P_QK:_P_QK + QK_W])
    g16 = _dot(hn, w_ref[:, _P_MG:_P_MG + LANES])[:, :ML_GATE_W] + gb_ref[...]
    gt16 = ft(_P_MG, LANES)[:ML_GATE_W, :] + gbt_ref[...]
    g_ref[0] = g16
    mk_ref[0] = _dot(hn, w_ref[:, _P_MK:_P_MK + ML_QK_W]).astype(BF16)
    ss = _dot((acc * acc).astype(BF16), e_ref[...])
    vt_ref[0] = ft(_R_V, ATT_KV_W).astype(BF16)
    sot_ref[0] = _half_sigmoid(ft(_R_MO, ML_V_W)).astype(BF16)
    rb = _dot(_head_rms_scale(ss), et_ref[...])
    sgat_ref[0] = _half_sigmoid(ft(_R_GA, D)).astype(BF16)
    bc_ref[0] = _cum_gates_cols(g16, tril_ref[...], triu_ref[...])

    qn = acc * rb * qkw_ref[...]
    cos = cos_ref[...]
    sin = sin_ref[...]
    for gi in range(QK_W // LANES):
        xs = qn[:, gi * LANES:(gi + 1) * LANES]
        o = xs * cos + pltpu.roll(xs, LANES // 2, 1) * sin
        if gi < ATT_Q_W // LANES:
            q_ref[0, :, gi * LANES:(gi + 1) * LANES] = o.astype(BF16)
        else:
            _store_k_variants(k_ref, gi - ATT_Q_W // LANES, o)

    sgmt_ref[0] = _half_sigmoid(ft(_R_GM, D)).astype(BF16)
    br_ref[0] = _cum_gates_rows(gt16, tril_ref[...], triu_ref[...])
    mqt_ref[0] = ft(_R_MQ, ML_QK_W).astype(BF16)
    mvt_ref[0] = ft(_R_MV, ML_V_W).astype(BF16)


def _ctxproj_kernel(x_ref, mod_ref, n1w_ref, w_ref, e_ref, et_ref, qkw_ref,
                    gb_ref, gbt_ref, tril_ref, triu_ref,
                    k_ref, mk_ref, g_ref, bc_ref, vt_ref, mvt_ref, br_ref):
    hn = _norm_modulate(x_ref[0], n1w_ref, mod_ref)

    def ft(c0, width):
        return _dot_ft(w_ref[:, c0:c0 + width], hn)

    acc = _dot(hn, w_ref[:, _P_QK + ATT_Q_W:_P_QK + QK_W])
    g16 = _dot(hn, w_ref[:, _P_MG:_P_MG + LANES])[:, :ML_GATE_W] + gb_ref[...]
    gt16 = ft(_P_MG, LANES)[:ML_GATE_W, :] + gbt_ref[...]
    g_ref[0] = g16
    mk_ref[0] = _dot(hn, w_ref[:, _P_MK:_P_MK + ML_QK_W]).astype(BF16)
    ss = _dot((acc * acc).astype(BF16), e_ref[ATT_Q_W:QK_W, :])
    vt_ref[0] = ft(_R_V, ATT_KV_W).astype(BF16)
    rb = _dot(_head_rms_scale(ss), et_ref[:, ATT_Q_W:QK_W])
    mvt_ref[0] = ft(_R_MV, ML_V_W).astype(BF16)
    bc_ref[0] = _cum_gates_cols(g16, tril_ref[...], triu_ref[...])
    kn = acc * rb * qkw_ref[:, ATT_Q_W:QK_W]
    for pair in range(ATT_KV_W // LANES):
        _store_k_variants(k_ref, pair, kn[:, pair * LANES:(pair + 1) * LANES])
    br_ref[0] = _cum_gates_rows(gt16, tril_ref[...], triu_ref[...])


def _block_tri(n, block, lower):
    r = np.arange(n)[:, None]
    c = np.arange(n)[None, :]
    same = (r // block) == (c // block)
    return jnp.asarray(same & ((c <= r) if lower else (c >= r)), BF16)


def _const_spec(shape):
    return pl.BlockSpec(shape, lambda b, i: (0,) * len(shape))


def _inproj_call(x, mod6, n1w, w_p, e_mat, et_mat, qkw, cos_t, sin_t, gb, gbt, tm):
    B, L, _ = x.shape
    tril = _block_tri(tm, ML_CHUNK, True)
    triu = _block_tri(tm, ML_CHUNK, False)

    def rows(w):
        return pl.BlockSpec((1, tm, w), lambda b, i: (b, i, 0))

    def cols(h):
        return pl.BlockSpec((1, h, tm), lambda b, i: (b, 0, i))

    consts = [n1w, w_p, e_mat, et_mat, qkw]
    tail = [gb, gbt, tril, triu]
    in_specs = ([rows(D), _mod_spec(MOD_SH1, 2, lambda b, i: b)]
                + [_const_spec(a.shape) for a in consts]
                + [pl.BlockSpec((tm, LANES), lambda b, i: (i, 0))] * 2
                + [_const_spec(a.shape) for a in tail])
    out_specs = [rows(ATT_Q_W), rows(KX_W), rows(ML_QK_W), rows(ML_GATE_W), rows(ML_GATE_W),
                 cols(ATT_KV_W), cols(ML_QK_W), cols(ML_V_W), cols(ML_V_W), cols(D), cols(D), cols(ML_GATE_W)]
    out_shape = [
        jax.ShapeDtypeStruct((B, L, ATT_Q_W), BF16),
        jax.ShapeDtypeStruct((B, L, KX_W), BF16),
        jax.ShapeDtypeStruct((B, L, ML_QK_W), BF16),
        jax.ShapeDtypeStruct((B, L, ML_GATE_W), F32),
        jax.ShapeDtypeStruct((B, L, ML_GATE_W), F32),
        jax.ShapeDtypeStruct((B, ATT_KV_W, L), BF16),
        jax.ShapeDtypeStruct((B, ML_QK_W, L), BF16),
        jax.ShapeDtypeStruct((B, ML_V_W, L), BF16),
        jax.ShapeDtypeStruct((B, ML_V_W, L), BF16),
        jax.ShapeDtypeStruct((B, D, L), BF16),
        jax.ShapeDtypeStruct((B, D, L), BF16),
        jax.ShapeDtypeStruct((B, ML_GATE_W, L), F32),
    ]
    return pl.pallas_call(
        _inproj_kernel,
        grid=(B, L // tm),
        in_specs=in_specs,
        out_specs=out_specs,
        out_shape=out_shape,
        scratch_shapes=[pltpu.VMEM((tm, D), BF16)],
        compiler_params=_cparams(("arbitrary", "arbitrary")),
        name="inproj",
    )(x, mod6, *consts, cos_t, sin_t, *tail)


def _ctxproj_call(ctx, mod6, ctx_row, n1w, w_p, e_mat, et_mat, qkw, gb, gbt, tm):
    B, C, _ = ctx.shape
    tril = _block_tri(tm, ML_CHUNK, True)
    triu = _block_tri(tm, ML_CHUNK, False)

    def rows(w):
        return pl.BlockSpec((1, tm, w), lambda b, i: (b, i, 0))

    def cols(h):
        return pl.BlockSpec((1, h, tm), lambda b, i: (b, 0, i))

    consts = [n1w, w_p, e_mat, et_mat, qkw, gb, gbt, tril, triu]
    out_specs = [rows(KX_W), rows(ML_QK_W), rows(ML_GATE_W), rows(ML_GATE_W),
                 cols(ATT_KV_W), cols(ML_V_W), cols(ML_GATE_W)]
    out_shape = [
        jax.ShapeDtypeStruct((B, C, KX_W), BF16),
        jax.ShapeDtypeStruct((B, C, ML_QK_W), BF16),
        jax.ShapeDtypeStruct((B, C, ML_GATE_W), F32),
        jax.ShapeDtypeStruct((B, C, ML_GATE_W), F32),
        jax.ShapeDtypeStruct((B, ATT_KV_W, C), BF16),
        jax.ShapeDtypeStruct((B, ML_V_W, C), BF16),
        jax.ShapeDtypeStruct((B, ML_GATE_W, C), F32),
    ]
    return pl.pallas_call(
        _ctxproj_kernel,
        grid=(B, C // tm),
        in_specs=([rows(D), _mod_spec(MOD_SH1, 2, lambda b, i: ctx_row)]
                  + [_const_spec(a.shape) for a in consts]),
        out_specs=out_specs,
        out_shape=out_shape,
        compiler_params=_cparams(("arbitrary", "arbitrary")),
        name="ctxproj",
    )(ctx, mod6, *consts)


ATT_AHEAD = 12

ATT_QB = 4


def _attn_stream(sink_ref, q_ref, kc_ref, kp_ref, k0_ref, kn_ref, vc_ref, vp_ref, v0_ref, vn_ref, o_ref,
                 n_steps):
    i = pl.program_id(1)
    T = ATT_BLOCK
    hd = ATT_HEAD_DIM
    s_idx = lax.broadcasted_iota(jnp.int32, (T, 2 * T), 0)
    t_idx = lax.broadcasted_iota(jnp.int32, (T, 2 * T), 1) % T
    first = lax.broadcasted_iota(jnp.int32, (1, 2 * T), 1) < T

    k_own = k0_ref[0]
    v_own = v0_ref[0]
    k_blk = [kp_ref[0]] + [k_own[b * T:(b + 1) * T] for b in range(ATT_QB)] + [kn_ref[0]]
    v_blk = [vp_ref[0]] + [v_own[:, b * T:(b + 1) * T] for b in range(ATT_QB)] + [vn_ref[0]]
    ones_rows = jnp.ones((16, 3 * T + kc_ref.shape[1]), BF16)

    def window(qb):
        k_all = jnp.concatenate(k_blk[qb:qb + 3] + [kc_ref[0]], axis=0)
        vt_all = jnp.concatenate(v_blk[qb:qb + 3] + [vc_ref[0]], axis=1)
        ok_prev = (s_idx >= t_idx) & ((i > 0) if qb == 0 else True)
        ok_next = (s_idx <= t_idx) & ((i < n_steps - 1) if qb == ATT_QB - 1 else True)
        return k_all, vt_all, ok_prev, ok_next

    windows = [window(qb) for qb in range(ATT_QB)]
    per_qb = 2 * ATT_KV_HEADS

    def scores(n):
        qb, r = divmod(n, per_qb)
        kh, var = divmod(r, 2)
        q = q_ref[0, qb * T:(qb + 1) * T, :]
        q2 = jnp.concatenate([q[:, (2 * kh) * LANES:(2 * kh + 1) * LANES],
                              q[:, (2 * kh + 1) * LANES:(2 * kh + 2) * LANES]], axis=0)
        kk = windows[qb][0][:, (2 * kh + var) * LANES:(2 * kh + var + 1) * LANES]
        return _dot_nt(kk, q2)

    def finish(n, st):
        qb, r = divmod(n, per_qb)
        kh, var = divmod(r, 2)
        _, vt_all, ok_prev, ok_next = windows[qb]
        vt = vt_all[kh * hd:(kh + 1) * hd, :]
        st = jnp.concatenate([jnp.where(ok_prev, st[0:T], NEG_INF), st[T:2 * T],
                              jnp.where(ok_next, st[2 * T:3 * T], NEG_INF), st[3 * T:]], axis=0)
        h0 = ATT_GROUP * kh + var
        h1 = h0 + 2
        sink = jnp.where(first, sink_ref[h0], sink_ref[h1]) * LOG2E
        m = jnp.maximum(jnp.max(st, axis=0, keepdims=True), sink)
        p = jnp.exp2(st - m)
        ot = _dot(jnp.concatenate([vt, ones_rows], axis=0), p.astype(BF16))
        denom = ot[hd:hd + 1, :] + jnp.exp2(sink - m)
        ot = ot[0:hd, :] * (1.0 / denom)
        o_ref[0, h0 * hd:(h0 + 1) * hd, qb * T:(qb + 1) * T] = ot[:, 0:T].astype(BF16)
        o_ref[0, h1 * hd:(h1 + 1) * hd, qb * T:(qb + 1) * T] = ot[:, T:2 * T].astype(BF16)

    return ATT_QB * per_qb, scores, finish


def _attn_specs(L, C):
    T = ATT_BLOCK
    nb = L // T
    TQ = ATT_QB * T

    def edge(i, off):
        return jnp.clip(i * ATT_QB + (off if off < 0 else ATT_QB), 0, nb - 1)

    in_specs = [pl.BlockSpec(memory_space=pltpu.SMEM),
                pl.BlockSpec((1, TQ, ATT_Q_W), lambda b, i: (b, i, 0)),
                pl.BlockSpec((1, C, KX_W), lambda b, i: (b, 0, 0)),
                pl.BlockSpec((1, T, KX_W), lambda b, i: (b, edge(i, -1), 0)),
                pl.BlockSpec((1, TQ, KX_W), lambda b, i: (b, i, 0)),
                pl.BlockSpec((1, T, KX_W), lambda b, i: (b, edge(i, 1), 0)),
                pl.BlockSpec((1, ATT_KV_W, C), lambda b, i: (b, 0, 0)),
                pl.BlockSpec((1, ATT_KV_W, T), lambda b, i: (b, 0, edge(i, -1))),
                pl.BlockSpec((1, ATT_KV_W, TQ), lambda b, i: (b, 0, i)),
                pl.BlockSpec((1, ATT_KV_W, T), lambda b, i: (b, 0, edge(i, 1)))]
    return in_specs, pl.BlockSpec((1, ATT_Q_W, TQ), lambda b, i: (b, 0, i))


N_CHAIN = 2 * ML_HEADS


def _mlstm_load_state(c_ref, n_ref, m_ref):
    return [(c_ref[ci], n_ref[ci], m_ref[ci, 0:1, 0:1]) for ci in range(N_CHAIN)]


def _mlstm_phase1(dirs, state, item, with_h, cps):
    T = ML_CHUNK
    sub, rest = divmod(item, N_CHAIN)
    d, h = divmod(rest, ML_HEADS)
    qt_ref, k_ref, vt_ref, g_ref, bc_ref, br_ref, h_ref = dirs[d]
    row = lax.broadcasted_iota(jnp.int32, (T, T), 0)
    col = lax.broadcasted_iota(jnp.int32, (T, T), 1)
    mask = (col >= row) if d == 0 else (col <= row)
    last = T - 1 if d == 0 else 0
    sc = sub if d == 0 else cps - 1 - sub
    tok = slice(sc * T, (sc + 1) * T)
    ci = d * ML_HEADS + h
    gi = d * 2 * ML_HEADS + h
    fi = gi + ML_HEADS
    k = k_ref[0, tok, h * ML_QK_DIM:(h + 1) * ML_QK_DIM]
    vt = vt_ref[0, h * ML_V_DIM:(h + 1) * ML_V_DIM, tok]
    u_col = g_ref[0, tok, gi:gi + 1] - bc_ref[0, tok, fi:fi + 1]
    b_row = br_ref[0, fi:fi + 1, tok]
    ct_old, n_old, m_old = state[ci]
    qt = st = qn2 = None
    if with_h:
        qt = qt_ref[0, h * ML_QK_DIM:(h + 1) * ML_QK_DIM, tok]
        st = _dot(k, qt)
        top = lax.broadcasted_iota(jnp.int32, (8, ML_QK_DIM), 0) == 0
        n_hi = n_old.astype(BF16)
        n_lo = (n_old - n_hi.astype(F32)).astype(BF16)
        qn2 = _dot(jnp.where(top, n_hi, n_lo), qt)
    m_last = jnp.maximum(jnp.max(u_col, axis=0, keepdims=True), m_old)
    decay = jnp.exp(m_old - m_last)
    kw = (k.astype(F32) * jnp.exp(u_col - m_last)).astype(BF16)
    state[ci] = (decay * ct_old + _dot(vt, kw), decay * n_old + _dot(jnp.ones((8, T), BF16), kw),
                 b_row[:, last:last + 1] + m_last)
    return (h, h_ref, tok, mask, qt, vt, u_col, b_row, m_old, ct_old, st, qn2)


def _mlstm_phase2(chain):
    h, h_ref, tok, mask, qt, vt, u_col, b_row, m_old, ct_old, st, qn2 = chain
    umat = jnp.where(mask, u_col, -jnp.inf)
    m_row = jnp.maximum(jnp.max(umat, axis=0, keepdims=True), m_old)
    pt = st * jnp.exp(umat - m_row)
    w_int = jnp.exp(m_old - m_row)
    e_row = jnp.exp(-(b_row + m_row))
    nq = jnp.sum(pt, axis=0, keepdims=True) + w_int * (qn2[0:1, :] + qn2[1:2, :])
    den = jnp.maximum(jnp.abs(nq), e_row)
    lhs = jnp.concatenate([vt, ct_old.astype(BF16)], axis=1)
    rhs = jnp.concatenate([pt.astype(BF16), (qt.astype(F32) * w_int).astype(BF16)], axis=0)
    h_ref[0, h * ML_V_DIM:(h + 1) * ML_V_DIM, tok] = (_dot(lhs, rhs) * (1.0 / den)).astype(BF16)


def _mlstm_commit(state, c_ref, n_ref, m_ref):
    for ci, (c_new, n_new, m_new) in enumerate(state):
        c_ref[ci] = c_new
        n_ref[ci] = n_new
        m_ref[ci] = jnp.broadcast_to(m_new, (8, LANES))


def _mlstm_ctx_kernel(kf_ref, vtf_ref, gf_ref, bcf_ref, brf_ref, kb_ref, vtb_ref, gb_ref, bcb_ref, brb_ref,
                      c_ref, n_ref, m_ref, *, cps):
    @pl.when(pl.program_id(1) == 0)
    def _():
        c_ref[...] = jnp.zeros_like(c_ref)
        n_ref[...] = jnp.zeros_like(n_ref)
        m_ref[...] = jnp.zeros_like(m_ref)

    dirs = ((None, kf_ref, vtf_ref, gf_ref, bcf_ref, brf_ref, None),
            (None, kb_ref, vtb_ref, gb_ref, bcb_ref, brb_ref, None))
    state_refs = (c_ref.at[0], n_ref.at[0], m_ref.at[0])
    state = _mlstm_load_state(*state_refs)
    for item in range(cps * N_CHAIN):
        _mlstm_phase1(dirs, state, item, with_h=False, cps=cps)
    _mlstm_commit(state, *state_refs)


def _mixer_kernel(sink_ref, q_ref, kc_ref, kp_ref, k0_ref, kn_ref, vc_ref, vp_ref, v0_ref, vn_ref,
                  c0_ref, n0_ref, m0_ref, qtf_ref, kf_ref, vtf_ref, gf_ref, bcf_ref, brf_ref,
                  qtb_ref, kb_ref, vtb_ref, gb_ref, bcb_ref, brb_ref,
                  att_ref, hf_ref, hb_ref, c_ref, n_ref, m_ref, *, n_steps, cps):
    @pl.when(pl.program_id(1) == 0)
    def _():
        c_ref[...] = c0_ref[0]
        n_ref[...] = n0_ref[0]
        m_ref[...] = m0_ref[0]

    n_iter, scores, finish = _attn_stream(sink_ref, q_ref, kc_ref, kp_ref, k0_ref, kn_ref,
                                          vc_ref, vp_ref, v0_ref, vn_ref, att_ref, n_steps)
    pending = [scores(n) for n in range(ATT_AHEAD)]
    dirs = ((qtf_ref, kf_ref, vtf_ref, gf_ref, bcf_ref, brf_ref, hf_ref),
            (qtb_ref, kb_ref, vtb_ref, gb_ref, bcb_ref, brb_ref, hb_ref))
    state = _mlstm_load_state(c_ref, n_ref, m_ref)
    n_items = cps * N_CHAIN
    items = [_mlstm_phase1(dirs, state, it, with_h=True, cps=cps) for it in range(ML_AHEAD)]
    for n in range(max(n_iter, n_items)):
        if n < n_iter:
            st = pending.pop(0)
            if n + ATT_AHEAD < n_iter:
                pending.append(scores(n + ATT_AHEAD))
            finish(n, st)
        if n < n_items:
            if n + ML_AHEAD < n_items:
                items.append(_mlstm_phase1(dirs, state, n + ML_AHEAD, with_h=True, cps=cps))
            _mlstm_phase2(items.pop(0))
    _mlstm_commit(state, c_ref, n_ref, m_ref)


def _mlstm_specs(T, order, with_q):
    specs = [
        pl.BlockSpec((1, ML_QK_W, T), lambda b, j: (b, 0, order(j))),
        pl.BlockSpec((1, T, ML_QK_W), lambda b, j: (b, order(j), 0)),
        pl.BlockSpec((1, ML_V_W, T), lambda b, j: (b, 0, order(j))),
        pl.BlockSpec((1, T, ML_GATE_W), lambda b, j: (b, order(j), 0)),
        pl.BlockSpec((1, T, ML_GATE_W), lambda b, j: (b, order(j), 0)),
        pl.BlockSpec((1, ML_GATE_W, T), lambda b, j: (b, 0, order(j))),
    ]
    return specs if with_q else specs[1:]


_STATE_SHAPES = ((N_CHAIN, ML_V_DIM, ML_QK_DIM), (N_CHAIN, 8, ML_QK_DIM), (N_CHAIN, 8, LANES))


ML_AHEAD = 32
ML_CPS = 4
ML_CTX_CPS = 2


def _mlstm_ctx_call(mk, mvt, g, bc, br):
    B, C, _ = mk.shape
    T = ML_CTX_CPS * ML_CHUNK
    nc = C // T
    state_specs = [pl.BlockSpec((1,) + s, lambda b, j: (b, 0, 0, 0)) for s in _STATE_SHAPES]
    return pl.pallas_call(
        functools.partial(_mlstm_ctx_kernel, cps=ML_CTX_CPS),
        grid=(B, nc),
        in_specs=_mlstm_specs(T, lambda j: j, False) + _mlstm_specs(T, lambda j: nc - 1 - j, False),
        out_specs=state_specs,
        out_shape=[jax.ShapeDtypeStruct((B,) + s, F32) for s in _STATE_SHAPES],
        compiler_params=_cparams(("arbitrary", "arbitrary")),
        name="mlstm_ctx",
    )(mk, mvt, g, bc, br, mk, mvt, g, bc, br)


def _mixer_call(sink, q, kx, vt, kx_c, vt_c, state, mqt, mk, mvt, g, bc, br):
    B, L, _ = mk.shape
    C = kx_c.shape[1]
    T = ML_CPS * ML_CHUNK
    assert T == ATT_QB * ATT_BLOCK
    nc = L // T
    att_in, att_out = _attn_specs(L, C)
    state_specs = [pl.BlockSpec((1,) + s, lambda b, j: (b, 0, 0, 0)) for s in _STATE_SHAPES]
    out_specs = [att_out,
                 pl.BlockSpec((1, ML_V_W, T), lambda b, j: (b, 0, j)),
                 pl.BlockSpec((1, ML_V_W, T), lambda b, j: (b, 0, nc - 1 - j))]
    return pl.pallas_call(
        functools.partial(_mixer_kernel, n_steps=nc, cps=ML_CPS),
        grid=(B, nc),
        in_specs=(att_in + state_specs + _mlstm_specs(T, lambda j: j, True)
                  + _mlstm_specs(T, lambda j: nc - 1 - j, True)),
        out_specs=out_specs,
        out_shape=[jax.ShapeDtypeStruct((B, ATT_Q_W, L), BF16)] + [jax.ShapeDtypeStruct((B, ML_V_W, L), BF16)] * 2,
        scratch_shapes=[pltpu.VMEM(s, F32) for s in _STATE_SHAPES],
        compiler_params=_cparams(("arbitrary", "arbitrary")),
        name="mixer",
    )(sink, q, kx_c, kx, kx, kx, vt_c, vt, vt, vt, *state, mqt, mk, mvt, g, bc, br, mqt, mk, mvt, g, bc, br)


def _merge_kernel(att_ref, hf_ref, hb_ref, so_ref, sga_ref, sgm_ref, x_ref, g1_ref, mod2_ref, mlw_ref, n2w_ref,
                  wa_ref, wm_ref, wo_ref, xmid_ref, h2_ref, y_ref):
    s = pl.program_id(0)
    last = pl.num_programs(0) - 1
    cur = s % 2

    def body(do_branch, do_out):
        if do_out:
            y2 = _dot_tn(y_ref[1 - cur], wo_ref[...])
        if do_branch:
            ya = _dot_tn(wa_ref[...], att_ref[0])
            ht = hf_ref[0].astype(F32) + hb_ref[0].astype(F32)
            parts = []
            for h in range(ML_HEADS):
                seg = ht[h * ML_V_DIM:(h + 1) * ML_V_DIM, :]
                ms = jnp.mean(seg * seg, axis=0, keepdims=True)
                parts.append(seg * lax.rsqrt(ms + EPS))
            ml = (jnp.concatenate(parts, axis=0) * mlw_ref[...] * so_ref[0].astype(F32)).astype(BF16)
            ym = _dot_tn(wm_ref[...], ml)
        if do_out:
            xm = x_ref[0] + g1_ref[0, 0] * y2
            xmid_ref[0] = xm
            ms = jnp.mean(xm * xm, axis=-1, keepdims=True)
            h2 = xm * lax.rsqrt(ms + EPS) * n2w_ref[...]
            h2_ref[0] = (h2 * (1.0 + mod2_ref[1, 0]) + mod2_ref[0, 0]).astype(BF16)
        if do_branch:
            y_ref[cur] = (sga_ref[0].astype(F32) * ya + sgm_ref[0].astype(F32) * ym).astype(BF16)

    pl.when(s == 0)(lambda: body(True, False))
    pl.when((s > 0) & (s < last))(lambda: body(True, True))
    pl.when(s == last)(lambda: body(False, True))


def _merge_call(att_t, hf_t, hb_t, so_t, sga_t, sgm_t, x, mod6, mlw_b, n2w, wa, wm, wo, tm):
    B, L, _ = x.shape
    nt = L // tm
    n_all = B * nt

    def tile_in(s):
        t = jnp.minimum(s, n_all - 1)
        return t // nt, t % nt

    def tile_out(s):
        t = jnp.maximum(s - 1, 0)
        return t // nt, t % nt

    def in_t(h):
        return pl.BlockSpec((1, h, tm), lambda s: (tile_in(s)[0], 0, tile_in(s)[1]))

    def out_rows(w):
        return pl.BlockSpec((1, tm, w), lambda s: (*tile_out(s), 0))

    def const(shape):
        return pl.BlockSpec(shape, lambda s: (0,) * len(shape))

    return pl.pallas_call(
        _merge_kernel,
        grid=(n_all + 1,),
        in_specs=[in_t(D), in_t(D), in_t(D), in_t(D), in_t(D), in_t(D), out_rows(D),
                  _mod_spec(MOD_G1, 1, lambda s: tile_out(s)[0]),
                  _mod_spec(MOD_SH2, 2, lambda s: tile_out(s)[0]),
                  const((D, tm)), const((1, D)), const((D, D)), const((D, D)), const((D, D))],
        out_specs=[out_rows(D), out_rows(D)],
        out_shape=[jax.ShapeDtypeStruct((B, L, D), F32), jax.ShapeDtypeStruct((B, L, D), BF16)],
        scratch_shapes=[pltpu.VMEM((2, D, tm), BF16)],
        compiler_params=_cparams(("arbitrary",)),
        name="merge",
    )(att_t, hf_t, hb_t, so_t, sga_t, sgm_t, x, mod6, mod6, mlw_b, n2w, wa, wm, wo)


HALO = 16
FFN_AHEAD = 1


def _ffn_kernel(h_ref, hp_ref, hn_ref, xmid_ref, mod_ref, wup_ref, cw_ref, cb_ref, wdn_ref, o_ref,
                act_ref, *, n_tiles, tn, dn):
    s = pl.program_id(0)
    last = pl.num_programs(0) - 1

    @pl.when(s == 0)
    def _():
        _ffn_body(h_ref, hp_ref, hn_ref, xmid_ref, mod_ref, wup_ref, cw_ref, cb_ref, wdn_ref, o_ref, act_ref,
                  n_tiles=n_tiles, tn=tn, dn=dn, do_up=True, do_down=False)

    @pl.when((s > 0) & (s < last))
    def _():
        _ffn_body(h_ref, hp_ref, hn_ref, xmid_ref, mod_ref, wup_ref, cw_ref, cb_ref, wdn_ref, o_ref, act_ref,
                  n_tiles=n_tiles, tn=tn, dn=dn, do_up=True, do_down=True)

    @pl.when(s == last)
    def _():
        _ffn_body(h_ref, hp_ref, hn_ref, xmid_ref, mod_ref, wup_ref, cw_ref, cb_ref, wdn_ref, o_ref, act_ref,
                  n_tiles=n_tiles, tn=tn, dn=dn, do_up=False, do_down=True)


def _ffn_body(h_ref, hp_ref, hn_ref, xmid_ref, mod_ref, wup_ref, cw_ref, cb_ref, wdn_ref, o_ref, act_ref,
              *, n_tiles, tn, dn, do_up, do_down):
    s = pl.program_id(0)
    i = s % n_tiles
    cur = s % 2
    tm = h_ref.shape[1]
    n_chunks = D_FF // tn
    n_dn = D // dn
    act_prev = act_ref[1 - cur] if do_down else None

    def down(k):
        cols = slice(k * dn, (k + 1) * dn)
        o_ref[0, :, cols] = xmid_ref[0, :, cols] + mod_ref[0, 0, :, cols] * _dot(act_prev, wdn_ref[:, cols])

    if not do_up:
        for k in range(n_dn):
            down(k)
        return

    h = h_ref[0]
    prev_row = jnp.where(i > 0, hp_ref[0].astype(F32)[HALO - 1:HALO, :], 0.0)
    next_row = jnp.where(i < n_tiles - 1, hn_ref[0].astype(F32)[0:1, :], 0.0)
    top = lax.broadcasted_iota(jnp.int32, (16, D), 0) < 8
    edge = jnp.where(top, prev_row, next_row).astype(BF16)
    row8 = lax.broadcasted_iota(jnp.int32, (8, tn), 0)
    h_ext = jnp.concatenate([h, edge], axis=0)

    def up(c0):
        u_ext = _dot(h_ext, wup_ref[:, c0:c0 + tn])
        return u_ext[:tm], u_ext[tm:]

    def conv(u, ue, c0):
        below = pltpu.roll(u, 1, 0)
        above = pltpu.roll(u, tm - 1, 0)
        below = jnp.concatenate([jnp.where(row8 == 0, ue[0:8], below[0:8]), below[8:]], axis=0)
        above = jnp.concatenate([above[:tm - 8], jnp.where(row8 == 7, ue[8:16], above[tm - 8:])], axis=0)
        cw = cw_ref[:, c0:c0 + tn]
        return cb_ref[:, c0:c0 + tn] + below * cw[0:1] + u * cw[1:2] + above * cw[2:3]

    pending = [(up(c * tn), up(D_FF + c * tn)) for c in range(FFN_AHEAD)]
    done = 0
    for c in range(n_chunks):
        (ua, uae), (ug, uge) = pending.pop(0)
        if c + FFN_AHEAD < n_chunks:
            pending.append((up((c + FFN_AHEAD) * tn), up(D_FF + (c + FFN_AHEAD) * tn)))
        while do_down and done * n_chunks < (c + 1) * n_dn:
            down(done)
            done += 1
        a = conv(ua, uae, c * tn)
        hg = conv(ug, uge, D_FF + c * tn)
        act_ref[cur, :, c * tn:(c + 1) * tn] = ((hg + hg * jnp.tanh(hg)) * a).astype(BF16)


def _ffn_call(h2, xmid, g2, wup, cw, cb, wdn, tm, tn, dn):
    B, L, _ = xmid.shape
    nt = L // tm
    n_all = B * nt
    hb = tm // HALO
    nhb = L // HALO

    def tile_in(s):
        t = jnp.minimum(s, n_all - 1)
        return t // nt, t % nt

    def tile_out(s):
        t = jnp.maximum(s - 1, 0)
        return t // nt, t % nt

    def in_spec():
        return pl.BlockSpec((1, tm, D), lambda s: (*tile_in(s), 0))

    def out_spec():
        return pl.BlockSpec((1, tm, D), lambda s: (*tile_out(s), 0))

    def prev_halo(s):
        b, i = tile_in(s)
        return b, jnp.maximum(i * hb - 1, 0), 0

    def next_halo(s):
        b, i = tile_in(s)
        return b, jnp.minimum((i + 1) * hb, nhb - 1), 0

    def const(shape):
        return pl.BlockSpec(shape, lambda s: (0,) * len(shape))

    return pl.pallas_call(
        functools.partial(_ffn_kernel, n_tiles=nt, tn=tn, dn=dn),
        grid=(n_all + 1,),
        in_specs=[in_spec(),
                  pl.BlockSpec((1, HALO, D), prev_halo),
                  pl.BlockSpec((1, HALO, D), next_halo),
                  out_spec(),
                  _mod_spec(MOD_G2, 1, lambda s: tile_out(s)[0]),
                  const(wup.shape), const(cw.shape), const(cb.shape), const(wdn.shape)],
        out_specs=out_spec(),
        out_shape=jax.ShapeDtypeStruct((B, L, D), F32),
        scratch_shapes=[pltpu.VMEM((2, tm, D_FF), BF16)],
        compiler_params=_cparams(("arbitrary",)),
        name="ffn",
    )(h2, h2, h2, xmid, g2, wup, cw, cb, wdn)


def _pair_perm(n_heads):
    half = ATT_HEAD_DIM // 2
    idx = []
    for p in range(n_heads // 2):
        for sub in range(4):
            head = 2 * p + (sub % 2)
            d0 = (sub // 2) * half
            idx.extend(head * ATT_HEAD_DIM + d0 + e for e in range(half))
    return np.asarray(idx, np.int32)


def _rope_tables(L):
    f32 = np.float32
    rows = L // GRID_W
    row = np.repeat(np.arange(rows, dtype=f32), GRID_W)
    col = np.tile(np.arange(GRID_W, dtype=f32), rows)
    n_freq = ATT_HEAD_DIM // 4
    inv_freq = (f32(ROPE_BASE) ** (-np.arange(n_freq, dtype=f32) / f32(n_freq))).astype(f32)
    ang = np.concatenate([row[:, None] * inv_freq, col[:, None] * inv_freq], axis=-1).astype(f32)
    cos = np.tile(np.cos(ang).astype(f32), (1, 4))
    sin = np.tile(np.sin(ang).astype(f32), (1, 4))
    sign = np.where(np.arange(LANES) < LANES // 2, -1.0, 1.0).astype(f32)
    return jnp.asarray(cos), jnp.asarray(sin * sign)


def kernel(x, c, ctx, c_ctx, w_mod, b_mod, norm1_w, w_in, q_norm_w, k_norm_w, attn_sink, ml_gate_b, ml_norm_w,
           w_branch_att, w_branch_ml, w_out, norm2_w, w_up, conv_w, conv_b, w_down):
    B, L, _ = x.shape
    C = ctx.shape[1]
    assert L % 512 == 0 and C % 256 == 0 and L % GRID_W == 0
    l = 0
    tm_merge = 512

    n_rows = -(-(B + 1) // 16) * 16
    cc = jnp.concatenate([c, c_ctx[None, :], jnp.zeros((n_rows - B - 1, D), F32)], axis=0)
    mod6 = _mod_call(cc, w_mod[l], b_mod[l][None, :])

    w = w_in[l]
    qperm = _pair_perm(ATT_HEADS)
    kperm = _pair_perm(ATT_KV_HEADS)
    def pair_cols(wc, n_heads):
        half = ATT_HEAD_DIM // 2
        wc = wc.reshape(D, n_heads // 2, 2, 2, half).transpose(0, 1, 3, 2, 4)
        return wc.reshape(D, n_heads * ATT_HEAD_DIM)

    w_q = pair_cols(w[:, _O_AQ:_O_AQ + ATT_Q_W], ATT_HEADS)
    w_k = pair_cols(w[:, _O_AK:_O_AK + ATT_KV_W], ATT_KV_HEADS)
    w_g = jnp.pad(w[:, _O_MG:_O_MG + ML_GATE_W], ((0, 0), (0, LANES - ML_GATE_W)))
    w_mk = w[:, _O_MK:_O_MK + ML_QK_W] * (ML_QK_DIM ** -0.5)
    w_p = jnp.concatenate([w_q, w_k, w_mk, w_g,
                           w[:, _O_AV:_O_AV + ATT_KV_W], w[:, _O_MQ:_O_MQ + ML_QK_W], w[:, _O_MV:_O_MV + ML_V_W],
                           0.5 * w[:, _O_MO:_O_MO + ML_V_W], 0.5 * w[:, _O_GA:_O_GA + D],
                           0.5 * w[:, _O_GM:_O_GM + D]], axis=1).astype(BF16)

    head_of_col = np.concatenate([qperm // ATT_HEAD_DIM, ATT_HEADS + kperm // ATT_HEAD_DIM])
    e_np = (head_of_col[:, None] == np.arange(LANES)[None, :]).astype(np.float32)
    e_mat = jnp.asarray(e_np, BF16)
    et_mat = jnp.asarray(np.concatenate([e_np.T, e_np.T], axis=0), BF16)
    def pair_tiled(wn, n_heads):
        half = ATT_HEAD_DIM // 2
        return jnp.tile(jnp.concatenate([wn[:half], wn[:half], wn[half:], wn[half:]]), n_heads // 2)

    qkw = jnp.concatenate([pair_tiled(q_norm_w[l], ATT_HEADS) * (ATT_SCALE * LOG2E),
                           pair_tiled(k_norm_w[l], ATT_KV_HEADS)])[None, :]
    cos_t, sin_t = _rope_tables(L)
    gb = ml_gate_b[l].reshape(1, ML_GATE_W)
    gbt = ml_gate_b[l].reshape(ML_GATE_W, 1)
    n1w = norm1_w[l][None, :]

    kx_c, mk_c, g_c, bc_c, vt_c, mvt_c, br_c = _ctxproj_call(
        ctx, mod6, B, n1w, w_p, e_mat, et_mat, qkw, gb, gbt, tm=256)
    q, kx, mk, g, bc, vt, mqt, mvt, sot, sgat, sgmt, br = _inproj_call(
        x, mod6, n1w, w_p, e_mat, et_mat, qkw, cos_t, sin_t, gb, gbt, tm=512)

    state = _mlstm_ctx_call(mk_c, mvt_c, g_c, bc_c, br_c)
    att_t, hf_t, hb_t = _mixer_call(attn_sink[l], q, kx, vt, kx_c, vt_c, state, mqt, mk, mvt, g, bc, br)

    mlw_b = jnp.broadcast_to(ml_norm_w[l][:, None], (ML_V_W, tm_merge))
    xmid, h2 = _merge_call(att_t, hf_t, hb_t, sot, sgat, sgmt, x, mod6, mlw_b, norm2_w[l][None, :],
                           w_branch_att[l].astype(BF16), w_branch_ml[l].astype(BF16), w_out[l].astype(BF16),
                           tm=tm_merge)
    gate_half = jnp.where(jnp.arange(2 * D_FF) < D_FF, 1.0, 0.5).astype(F32)
    out = _ffn_call(h2, xmid, mod6, w_up[l].astype(BF16), conv_w[l] * gate_half, (conv_b[l] * gate_half)[None, :],
                    w_down[l].astype(BF16), tm=512, tn=256, dn=256)
    return out
```

```python
import functools

import jax
import jax.numpy as jnp
import numpy as np
from jax import lax
from jax.experimental import pallas as pl
from jax.experimental.pallas import tpu as pltpu

D = 1024
GRID_W = 64
ATT_HEADS = 16
ATT_KV_HEADS = 4
ATT_HEAD_DIM = 64
ATT_GROUP = ATT_HEADS // ATT_KV_HEADS
ATT_BLOCK = 128
WINDOW = 128
ROPE_BASE = 10000.0
ATT_SCALE = ATT_HEAD_DIM ** -0.5
LOG2E = 1.4426950408889634
ML_HEADS = 4
ML_QK_DIM = 128
ML_V_DIM = 256
ML_CHUNK = 128
D_FF = 2816
EPS = 1e-6
NEG_INF = -1e30

ATT_Q_W = ATT_HEADS * ATT_HEAD_DIM
ATT_KV_W = ATT_KV_HEADS * ATT_HEAD_DIM
ML_QK_W = ML_HEADS * ML_QK_DIM
ML_V_W = ML_HEADS * ML_V_DIM
ML_GATE_W = 2 * 2 * ML_HEADS

LANES = 128
KX_W = ATT_KV_HEADS * 2 * LANES
VMEM_LIMIT = 56 * 1024 * 1024

BF16 = jnp.bfloat16
F32 = jnp.float32

_O_AQ = 0
_O_AK = _O_AQ + ATT_Q_W
_O_AV = _O_AK + ATT_KV_W
_O_MQ = _O_AV + ATT_KV_W
_O_MK = _O_MQ + ML_QK_W
_O_MV = _O_MK + ML_QK_W
_O_MO = _O_MV + ML_V_W
_O_MG = _O_MO + ML_V_W
_O_GA = _O_MG + ML_GATE_W
_O_GM = _O_GA + D

QK_W = ATT_Q_W + ATT_KV_W
_P_QK = 0
_P_MK = _P_QK + QK_W
_P_MG = _P_MK + ML_QK_W
_P_END = _P_MG + LANES
_A_COLS = (_O_AV, _O_MK)
_B_COLS = (_O_MV, _O_MG)
_C_COLS = (_O_GA, _O_GM + D)


def _dot(a, b):
    return jnp.dot(a, b, preferred_element_type=F32)


def _dot_nt(a, b):
    return lax.dot_general(a, b, (((1,), (1,)), ((), ())), preferred_element_type=F32)


def _dot_tn(a, b):
    return lax.dot_general(a, b, (((0,), (0,)), ((), ())), preferred_element_type=F32)


def _dot_ft(w, x):
    return lax.dot_general(w, x, (((0,), (1,)), ((), ())), preferred_element_type=F32)


def _cparams(sem):
    return pltpu.CompilerParams(dimension_semantics=sem, vmem_limit_bytes=VMEM_LIMIT)


def _mod_kernel(c_ref, w_ref, b_ref, o_ref):
    c = c_ref[...]
    a = c * jax.nn.sigmoid(c)
    a_hi = a.astype(BF16)
    a_lo = (a - a_hi.astype(F32)).astype(BF16)
    w = w_ref[...]
    w_hi = w.astype(BF16)
    w_lo = (w - w_hi.astype(F32)).astype(BF16)
    rows = a.shape[0]
    both = _dot(jnp.concatenate([a_hi, a_lo], axis=0), w_hi)
    o_ref[0, :, 0, :] = both[:rows] + both[rows:] + _dot(a_hi, w_lo) + b_ref[...]


MOD_SH1, MOD_SC1, MOD_SH2, MOD_SC2, MOD_G1, MOD_G2 = range(6)


def _mod_call(cc, w_mod, b_mod):
    rows = cc.shape[0]
    n_seg = w_mod.shape[1] // D
    assert n_seg == 6

    def out_pos(j):
        return jnp.where(j == 2, MOD_G1, jnp.where((j == 3) | (j == 4), j - 1, j))

    return pl.pallas_call(
        _mod_kernel,
        grid=(n_seg,),
        in_specs=[pl.BlockSpec((rows, D), lambda j: (0, 0)),
                  pl.BlockSpec((D, D), lambda j: (0, j)),
                  pl.BlockSpec((1, D), lambda j: (0, j))],
        out_specs=pl.BlockSpec((1, rows, 1, D), lambda j: (out_pos(j), 0, 0, 0)),
        out_shape=jax.ShapeDtypeStruct((n_seg, rows, 1, D), F32),
        compiler_params=_cparams(("arbitrary",)),
        name="mod",
    )(cc, w_mod, b_mod)


def _mod_spec(seg, n_seg, row_of):
    return pl.BlockSpec((n_seg, 1, 1, D), lambda *ids: (seg // n_seg, row_of(*ids), 0, 0))


def _split2(x):
    x1 = x.astype(BF16)
    x2 = (x - x1.astype(F32)).astype(BF16)
    return x1, x2


def _norm_modulate(x, n1w_ref, mod_ref):
    ms = jnp.mean(x * x, axis=-1, keepdims=True)
    y = x * lax.rsqrt(ms + EPS) * n1w_ref[...]
    return (y * (1.0 + mod_ref[1, 0]) + mod_ref[0, 0]).astype(BF16)


def _sigmoid(x):
    return 0.5 * jnp.tanh(0.5 * x) + 0.5


def _head_rms_scale(ss):
    r = lax.rsqrt(ss * (1.0 / ATT_HEAD_DIM) + EPS)
    r_hi = r.astype(BF16)
    r_lo = (r - r_hi.astype(F32)).astype(BF16)
    return jnp.concatenate([r_hi, r_lo], axis=1)


def _store_k_variants(k_ref, pair, o):
    lane = lax.broadcasted_iota(jnp.int32, (1, LANES), 1)
    keep = ((lane // 32) % 2) == 0
    c0 = 4 * pair * LANES
    k_ref[0, :, c0:c0 + LANES] = jnp.where(keep, o, 0.0).astype(BF16)
    k_ref[0, :, c0 + LANES:c0 + 2 * LANES] = jnp.where(keep, 0.0, pltpu.roll(o, 32, 1)).astype(BF16)
    k_ref[0, :, c0 + 2 * LANES:c0 + 3 * LANES] = jnp.where(keep, pltpu.roll(o, 96, 1), 0.0).astype(BF16)
    k_ref[0, :, c0 + 3 * LANES:c0 + 4 * LANES] = jnp.where(keep, 0.0, o).astype(BF16)


def _cum_gates_cols(g16, tri_lo, tri_up):
    parts = _split2(jax.nn.log_sigmoid(g16))
    fwd_col = lax.broadcasted_iota(jnp.int32, (1, ML_GATE_W), 1) < ML_GATE_W // 2
    return jnp.where(fwd_col, sum(_dot(tri_lo, p) for p in parts), sum(_dot(tri_up, p) for p in parts))


def _cum_gates_rows(gt16, tri_lo, tri_up):
    parts = _split2(jax.nn.log_sigmoid(gt16))
    fwd_row = lax.broadcasted_iota(jnp.int32, (ML_GATE_W, 1), 0) < ML_GATE_W // 2
    return jnp.where(fwd_row, sum(_dot(p, tri_up) for p in parts), sum(_dot(p, tri_lo) for p in parts))


def _inproj_kernel(x_ref, mod_ref, n1w_ref, w_ref, wa_ref, wb_ref, wc_ref, e_ref, et_ref, qkw_ref,
                   cos_ref, sin_ref, gb_ref, gbt_ref, tril_ref, triu_ref,
                   q_ref, k_ref, mk_ref, g_ref, bc_ref, vt_ref, mqt_ref, mvt_ref, sot_ref, sgat_ref, sgmt_ref, br_ref,
                   hn_ref):
    hn_ref[...] = _norm_modulate(x_ref[0], n1w_ref, mod_ref)
    hn = hn_ref[...]

    def ft(ref, c0, width):
        return _dot_ft(ref[:, c0:c0 + width], hn)

    acc = _dot(hn, w_ref[:, _P_QK:_P_QK + QK_W])
    g16 = _dot(hn, w_ref[:, _P_MG:_P_MG + LANES])[:, :ML_GATE_W] + gb_ref[...]
    gt16 = ft(w_ref, _P_MG, LANES)[:ML_GATE_W, :] + gbt_ref[...]
    g_ref[0] = g16
    mk_ref[0] = _dot(hn, w_ref[:, _P_MK:_P_MK + ML_QK_W]).astype(BF16)
    ss = _dot((acc * acc).astype(BF16), e_ref[...])
    vt_ref[0] = ft(wa_ref, 0, ATT_KV_W).astype(BF16)
    sot_ref[0] = _sigmoid(ft(wb_ref, ML_V_W, ML_V_W)).astype(BF16)
    rb = _dot(_head_rms_scale(ss), et_ref[...])
    sgat_ref[0] = _sigmoid(ft(wc_ref, 0, D)).astype(BF16)
    bc_ref[0] = _cum_gates_cols(g16, tril_ref[...], triu_ref[...])

    qn = acc * rb * qkw_ref[...]
    cos = cos_ref[...]
    sin = sin_ref[...]
    for gi in range(QK_W // LANES):
        xs = qn[:, gi * LANES:(gi + 1) * LANES]
        o = xs * cos + pltpu.roll(xs, LANES // 2, 1) * sin
        if gi < ATT_Q_W // LANES:
            q_ref[0, :, gi * LANES:(gi + 1) * LANES] = o.astype(BF16)
        else:
            _store_k_variants(k_ref, gi - ATT_Q_W // LANES, o)

    sgmt_ref[0] = _sigmoid(ft(wc_ref, D, D)).astype(BF16)
    br_ref[0] = _cum_gates_rows(gt16, tril_ref[...], triu_ref[...])
    mqt_ref[0] = ft(wa_ref, ATT_KV_W, ML_QK_W).astype(BF16)
    mvt_ref[0] = ft(wb_ref, 0, ML_V_W).astype(BF16)


def _ctxproj_kernel(x_ref, mod_ref, n1w_ref, w_ref, wv_ref, wmv_ref, e_ref, et_ref, qkw_ref,
                    gb_ref, gbt_ref, tril_ref, triu_ref,
                    k_ref, mk_ref, g_ref, bc_ref, vt_ref, mvt_ref, br_ref):
    hn = _norm_modulate(x_ref[0], n1w_ref, mod_ref)
    acc = _dot(hn, w_ref[:, _P_QK + ATT_Q_W:_P_QK + QK_W])
    g16 = _dot(hn, w_ref[:, _P_MG:_P_MG + LANES])[:, :ML_GATE_W] + gb_ref[...]
    gt16 = _dot_ft(w_ref[:, _P_MG:_P_MG + LANES], hn)[:ML_GATE_W, :] + gbt_ref[...]
    g_ref[0] = g16
    mk_ref[0] = _dot(hn, w_ref[:, _P_MK:_P_MK + ML_QK_W]).astype(BF16)
    ss = _dot((acc * acc).astype(BF16), e_ref[ATT_Q_W:QK_W, :])
    vt_ref[0] = _dot_ft(wv_ref[...], hn).astype(BF16)
    rb = _dot(_head_rms_scale(ss), et_ref[:, ATT_Q_W:QK_W])
    mvt_ref[0] = _dot_ft(wmv_ref[...], hn).astype(BF16)
    bc_ref[0] = _cum_gates_cols(g16, tril_ref[...], triu_ref[...])
    kn = acc * rb * qkw_ref[:, ATT_Q_W:QK_W]
    for pair in range(ATT_KV_W // LANES):
        _store_k_variants(k_ref, pair, kn[:, pair * LANES:(pair + 1) * LANES])
    br_ref[0] = _cum_gates_rows(gt16, tril_ref[...], triu_ref[...])


def _block_tri(n, block, lower):
    r = np.arange(n)[:, None]
    c = np.arange(n)[None, :]
    same = (r // block) == (c // block)
    return jnp.asarray(same & ((c <= r) if lower else (c >= r)), BF16)


def _const_spec(shape):
    return pl.BlockSpec(shape, lambda b, i: (0,) * len(shape))


def _inproj_call(x, mod6, n1w, w_p, w_a, w_b, w_c, e_mat, et_mat, qkw, cos_t, sin_t, gb, gbt, tm):
    B, L, _ = x.shape
    tril = _block_tri(tm, ML_CHUNK, True)
    triu = _block_tri(tm, ML_CHUNK, False)

    def rows(w):
        return pl.BlockSpec((1, tm, w), lambda b, i: (b, i, 0))

    def cols(h):
        return pl.BlockSpec((1, h, tm), lambda b, i: (b, 0, i))

    consts = [n1w, w_p, w_a, w_b, w_c, e_mat, et_mat, qkw]
    tail = [gb, gbt, tril, triu]
    in_specs = ([rows(D), _mod_spec(MOD_SH1, 2, lambda b, i: b)]
                + [_const_spec(a.shape) for a in consts]
                + [pl.BlockSpec((tm, LANES), lambda b, i: (i, 0))] * 2
                + [_const_spec(a.shape) for a in tail])
    out_specs = [rows(ATT_Q_W), rows(KX_W), rows(ML_QK_W), rows(ML_GATE_W), rows(ML_GATE_W),
                 cols(ATT_KV_W), cols(ML_QK_W), cols(ML_V_W), cols(ML_V_W), cols(D), cols(D), cols(ML_GATE_W)]
    out_shape = [
        jax.ShapeDtypeStruct((B, L, ATT_Q_W), BF16),
        jax.ShapeDtypeStruct((B, L, KX_W), BF16),
        jax.ShapeDtypeStruct((B, L, ML_QK_W), BF16),
        jax.ShapeDtypeStruct((B, L, ML_GATE_W), F32),
        jax.ShapeDtypeStruct((B, L, ML_GATE_W), F32),
        jax.ShapeDtypeStruct((B, ATT_KV_W, L), BF16),
        jax.ShapeDtypeStruct((B, ML_QK_W, L), BF16),
        jax.ShapeDtypeStruct((B, ML_V_W, L), BF16),
        jax.ShapeDtypeStruct((B, ML_V_W, L), BF16),
        jax.ShapeDtypeStruct((B, D, L), BF16),
        jax.ShapeDtypeStruct((B, D, L), BF16),
        jax.ShapeDtypeStruct((B, ML_GATE_W, L), F32),
    ]
    return pl.pallas_call(
        _inproj_kernel,
        grid=(B, L // tm),
        in_specs=in_specs,
        out_specs=out_specs,
        out_shape=out_shape,
        scratch_shapes=[pltpu.VMEM((tm, D), BF16)],
        compiler_params=_cparams(("arbitrary", "arbitrary")),
        name="inproj",
    )(x, mod6, *consts, cos_t, sin_t, *tail)


def _ctxproj_call(ctx, mod6, ctx_row, n1w, w_p, w_a, w_b, e_mat, et_mat, qkw, gb, gbt, tm):
    B, C, _ = ctx.shape
    tril = _block_tri(tm, ML_CHUNK, True)
    triu = _block_tri(tm, ML_CHUNK, False)

    def rows(w):
        return pl.BlockSpec((1, tm, w), lambda b, i: (b, i, 0))

    def cols(h):
        return pl.BlockSpec((1, h, tm), lambda b, i: (b, 0, i))

    consts = [e_mat, et_mat, qkw, gb, gbt, tril, triu]
    w_specs = [_const_spec(n1w.shape), _const_spec(w_p.shape),
               pl.BlockSpec((D, ATT_KV_W), lambda b, i: (0, 0)), pl.BlockSpec((D, ML_V_W), lambda b, i: (0, 0))]
    out_specs = [rows(KX_W), rows(ML_QK_W), rows(ML_GATE_W), rows(ML_GATE_W),
                 cols(ATT_KV_W), cols(ML_V_W), cols(ML_GATE_W)]
    out_shape = [
        jax.ShapeDtypeStruct((B, C, KX_W), BF16),
        jax.ShapeDtypeStruct((B, C, ML_QK_W), BF16),
        jax.ShapeDtypeStruct((B, C, ML_GATE_W), F32),
        jax.ShapeDtypeStruct((B, C, ML_GATE_W), F32),
        jax.ShapeDtypeStruct((B, ATT_KV_W, C), BF16),
        jax.ShapeDtypeStruct((B, ML_V_W, C), BF16),
        jax.ShapeDtypeStruct((B, ML_GATE_W, C), F32),
    ]
    return pl.pallas_call(
        _ctxproj_kernel,
        grid=(B, C // tm),
        in_specs=([rows(D), _mod_spec(MOD_SH1, 2, lambda b, i: ctx_row)] + w_specs
                  + [_const_spec(a.shape) for a in consts]),
        out_specs=out_specs,
        out_shape=out_shape,
        compiler_params=_cparams(("arbitrary", "arbitrary")),
        name="ctxproj",
    )(ctx, mod6, n1w, w_p, w_a, w_b, *consts)


ATT_AHEAD = 12

ATT_QB = 4


def _attn_stream(sink_ref, q_ref, kc_ref, kp_ref, k0_ref, kn_ref, vc_ref, vp_ref, v0_ref, vn_ref, o_ref,
                 n_steps):
    i = pl.program_id(1)
    T = ATT_BLOCK
    hd = ATT_HEAD_DIM
    s_idx = lax.broadcasted_iota(jnp.int32, (T, 2 * T), 0)
    t_idx = lax.broadcasted_iota(jnp.int32, (T, 2 * T), 1) % T
    first = lax.broadcasted_iota(jnp.int32, (1, 2 * T), 1) < T

    k_own = k0_ref[0]
    v_own = v0_ref[0]
    k_blk = [kp_ref[0]] + [k_own[b * T:(b + 1) * T] for b in range(ATT_QB)] + [kn_ref[0]]
    v_blk = [vp_ref[0]] + [v_own[:, b * T:(b + 1) * T] for b in range(ATT_QB)] + [vn_ref[0]]
    ones_rows = jnp.ones((16, 3 * T + kc_ref.shape[1]), BF16)

    def window(qb):
        k_all = jnp.concatenate(k_blk[qb:qb + 3] + [kc_ref[0]], axis=0)
        vt_all = jnp.concatenate(v_blk[qb:qb + 3] + [vc_ref[0]], axis=1)
        ok_prev = (s_idx >= t_idx) & ((i > 0) if qb == 0 else True)
        ok_next = (s_idx <= t_idx) & ((i < n_steps - 1) if qb == ATT_QB - 1 else True)
        return k_all, vt_all, ok_prev, ok_next

    windows = [window(qb) for qb in range(ATT_QB)]
    per_qb = 2 * ATT_KV_HEADS

    def scores(n):
        qb, r = divmod(n, per_qb)
        kh, var = divmod(r, 2)
        q = q_ref[0, qb * T:(qb + 1) * T, :]
        q2 = jnp.concatenate([q[:, (2 * kh) * LANES:(2 * kh + 1) * LANES],
                              q[:, (2 * kh + 1) * LANES:(2 * kh + 2) * LANES]], axis=0)
        kk = windows[qb][0][:, (2 * kh + var) * LANES:(2 * kh + var + 1) * LANES]
        return _dot_nt(kk, q2)

    def finish(n, st):
        qb, r = divmod(n, per_qb)
        kh, var = divmod(r, 2)
        _, vt_all, ok_prev, ok_next = windows[qb]
        vt = vt_all[kh * hd:(kh + 1) * hd, :]
        st = jnp.concatenate([jnp.where(ok_prev, st[0:T], NEG_INF), st[T:2 * T],
                              jnp.where(ok_next, st[2 * T:3 * T], NEG_INF), st[3 * T:]], axis=0)
        h0 = ATT_GROUP * kh + var
        h1 = h0 + 2
        sink = jnp.where(first, sink_ref[h0], sink_ref[h1]) * LOG2E
        m = jnp.maximum(jnp.max(st, axis=0, keepdims=True), sink)
        p = jnp.exp2(st - m)
        ot = _dot(jnp.concatenate([vt, ones_rows], axis=0), p.astype(BF16))
        denom = ot[hd:hd + 1, :] + jnp.exp2(sink - m)
        ot = ot[0:hd, :] * (1.0 / denom)
        o_ref[0, h0 * hd:(h0 + 1) * hd, qb * T:(qb + 1) * T] = ot[:, 0:T].astype(BF16)
        o_ref[0, h1 * hd:(h1 + 1) * hd, qb * T:(qb + 1) * T] = ot[:, T:2 * T].astype(BF16)

    return ATT_QB * per_qb, scores, finish


def _attn_specs(L, C):
    T = ATT_BLOCK
    nb = L // T
    TQ = ATT_QB * T

    def edge(i, off):
        return jnp.clip(i * ATT_QB + (off if off < 0 else ATT_QB), 0, nb - 1)

    in_specs = [pl.BlockSpec(memory_space=pltpu.SMEM),
                pl.BlockSpec((1, TQ, ATT_Q_W), lambda b, i: (b, i, 0)),
                pl.BlockSpec((1, C, KX_W), lambda b, i: (b, 0, 0)),
                pl.BlockSpec((1, T, KX_W), lambda b, i: (b, edge(i, -1), 0)),
                pl.BlockSpec((1, TQ, KX_W), lambda b, i: (b, i, 0)),
                pl.BlockSpec((1, T, KX_W), lambda b, i: (b, edge(i, 1), 0)),
                pl.BlockSpec((1, ATT_KV_W, C), lambda b, i: (b, 0, 0)),
                pl.BlockSpec((1, ATT_KV_W, T), lambda b, i: (b, 0, edge(i, -1))),
                pl.BlockSpec((1, ATT_KV_W, TQ), lambda b, i: (b, 0, i)),
                pl.BlockSpec((1, ATT_KV_W, T), lambda b, i: (b, 0, edge(i, 1)))]
    return in_specs, pl.BlockSpec((1, ATT_Q_W, TQ), lambda b, i: (b, 0, i))


N_CHAIN = 2 * ML_HEADS


def _mlstm_load_state(c_ref, n_ref, m_ref):
    return [(c_ref[ci], n_ref[ci], m_ref[ci, 0:1, 0:1]) for ci in range(N_CHAIN)]


def _mlstm_phase1(dirs, state, item, with_h, cps):
    T = ML_CHUNK
    sub, rest = divmod(item, N_CHAIN)
    d, h = divmod(rest, ML_HEADS)
    qt_ref, k_ref, vt_ref, g_ref, bc_ref, br_ref, h_ref = dirs[d]
    row = lax.broadcasted_iota(jnp.int32, (T, T), 0)
    col = lax.broadcasted_iota(jnp.int32, (T, T), 1)
    mask = (col >= row) if d == 0 else (col <= row)
    last = T - 1 if d == 0 else 0
    sc = sub if d == 0 else cps - 1 - sub
    tok = slice(sc * T, (sc + 1) * T)
    ci = d * ML_HEADS + h
    gi = d * 2 * ML_HEADS + h
    fi = gi + ML_HEADS
    k = k_ref[0, tok, h * ML_QK_DIM:(h + 1) * ML_QK_DIM]
    vt = vt_ref[0, h * ML_V_DIM:(h + 1) * ML_V_DIM, tok]
    u_col = g_ref[0, tok, gi:gi + 1] - bc_ref[0, tok, fi:fi + 1]
    b_row = br_ref[0, fi:fi + 1, tok]
    ct_old, n_old, m_old = state[ci]
    qt = st = qn2 = None
    if with_h:
        qt = qt_ref[0, h * ML_QK_DIM:(h + 1) * ML_QK_DIM, tok]
        st = _dot(k, qt)
        top = lax.broadcasted_iota(jnp.int32, (8, ML_QK_DIM), 0) == 0
        n_hi = n_old.astype(BF16)
        n_lo = (n_old - n_hi.astype(F32)).astype(BF16)
        qn2 = _dot(jnp.where(top, n_hi, n_lo), qt)
    m_last = jnp.maximum(jnp.max(u_col, axis=0, keepdims=True), m_old)
    decay = jnp.exp(m_old - m_last)
    kw = (k.astype(F32) * jnp.exp(u_col - m_last)).astype(BF16)
    state[ci] = (decay * ct_old + _dot(vt, kw), decay * n_old + _dot(jnp.ones((8, T), BF16), kw),
                 b_row[:, last:last + 1] + m_last)
    return (h, h_ref, tok, mask, qt, vt, u_col, b_row, m_old, ct_old, st, qn2)


def _mlstm_phase2(chain):
    h, h_ref, tok, mask, qt, vt, u_col, b_row, m_old, ct_old, st, qn2 = chain
    umat = jnp.where(mask, u_col, -jnp.inf)
    m_row = jnp.maximum(jnp.max(umat, axis=0, keepdims=True), m_old)
    pt = st * jnp.exp(umat - m_row)
    w_int = jnp.exp(m_old - m_row)
    e_row = jnp.exp(-(b_row + m_row))
    nq = jnp.sum(pt, axis=0, keepdims=True) + w_int * (qn2[0:1, :] + qn2[1:2, :])
    den = jnp.maximum(jnp.abs(nq), e_row)
    lhs = jnp.concatenate([vt, ct_old.astype(BF16)], axis=1)
    rhs = jnp.concatenate([pt.astype(BF16), (qt.astype(F32) * w_int).astype(BF16)], axis=0)
    h_ref[0, h * ML_V_DIM:(h + 1) * ML_V_DIM, tok] = (_dot(lhs, rhs) * (1.0 / den)).astype(BF16)


def _mlstm_commit(state, c_ref, n_ref, m_ref):
    for ci, (c_new, n_new, m_new) in enumerate(state):
        c_ref[ci] = c_new
        n_ref[ci] = n_new
        m_ref[ci] = jnp.broadcast_to(m_new, (8, LANES))


def _mlstm_ctx_kernel(kf_ref, vtf_ref, gf_ref, bcf_ref, brf_ref, kb_ref, vtb_ref, gb_ref, bcb_ref, brb_ref,
                      c_ref, n_ref, m_ref, *, cps):
    @pl.when(pl.program_id(1) == 0)
    def _():
        c_ref[...] = jnp.zeros_like(c_ref)
        n_ref[...] = jnp.zeros_like(n_ref)
        m_ref[...] = jnp.zeros_like(m_ref)

    dirs = ((None, kf_ref, vtf_ref, gf_ref, bcf_ref, brf_ref, None),
            (None, kb_ref, vtb_ref, gb_ref, bcb_ref, brb_ref, None))
    state_refs = (c_ref.at[0], n_ref.at[0], m_ref.at[0])
    state = _mlstm_load_state(*state_refs)
    for item in range(cps * N_CHAIN):
        _mlstm_phase1(dirs, state, item, with_h=False, cps=cps)
    _mlstm_commit(state, *state_refs)


def _mixer_kernel(sink_ref, q_ref, kc_ref, kp_ref, k0_ref, kn_ref, vc_ref, vp_ref, v0_ref, vn_ref,
                  c0_ref, n0_ref, m0_ref, qtf_ref, kf_ref, vtf_ref, gf_ref, bcf_ref, brf_ref,
                  qtb_ref, kb_ref, vtb_ref, gb_ref, bcb_ref, brb_ref,
                  att_ref, hf_ref, hb_ref, c_ref, n_ref, m_ref, *, n_steps, cps):
    @pl.when(pl.program_id(1) == 0)
    def _():
        c_ref[...] = c0_ref[0]
        n_ref[...] = n0_ref[0]
        m_ref[...] = m0_ref[0]

    n_iter, scores, finish = _attn_stream(sink_ref, q_ref, kc_ref, kp_ref, k0_ref, kn_ref,
                                          vc_ref, vp_ref, v0_ref, vn_ref, att_ref, n_steps)
    pending = [scores(n) for n in range(ATT_AHEAD)]
    dirs = ((qtf_ref, kf_ref, vtf_ref, gf_ref, bcf_ref, brf_ref, hf_ref),
            (qtb_ref, kb_ref, vtb_ref, gb_ref, bcb_ref, brb_ref, hb_ref))
    state = _mlstm_load_state(c_ref, n_ref, m_ref)
    n_items = cps * N_CHAIN
    items = [_mlstm_phase1(dirs, state, it, with_h=True, cps=cps) for it in range(ML_AHEAD)]
    for n in range(max(n_iter, n_items)):
        if n < n_iter:
            st = pending.pop(0)
            if n + ATT_AHEAD < n_iter:
                pending.append(scores(n + ATT_AHEAD))
            finish(n, st)
        if n < n_items:
            if n + ML_AHEAD < n_items:
                items.append(_mlstm_phase1(dirs, state, n + ML_AHEAD, with_h=True, cps=cps))
            _mlstm_phase2(items.pop(0))
    _mlstm_commit(state, c_ref, n_ref, m_ref)


def _mlstm_specs(T, order, with_q):
    specs = [
        pl.BlockSpec((1, ML_QK_W, T), lambda b, j: (b, 0, order(j))),
        pl.BlockSpec((1, T, ML_QK_W), lambda b, j: (b, order(j), 0)),
        pl.BlockSpec((1, ML_V_W, T), lambda b, j: (b, 0, order(j))),
        pl.BlockSpec((1, T, ML_GATE_W), lambda b, j: (b, order(j), 0)),
        pl.BlockSpec((1, T, ML_GATE_W), lambda b, j: (b, order(j), 0)),
        pl.BlockSpec((1, ML_GATE_W, T), lambda b, j: (b, 0, order(j))),
    ]
    return specs if with_q else specs[1:]


_STATE_SHAPES = ((N_CHAIN, ML_V_DIM, ML_QK_DIM), (N_CHAIN, 8, ML_QK_DIM), (N_CHAIN, 8, LANES))


ML_AHEAD = 32
ML_CPS = 4
ML_CTX_CPS = 2


def _mlstm_ctx_call(mk, mvt, g, bc, br):
    B, C, _ = mk.shape
    T = ML_CTX_CPS * ML_CHUNK
    nc = C // T
    state_specs = [pl.BlockSpec((1,) + s, lambda b, j: (b, 0, 0, 0)) for s in _STATE_SHAPES]
    return pl.pallas_call(
        functools.partial(_mlstm_ctx_kernel, cps=ML_CTX_CPS),
        grid=(B, nc),
        in_specs=_mlstm_specs(T, lambda j: j, False) + _mlstm_specs(T, lambda j: nc - 1 - j, False),
        out_specs=state_specs,
        out_shape=[jax.ShapeDtypeStruct((B,) + s, F32) for s in _STATE_SHAPES],
        compiler_params=_cparams(("arbitrary", "arbitrary")),
        name="mlstm_ctx",
    )(mk, mvt, g, bc, br, mk, mvt, g, bc, br)


def _mixer_call(sink, q, kx, vt, kx_c, vt_c, state, mqt, mk, mvt, g, bc, br):
    B, L, _ = mk.shape
    C = kx_c.shape[1]
    T = ML_CPS * ML_CHUNK
    assert T == ATT_QB * ATT_BLOCK
    nc = L // T
    att_in, att_out = _attn_specs(L, C)
    state_specs = [pl.BlockSpec((1,) + s, lambda b, j: (b, 0, 0, 0)) for s in _STATE_SHAPES]
    out_specs = [att_out,
                 pl.BlockSpec((1, ML_V_W, T), lambda b, j: (b, 0, j)),
                 pl.BlockSpec((1, ML_V_W, T), lambda b, j: (b, 0, nc - 1 - j))]
    return pl.pallas_call(
        functools.partial(_mixer_kernel, n_steps=nc, cps=ML_CPS),
        grid=(B, nc),
        in_specs=(att_in + state_specs + _mlstm_specs(T, lambda j: j, True)
                  + _mlstm_specs(T, lambda j: nc - 1 - j, True)),
        out_specs=out_specs,
        out_shape=[jax.ShapeDtypeStruct((B, ATT_Q_W, L), BF16)] + [jax.ShapeDtypeStruct((B, ML_V_W, L), BF16)] * 2,
        scratch_shapes=[pltpu.VMEM(s, F32) for s in _STATE_SHAPES],
        compiler_params=_cparams(("arbitrary", "arbitrary")),
        name="mixer",
    )(sink, q, kx_c, kx, kx, kx, vt_c, vt, vt, vt, *state, mqt, mk, mvt, g, bc, br, mqt, mk, mvt, g, bc, br)


def _merge_kernel(att_ref, hf_ref, hb_ref, so_ref, sga_ref, sgm_ref, x_ref, g1_ref, mod2_ref, mlw_ref, n2w_ref,
                  wa_ref, wm_ref, wo_ref, xmid_ref, h2_ref, y_ref):
    s = pl.program_id(0)
    last = pl.num_programs(0) - 1
    cur = s % 2

    def body(do_branch, do_out):
        if do_out:
            y2 = _dot_tn(y_ref[1 - cur], wo_ref[...])
        if do_branch:
            ya = _dot_tn(wa_ref[...], att_ref[0])
            ht = hf_ref[0].astype(F32) + hb_ref[0].astype(F32)
            parts = []
            for h in range(ML_HEADS):
                seg = ht[h * ML_V_DIM:(h + 1) * ML_V_DIM, :]
                ms = jnp.mean(seg * seg, axis=0, keepdims=True)
                parts.append(seg * lax.rsqrt(ms + EPS))
            ml = (jnp.concatenate(parts, axis=0) * mlw_ref[...] * so_ref[0].astype(F32)).astype(BF16)
            ym = _dot_tn(wm_ref[...], ml)
        if do_out:
            xm = x_ref[0] + g1_ref[0, 0] * y2
            xmid_ref[0] = xm
            ms = jnp.mean(xm * xm, axis=-1, keepdims=True)
            h2 = xm * lax.rsqrt(ms + EPS) * n2w_ref[...]
            h2_ref[0] = (h2 * (1.0 + mod2_ref[1, 0]) + mod2_ref[0, 0]).astype(BF16)
        if do_branch:
            y_ref[cur] = (sga_ref[0].astype(F32) * ya + sgm_ref[0].astype(F32) * ym).astype(BF16)

    pl.when(s == 0)(lambda: body(True, False))
    pl.when((s > 0) & (s < last))(lambda: body(True, True))
    pl.when(s == last)(lambda: body(False, True))


def _merge_call(att_t, hf_t, hb_t, so_t, sga_t, sgm_t, x, mod6, mlw_b, n2w, wa, wm, wo, tm):
    B, L, _ = x.shape
    nt = L // tm
    n_all = B * nt

    def tile_in(s):
        t = jnp.minimum(s, n_all - 1)
        return t // nt, t % nt

    def tile_out(s):
        t = jnp.maximum(s - 1, 0)
        return t // nt, t % nt

    def in_t(h):
        return pl.BlockSpec((1, h, tm), lambda s: (tile_in(s)[0], 0, tile_in(s)[1]))

    def out_rows(w):
        return pl.BlockSpec((1, tm, w), lambda s: (*tile_out(s), 0))

    def const(shape):
        return pl.BlockSpec(shape, lambda s: (0,) * len(shape))

    return pl.pallas_call(
        _merge_kernel,
        grid=(n_all + 1,),
        in_specs=[in_t(D), in_t(D), in_t(D), in_t(D), in_t(D), in_t(D), out_rows(D),
                  _mod_spec(MOD_G1, 1, lambda s: tile_out(s)[0]),
                  _mod_spec(MOD_SH2, 2, lambda s: tile_out(s)[0]),
                  const((D, tm)), const((1, D)), const((D, D)), const((D, D)), const((D, D))],
        out_specs=[out_rows(D), out_rows(D)],
        out_shape=[jax.ShapeDtypeStruct((B, L, D), F32), jax.ShapeDtypeStruct((B, L, D), BF16)],
        scratch_shapes=[pltpu.VMEM((2, D, tm), BF16)],
        compiler_params=_cparams(("arbitrary",)),
        name="merge",
    )(att_t, hf_t, hb_t, so_t, sga_t, sgm_t, x, mod6, mod6, mlw_b, n2w, wa, wm, wo)


HALO = 16
FFN_AHEAD = 1


def _ffn_kernel(h_ref, hp_ref, hn_ref, xmid_ref, mod_ref, wup_ref, cw_ref, cb_ref, wdn_ref, o_ref,
                act_ref, *, n_tiles, tn, dn):
    s = pl.program_id(0)
    last = pl.num_programs(0) - 1

    @pl.when(s == 0)
    def _():
        _ffn_body(h_ref, hp_ref, hn_ref, xmid_ref, mod_ref, wup_ref, cw_ref, cb_ref, wdn_ref, o_ref, act_ref,
                  n_tiles=n_tiles, tn=tn, dn=dn, do_up=True, do_down=False)

    @pl.when((s > 0) & (s < last))
    def _():
        _ffn_body(h_ref, hp_ref, hn_ref, xmid_ref, mod_ref, wup_ref, cw_ref, cb_ref, wdn_ref, o_ref, act_ref,
                  n_tiles=n_tiles, tn=tn, dn=dn, do_up=True, do_down=True)

    @pl.when(s == last)
    def _():
        _ffn_body(h_ref, hp_ref, hn_ref, xmid_ref, mod_ref, wup_ref, cw_ref, cb_ref, wdn_ref, o_ref, act_ref,
                  n_tiles=n_tiles, tn=tn, dn=dn, do_up=False, do_down=True)


def _ffn_body(h_ref, hp_ref, hn_ref, xmid_ref, mod_ref, wup_ref, cw_ref, cb_ref, wdn_ref, o_ref, act_ref,
              *, n_tiles, tn, dn, do_up, do_down):
    s = pl.program_id(0)
    i = s % n_tiles
    cur = s % 2
    tm = h_ref.shape[1]
    n_chunks = D_FF // tn
    n_dn = D // dn
    act_prev = act_ref[1 - cur] if do_down else None

    def down(k):
        cols = slice(k * dn, (k + 1) * dn)
        o_ref[0, :, cols] = xmid_ref[0, :, cols] + mod_ref[0, 0, :, cols] * _dot(act_prev, wdn_ref[:, cols])

    if not do_up:
        for k in range(n_dn):
            down(k)
        return

    h = h_ref[0]
    prev_row = jnp.where(i > 0, hp_ref[0].astype(F32)[HALO - 1:HALO, :], 0.0)
    next_row = jnp.where(i < n_tiles - 1, hn_ref[0].astype(F32)[0:1, :], 0.0)
    top = lax.broadcasted_iota(jnp.int32, (16, D), 0) < 8
    edge = jnp.where(top, prev_row, next_row).astype(BF16)
    row8 = lax.broadcasted_iota(jnp.int32, (8, tn), 0)
    h_ext = jnp.concatenate([h, edge], axis=0)

    def up(c0):
        u_ext = _dot(h_ext, wup_ref[:, c0:c0 + tn])
        return u_ext[:tm], u_ext[tm:]

    def conv(u, ue, c0):
        below = pltpu.roll(u, 1, 0)
        above = pltpu.roll(u, tm - 1, 0)
        below = jnp.concatenate([jnp.where(row8 == 0, ue[0:8], below[0:8]), below[8:]], axis=0)
        above = jnp.concatenate([above[:tm - 8], jnp.where(row8 == 7, ue[8:16], above[tm - 8:])], axis=0)
        cw = cw_ref[:, c0:c0 + tn]
        return cb_ref[:, c0:c0 + tn] + below * cw[0:1] + u * cw[1:2] + above * cw[2:3]

    pending = [(up(c * tn), up(D_FF + c * tn)) for c in range(FFN_AHEAD)]
    done = 0
    for c in range(n_chunks):
        (ua, uae), (ug, uge) = pending.pop(0)
        if c + FFN_AHEAD < n_chunks:
            pending.append((up((c + FFN_AHEAD) * tn), up(D_FF + (c + FFN_AHEAD) * tn)))
        while do_down and done * n_chunks < (c + 1) * n_dn:
            down(done)
            done += 1
        a = conv(ua, uae, c * tn)
        hg = conv(ug, uge, D_FF + c * tn)
        act_ref[cur, :, c * tn:(c + 1) * tn] = ((hg + hg * jnp.tanh(hg)) * a).astype(BF16)


def _ffn_call(h2, xmid, g2, wup, cw, cb, wdn, tm, tn, dn):
    B, L, _ = xmid.shape
    nt = L // tm
    n_all = B * nt
    hb = tm // HALO
    nhb = L // HALO

    def tile_in(s):
        t = jnp.minimum(s, n_all - 1)
        return t // nt, t % nt

    def tile_out(s):
        t = jnp.maximum(s - 1, 0)
        return t // nt, t % nt

    def in_spec():
        return pl.BlockSpec((1, tm, D), lambda s: (*tile_in(s), 0))

    def out_spec():
        return pl.BlockSpec((1, tm, D), lambda s: (*tile_out(s), 0))

    def prev_halo(s):
        b, i = tile_in(s)
        return b, jnp.maximum(i * hb - 1, 0), 0

    def next_halo(s):
        b, i = tile_in(s)
        return b, jnp.minimum((i + 1) * hb, nhb - 1), 0

    def const(shape):
        return pl.BlockSpec(shape, lambda s: (0,) * len(shape))

    return pl.pallas_call(
        functools.partial(_ffn_kernel, n_tiles=nt, tn=tn, dn=dn),
        grid=(n_all + 1,),
        in_specs=[in_spec(),
                  pl.BlockSpec((1, HALO, D), prev_halo),
                  pl.BlockSpec((1, HALO, D), next_halo),
                  out_spec(),
                  _mod_spec(MOD_G2, 1, lambda s: tile_out(s)[0]),
                  const(wup.shape), const(cw.shape), const(cb.shape), const(wdn.shape)],
        out_specs=out_spec(),
        out_shape=jax.ShapeDtypeStruct((B, L, D), F32),
        scratch_shapes=[pltpu.VMEM((2, tm, D_FF), BF16)],
        compiler_params=_cparams(("arbitrary",)),
        name="ffn",
    )(h2, h2, h2, xmid, g2, wup, cw, cb, wdn)


def _pair_perm(n_heads):
    half = ATT_HEAD_DIM // 2
    idx = []
    for p in range(n_heads // 2):
        for sub in range(4):
            head = 2 * p + (sub % 2)
            d0 = (sub // 2) * half
            idx.extend(head * ATT_HEAD_DIM + d0 + e for e in range(half))
    return np.asarray(idx, np.int32)


def _rope_tables(L):
    f32 = np.float32
    rows = L // GRID_W
    row = np.repeat(np.arange(rows, dtype=f32), GRID_W)
    col = np.tile(np.arange(GRID_W, dtype=f32), rows)
    n_freq = ATT_HEAD_DIM // 4
    inv_freq = (f32(ROPE_BASE) ** (-np.arange(n_freq, dtype=f32) / f32(n_freq))).astype(f32)
    ang = np.concatenate([row[:, None] * inv_freq, col[:, None] * inv_freq], axis=-1).astype(f32)
    cos = np.tile(np.cos(ang).astype(f32), (1, 4))
    sin = np.tile(np.sin(ang).astype(f32), (1, 4))
    sign = np.where(np.arange(LANES) < LANES // 2, -1.0, 1.0).astype(f32)
    return jnp.asarray(cos), jnp.asarray(sin * sign)


def kernel(x, c, ctx, c_ctx, w_mod, b_mod, norm1_w, w_in, q_norm_w, k_norm_w, attn_sink, ml_gate_b, ml_norm_w,
           w_branch_att, w_branch_ml, w_out, norm2_w, w_up, conv_w, conv_b, w_down):
    B, L, _ = x.shape
    C = ctx.shape[1]
    assert L % 512 == 0 and C % 256 == 0 and L % GRID_W == 0
    l = 0
    tm_merge = 512

    n_rows = -(-(B + 1) // 16) * 16
    cc = jnp.concatenate([c, c_ctx[None, :], jnp.zeros((n_rows - B - 1, D), F32)], axis=0)
    mod6 = _mod_call(cc, w_mod[l], b_mod[l][None, :])

    w = w_in[l]
    qperm = _pair_perm(ATT_HEADS)
    kperm = _pair_perm(ATT_KV_HEADS)
    def pair_cols(wc, n_heads):
        half = ATT_HEAD_DIM // 2
        wc = wc.reshape(D, n_heads // 2, 2, 2, half).transpose(0, 1, 3, 2, 4)
        return wc.reshape(D, n_heads * ATT_HEAD_DIM)

    w_q = pair_cols(w[:, _O_AQ:_O_AQ + ATT_Q_W], ATT_HEADS)
    w_k = pair_cols(w[:, _O_AK:_O_AK + ATT_KV_W], ATT_KV_HEADS)
    w_g = jnp.pad(w[:, _O_MG:_O_MG + ML_GATE_W], ((0, 0), (0, LANES - ML_GATE_W)))
    w_mk = w[:, _O_MK:_O_MK + ML_QK_W] * (ML_QK_DIM ** -0.5)
    w_p = jnp.concatenate([w_q, w_k, w_mk, w_g], axis=1).astype(BF16)
    w_a = w[:, _A_COLS[0]:_A_COLS[1]].astype(BF16)
    w_b = w[:, _B_COLS[0]:_B_COLS[1]].astype(BF16)
    w_c = w[:, _C_COLS[0]:_C_COLS[1]].astype(BF16)

    head_of_col = np.concatenate([qperm // ATT_HEAD_DIM, ATT_HEADS + kperm // ATT_HEAD_DIM])
    e_np = (head_of_col[:, None] == np.arange(LANES)[None, :]).astype(np.float32)
    e_mat = jnp.asarray(e_np, BF16)
    et_mat = jnp.asarray(np.concatenate([e_np.T, e_np.T], axis=0), BF16)
    def pair_tiled(wn, n_heads):
        half = ATT_HEAD_DIM // 2
        return jnp.tile(jnp.concatenate([wn[:half], wn[:half], wn[half:], wn[half:]]), n_heads // 2)

    qkw = jnp.concatenate([pair_tiled(q_norm_w[l], ATT_HEADS) * (ATT_SCALE * LOG2E),
                           pair_tiled(k_norm_w[l], ATT_KV_HEADS)])[None, :]
    cos_t, sin_t = _rope_tables(L)
    gb = ml_gate_b[l].reshape(1, ML_GATE_W)
    gbt = ml_gate_b[l].reshape(ML_GATE_W, 1)
    n1w = norm1_w[l][None, :]

    kx_c, mk_c, g_c, bc_c, vt_c, mvt_c, br_c = _ctxproj_call(
        ctx, mod6, B, n1w, w_p, w_a, w_b, e_mat, et_mat, qkw, gb, gbt, tm=256)
    q, kx, mk, g, bc, vt, mqt, mvt, sot, sgat, sgmt, br = _inproj_call(
        x, mod6, n1w, w_p, w_a, w_b, w_c, e_mat, et_mat, qkw, cos_t, sin_t, gb, gbt, tm=512)

    state = _mlstm_ctx_call(mk_c, mvt_c, g_c, bc_c, br_c)
    att_t, hf_t, hb_t = _mixer_call(attn_sink[l], q, kx, vt, kx_c, vt_c, state, mqt, mk, mvt, g, bc, br)

    mlw_b = jnp.broadcast_to(ml_norm_w[l][:, None], (ML_V_W, tm_merge))
    xmid, h2 = _merge_call(att_t, hf_t, hb_t, sot, sgat, sgmt, x, mod6, mlw_b, norm2_w[l][None, :],
                           w_branch_att[l].astype(BF16), w_branch_ml[l].astype(BF16), w_out[l].astype(BF16),
                           tm=tm_merge)
    gate_half = jnp.where(jnp.arange(2 * D_FF) < D_FF, 1.0, 0.5).astype(F32)
    out = _ffn_call(h2, xmid, mod6, w_up[l].astype(BF16), conv_w[l] * gate_half, (conv_b[l] * gate_half)[None, :],
                    w_down[l].astype(BF16), tm=512, tn=256, dn=256)
    return out
```

```python
import functools

import jax
import jax.numpy as jnp
import numpy as np
from jax import lax
from jax.experimental import pallas as pl
from jax.experimental.pallas import tpu as pltpu

D = 1024
GRID_W = 64
ATT_HEADS = 16
ATT_KV_HEADS = 4
ATT_HEAD_DIM = 64
ATT_GROUP = ATT_HEADS // ATT_KV_HEADS
ATT_BLOCK = 128
WINDOW = 128
ROPE_BASE = 10000.0
ATT_SCALE = ATT_HEAD_DIM ** -0.5
LOG2E = 1.4426950408889634
ML_HEADS = 4
ML_QK_DIM = 128
ML_V_DIM = 256
ML_CHUNK = 128
D_FF = 2816
EPS = 1e-6
NEG_INF = -1e30

ATT_Q_W = ATT_HEADS * ATT_HEAD_DIM
ATT_KV_W = ATT_KV_HEADS * ATT_HEAD_DIM
ML_QK_W = ML_HEADS * ML_QK_DIM
ML_V_W = ML_HEADS * ML_V_DIM
ML_GATE_W = 2 * 2 * ML_HEADS

LANES = 128
KX_W = ATT_KV_HEADS * 2 * LANES
VMEM_LIMIT = 56 * 1024 * 1024

BF16 = jnp.bfloat16
F32 = jnp.float32

_O_AQ = 0
_O_AK = _O_AQ + ATT_Q_W
_O_AV = _O_AK + ATT_KV_W
_O_MQ = _O_AV + ATT_KV_W
_O_MK = _O_MQ + ML_QK_W
_O_MV = _O_MK + ML_QK_W
_O_MO = _O_MV + ML_V_W
_O_MG = _O_MO + ML_V_W
_O_GA = _O_MG + ML_GATE_W
_O_GM = _O_GA + D

QK_W = ATT_Q_W + ATT_KV_W
_P_QK = 0
_P_MK = _P_QK + QK_W
_P_MG = _P_MK + ML_QK_W
_P_END = _P_MG + LANES
_A_COLS = (_O_AV, _O_MK)
_B_COLS = (_O_MV, _O_MG)
_C_COLS = (_O_MG, _O_GM + D)
_C_OFF = _O_GA - _O_MG


def _dot(a, b):
    return jnp.dot(a, b, preferred_element_type=F32)


def _dot_nt(a, b):
    return lax.dot_general(a, b, (((1,), (1,)), ((), ())), preferred_element_type=F32)


def _dot_tn(a, b):
    return lax.dot_general(a, b, (((0,), (0,)), ((), ())), preferred_element_type=F32)


def _dot_ft(w, x):
    return lax.dot_general(w, x, (((0,), (1,)), ((), ())), preferred_element_type=F32)


def _cparams(sem):
    return pltpu.CompilerParams(dimension_semantics=sem, vmem_limit_bytes=VMEM_LIMIT)


def _mod_kernel(c_ref, w_ref, b_ref, o_ref):
    c = c_ref[...]
    a = c * jax.nn.sigmoid(c)
    a_hi = a.astype(BF16)
    a_lo = (a - a_hi.astype(F32)).astype(BF16)
    w = w_ref[...]
    w_hi = w.astype(BF16)
    w_lo = (w - w_hi.astype(F32)).astype(BF16)
    rows = a.shape[0]
    both = _dot(jnp.concatenate([a_hi, a_lo], axis=0), w_hi)
    o_ref[0, :, 0, :] = both[:rows] + both[rows:] + _dot(a_hi, w_lo) + b_ref[...]


MOD_SH1, MOD_SC1, MOD_SH2, MOD_SC2, MOD_G1, MOD_G2 = range(6)


def _mod_call(cc, w_mod, b_mod):
    rows = cc.shape[0]
    n_seg = w_mod.shape[1] // D
    assert n_seg == 6

    def out_pos(j):
        return jnp.where(j == 2, MOD_G1, jnp.where((j == 3) | (j == 4), j - 1, j))

    return pl.pallas_call(
        _mod_kernel,
        grid=(n_seg,),
        in_specs=[pl.BlockSpec((rows, D), lambda j: (0, 0)),
                  pl.BlockSpec((D, D), lambda j: (0, j)),
                  pl.BlockSpec((1, D), lambda j: (0, j))],
        out_specs=pl.BlockSpec((1, rows, 1, D), lambda j: (out_pos(j), 0, 0, 0)),
        out_shape=jax.ShapeDtypeStruct((n_seg, rows, 1, D), F32),
        compiler_params=_cparams(("arbitrary",)),
        name="mod",
    )(cc, w_mod, b_mod)


def _mod_spec(seg, n_seg, row_of):
    return pl.BlockSpec((n_seg, 1, 1, D), lambda *ids: (seg // n_seg, row_of(*ids), 0, 0))


def _split2(x):
    x1 = x.astype(BF16)
    x2 = (x - x1.astype(F32)).astype(BF16)
    return x1, x2


def _norm_modulate(x, n1w_ref, mod_ref):
    ms = jnp.mean(x * x, axis=-1, keepdims=True)
    y = x * lax.rsqrt(ms + EPS) * n1w_ref[...]
    return (y * (1.0 + mod_ref[1, 0]) + mod_ref[0, 0]).astype(BF16)


def _sigmoid(x):
    return 0.5 * jnp.tanh(0.5 * x) + 0.5


def _head_rms_scale(ss):
    r = lax.rsqrt(ss * (1.0 / ATT_HEAD_DIM) + EPS)
    r_hi = r.astype(BF16)
    r_lo = (r - r_hi.astype(F32)).astype(BF16)
    return jnp.concatenate([r_hi, r_lo], axis=1)


def _store_k_variants(k_ref, pair, o):
    lane = lax.broadcasted_iota(jnp.int32, (1, LANES), 1)
    keep = ((lane // 32) % 2) == 0
    c0 = 4 * pair * LANES
    k_ref[0, :, c0:c0 + LANES] = jnp.where(keep, o, 0.0).astype(BF16)
    k_ref[0, :, c0 + LANES:c0 + 2 * LANES] = jnp.where(keep, 0.0, pltpu.roll(o, 32, 1)).astype(BF16)
    k_ref[0, :, c0 + 2 * LANES:c0 + 3 * LANES] = jnp.where(keep, pltpu.roll(o, 96, 1), 0.0).astype(BF16)
    k_ref[0, :, c0 + 3 * LANES:c0 + 4 * LANES] = jnp.where(keep, 0.0, o).astype(BF16)


def _cum_gates_cols(g16, tri_lo, tri_up):
    parts = _split2(jax.nn.log_sigmoid(g16))
    fwd_col = lax.broadcasted_iota(jnp.int32, (1, ML_GATE_W), 1) < ML_GATE_W // 2
    return jnp.where(fwd_col, sum(_dot(tri_lo, p) for p in parts), sum(_dot(tri_up, p) for p in parts))


def _cum_gates_rows(gt16, tri_lo, tri_up):
    parts = _split2(jax.nn.log_sigmoid(gt16))
    fwd_row = lax.broadcasted_iota(jnp.int32, (ML_GATE_W, 1), 0) < ML_GATE_W // 2
    return jnp.where(fwd_row, sum(_dot(p, tri_up) for p in parts), sum(_dot(p, tri_lo) for p in parts))


def _inproj_kernel(x_ref, mod_ref, n1w_ref, w_ref, wa_ref, wb_ref, wc_ref, e_ref, et_ref, qkw_ref,
                   cos_ref, sin_ref, gb_ref, gbt_ref, tril_ref, triu_ref,
                   q_ref, k_ref, mk_ref, g_ref, bc_ref, vt_ref, mqt_ref, mvt_ref, sot_ref, sgat_ref, sgmt_ref, br_ref,
                   hn_ref):
    hn_ref[...] = _norm_modulate(x_ref[0], n1w_ref, mod_ref)
    hn = hn_ref[...]

    def ft(ref, c0, width):
        return _dot_ft(ref[:, c0:c0 + width], hn)

    acc = _dot(hn, w_ref[:, _P_QK:_P_QK + QK_W])
    g16 = _dot(hn, w_ref[:, _P_MG:_P_MG + LANES])[:, :ML_GATE_W] + gb_ref[...]
    gt16 = ft(w_ref, _P_MG, LANES)[:ML_GATE_W, :] + gbt_ref[...]
    g_ref[0] = g16
    mk_ref[0] = _dot(hn, w_ref[:, _P_MK:_P_MK + ML_QK_W]).astype(BF16)
    ss = _dot((acc * acc).astype(BF16), e_ref[...])
    vt_ref[0] = ft(wa_ref, 0, ATT_KV_W).astype(BF16)
    sot_ref[0] = _sigmoid(ft(wb_ref, ML_V_W, ML_V_W)).astype(BF16)
    rb = _dot(_head_rms_scale(ss), et_ref[...])
    sgat_ref[0] = _sigmoid(ft(wc_ref, _C_OFF, D)).astype(BF16)
    bc_ref[0] = _cum_gates_cols(g16, tril_ref[...], triu_ref[...])

    qn = acc * rb * qkw_ref[...]
    cos = cos_ref[...]
    sin = sin_ref[...]
    for gi in range(QK_W // LANES):
        xs = qn[:, gi * LANES:(gi + 1) * LANES]
        o = xs * cos + pltpu.roll(xs, LANES // 2, 1) * sin
        if gi < ATT_Q_W // LANES:
            q_ref[0, :, gi * LANES:(gi + 1) * LANES] = o.astype(BF16)
        else:
            _store_k_variants(k_ref, gi - ATT_Q_W // LANES, o)

    sgmt_ref[0] = _sigmoid(ft(wc_ref, _C_OFF + D, D)).astype(BF16)
    br_ref[0] = _cum_gates_rows(gt16, tril_ref[...], triu_ref[...])
    mqt_ref[0] = ft(wa_ref, ATT_KV_W, ML_QK_W).astype(BF16)
    mvt_ref[0] = ft(wb_ref, 0, ML_V_W).astype(BF16)


def _ctxproj_kernel(x_ref, mod_ref, n1w_ref, w_ref, wv_ref, wmv_ref, e_ref, et_ref, qkw_ref,
                    gb_ref, gbt_ref, tril_ref, triu_ref,
                    k_ref, mk_ref, g_ref, bc_ref, vt_ref, mvt_ref, br_ref):
    hn = _norm_modulate(x_ref[0], n1w_ref, mod_ref)
    acc = _dot(hn, w_ref[:, _P_QK + ATT_Q_W:_P_QK + QK_W])
    g16 = _dot(hn, w_ref[:, _P_MG:_P_MG + LANES])[:, :ML_GATE_W] + gb_ref[...]
    gt16 = _dot_ft(w_ref[:, _P_MG:_P_MG + LANES], hn)[:ML_GATE_W, :] + gbt_ref[...]
    g_ref[0] = g16
    mk_ref[0] = _dot(hn, w_ref[:, _P_MK:_P_MK + ML_QK_W]).astype(BF16)
    ss = _dot((acc * acc).astype(BF16), e_ref[ATT_Q_W:QK_W, :])
    vt_ref[0] = _dot_ft(wv_ref[...], hn).astype(BF16)
    rb = _dot(_head_rms_scale(ss), et_ref[:, ATT_Q_W:QK_W])
    mvt_ref[0] = _dot_ft(wmv_ref[...], hn).astype(BF16)
    bc_ref[0] = _cum_gates_cols(g16, tril_ref[...], triu_ref[...])
    kn = acc * rb * qkw_ref[:, ATT_Q_W:QK_W]
    for pair in range(ATT_KV_W // LANES):
        _store_k_variants(k_ref, pair, kn[:, pair * LANES:(pair + 1) * LANES])
    br_ref[0] = _cum_gates_rows(gt16, tril_ref[...], triu_ref[...])


def _block_tri(n, block, lower):
    r = np.arange(n)[:, None]
    c = np.arange(n)[None, :]
    same = (r // block) == (c // block)
    return jnp.asarray(same & ((c <= r) if lower else (c >= r)), BF16)


def _const_spec(shape):
    return pl.BlockSpec(shape, lambda b, i: (0,) * len(shape))


def _inproj_call(x, mod6, n1w, w_p, w_a, w_b, w_c, e_mat, et_mat, qkw, cos_t, sin_t, gb, gbt, tm):
    B, L, _ = x.shape
    tril = _block_tri(tm, ML_CHUNK, True)
    triu = _block_tri(tm, ML_CHUNK, False)

    def rows(w):
        return pl.BlockSpec((1, tm, w), lambda b, i: (b, i, 0))

    def cols(h):
        return pl.BlockSpec((1, h, tm), lambda b, i: (b, 0, i))

    consts = [n1w, w_p, w_a, w_b, w_c, e_mat, et_mat, qkw]
    tail = [gb, gbt, tril, triu]
    in_specs = ([rows(D), _mod_spec(MOD_SH1, 2, lambda b, i: b)]
                + [_const_spec(a.shape) for a in consts]
                + [pl.BlockSpec((tm, LANES), lambda b, i: (i, 0))] * 2
                + [_const_spec(a.shape) for a in tail])
    out_specs = [rows(ATT_Q_W), rows(KX_W), rows(ML_QK_W), rows(ML_GATE_W), rows(ML_GATE_W),
                 cols(ATT_KV_W), cols(ML_QK_W), cols(ML_V_W), cols(ML_V_W), cols(D), cols(D), cols(ML_GATE_W)]
    out_shape = [
        jax.ShapeDtypeStruct((B, L, ATT_Q_W), BF16),
        jax.ShapeDtypeStruct((B, L, KX_W), BF16),
        jax.ShapeDtypeStruct((B, L, ML_QK_W), BF16),
        jax.ShapeDtypeStruct((B, L, ML_GATE_W), F32),
        jax.ShapeDtypeStruct((B, L, ML_GATE_W), F32),
        jax.ShapeDtypeStruct((B, ATT_KV_W, L), BF16),
        jax.ShapeDtypeStruct((B, ML_QK_W, L), BF16),
        jax.ShapeDtypeStruct((B, ML_V_W, L), BF16),
        jax.ShapeDtypeStruct((B, ML_V_W, L), BF16),
        jax.ShapeDtypeStruct((B, D, L), BF16),
        jax.ShapeDtypeStruct((B, D, L), BF16),
        jax.ShapeDtypeStruct((B, ML_GATE_W, L), F32),
    ]
    return pl.pallas_call(
        _inproj_kernel,
        grid=(B, L // tm),
        in_specs=in_specs,
        out_specs=out_specs,
        out_shape=out_shape,
        scratch_shapes=[pltpu.VMEM((tm, D), BF16)],
        compiler_params=_cparams(("arbitrary", "arbitrary")),
        name="inproj",
    )(x, mod6, *consts, cos_t, sin_t, *tail)


def _ctxproj_call(ctx, mod6, ctx_row, n1w, w_p, w_a, w_b, e_mat, et_mat, qkw, gb, gbt, tm):
    B, C, _ = ctx.shape
    tril = _block_tri(tm, ML_CHUNK, True)
    triu = _block_tri(tm, ML_CHUNK, False)

    def rows(w):
        return pl.BlockSpec((1, tm, w), lambda b, i: (b, i, 0))

    def cols(h):
        return pl.BlockSpec((1, h, tm), lambda b, i: (b, 0, i))

    consts = [e_mat, et_mat, qkw, gb, gbt, tril, triu]
    w_specs = [_const_spec(n1w.shape), _const_spec(w_p.shape),
               pl.BlockSpec((D, ATT_KV_W), lambda b, i: (0, 0)), pl.BlockSpec((D, ML_V_W), lambda b, i: (0, 0))]
    out_specs = [rows(KX_W), rows(ML_QK_W), rows(ML_GATE_W), rows(ML_GATE_W),
                 cols(ATT_KV_W), cols(ML_V_W), cols(ML_GATE_W)]
    out_shape = [
        jax.ShapeDtypeStruct((B, C, KX_W), BF16),
        jax.ShapeDtypeStruct((B, C, ML_QK_W), BF16),
        jax.ShapeDtypeStruct((B, C, ML_GATE_W), F32),
        jax.ShapeDtypeStruct((B, C, ML_GATE_W), F32),
        jax.ShapeDtypeStruct((B, ATT_KV_W, C), BF16),
        jax.ShapeDtypeStruct((B, ML_V_W, C), BF16),
        jax.ShapeDtypeStruct((B, ML_GATE_W, C), F32),
    ]
    return pl.pallas_call(
        _ctxproj_kernel,
        grid=(B, C // tm),
        in_specs=([rows(D), _mod_spec(MOD_SH1, 2, lambda b, i: ctx_row)] + w_specs
                  + [_const_spec(a.shape) for a in consts]),
        out_specs=out_specs,
        out_shape=out_shape,
        compiler_params=_cparams(("arbitrary", "arbitrary")),
        name="ctxproj",
    )(ctx, mod6, n1w, w_p, w_a, w_b, *consts)


ATT_AHEAD = 12

ATT_QB = 4


def _attn_stream(sink_ref, q_ref, kc_ref, kp_ref, k0_ref, kn_ref, vc_ref, vp_ref, v0_ref, vn_ref, o_ref,
                 n_steps):
    i = pl.program_id(1)
    T = ATT_BLOCK
    hd = ATT_HEAD_DIM
    s_idx = lax.broadcasted_iota(jnp.int32, (T, 2 * T), 0)
    t_idx = lax.broadcasted_iota(jnp.int32, (T, 2 * T), 1) % T
    first = lax.broadcasted_iota(jnp.int32, (1, 2 * T), 1) < T

    k_own = k0_ref[0]
    v_own = v0_ref[0]
    k_blk = [kp_ref[0]] + [k_own[b * T:(b + 1) * T] for b in range(ATT_QB)] + [kn_ref[0]]
    v_blk = [vp_ref[0]] + [v_own[:, b * T:(b + 1) * T] for b in range(ATT_QB)] + [vn_ref[0]]
    ones_rows = jnp.ones((16, 3 * T + kc_ref.shape[1]), BF16)

    def window(qb):
        k_all = jnp.concatenate(k_blk[qb:qb + 3] + [kc_ref[0]], axis=0)
        vt_all = jnp.concatenate(v_blk[qb:qb + 3] + [vc_ref[0]], axis=1)
        ok_prev = (s_idx >= t_idx) & ((i > 0) if qb == 0 else True)
        ok_next = (s_idx <= t_idx) & ((i < n_steps - 1) if qb == ATT_QB - 1 else True)
        return k_all, vt_all, ok_prev, ok_next

    windows = [window(qb) for qb in range(ATT_QB)]
    per_qb = 2 * ATT_KV_HEADS

    def scores(n):
        qb, r = divmod(n, per_qb)
        kh, var = divmod(r, 2)
        q = q_ref[0, qb * T:(qb + 1) * T, :]
        q2 = jnp.concatenate([q[:, (2 * kh) * LANES:(2 * kh + 1) * LANES],
                              q[:, (2 * kh + 1) * LANES:(2 * kh + 2) * LANES]], axis=0)
        kk = windows[qb][0][:, (2 * kh + var) * LANES:(2 * kh + var + 1) * LANES]
        return _dot_nt(kk, q2)

    def finish(n, st):
        qb, r = divmod(n, per_qb)
        kh, var = divmod(r, 2)
        _, vt_all, ok_prev, ok_next = windows[qb]
        vt = vt_all[kh * hd:(kh + 1) * hd, :]
        st = jnp.concatenate([jnp.where(ok_prev, st[0:T], NEG_INF), st[T:2 * T],
                              jnp.where(ok_next, st[2 * T:3 * T], NEG_INF), st[3 * T:]], axis=0)
        h0 = ATT_GROUP * kh + var
        h1 = h0 + 2
        sink = jnp.where(first, sink_ref[h0], sink_ref[h1]) * LOG2E
        m = jnp.maximum(jnp.max(st, axis=0, keepdims=True), sink)
        p = jnp.exp2(st - m)
        ot = _dot(jnp.concatenate([vt, ones_rows], axis=0), p.astype(BF16))
        denom = ot[hd:hd + 1, :] + jnp.exp2(sink - m)
        ot = ot[0:hd, :] * (1.0 / denom)
        o_ref[0, h0 * hd:(h0 + 1) * hd, qb * T:(qb + 1) * T] = ot[:, 0:T].astype(BF16)
        o_ref[0, h1 * hd:(h1 + 1) * hd, qb * T:(qb + 1) * T] = ot[:, T:2 * T].astype(BF16)

    return ATT_QB * per_qb, scores, finish


def _attn_specs(L, C):
    T = ATT_BLOCK
    nb = L // T
    TQ = ATT_QB * T

    def edge(i, off):
        return jnp.clip(i * ATT_QB + (off if off < 0 else ATT_QB), 0, nb - 1)

    in_specs = [pl.BlockSpec(memory_space=pltpu.SMEM),
                pl.BlockSpec((1, TQ, ATT_Q_W), lambda b, i: (b, i, 0)),
                pl.BlockSpec((1, C, KX_W), lambda b, i: (b, 0, 0)),
                pl.BlockSpec((1, T, KX_W), lambda b, i: (b, edge(i, -1), 0)),
                pl.BlockSpec((1, TQ, KX_W), lambda b, i: (b, i, 0)),
                pl.BlockSpec((1, T, KX_W), lambda b, i: (b, edge(i, 1), 0)),
                pl.BlockSpec((1, ATT_KV_W, C), lambda b, i: (b, 0, 0)),
                pl.BlockSpec((1, ATT_KV_W, T), lambda b, i: (b, 0, edge(i, -1))),
                pl.BlockSpec((1, ATT_KV_W, TQ), lambda b, i: (b, 0, i)),
                pl.BlockSpec((1, ATT_KV_W, T), lambda b, i: (b, 0, edge(i, 1)))]
    return in_specs, pl.BlockSpec((1, ATT_Q_W, TQ), lambda b, i: (b, 0, i))


N_CHAIN = 2 * ML_HEADS


def _mlstm_load_state(c_ref, n_ref, m_ref):
    return [(c_ref[ci], n_ref[ci], m_ref[ci, 0:1, 0:1]) for ci in range(N_CHAIN)]


def _mlstm_phase1(dirs, state, item, with_h, cps):
    T = ML_CHUNK
    sub, rest = divmod(item, N_CHAIN)
    d, h = divmod(rest, ML_HEADS)
    qt_ref, k_ref, vt_ref, g_ref, bc_ref, br_ref, h_ref = dirs[d]
    row = lax.broadcasted_iota(jnp.int32, (T, T), 0)
    col = lax.broadcasted_iota(jnp.int32, (T, T), 1)
    mask = (col >= row) if d == 0 else (col <= row)
    last = T - 1 if d == 0 else 0
    sc = sub if d == 0 else cps - 1 - sub
    tok = slice(sc * T, (sc + 1) * T)
    ci = d * ML_HEADS + h
    gi = d * 2 * ML_HEADS + h
    fi = gi + ML_HEADS
    k = k_ref[0, tok, h * ML_QK_DIM:(h + 1) * ML_QK_DIM]
    vt = vt_ref[0, h * ML_V_DIM:(h + 1) * ML_V_DIM, tok]
    u_col = g_ref[0, tok, gi:gi + 1] - bc_ref[0, tok, fi:fi + 1]
    b_row = br_ref[0, fi:fi + 1, tok]
    ct_old, n_old, m_old = state[ci]
    qt = st = qn2 = None
    if with_h:
        qt = qt_ref[0, h * ML_QK_DIM:(h + 1) * ML_QK_DIM, tok]
        st = _dot(k, qt)
        top = lax.broadcasted_iota(jnp.int32, (8, ML_QK_DIM), 0) == 0
        n_hi = n_old.astype(BF16)
        n_lo = (n_old - n_hi.astype(F32)).astype(BF16)
        qn2 = _dot(jnp.where(top, n_hi, n_lo), qt)
    m_last = jnp.maximum(jnp.max(u_col, axis=0, keepdims=True), m_old)
    decay = jnp.exp(m_old - m_last)
    kw = (k.astype(F32) * jnp.exp(u_col - m_last)).astype(BF16)
    state[ci] = (decay * ct_old + _dot(vt, kw), decay * n_old + _dot(jnp.ones((8, T), BF16), kw),
                 b_row[:, last:last + 1] + m_last)
    return (h, h_ref, tok, mask, qt, vt, u_col, b_row, m_old, ct_old, st, qn2)


def _mlstm_phase2(chain):
    h, h_ref, tok, mask, qt, vt, u_col, b_row, m_old, ct_old, st, qn2 = chain
    umat = jnp.where(mask, u_col, -jnp.inf)
    m_row = jnp.maximum(jnp.max(umat, axis=0, keepdims=True), m_old)
    pt = st * jnp.exp(umat - m_row)
    w_int = jnp.exp(m_old - m_row)
    e_row = jnp.exp(-(b_row + m_row))
    nq = jnp.sum(pt, axis=0, keepdims=True) + w_int * (qn2[0:1, :] + qn2[1:2, :])
    den = jnp.maximum(jnp.abs(nq), e_row)
    lhs = jnp.concatenate([vt, ct_old.astype(BF16)], axis=1)
    rhs = jnp.concatenate([pt.astype(BF16), (qt.astype(F32) * w_int).astype(BF16)], axis=0)
    h_ref[0, h * ML_V_DIM:(h + 1) * ML_V_DIM, tok] = (_dot(lhs, rhs) * (1.0 / den)).astype(BF16)


def _mlstm_commit(state, c_ref, n_ref, m_ref):
    for ci, (c_new, n_new, m_new) in enumerate(state):
        c_ref[ci] = c_new
        n_ref[ci] = n_new
        m_ref[ci] = jnp.broadcast_to(m_new, (8, LANES))


def _mlstm_ctx_kernel(kf_ref, vtf_ref, gf_ref, bcf_ref, brf_ref, kb_ref, vtb_ref, gb_ref, bcb_ref, brb_ref,
                      c_ref, n_ref, m_ref, *, cps):
    @pl.when(pl.program_id(1) == 0)
    def _():
        c_ref[...] = jnp.zeros_like(c_ref)
        n_ref[...] = jnp.zeros_like(n_ref)
        m_ref[...] = jnp.zeros_like(m_ref)

    dirs = ((None, kf_ref, vtf_ref, gf_ref, bcf_ref, brf_ref, None),
            (None, kb_ref, vtb_ref, gb_ref, bcb_ref, brb_ref, None))
    state_refs = (c_ref.at[0], n_ref.at[0], m_ref.at[0])
    state = _mlstm_load_state(*state_refs)
    for item in range(cps * N_CHAIN):
        _mlstm_phase1(dirs, state, item, with_h=False, cps=cps)
    _mlstm_commit(state, *state_refs)


def _mixer_kernel(sink_ref, q_ref, kc_ref, kp_ref, k0_ref, kn_ref, vc_ref, vp_ref, v0_ref, vn_ref,
                  c0_ref, n0_ref, m0_ref, qtf_ref, kf_ref, vtf_ref, gf_ref, bcf_ref, brf_ref,
                  qtb_ref, kb_ref, vtb_ref, gb_ref, bcb_ref, brb_ref,
                  att_ref, hf_ref, hb_ref, c_ref, n_ref, m_ref, *, n_steps, cps):
    @pl.when(pl.program_id(1) == 0)
    def _():
        c_ref[...] = c0_ref[0]
        n_ref[...] = n0_ref[0]
        m_ref[...] = m0_ref[0]

    n_iter, scores, finish = _attn_stream(sink_ref, q_ref, kc_ref, kp_ref, k0_ref, kn_ref,
                                          vc_ref, vp_ref, v0_ref, vn_ref, att_ref, n_steps)
    pending = [scores(n) for n in range(ATT_AHEAD)]
    dirs = ((qtf_ref, kf_ref, vtf_ref, gf_ref, bcf_ref, brf_ref, hf_ref),
            (qtb_ref, kb_ref, vtb_ref, gb_ref, bcb_ref, brb_ref, hb_ref))
    state = _mlstm_load_state(c_ref, n_ref, m_ref)
    n_items = cps * N_CHAIN
    items = [_mlstm_phase1(dirs, state, it, with_h=True, cps=cps) for it in range(ML_AHEAD)]
    for n in range(max(n_iter, n_items)):
        if n < n_iter:
            st = pending.pop(0)
            if n + ATT_AHEAD < n_iter:
                pending.append(scores(n + ATT_AHEAD))
            finish(n, st)
        if n < n_items:
            if n + ML_AHEAD < n_items:
                items.append(_mlstm_phase1(dirs, state, n + ML_AHEAD, with_h=True, cps=cps))
            _mlstm_phase2(items.pop(0))
    _mlstm_commit(state, c_ref, n_ref, m_ref)


def _mlstm_specs(T, order, with_q):
    specs = [
        pl.BlockSpec((1, ML_QK_W, T), lambda b, j: (b, 0, order(j))),
        pl.BlockSpec((1, T, ML_QK_W), lambda b, j: (b, order(j), 0)),
        pl.BlockSpec((1, ML_V_W, T), lambda b, j: (b, 0, order(j))),
        pl.BlockSpec((1, T, ML_GATE_W), lambda b, j: (b, order(j), 0)),
        pl.BlockSpec((1, T, ML_GATE_W), lambda b, j: (b, order(j), 0)),
        pl.BlockSpec((1, ML_GATE_W, T), lambda b, j: (b, 0, order(j))),
    ]
    return specs if with_q else specs[1:]


_STATE_SHAPES = ((N_CHAIN, ML_V_DIM, ML_QK_DIM), (N_CHAIN, 8, ML_QK_DIM), (N_CHAIN, 8, LANES))


ML_AHEAD = 32
ML_CPS = 4
ML_CTX_CPS = 2


def _mlstm_ctx_call(mk, mvt, g, bc, br):
    B, C, _ = mk.shape
    T = ML_CTX_CPS * ML_CHUNK
    nc = C // T
    state_specs = [pl.BlockSpec((1,) + s, lambda b, j: (b, 0, 0, 0)) for s in _STATE_SHAPES]
    return pl.pallas_call(
        functools.partial(_mlstm_ctx_kernel, cps=ML_CTX_CPS),
        grid=(B, nc),
        in_specs=_mlstm_specs(T, lambda j: j, False) + _mlstm_specs(T, lambda j: nc - 1 - j, False),
        out_specs=state_specs,
        out_shape=[jax.ShapeDtypeStruct((B,) + s, F32) for s in _STATE_SHAPES],
        compiler_params=_cparams(("arbitrary", "arbitrary")),
        name="mlstm_ctx",
    )(mk, mvt, g, bc, br, mk, mvt, g, bc, br)


def _mixer_call(sink, q, kx, vt, kx_c, vt_c, state, mqt, mk, mvt, g, bc, br):
    B, L, _ = mk.shape
    C = kx_c.shape[1]
    T = ML_CPS * ML_CHUNK
    assert T == ATT_QB * ATT_BLOCK
    nc = L // T
    att_in, att_out = _attn_specs(L, C)
    state_specs = [pl.BlockSpec((1,) + s, lambda b, j: (b, 0, 0, 0)) for s in _STATE_SHAPES]
    out_specs = [att_out,
                 pl.BlockSpec((1, ML_V_W, T), lambda b, j: (b, 0, j)),
                 pl.BlockSpec((1, ML_V_W, T), lambda b, j: (b, 0, nc - 1 - j))]
    return pl.pallas_call(
        functools.partial(_mixer_kernel, n_steps=nc, cps=ML_CPS),
        grid=(B, nc),
        in_specs=(att_in + state_specs + _mlstm_specs(T, lambda j: j, True)
                  + _mlstm_specs(T, lambda j: nc - 1 - j, True)),
        out_specs=out_specs,
        out_shape=[jax.ShapeDtypeStruct((B, ATT_Q_W, L), BF16)] + [jax.ShapeDtypeStruct((B, ML_V_W, L), BF16)] * 2,
        scratch_shapes=[pltpu.VMEM(s, F32) for s in _STATE_SHAPES],
        compiler_params=_cparams(("arbitrary", "arbitrary")),
        name="mixer",
    )(sink, q, kx_c, kx, kx, kx, vt_c, vt, vt, vt, *state, mqt, mk, mvt, g, bc, br, mqt, mk, mvt, g, bc, br)


def _merge_kernel(att_ref, hf_ref, hb_ref, so_ref, sga_ref, sgm_ref, x_ref, g1_ref, mod2_ref, mlw_ref, n2w_ref,
                  wa_ref, wm_ref, wo_ref, xmid_ref, h2_ref, y_ref):
    s = pl.program_id(0)
    last = pl.num_programs(0) - 1
    cur = s % 2

    def body(do_branch, do_out):
        if do_out:
            y2 = _dot_tn(y_ref[1 - cur], wo_ref[...])
        if do_branch:
            ya = _dot_tn(wa_ref[...], att_ref[0])
            ht = hf_ref[0].astype(F32) + hb_ref[0].astype(F32)
            parts = []
            for h in range(ML_HEADS):
                seg = ht[h * ML_V_DIM:(h + 1) * ML_V_DIM, :]
                ms = jnp.mean(seg * seg, axis=0, keepdims=True)
                parts.append(seg * lax.rsqrt(ms + EPS))
            ml = (jnp.concatenate(parts, axis=0) * mlw_ref[...] * so_ref[0].astype(F32)).astype(BF16)
            ym = _dot_tn(wm_ref[...], ml)
        if do_out:
            xm = x_ref[0] + g1_ref[0, 0] * y2
            xmid_ref[0] = xm
            ms = jnp.mean(xm * xm, axis=-1, keepdims=True)
            h2 = xm * lax.rsqrt(ms + EPS) * n2w_ref[...]
            h2_ref[0] = (h2 * (1.0 + mod2_ref[1, 0]) + mod2_ref[0, 0]).astype(BF16)
        if do_branch:
            y_ref[cur] = (sga_ref[0].astype(F32) * ya + sgm_ref[0].astype(F32) * ym).astype(BF16)

    pl.when(s == 0)(lambda: body(True, False))
    pl.when((s > 0) & (s < last))(lambda: body(True, True))
    pl.when(s == last)(lambda: body(False, True))


def _merge_call(att_t, hf_t, hb_t, so_t, sga_t, sgm_t, x, mod6, mlw_b, n2w, wa, wm, wo, tm):
    B, L, _ = x.shape
    nt = L // tm
    n_all = B * nt

    def tile_in(s):
        t = jnp.minimum(s, n_all - 1)
        return t // nt, t % nt

    def tile_out(s):
        t = jnp.maximum(s - 1, 0)
        return t // nt, t % nt

    def in_t(h):
        return pl.BlockSpec((1, h, tm), lambda s: (tile_in(s)[0], 0, tile_in(s)[1]))

    def out_rows(w):
        return pl.BlockSpec((1, tm, w), lambda s: (*tile_out(s), 0))

    def const(shape):
        return pl.BlockSpec(shape, lambda s: (0,) * len(shape))

    return pl.pallas_call(
        _merge_kernel,
        grid=(n_all + 1,),
        in_specs=[in_t(D), in_t(D), in_t(D), in_t(D), in_t(D), in_t(D), out_rows(D),
                  _mod_spec(MOD_G1, 1, lambda s: tile_out(s)[0]),
                  _mod_spec(MOD_SH2, 2, lambda s: tile_out(s)[0]),
                  const((D, tm)), const((1, D)), const((D, D)), const((D, D)), const((D, D))],
        out_specs=[out_rows(D), out_rows(D)],
        out_shape=[jax.ShapeDtypeStruct((B, L, D), F32), jax.ShapeDtypeStruct((B, L, D), BF16)],
        scratch_shapes=[pltpu.VMEM((2, D, tm), BF16)],
        compiler_params=_cparams(("arbitrary",)),
        name="merge",
    )(att_t, hf_t, hb_t, so_t, sga_t, sgm_t, x, mod6, mod6, mlw_b, n2w, wa, wm, wo)


HALO = 16
FFN_AHEAD = 1


def _ffn_kernel(h_ref, hp_ref, hn_ref, xmid_ref, mod_ref, wup_ref, cw_ref, cb_ref, wdn_ref, o_ref,
                act_ref, *, n_tiles, tn, dn):
    s = pl.program_id(0)
    last = pl.num_programs(0) - 1

    @pl.when(s == 0)
    def _():
        _ffn_body(h_ref, hp_ref, hn_ref, xmid_ref, mod_ref, wup_ref, cw_ref, cb_ref, wdn_ref, o_ref, act_ref,
                  n_tiles=n_tiles, tn=tn, dn=dn, do_up=True, do_down=False)

    @pl.when((s > 0) & (s < last))
    def _():
        _ffn_body(h_ref, hp_ref, hn_ref, xmid_ref, mod_ref, wup_ref, cw_ref, cb_ref, wdn_ref, o_ref, act_ref,
                  n_tiles=n_tiles, tn=tn, dn=dn, do_up=True, do_down=True)

    @pl.when(s == last)
    def _():
        _ffn_body(h_ref, hp_ref, hn_ref, xmid_ref, mod_ref, wup_ref, cw_ref, cb_ref, wdn_ref, o_ref, act_ref,
                  n_tiles=n_tiles, tn=tn, dn=dn, do_up=False, do_down=True)


def _ffn_body(h_ref, hp_ref, hn_ref, xmid_ref, mod_ref, wup_ref, cw_ref, cb_ref, wdn_ref, o_ref, act_ref,
              *, n_tiles, tn, dn, do_up, do_down):
    s = pl.program_id(0)
    i = s % n_tiles
    cur = s % 2
    tm = h_ref.shape[1]
    n_chunks = D_FF // tn
    n_dn = D // dn
    act_prev = act_ref[1 - cur] if do_down else None

    def down(k):
        cols = slice(k * dn, (k + 1) * dn)
        o_ref[0, :, cols] = xmid_ref[0, :, cols] + mod_ref[0, 0, :, cols] * _dot(act_prev, wdn_ref[:, cols])

    if not do_up:
        for k in range(n_dn):
            down(k)
        return

    h = h_ref[0]
    prev_row = jnp.where(i > 0, hp_ref[0].astype(F32)[HALO - 1:HALO, :], 0.0)
    next_row = jnp.where(i < n_tiles - 1, hn_ref[0].astype(F32)[0:1, :], 0.0)
    top = lax.broadcasted_iota(jnp.int32, (16, D), 0) < 8
    edge = jnp.where(top, prev_row, next_row).astype(BF16)
    row8 = lax.broadcasted_iota(jnp.int32, (8, tn), 0)
    h_ext = jnp.concatenate([h, edge], axis=0)

    def up(c0):
        u_ext = _dot(h_ext, wup_ref[:, c0:c0 + tn])
        return u_ext[:tm], u_ext[tm:]

    def conv(u, ue, c0):
        below = pltpu.roll(u, 1, 0)
        above = pltpu.roll(u, tm - 1, 0)
        below = jnp.concatenate([jnp.where(row8 == 0, ue[0:8], below[0:8]), below[8:]], axis=0)
        above = jnp.concatenate([above[:tm - 8], jnp.where(row8 == 7, ue[8:16], above[tm - 8:])], axis=0)
        cw = cw_ref[:, c0:c0 + tn]
        return cb_ref[:, c0:c0 + tn] + below * cw[0:1] + u * cw[1:2] + above * cw[2:3]

    pending = [(up(c * tn), up(D_FF + c * tn)) for c in range(FFN_AHEAD)]
    done = 0
    for c in range(n_chunks):
        (ua, uae), (ug, uge) = pending.pop(0)
        if c + FFN_AHEAD < n_chunks:
            pending.append((up((c + FFN_AHEAD) * tn), up(D_FF + (c + FFN_AHEAD) * tn)))
        while do_down and done * n_chunks < (c + 1) * n_dn:
            down(done)
            done += 1
        a = conv(ua, uae, c * tn)
        hg = conv(ug, uge, D_FF + c * tn)
        act_ref[cur, :, c * tn:(c + 1) * tn] = ((hg + hg * jnp.tanh(hg)) * a).astype(BF16)


def _ffn_call(h2, xmid, g2, wup, cw, cb, wdn, tm, tn, dn):
    B, L, _ = xmid.shape
    nt = L // tm
    n_all = B * nt
    hb = tm // HALO
    nhb = L // HALO

    def tile_in(s):
        t = jnp.minimum(s, n_all - 1)
        return t // nt, t % nt

    def tile_out(s):
        t = jnp.maximum(s - 1, 0)
        return t // nt, t % nt

    def in_spec():
        return pl.BlockSpec((1, tm, D), lambda s: (*tile_in(s), 0))

    def out_spec():
        return pl.BlockSpec((1, tm, D), lambda s: (*tile_out(s), 0))

    def prev_halo(s):
        b, i = tile_in(s)
        return b, jnp.maximum(i * hb - 1, 0), 0

    def next_halo(s):
        b, i = tile_in(s)
        return b, jnp.minimum((i + 1) * hb, nhb - 1), 0

    def const(shape):
        return pl.BlockSpec(shape, lambda s: (0,) * len(shape))

    return pl.pallas_call(
        functools.partial(_ffn_kernel, n_tiles=nt, tn=tn, dn=dn),
        grid=(n_all + 1,),
        in_specs=[in_spec(),
                  pl.BlockSpec((1, HALO, D), prev_halo),
                  pl.BlockSpec((1, HALO, D), next_halo),
                  out_spec(),
                  _mod_spec(MOD_G2, 1, lambda s: tile_out(s)[0]),
                  const(wup.shape), const(cw.shape), const(cb.shape), const(wdn.shape)],
        out_specs=out_spec(),
        out_shape=jax.ShapeDtypeStruct((B, L, D), F32),
        scratch_shapes=[pltpu.VMEM((2, tm, D_FF), BF16)],
        compiler_params=_cparams(("arbitrary",)),
        name="ffn",
    )(h2, h2, h2, xmid, g2, wup, cw, cb, wdn)


def _pair_perm(n_heads):
    half = ATT_HEAD_DIM // 2
    idx = []
    for p in range(n_heads // 2):
        for sub in range(4):
            head = 2 * p + (sub % 2)
            d0 = (sub // 2) * half
            idx.extend(head * ATT_HEAD_DIM + d0 + e for e in range(half))
    return np.asarray(idx, np.int32)


def _rope_tables(L):
    f32 = np.float32
    rows = L // GRID_W
    row = np.repeat(np.arange(rows, dtype=f32), GRID_W)
    col = np.tile(np.arange(GRID_W, dtype=f32), rows)
    n_freq = ATT_HEAD_DIM // 4
    inv_freq = (f32(ROPE_BASE) ** (-np.arange(n_freq, dtype=f32) / f32(n_freq))).astype(f32)
    ang = np.concatenate([row[:, None] * inv_freq, col[:, None] * inv_freq], axis=-1).astype(f32)
    cos = np.tile(np.cos(ang).astype(f32), (1, 4))
    sin = np.tile(np.sin(ang).astype(f32), (1, 4))
    sign = np.where(np.arange(LANES) < LANES // 2, -1.0, 1.0).astype(f32)
    return jnp.asarray(cos), jnp.asarray(sin * sign)


def kernel(x, c, ctx, c_ctx, w_mod, b_mod, norm1_w, w_in, q_norm_w, k_norm_w, attn_sink, ml_gate_b, ml_norm_w,
           w_branch_att, w_branch_ml, w_out, norm2_w, w_up, conv_w, conv_b, w_down):
    B, L, _ = x.shape
    C = ctx.shape[1]
    assert L % 512 == 0 and C % 256 == 0 and L % GRID_W == 0
    l = 0
    tm_merge = 512

    n_rows = -(-(B + 1) // 16) * 16
    cc = jnp.concatenate([c, c_ctx[None, :], jnp.zeros((n_rows - B - 1, D), F32)], axis=0)
    mod6 = _mod_call(cc, w_mod[l], b_mod[l][None, :])

    w = w_in[l]
    qperm = _pair_perm(ATT_HEADS)
    kperm = _pair_perm(ATT_KV_HEADS)
    def pair_cols(wc, n_heads):
        half = ATT_HEAD_DIM // 2
        wc = wc.reshape(D, n_heads // 2, 2, 2, half).transpose(0, 1, 3, 2, 4)
        return wc.reshape(D, n_heads * ATT_HEAD_DIM)

    w_q = pair_cols(w[:, _O_AQ:_O_AQ + ATT_Q_W], ATT_HEADS)
    w_k = pair_cols(w[:, _O_AK:_O_AK + ATT_KV_W], ATT_KV_HEADS)
    w_g = jnp.pad(w[:, _O_MG:_O_MG + ML_GATE_W], ((0, 0), (0, LANES - ML_GATE_W)))
    w_mk = w[:, _O_MK:_O_MK + ML_QK_W] * (ML_QK_DIM ** -0.5)
    w_p = jnp.concatenate([w_q, w_k, w_mk, w_g], axis=1).astype(BF16)
    w_a = w[:, _A_COLS[0]:_A_COLS[1]].astype(BF16)
    w_b = w[:, _B_COLS[0]:_B_COLS[1]].astype(BF16)
    w_c = w[:, _C_COLS[0]:_C_COLS[1]].astype(BF16)

    head_of_col = np.concatenate([qperm // ATT_HEAD_DIM, ATT_HEADS + kperm // ATT_HEAD_DIM])
    e_np = (head_of_col[:, None] == np.arange(LANES)[None, :]).astype(np.float32)
    e_mat = jnp.asarray(e_np, BF16)
    et_mat = jnp.asarray(np.concatenate([e_np.T, e_np.T], axis=0), BF16)
    def pair_tiled(wn, n_heads):
        half = ATT_HEAD_DIM // 2
        return jnp.tile(jnp.concatenate([wn[:half], wn[:half], wn[half:], wn[half:]]), n_heads // 2)

    qkw = jnp.concatenate([pair_tiled(q_norm_w[l], ATT_HEADS) * (ATT_SCALE * LOG2E),
                           pair_tiled(k_norm_w[l], ATT_KV_HEADS)])[None, :]
    cos_t, sin_t = _rope_tables(L)
    gb = ml_gate_b[l].reshape(1, ML_GATE_W)
    gbt = ml_gate_b[l].reshape(ML_GATE_W, 1)
    n1w = norm1_w[l][None, :]

    kx_c, mk_c, g_c, bc_c, vt_c, mvt_c, br_c = _ctxproj_call(
        ctx, mod6, B, n1w, w_p, w_a, w_b, e_mat, et_mat, qkw, gb, gbt, tm=256)
    q, kx, mk, g, bc, vt, mqt, mvt, sot, sgat, sgmt, br = _inproj_call(
        x, mod6, n1w, w_p, w_a, w_b, w_c, e_mat, et_mat, qkw, cos_t, sin_t, gb, gbt, tm=512)

    state = _mlstm_ctx_call(mk_c, mvt_c, g_c, bc_c, br_c)
    att_t, hf_t, hb_t = _mixer_call(attn_sink[l], q, kx, vt, kx_c, vt_c, state, mqt, mk, mvt, g, bc, br)

    mlw_b = jnp.broadcast_to(ml_norm_w[l][:, None], (ML_V_W, tm_merge))
    xmid, h2 = _merge_call(att_t, hf_t, hb_t, sot, sgat, sgmt, x, mod6, mlw_b, norm2_w[l][None, :],
                           w_branch_att[l].astype(BF16), w_branch_ml[l].astype(BF16), w_out[l].astype(BF16),
                           tm=tm_merge)
    gate_half = jnp.where(jnp.arange(2 * D_FF) < D_FF, 1.0, 0.5).astype(F32)
    out = _ffn_call(h2, xmid, mod6, w_up[l].astype(BF16), conv_w[l] * gate_half, (conv_b[l] * gate_half)[None, :],
                    w_down[l].astype(BF16), tm=512, tn=256, dn=256)
    return out
```

```python
import functools

import jax
import jax.numpy as jnp
import numpy as np
from jax import lax
from jax.experimental import pallas as pl
from jax.experimental.pallas import tpu as pltpu

D = 1024
GRID_W = 64
ATT_HEADS = 16
ATT_KV_HEADS = 4
ATT_HEAD_DIM = 64
ATT_GROUP = ATT_HEADS // ATT_KV_HEADS
ATT_BLOCK = 128
WINDOW = 128
ROPE_BASE = 10000.0
ATT_SCALE = ATT_HEAD_DIM ** -0.5
LOG2E = 1.4426950408889634
ML_HEADS = 4
ML_QK_DIM = 128
ML_V_DIM = 256
ML_CHUNK = 128
D_FF = 2816
EPS = 1e-6
NEG_INF = -1e30

ATT_Q_W = ATT_HEADS * ATT_HEAD_DIM
ATT_KV_W = ATT_KV_HEADS * ATT_HEAD_DIM
ML_QK_W = ML_HEADS * ML_QK_DIM
ML_V_W = ML_HEADS * ML_V_DIM
ML_GATE_W = 2 * 2 * ML_HEADS

LANES = 128
KX_W = ATT_KV_HEADS * 2 * LANES
VMEM_LIMIT = 56 * 1024 * 1024

BF16 = jnp.bfloat16
F32 = jnp.float32

_O_AQ = 0
_O_AK = _O_AQ + ATT_Q_W
_O_AV = _O_AK + ATT_KV_W
_O_MQ = _O_AV + ATT_KV_W
_O_MK = _O_MQ + ML_QK_W
_O_MV = _O_MK + ML_QK_W
_O_MO = _O_MV + ML_V_W
_O_MG = _O_MO + ML_V_W
_O_GA = _O_MG + ML_GATE_W
_O_GM = _O_GA + D

QK_W = ATT_Q_W + ATT_KV_W
_P_QK = 0
_P_MK = _P_QK + QK_W
_P_MG = _P_MK + ML_QK_W
_P_END = _P_MG + LANES
_R_V = 0
_R_MQ = _R_V + ATT_KV_W
_R_MV = _R_MQ + ML_QK_W
_R_MO = _R_MV + ML_V_W
_R_GA = _R_MO + ML_V_W
_R_GM = _R_GA + D
_R_END = _R_GM + D


def _dot(a, b):
    return jnp.dot(a, b, preferred_element_type=F32)


def _dot_nt(a, b):
    return lax.dot_general(a, b, (((1,), (1,)), ((), ())), preferred_element_type=F32)


def _dot_tn(a, b):
    return lax.dot_general(a, b, (((0,), (0,)), ((), ())), preferred_element_type=F32)


def _cparams(sem):
    return pltpu.CompilerParams(dimension_semantics=sem, vmem_limit_bytes=VMEM_LIMIT)


def _mod_kernel(c_ref, w_ref, b_ref, o_ref):
    c = c_ref[...]
    a = c * jax.nn.sigmoid(c)
    a_hi = a.astype(BF16)
    a_lo = (a - a_hi.astype(F32)).astype(BF16)
    w = w_ref[...]
    w_hi = w.astype(BF16)
    w_lo = (w - w_hi.astype(F32)).astype(BF16)
    rows = a.shape[0]
    both = _dot(jnp.concatenate([a_hi, a_lo], axis=0), w_hi)
    o_ref[0, :, 0, :] = both[:rows] + both[rows:] + _dot(a_hi, w_lo) + b_ref[...]


MOD_SH1, MOD_SC1, MOD_SH2, MOD_SC2, MOD_G1, MOD_G2 = range(6)


def _mod_call(cc, w_mod, b_mod):
    rows = cc.shape[0]
    n_seg = w_mod.shape[1] // D
    assert n_seg == 6

    def out_pos(j):
        return jnp.where(j == 2, MOD_G1, jnp.where((j == 3) | (j == 4), j - 1, j))

    return pl.pallas_call(
        _mod_kernel,
        grid=(n_seg,),
        in_specs=[pl.BlockSpec((rows, D), lambda j: (0, 0)),
                  pl.BlockSpec((D, D), lambda j: (0, j)),
                  pl.BlockSpec((1, D), lambda j: (0, j))],
        out_specs=pl.BlockSpec((1, rows, 1, D), lambda j: (out_pos(j), 0, 0, 0)),
        out_shape=jax.ShapeDtypeStruct((n_seg, rows, 1, D), F32),
        compiler_params=_cparams(("arbitrary",)),
        name="mod",
    )(cc, w_mod, b_mod)


def _mod_spec(seg, n_seg, row_of):
    return pl.BlockSpec((n_seg, 1, 1, D), lambda *ids: (seg // n_seg, row_of(*ids), 0, 0))


def _split2(x):
    x1 = x.astype(BF16)
    x2 = (x - x1.astype(F32)).astype(BF16)
    return x1, x2


def _norm_modulate(x, n1w_ref, mod_ref):
    ms = jnp.mean(x * x, axis=-1, keepdims=True)
    y = x * lax.rsqrt(ms + EPS) * n1w_ref[...]
    return (y * (1.0 + mod_ref[1, 0]) + mod_ref[0, 0]).astype(BF16)


def _sigmoid(x):
    return 0.5 * jnp.tanh(0.5 * x) + 0.5


def _head_rms_scale(ss):
    r = lax.rsqrt(ss * (1.0 / ATT_HEAD_DIM) + EPS)
    r_hi = r.astype(BF16)
    r_lo = (r - r_hi.astype(F32)).astype(BF16)
    return jnp.concatenate([r_hi, r_lo], axis=1)


def _store_k_variants(k_ref, pair, o):
    lane = lax.broadcasted_iota(jnp.int32, (1, LANES), 1)
    keep = ((lane // 32) % 2) == 0
    c0 = 4 * pair * LANES
    k_ref[0, :, c0:c0 + LANES] = jnp.where(keep, o, 0.0).astype(BF16)
    k_ref[0, :, c0 + LANES:c0 + 2 * LANES] = jnp.where(keep, 0.0, pltpu.roll(o, 32, 1)).astype(BF16)
    k_ref[0, :, c0 + 2 * LANES:c0 + 3 * LANES] = jnp.where(keep, pltpu.roll(o, 96, 1), 0.0).astype(BF16)
    k_ref[0, :, c0 + 3 * LANES:c0 + 4 * LANES] = jnp.where(keep, 0.0, o).astype(BF16)


def _cum_gates_cols(g16, tri_lo, tri_up):
    parts = _split2(jax.nn.log_sigmoid(g16))
    fwd_col = lax.broadcasted_iota(jnp.int32, (1, ML_GATE_W), 1) < ML_GATE_W // 2
    return jnp.where(fwd_col, sum(_dot(tri_lo, p) for p in parts), sum(_dot(tri_up, p) for p in parts))


def _cum_gates_rows(gt16, tri_lo, tri_up):
    parts = _split2(jax.nn.log_sigmoid(gt16))
    fwd_row = lax.broadcasted_iota(jnp.int32, (ML_GATE_W, 1), 0) < ML_GATE_W // 2
    return jnp.where(fwd_row, sum(_dot(p, tri_up) for p in parts), sum(_dot(p, tri_lo) for p in parts))


def _inproj_kernel(x_ref, mod_ref, n1w_ref, w_ref, wt_ref, wgt_ref, e_ref, et_ref, qkw_ref,
                   cos_ref, sin_ref, gb_ref, gbt_ref, tril_ref, triu_ref,
                   q_ref, k_ref, mk_ref, g_ref, bc_ref, vt_ref, mqt_ref, mvt_ref, sot_ref, sgat_ref, sgmt_ref, br_ref,
                   hn_ref):
    hn_ref[...] = _norm_modulate(x_ref[0], n1w_ref, mod_ref)
    hn = hn_ref[...]

    def ft(r0, height):
        return _dot_nt(wt_ref[r0:r0 + height, :], hn)

    acc = _dot(hn, w_ref[:, _P_QK:_P_QK + QK_W])
    g16 = _dot(hn, w_ref[:, _P_MG:_P_MG + LANES])[:, :ML_GATE_W] + gb_ref[...]
    gt16 = _dot_nt(wgt_ref[...], hn) + gbt_ref[...]
    g_ref[0] = g16
    mk_ref[0] = _dot(hn, w_ref[:, _P_MK:_P_MK + ML_QK_W]).astype(BF16)
    ss = _dot((acc * acc).astype(BF16), e_ref[...])
    vt_ref[0] = ft(_R_V, ATT_KV_W).astype(BF16)
    sot_ref[0] = _sigmoid(ft(_R_MO, ML_V_W)).astype(BF16)
    rb = _dot(_head_rms_scale(ss), et_ref[...])
    sgat_ref[0] = _sigmoid(ft(_R_GA, D)).astype(BF16)
    bc_ref[0] = _cum_gates_cols(g16, tril_ref[...], triu_ref[...])

    qn = acc * rb * qkw_ref[...]
    cos = cos_ref[...]
    sin = sin_ref[...]
    for gi in range(QK_W // LANES):
        xs = qn[:, gi * LANES:(gi + 1) * LANES]
        o = xs * cos + pltpu.roll(xs, LANES // 2, 1) * sin
        if gi < ATT_Q_W // LANES:
            q_ref[0, :, gi * LANES:(gi + 1) * LANES] = o.astype(BF16)
        else:
            _store_k_variants(k_ref, gi - ATT_Q_W // LANES, o)

    sgmt_ref[0] = _sigmoid(ft(_R_GM, D)).astype(BF16)
    br_ref[0] = _cum_gates_rows(gt16, tril_ref[...], triu_ref[...])
    mqt_ref[0] = ft(_R_MQ, ML_QK_W).astype(BF16)
    mvt_ref[0] = ft(_R_MV, ML_V_W).astype(BF16)


def _ctxproj_kernel(x_ref, mod_ref, n1w_ref, w_ref, wt_ref, wgt_ref, e_ref, et_ref, qkw_ref,
                    gb_ref, gbt_ref, tril_ref, triu_ref,
                    k_ref, mk_ref, g_ref, bc_ref, vt_ref, mvt_ref, br_ref):
    hn = _norm_modulate(x_ref[0], n1w_ref, mod_ref)
    acc = _dot(hn, w_ref[:, _P_QK + ATT_Q_W:_P_QK + QK_W])
    g16 = _dot(hn, w_ref[:, _P_MG:_P_MG + LANES])[:, :ML_GATE_W] + gb_ref[...]
    gt16 = _dot_nt(wgt_ref[...], hn) + gbt_ref[...]
    g_ref[0] = g16
    mk_ref[0] = _dot(hn, w_ref[:, _P_MK:_P_MK + ML_QK_W]).astype(BF16)
    ss = _dot((acc * acc).astype(BF16), e_ref[ATT_Q_W:QK_W, :])
    vt_ref[0] = _dot_nt(wt_ref[_R_V:_R_V + ATT_KV_W, :], hn).astype(BF16)
    rb = _dot(_head_rms_scale(ss), et_ref[:, ATT_Q_W:QK_W])
    mvt_ref[0] = _dot_nt(wt_ref[_R_MV:_R_MV + ML_V_W, :], hn).astype(BF16)
    bc_ref[0] = _cum_gates_cols(g16, tril_ref[...], triu_ref[...])
    kn = acc * rb * qkw_ref[:, ATT_Q_W:QK_W]
    for pair in range(ATT_KV_W // LANES):
        _store_k_variants(k_ref, pair, kn[:, pair * LANES:(pair + 1) * LANES])
    br_ref[0] = _cum_gates_rows(gt16, tril_ref[...], triu_ref[...])


def _block_tri(n, block, lower):
    r = np.arange(n)[:, None]
    c = np.arange(n)[None, :]
    same = (r // block) == (c // block)
    return jnp.asarray(same & ((c <= r) if lower else (c >= r)), BF16)


def _const_spec(shape):
    return pl.BlockSpec(shape, lambda b, i: (0,) * len(shape))


def _inproj_call(x, mod6, n1w, w_p, w_t, wgt, e_mat, et_mat, qkw, cos_t, sin_t, gb, gbt, tm):
    B, L, _ = x.shape
    tril = _block_tri(tm, ML_CHUNK, True)
    triu = _block_tri(tm, ML_CHUNK, False)

    def rows(w):
        return pl.BlockSpec((1, tm, w), lambda b, i: (b, i, 0))

    def cols(h):
        return pl.BlockSpec((1, h, tm), lambda b, i: (b, 0, i))

    consts = [n1w, w_p, w_t, wgt, e_mat, et_mat, qkw]
    tail = [gb, gbt, tril, triu]
    in_specs = ([rows(D), _mod_spec(MOD_SH1, 2, lambda b, i: b)]
                + [_const_spec(a.shape) for a in consts]
                + [pl.BlockSpec((tm, LANES), lambda b, i: (i, 0))] * 2
                + [_const_spec(a.shape) for a in tail])
    out_specs = [rows(ATT_Q_W), rows(KX_W), rows(ML_QK_W), rows(ML_GATE_W), rows(ML_GATE_W),
                 cols(ATT_KV_W), cols(ML_QK_W), cols(ML_V_W), cols(ML_V_W), cols(D), cols(D), cols(ML_GATE_W)]
    out_shape = [
        jax.ShapeDtypeStruct((B, L, ATT_Q_W), BF16),
        jax.ShapeDtypeStruct((B, L, KX_W), BF16),
        jax.ShapeDtypeStruct((B, L, ML_QK_W), BF16),
        jax.ShapeDtypeStruct((B, L, ML_GATE_W), F32),
        jax.ShapeDtypeStruct((B, L, ML_GATE_W), F32),
        jax.ShapeDtypeStruct((B, ATT_KV_W, L), BF16),
        jax.ShapeDtypeStruct((B, ML_QK_W, L), BF16),
        jax.ShapeDtypeStruct((B, ML_V_W, L), BF16),
        jax.ShapeDtypeStruct((B, ML_V_W, L), BF16),
        jax.ShapeDtypeStruct((B, D, L), BF16),
        jax.ShapeDtypeStruct((B, D, L), BF16),
        jax.ShapeDtypeStruct((B, ML_GATE_W, L), F32),
    ]
    return pl.pallas_call(
        _inproj_kernel,
        grid=(B, L // tm),
        in_specs=in_specs,
        out_specs=out_specs,
        out_shape=out_shape,
        scratch_shapes=[pltpu.VMEM((tm, D), BF16)],
        compiler_params=_cparams(("arbitrary", "arbitrary")),
        name="inproj",
    )(x, mod6, *consts, cos_t, sin_t, *tail)


def _ctxproj_call(ctx, mod6, ctx_row, n1w, w_p, w_t, wgt, e_mat, et_mat, qkw, gb, gbt, tm):
    B, C, _ = ctx.shape
    tril = _block_tri(tm, ML_CHUNK, True)
    triu = _block_tri(tm, ML_CHUNK, False)

    def rows(w):
        return pl.BlockSpec((1, tm, w), lambda b, i: (b, i, 0))

    def cols(h):
        return pl.BlockSpec((1, h, tm), lambda b, i: (b, 0, i))

    consts = [n1w, w_p, w_t, wgt, e_mat, et_mat, qkw, gb, gbt, tril, triu]
    out_specs = [rows(KX_W), rows(ML_QK_W), rows(ML_GATE_W), rows(ML_GATE_W),
                 cols(ATT_KV_W), cols(ML_V_W), cols(ML_GATE_W)]
    out_shape = [
        jax.ShapeDtypeStruct((B, C, KX_W), BF16),
        jax.ShapeDtypeStruct((B, C, ML_QK_W), BF16),
        jax.ShapeDtypeStruct((B, C, ML_GATE_W), F32),
        jax.ShapeDtypeStruct((B, C, ML_GATE_W), F32),
        jax.ShapeDtypeStruct((B, ATT_KV_W, C), BF16),
        jax.ShapeDtypeStruct((B, ML_V_W, C), BF16),
        jax.ShapeDtypeStruct((B, ML_GATE_W, C), F32),
    ]
    return pl.pallas_call(
        _ctxproj_kernel,
        grid=(B, C // tm),
        in_specs=([rows(D), _mod_spec(MOD_SH1, 2, lambda b, i: ctx_row)]
                  + [_const_spec(a.shape) for a in consts]),
        out_specs=out_specs,
        out_shape=out_shape,
        compiler_params=_cparams(("arbitrary", "arbitrary")),
        name="ctxproj",
    )(ctx, mod6, *consts)


ATT_AHEAD = 12

ATT_QB = 4


def _attn_stream(sink_ref, q_ref, kc_ref, kp_ref, k0_ref, kn_ref, vc_ref, vp_ref, v0_ref, vn_ref, o_ref,
                 n_steps):
    i = pl.program_id(1)
    T = ATT_BLOCK
    hd = ATT_HEAD_DIM
    s_idx = lax.broadcasted_iota(jnp.int32, (T, 2 * T), 0)
    t_idx = lax.broadcasted_iota(jnp.int32, (T, 2 * T), 1) % T
    first = lax.broadcasted_iota(jnp.int32, (1, 2 * T), 1) < T

    k_own = k0_ref[0]
    v_own = v0_ref[0]
    k_blk = [kp_ref[0]] + [k_own[b * T:(b + 1) * T] for b in range(ATT_QB)] + [kn_ref[0]]
    v_blk = [vp_ref[0]] + [v_own[:, b * T:(b + 1) * T] for b in range(ATT_QB)] + [vn_ref[0]]
    ones_rows = jnp.ones((16, 3 * T + kc_ref.shape[1]), BF16)

    def window(qb):
        k_all = jnp.concatenate(k_blk[qb:qb + 3] + [kc_ref[0]], axis=0)
        vt_all = jnp.concatenate(v_blk[qb:qb + 3] + [vc_ref[0]], axis=1)
        ok_prev = (s_idx >= t_idx) & ((i > 0) if qb == 0 else True)
        ok_next = (s_idx <= t_idx) & ((i < n_steps - 1) if qb == ATT_QB - 1 else True)
        return k_all, vt_all, ok_prev, ok_next

    windows = [window(qb) for qb in range(ATT_QB)]
    per_qb = 2 * ATT_KV_HEADS

    def scores(n):
        qb, r = divmod(n, per_qb)
        kh, var = divmod(r, 2)
        q = q_ref[0, qb * T:(qb + 1) * T, :]
        q2 = jnp.concatenate([q[:, (2 * kh) * LANES:(2 * kh + 1) * LANES],
                              q[:, (2 * kh + 1) * LANES:(2 * kh + 2) * LANES]], axis=0)
        kk = windows[qb][0][:, (2 * kh + var) * LANES:(2 * kh + var + 1) * LANES]
        return _dot_nt(kk, q2)

    def finish(n, st):
        qb, r = divmod(n, per_qb)
        kh, var = divmod(r, 2)
        _, vt_all, ok_prev, ok_next = windows[qb]
        vt = vt_all[kh * hd:(kh + 1) * hd, :]
        st = jnp.concatenate([jnp.where(ok_prev, st[0:T], NEG_INF), st[T:2 * T],
                              jnp.where(ok_next, st[2 * T:3 * T], NEG_INF), st[3 * T:]], axis=0)
        h0 = ATT_GROUP * kh + var
        h1 = h0 + 2
        sink = jnp.where(first, sink_ref[h0], sink_ref[h1]) * LOG2E
        m = jnp.maximum(jnp.max(st, axis=0, keepdims=True), sink)
        p = jnp.exp2(st - m)
        ot = _dot(jnp.concatenate([vt, ones_rows], axis=0), p.astype(BF16))
        denom = ot[hd:hd + 1, :] + jnp.exp2(sink - m)
        ot = ot[0:hd, :] * (1.0 / denom)
        o_ref[0, h0 * hd:(h0 + 1) * hd, qb * T:(qb + 1) * T] = ot[:, 0:T].astype(BF16)
        o_ref[0, h1 * hd:(h1 + 1) * hd, qb * T:(qb + 1) * T] = ot[:, T:2 * T].astype(BF16)

    return ATT_QB * per_qb, scores, finish


def _attn_specs(L, C):
    T = ATT_BLOCK
    nb = L // T
    TQ = ATT_QB * T

    def edge(i, off):
        return jnp.clip(i * ATT_QB + (off if off < 0 else ATT_QB), 0, nb - 1)

    in_specs = [pl.BlockSpec(memory_space=pltpu.SMEM),
                pl.BlockSpec((1, TQ, ATT_Q_W), lambda b, i: (b, i, 0)),
                pl.BlockSpec((1, C, KX_W), lambda b, i: (b, 0, 0)),
                pl.BlockSpec((1, T, KX_W), lambda b, i: (b, edge(i, -1), 0)),
                pl.BlockSpec((1, TQ, KX_W), lambda b, i: (b, i, 0)),
                pl.BlockSpec((1, T, KX_W), lambda b, i: (b, edge(i, 1), 0)),
                pl.BlockSpec((1, ATT_KV_W, C), lambda b, i: (b, 0, 0)),
                pl.BlockSpec((1, ATT_KV_W, T), lambda b, i: (b, 0, edge(i, -1))),
                pl.BlockSpec((1, ATT_KV_W, TQ), lambda b, i: (b, 0, i)),
                pl.BlockSpec((1, ATT_KV_W, T), lambda b, i: (b, 0, edge(i, 1)))]
    return in_specs, pl.BlockSpec((1, ATT_Q_W, TQ), lambda b, i: (b, 0, i))


N_CHAIN = 2 * ML_HEADS


def _mlstm_load_state(c_ref, n_ref, m_ref):
    return [(c_ref[ci], n_ref[ci], m_ref[ci, 0:1, 0:1]) for ci in range(N_CHAIN)]


def _mlstm_phase1(dirs, state, item, with_h, cps):
    T = ML_CHUNK
    sub, rest = divmod(item, N_CHAIN)
    d, h = divmod(rest, ML_HEADS)
    qt_ref, k_ref, vt_ref, g_ref, bc_ref, br_ref, h_ref = dirs[d]
    row = lax.broadcasted_iota(jnp.int32, (T, T), 0)
    col = lax.broadcasted_iota(jnp.int32, (T, T), 1)
    mask = (col >= row) if d == 0 else (col <= row)
    last = T - 1 if d == 0 else 0
    sc = sub if d == 0 else cps - 1 - sub
    tok = slice(sc * T, (sc + 1) * T)
    ci = d * ML_HEADS + h
    gi = d * 2 * ML_HEADS + h
    fi = gi + ML_HEADS
    k = k_ref[0, tok, h * ML_QK_DIM:(h + 1) * ML_QK_DIM]
    vt = vt_ref[0, h * ML_V_DIM:(h + 1) * ML_V_DIM, tok]
    u_col = g_ref[0, tok, gi:gi + 1] - bc_ref[0, tok, fi:fi + 1]
    b_row = br_ref[0, fi:fi + 1, tok]
    ct_old, n_old, m_old = state[ci]
    qt = st = qn2 = None
    if with_h:
        qt = qt_ref[0, h * ML_QK_DIM:(h + 1) * ML_QK_DIM, tok]
        st = _dot(k, qt)
        top = lax.broadcasted_iota(jnp.int32, (8, ML_QK_DIM), 0) == 0
        n_hi = n_old.astype(BF16)
        n_lo = (n_old - n_hi.astype(F32)).astype(BF16)
        qn2 = _dot(jnp.where(top, n_hi, n_lo), qt)
    m_last = jnp.maximum(jnp.max(u_col, axis=0, keepdims=True), m_old)
    decay = jnp.exp(m_old - m_last)
    kw = (k.astype(F32) * jnp.exp(u_col - m_last)).astype(BF16)
    state[ci] = (decay * ct_old + _dot(vt, kw), decay * n_old + _dot(jnp.ones((8, T), BF16), kw),
                 b_row[:, last:last + 1] + m_last)
    return (h, h_ref, tok, mask, qt, vt, u_col, b_row, m_old, ct_old, st, qn2)


def _mlstm_phase2(chain):
    h, h_ref, tok, mask, qt, vt, u_col, b_row, m_old, ct_old, st, qn2 = chain
    umat = jnp.where(mask, u_col, -jnp.inf)
    m_row = jnp.maximum(jnp.max(umat, axis=0, keepdims=True), m_old)
    pt = st * jnp.exp(umat - m_row)
    w_int = jnp.exp(m_old - m_row)
    e_row = jnp.exp(-(b_row + m_row))
    nq = jnp.sum(pt, axis=0, keepdims=True) + w_int * (qn2[0:1, :] + qn2[1:2, :])
    den = jnp.maximum(jnp.abs(nq), e_row)
    lhs = jnp.concatenate([vt, ct_old.astype(BF16)], axis=1)
    rhs = jnp.concatenate([pt.astype(BF16), (qt.astype(F32) * w_int).astype(BF16)], axis=0)
    h_ref[0, h * ML_V_DIM:(h + 1) * ML_V_DIM, tok] = (_dot(lhs, rhs) * (1.0 / den)).astype(BF16)


def _mlstm_commit(state, c_ref, n_ref, m_ref):
    for ci, (c_new, n_new, m_new) in enumerate(state):
        c_ref[ci] = c_new
        n_ref[ci] = n_new
        m_ref[ci] = jnp.broadcast_to(m_new, (8, LANES))


def _mlstm_ctx_kernel(kf_ref, vtf_ref, gf_ref, bcf_ref, brf_ref, kb_ref, vtb_ref, gb_ref, bcb_ref, brb_ref,
                      c_ref, n_ref, m_ref, *, cps):
    @pl.when(pl.program_id(1) == 0)
    def _():
        c_ref[...] = jnp.zeros_like(c_ref)
        n_ref[...] = jnp.zeros_like(n_ref)
        m_ref[...] = jnp.zeros_like(m_ref)

    dirs = ((None, kf_ref, vtf_ref, gf_ref, bcf_ref, brf_ref, None),
            (None, kb_ref, vtb_ref, gb_ref, bcb_ref, brb_ref, None))
    state_refs = (c_ref.at[0], n_ref.at[0], m_ref.at[0])
    state = _mlstm_load_state(*state_refs)
    for item in range(cps * N_CHAIN):
        _mlstm_phase1(dirs, state, item, with_h=False, cps=cps)
    _mlstm_commit(state, *state_refs)


def _mixer_kernel(sink_ref, q_ref, kc_ref, kp_ref, k0_ref, kn_ref, vc_ref, vp_ref, v0_ref, vn_ref,
                  c0_ref, n0_ref, m0_ref, qtf_ref, kf_ref, vtf_ref, gf_ref, bcf_ref, brf_ref,
                  qtb_ref, kb_ref, vtb_ref, gb_ref, bcb_ref, brb_ref,
                  att_ref, hf_ref, hb_ref, c_ref, n_ref, m_ref, *, n_steps, cps):
    @pl.when(pl.program_id(1) == 0)
    def _():
        c_ref[...] = c0_ref[0]
        n_ref[...] = n0_ref[0]
        m_ref[...] = m0_ref[0]

    n_iter, scores, finish = _attn_stream(sink_ref, q_ref, kc_ref, kp_ref, k0_ref, kn_ref,
                                          vc_ref, vp_ref, v0_ref, vn_ref, att_ref, n_steps)
    pending = [scores(n) for n in range(ATT_AHEAD)]
    dirs = ((qtf_ref, kf_ref, vtf_ref, gf_ref, bcf_ref, brf_ref, hf_ref),
            (qtb_ref, kb_ref, vtb_ref, gb_ref, bcb_ref, brb_ref, hb_ref))
    state = _mlstm_load_state(c_ref, n_ref, m_ref)
    n_items = cps * N_CHAIN
    items = [_mlstm_phase1(dirs, state, it, with_h=True, cps=cps) for it in range(ML_AHEAD)]
    for n in range(max(n_iter, n_items)):
        if n < n_iter:
            st = pending.pop(0)
            if n + ATT_AHEAD < n_iter:
                pending.append(scores(n + ATT_AHEAD))
            finish(n, st)
        if n < n_items:
            if n + ML_AHEAD < n_items:
                items.append(_mlstm_phase1(dirs, state, n + ML_AHEAD, with_h=True, cps=cps))
            _mlstm_phase2(items.pop(0))
    _mlstm_commit(state, c_ref, n_ref, m_ref)


def _mlstm_specs(T, order, with_q):
    specs = [
        pl.BlockSpec((1, ML_QK_W, T), lambda b, j: (b, 0, order(j))),
        pl.BlockSpec((1, T, ML_QK_W), lambda b, j: (b, order(j), 0)),
        pl.BlockSpec((1, ML_V_W, T), lambda b, j: (b, 0, order(j))),
        pl.BlockSpec((1, T, ML_GATE_W), lambda b, j: (b, order(j), 0)),
        pl.BlockSpec((1, T, ML_GATE_W), lambda b, j: (b, order(j), 0)),
        pl.BlockSpec((1, ML_GATE_W, T), lambda b, j: (b, 0, order(j))),
    ]
    return specs if with_q else specs[1:]


_STATE_SHAPES = ((N_CHAIN, ML_V_DIM, ML_QK_DIM), (N_CHAIN, 8, ML_QK_DIM), (N_CHAIN, 8, LANES))


ML_AHEAD = 32
ML_CPS = 4
ML_CTX_CPS = 2


def _mlstm_ctx_call(mk, mvt, g, bc, br):
    B, C, _ = mk.shape
    T = ML_CTX_CPS * ML_CHUNK
    nc = C // T
    state_specs = [pl.BlockSpec((1,) + s, lambda b, j: (b, 0, 0, 0)) for s in _STATE_SHAPES]
    return pl.pallas_call(
        functools.partial(_mlstm_ctx_kernel, cps=ML_CTX_CPS),
        grid=(B, nc),
        in_specs=_mlstm_specs(T, lambda j: j, False) + _mlstm_specs(T, lambda j: nc - 1 - j, False),
        out_specs=state_specs,
        out_shape=[jax.ShapeDtypeStruct((B,) + s, F32) for s in _STATE_SHAPES],
        compiler_params=_cparams(("arbitrary", "arbitrary")),
        name="mlstm_ctx",
    )(mk, mvt, g, bc, br, mk, mvt, g, bc, br)


def _mixer_call(sink, q, kx, vt, kx_c, vt_c, state, mqt, mk, mvt, g, bc, br):
    B, L, _ = mk.shape
    C = kx_c.shape[1]
    T = ML_CPS * ML_CHUNK
    assert T == ATT_QB * ATT_BLOCK
    nc = L // T
    att_in, att_out = _attn_specs(L, C)
    state_specs = [pl.BlockSpec((1,) + s, lambda b, j: (b, 0, 0, 0)) for s in _STATE_SHAPES]
    out_specs = [att_out,
                 pl.BlockSpec((1, ML_V_W, T), lambda b, j: (b, 0, j)),
                 pl.BlockSpec((1, ML_V_W, T), lambda b, j: (b, 0, nc - 1 - j))]
    return pl.pallas_call(
        functools.partial(_mixer_kernel, n_steps=nc, cps=ML_CPS),
        grid=(B, nc),
        in_specs=(att_in + state_specs + _mlstm_specs(T, lambda j: j, True)
                  + _mlstm_specs(T, lambda j: nc - 1 - j, True)),
        out_specs=out_specs,
        out_shape=[jax.ShapeDtypeStruct((B, ATT_Q_W, L), BF16)] + [jax.ShapeDtypeStruct((B, ML_V_W, L), BF16)] * 2,
        scratch_shapes=[pltpu.VMEM(s, F32) for s in _STATE_SHAPES],
        compiler_params=_cparams(("arbitrary", "arbitrary")),
        name="mixer",
    )(sink, q, kx_c, kx, kx, kx, vt_c, vt, vt, vt, *state, mqt, mk, mvt, g, bc, br, mqt, mk, mvt, g, bc, br)


def _merge_kernel(att_ref, hf_ref, hb_ref, so_ref, sga_ref, sgm_ref, x_ref, g1_ref, mod2_ref, mlw_ref, n2w_ref,
                  wa_ref, wm_ref, wo_ref, xmid_ref, h2_ref, y_ref):
    s = pl.program_id(0)
    last = pl.num_programs(0) - 1
    cur = s % 2

    def body(do_branch, do_out):
        if do_out:
            y2 = _dot_tn(y_ref[1 - cur], wo_ref[...])
        if do_branch:
            ya = _dot_tn(wa_ref[...], att_ref[0])
            ht = hf_ref[0].astype(F32) + hb_ref[0].astype(F32)
            parts = []
            for h in range(ML_HEADS):
                seg = ht[h * ML_V_DIM:(h + 1) * ML_V_DIM, :]
                ms = jnp.mean(seg * seg, axis=0, keepdims=True)
                parts.append(seg * lax.rsqrt(ms + EPS))
            ml = (jnp.concatenate(parts, axis=0) * mlw_ref[...] * so_ref[0].astype(F32)).astype(BF16)
            ym = _dot_tn(wm_ref[...], ml)
        if do_out:
            xm = x_ref[0] + g1_ref[0, 0] * y2
            xmid_ref[0] = xm
            ms = jnp.mean(xm * xm, axis=-1, keepdims=True)
            h2 = xm * lax.rsqrt(ms + EPS) * n2w_ref[...]
            h2_ref[0] = (h2 * (1.0 + mod2_ref[1, 0]) + mod2_ref[0, 0]).astype(BF16)
        if do_branch:
            y_ref[cur] = (sga_ref[0].astype(F32) * ya + sgm_ref[0].astype(F32) * ym).astype(BF16)

    pl.when(s == 0)(lambda: body(True, False))
    pl.when((s > 0) & (s < last))(lambda: body(True, True))
    pl.when(s == last)(lambda: body(False, True))


def _merge_call(att_t, hf_t, hb_t, so_t, sga_t, sgm_t, x, mod6, mlw_b, n2w, wa, wm, wo, tm):
    B, L, _ = x.shape
    nt = L // tm
    n_all = B * nt

    def tile_in(s):
        t = jnp.minimum(s, n_all - 1)
        return t // nt, t % nt

    def tile_out(s):
        t = jnp.maximum(s - 1, 0)
        return t // nt, t % nt

    def in_t(h):
        return pl.BlockSpec((1, h, tm), lambda s: (tile_in(s)[0], 0, tile_in(s)[1]))

    def out_rows(w):
        return pl.BlockSpec((1, tm, w), lambda s: (*tile_out(s), 0))

    def const(shape):
        return pl.BlockSpec(shape, lambda s: (0,) * len(shape))

    return pl.pallas_call(
        _merge_kernel,
        grid=(n_all + 1,),
        in_specs=[in_t(D), in_t(D), in_t(D), in_t(D), in_t(D), in_t(D), out_rows(D),
                  _mod_spec(MOD_G1, 1, lambda s: tile_out(s)[0]),
                  _mod_spec(MOD_SH2, 2, lambda s: tile_out(s)[0]),
                  const((D, tm)), const((1, D)), const((D, D)), const((D, D)), const((D, D))],
        out_specs=[out_rows(D), out_rows(D)],
        out_shape=[jax.ShapeDtypeStruct((B, L, D), F32), jax.ShapeDtypeStruct((B, L, D), BF16)],
        scratch_shapes=[pltpu.VMEM((2, D, tm), BF16)],
        compiler_params=_cparams(("arbitrary",)),
        name="merge",
    )(att_t, hf_t, hb_t, so_t, sga_t, sgm_t, x, mod6, mod6, mlw_b, n2w, wa, wm, wo)


HALO = 16
FFN_AHEAD = 1


def _ffn_kernel(h_ref, hp_ref, hn_ref, xmid_ref, mod_ref, wup_ref, cw_ref, cb_ref, wdn_ref, o_ref,
                act_ref, *, n_tiles, tn, dn):
    s = pl.program_id(0)
    last = pl.num_programs(0) - 1

    @pl.when(s == 0)
    def _():
        _ffn_body(h_ref, hp_ref, hn_ref, xmid_ref, mod_ref, wup_ref, cw_ref, cb_ref, wdn_ref, o_ref, act_ref,
                  n_tiles=n_tiles, tn=tn, dn=dn, do_up=True, do_down=False)

    @pl.when((s > 0) & (s < last))
    def _():
        _ffn_body(h_ref, hp_ref, hn_ref, xmid_ref, mod_ref, wup_ref, cw_ref, cb_ref, wdn_ref, o_ref, act_ref,
                  n_tiles=n_tiles, tn=tn, dn=dn, do_up=True, do_down=True)

    @pl.when(s == last)
    def _():
        _ffn_body(h_ref, hp_ref, hn_ref, xmid_ref, mod_ref, wup_ref, cw_ref, cb_ref, wdn_ref, o_ref, act_ref,
                  n_tiles=n_tiles, tn=tn, dn=dn, do_up=False, do_down=True)


def _ffn_body(h_ref, hp_ref, hn_ref, xmid_ref, mod_ref, wup_ref, cw_ref, cb_ref, wdn_ref, o_ref, act_ref,
              *, n_tiles, tn, dn, do_up, do_down):
    s = pl.program_id(0)
    i = s % n_tiles
    cur = s % 2
    tm = h_ref.shape[1]
    n_chunks = D_FF // tn
    n_dn = D // dn
    act_prev = act_ref[1 - cur] if do_down else None

    def down(k):
        cols = slice(k * dn, (k + 1) * dn)
        o_ref[0, :, cols] = xmid_ref[0, :, cols] + mod_ref[0, 0, :, cols] * _dot(act_prev, wdn_ref[:, cols])

    if not do_up:
        for k in range(n_dn):
            down(k)
        return

    h = h_ref[0]
    prev_row = jnp.where(i > 0, hp_ref[0].astype(F32)[HALO - 1:HALO, :], 0.0)
    next_row = jnp.where(i < n_tiles - 1, hn_ref[0].astype(F32)[0:1, :], 0.0)
    top = lax.broadcasted_iota(jnp.int32, (16, D), 0) < 8
    edge = jnp.where(top, prev_row, next_row).astype(BF16)
    row8 = lax.broadcasted_iota(jnp.int32, (8, tn), 0)
    h_ext = jnp.concatenate([h, edge], axis=0)

    def up(c0):
        u_ext = _dot(h_ext, wup_ref[:, c0:c0 + tn])
        return u_ext[:tm], u_ext[tm:]

    def conv(u, ue, c0):
        below = pltpu.roll(u, 1, 0)
        above = pltpu.roll(u, tm - 1, 0)
        below = jnp.concatenate([jnp.where(row8 == 0, ue[0:8], below[0:8]), below[8:]], axis=0)
        above = jnp.concatenate([above[:tm - 8], jnp.where(row8 == 7, ue[8:16], above[tm - 8:])], axis=0)
        cw = cw_ref[:, c0:c0 + tn]
        return cb_ref[:, c0:c0 + tn] + below * cw[0:1] + u * cw[1:2] + above * cw[2:3]

    pending = [(up(c * tn), up(D_FF + c * tn)) for c in range(FFN_AHEAD)]
    done = 0
    for c in range(n_chunks):
        (ua, uae), (ug, uge) = pending.pop(0)
        if c + FFN_AHEAD < n_chunks:
            pending.append((up((c + FFN_AHEAD) * tn), up(D_FF + (c + FFN_AHEAD) * tn)))
        while do_down and done * n_chunks < (c + 1) * n_dn:
            down(done)
            done += 1
        a = conv(ua, uae, c * tn)
        hg = conv(ug, uge, D_FF + c * tn)
        act_ref[cur, :, c * tn:(c + 1) * tn] = ((hg + hg * jnp.tanh(hg)) * a).astype(BF16)


def _ffn_call(h2, xmid, g2, wup, cw, cb, wdn, tm, tn, dn):
    B, L, _ = xmid.shape
    nt = L // tm
    n_all = B * nt
    hb = tm // HALO
    nhb = L // HALO

    def tile_in(s):
        t = jnp.minimum(s, n_all - 1)
        return t // nt, t % nt

    def tile_out(s):
        t = jnp.maximum(s - 1, 0)
        return t // nt, t % nt

    def in_spec():
        return pl.BlockSpec((1, tm, D), lambda s: (*tile_in(s), 0))

    def out_spec():
        return pl.BlockSpec((1, tm, D), lambda s: (*tile_out(s), 0))

    def prev_halo(s):
        b, i = tile_in(s)
        return b, jnp.maximum(i * hb - 1, 0), 0

    def next_halo(s):
        b, i = tile_in(s)
        return b, jnp.minimum((i + 1) * hb, nhb - 1), 0

    def const(shape):
        return pl.BlockSpec(shape, lambda s: (0,) * len(shape))

    return pl.pallas_call(
        functools.partial(_ffn_kernel, n_tiles=nt, tn=tn, dn=dn),
        grid=(n_all + 1,),
        in_specs=[in_spec(),
                  pl.BlockSpec((1, HALO, D), prev_halo),
                  pl.BlockSpec((1, HALO, D), next_halo),
                  out_spec(),
                  _mod_spec(MOD_G2, 1, lambda s: tile_out(s)[0]),
                  const(wup.shape), const(cw.shape), const(cb.shape), const(wdn.shape)],
        out_specs=out_spec(),
        out_shape=jax.ShapeDtypeStruct((B, L, D), F32),
        scratch_shapes=[pltpu.VMEM((2, tm, D_FF), BF16)],
        compiler_params=_cparams(("arbitrary",)),
        name="ffn",
    )(h2, h2, h2, xmid, g2, wup, cw, cb, wdn)


def _pair_perm(n_heads):
    half = ATT_HEAD_DIM // 2
    idx = []
    for p in range(n_heads // 2):
        for sub in range(4):
            head = 2 * p + (sub % 2)
            d0 = (sub // 2) * half
            idx.extend(head * ATT_HEAD_DIM + d0 + e for e in range(half))
    return np.asarray(idx, np.int32)


def _rope_tables(L):
    f32 = np.float32
    rows = L // GRID_W
    row = np.repeat(np.arange(rows, dtype=f32), GRID_W)
    col = np.tile(np.arange(GRID_W, dtype=f32), rows)
    n_freq = ATT_HEAD_DIM // 4
    inv_freq = (f32(ROPE_BASE) ** (-np.arange(n_freq, dtype=f32) / f32(n_freq))).astype(f32)
    ang = np.concatenate([row[:, None] * inv_freq, col[:, None] * inv_freq], axis=-1).astype(f32)
    cos = np.tile(np.cos(ang).astype(f32), (1, 4))
    sin = np.tile(np.sin(ang).astype(f32), (1, 4))
    sign = np.where(np.arange(LANES) < LANES // 2, -1.0, 1.0).astype(f32)
    return jnp.asarray(cos), jnp.asarray(sin * sign)


def kernel(x, c, ctx, c_ctx, w_mod, b_mod, norm1_w, w_in, q_norm_w, k_norm_w, attn_sink, ml_gate_b, ml_norm_w,
           w_branch_att, w_branch_ml, w_out, norm2_w, w_up, conv_w, conv_b, w_down):
    B, L, _ = x.shape
    C = ctx.shape[1]
    assert L % 512 == 0 and C % 256 == 0 and L % GRID_W == 0
    l = 0
    tm_merge = 512

    n_rows = -(-(B + 1) // 16) * 16
    cc = jnp.concatenate([c, c_ctx[None, :], jnp.zeros((n_rows - B - 1, D), F32)], axis=0)
    mod6 = _mod_call(cc, w_mod[l], b_mod[l][None, :])

    w = w_in[l]
    qperm = _pair_perm(ATT_HEADS)
    kperm = _pair_perm(ATT_KV_HEADS)
    def pair_cols(wc, n_heads):
        half = ATT_HEAD_DIM // 2
        wc = wc.reshape(D, n_heads // 2, 2, 2, half).transpose(0, 1, 3, 2, 4)
        return wc.reshape(D, n_heads * ATT_HEAD_DIM)

    w_q = pair_cols(w[:, _O_AQ:_O_AQ + ATT_Q_W], ATT_HEADS)
    w_k = pair_cols(w[:, _O_AK:_O_AK + ATT_KV_W], ATT_KV_HEADS)
    w_g = jnp.pad(w[:, _O_MG:_O_MG + ML_GATE_W], ((0, 0), (0, LANES - ML_GATE_W)))
    w_mk = w[:, _O_MK:_O_MK + ML_QK_W] * (ML_QK_DIM ** -0.5)
    w_p = jnp.concatenate([w_q, w_k, w_mk, w_g], axis=1).astype(BF16)
    w_t = jnp.concatenate([w[:, _O_AV:_O_AV + ATT_KV_W], w[:, _O_MQ:_O_MQ + ML_QK_W], w[:, _O_MV:_O_MV + ML_V_W],
                           w[:, _O_MO:_O_MO + ML_V_W], w[:, _O_GA:_O_GA + D], w[:, _O_GM:_O_GM + D]],
                          axis=1).T.astype(BF16)
    wgt = w[:, _O_MG:_O_MG + ML_GATE_W].T.astype(BF16)

    head_of_col = np.concatenate([qperm // ATT_HEAD_DIM, ATT_HEADS + kperm // ATT_HEAD_DIM])
    e_np = (head_of_col[:, None] == np.arange(LANES)[None, :]).astype(np.float32)
    e_mat = jnp.asarray(e_np, BF16)
    et_mat = jnp.asarray(np.concatenate([e_np.T, e_np.T], axis=0), BF16)
    def pair_tiled(wn, n_heads):
        half = ATT_HEAD_DIM // 2
        return jnp.tile(jnp.concatenate([wn[:half], wn[:half], wn[half:], wn[half:]]), n_heads // 2)

    qkw = jnp.concatenate([pair_tiled(q_norm_w[l], ATT_HEADS) * (ATT_SCALE * LOG2E),
                           pair_tiled(k_norm_w[l], ATT_KV_HEADS)])[None, :]
    cos_t, sin_t = _rope_tables(L)
    gb = ml_gate_b[l].reshape(1, ML_GATE_W)
    gbt = ml_gate_b[l].reshape(ML_GATE_W, 1)
    n1w = norm1_w[l][None, :]

    kx_c, mk_c, g_c, bc_c, vt_c, mvt_c, br_c = _ctxproj_call(
        ctx, mod6, B, n1w, w_p, w_t, wgt, e_mat, et_mat, qkw, gb, gbt, tm=256)
    q, kx, mk, g, bc, vt, mqt, mvt, sot, sgat, sgmt, br = _inproj_call(
        x, mod6, n1w, w_p, w_t, wgt, e_mat, et_mat, qkw, cos_t, sin_t, gb, gbt, tm=512)

    state = _mlstm_ctx_call(mk_c, mvt_c, g_c, bc_c, br_c)
    att_t, hf_t, hb_t = _mixer_call(attn_sink[l], q, kx, vt, kx_c, vt_c, state, mqt, mk, mvt, g, bc, br)

    mlw_b = jnp.broadcast_to(ml_norm_w[l][:, None], (ML_V_W, tm_merge))
    xmid, h2 = _merge_call(att_t, hf_t, hb_t, sot, sgat, sgmt, x, mod6, mlw_b, norm2_w[l][None, :],
                           w_branch_att[l].astype(BF16), w_branch_ml[l].astype(BF16), w_out[l].astype(BF16),
                           tm=tm_merge)
    gate_half = jnp.where(jnp.arange(2 * D_FF) < D_FF, 1.0, 0.5).astype(F32)
    out = _ffn_call(h2, xmid, mod6, w_up[l].astype(BF16), conv_w[l] * gate_half, (conv_b[l] * gate_half)[None, :],
                    w_down[l].astype(BF16), tm=512, tn=256, dn=256)
    return out
```

```python
import functools

import jax
import jax.numpy as jnp
import numpy as np
from jax import lax
from jax.experimental import pallas as pl
from jax.experimental.pallas import tpu as pltpu

D = 1024
GRID_W = 64
ATT_HEADS = 16
ATT_KV_HEADS = 4
ATT_HEAD_DIM = 64
ATT_GROUP = ATT_HEADS // ATT_KV_HEADS
ATT_BLOCK = 128
WINDOW = 128
ROPE_BASE = 10000.0
ATT_SCALE = ATT_HEAD_DIM ** -0.5
LOG2E = 1.4426950408889634
ML_HEADS = 4
ML_QK_DIM = 128
ML_V_DIM = 256
ML_CHUNK = 128
D_FF = 2816
EPS = 1e-6
NEG_INF = -1e30

ATT_Q_W = ATT_HEADS * ATT_HEAD_DIM
ATT_KV_W = ATT_KV_HEADS * ATT_HEAD_DIM
ML_QK_W = ML_HEADS * ML_QK_DIM
ML_V_W = ML_HEADS * ML_V_DIM
ML_GATE_W = 2 * 2 * ML_HEADS

LANES = 128
KX_W = ATT_KV_HEADS * 2 * LANES
VMEM_LIMIT = 56 * 1024 * 1024

TM_CTX = 256
TM_INPROJ = 512
TM_MIXER = 512
TM_MERGE = 512
TM_FFN = 512
FFN_UP_COLS = 256
FFN_DOWN_COLS = 256

BF16 = jnp.bfloat16
F32 = jnp.float32

_O_AQ = 0
_O_AK = _O_AQ + ATT_Q_W
_O_AV = _O_AK + ATT_KV_W
_O_MQ = _O_AV + ATT_KV_W
_O_MK = _O_MQ + ML_QK_W
_O_MV = _O_MK + ML_QK_W
_O_MO = _O_MV + ML_V_W
_O_MG = _O_MO + ML_V_W
_O_GA = _O_MG + ML_GATE_W
_O_GM = _O_GA + D

QK_W = ATT_Q_W + ATT_KV_W
_P_QK = 0
_P_MK = _P_QK + QK_W
_P_MG = _P_MK + ML_QK_W
_P_END = _P_MG + LANES
_R_V = 0
_R_MQ = _R_V + ATT_KV_W
_R_MV = _R_MQ + ML_QK_W
_R_MO = _R_MV + ML_V_W
_R_GA = _R_MO + ML_V_W
_R_GM = _R_GA + D
_R_END = _R_GM + D


def _dot(a, b):
    return jnp.dot(a, b, preferred_element_type=F32)


def _dot_nt(a, b):
    return lax.dot_general(a, b, (((1,), (1,)), ((), ())), preferred_element_type=F32)


def _dot_tn(a, b):
    return lax.dot_general(a, b, (((0,), (0,)), ((), ())), preferred_element_type=F32)


def _cparams(sem):
    return pltpu.CompilerParams(dimension_semantics=sem, vmem_limit_bytes=VMEM_LIMIT)


def _mod_kernel(c_ref, w_ref, b_ref, o_ref):
    c = c_ref[...]
    a = c * jax.nn.sigmoid(c)
    a_hi = a.astype(BF16)
    a_lo = (a - a_hi.astype(F32)).astype(BF16)
    w = w_ref[...]
    w_hi = w.astype(BF16)
    w_lo = (w - w_hi.astype(F32)).astype(BF16)
    rows = a.shape[0]
    both = _dot(jnp.concatenate([a_hi, a_lo], axis=0), w_hi)
    o_ref[0, :, 0, :] = both[:rows] + both[rows:] + _dot(a_hi, w_lo) + b_ref[...]


MOD_SH1, MOD_SC1, MOD_SH2, MOD_SC2, MOD_G1, MOD_G2 = range(6)


def _mod_call(cc, w_mod, b_mod):
    rows = cc.shape[0]
    n_seg = w_mod.shape[1] // D
    assert n_seg == 6

    def out_pos(j):
        return jnp.where(j == 2, MOD_G1, jnp.where((j == 3) | (j == 4), j - 1, j))

    return pl.pallas_call(
        _mod_kernel,
        grid=(n_seg,),
        in_specs=[pl.BlockSpec((rows, D), lambda j: (0, 0)),
                  pl.BlockSpec((D, D), lambda j: (0, j)),
                  pl.BlockSpec((1, D), lambda j: (0, j))],
        out_specs=pl.BlockSpec((1, rows, 1, D), lambda j: (out_pos(j), 0, 0, 0)),
        out_shape=jax.ShapeDtypeStruct((n_seg, rows, 1, D), F32),
        compiler_params=_cparams(("arbitrary",)),
        name="mod",
    )(cc, w_mod, b_mod)


def _mod_spec(seg, n_seg, row_of):
    return pl.BlockSpec((n_seg, 1, 1, D), lambda *ids: (seg // n_seg, row_of(*ids), 0, 0))


def _split2(x):
    x1 = x.astype(BF16)
    x2 = (x - x1.astype(F32)).astype(BF16)
    return x1, x2


def _norm_modulate(x, n1w_ref, mod_ref):
    ms = jnp.mean(x * x, axis=-1, keepdims=True)
    y = x * lax.rsqrt(ms + EPS) * n1w_ref[...]
    return (y * (1.0 + mod_ref[1, 0]) + mod_ref[0, 0]).astype(BF16)


def _sigmoid(x):
    return 0.5 * jnp.tanh(0.5 * x) + 0.5


def _head_rms_scale(ss):
    r = lax.rsqrt(ss * (1.0 / ATT_HEAD_DIM) + EPS)
    r_hi = r.astype(BF16)
    r_lo = (r - r_hi.astype(F32)).astype(BF16)
    return jnp.concatenate([r_hi, r_lo], axis=1)


def _store_k_variants(k_ref, pair, o):
    lane = lax.broadcasted_iota(jnp.int32, (1, LANES), 1)
    keep = ((lane // 32) % 2) == 0
    c0 = 4 * pair * LANES
    k_ref[0, :, c0:c0 + LANES] = jnp.where(keep, o, 0.0).astype(BF16)
    k_ref[0, :, c0 + LANES:c0 + 2 * LANES] = jnp.where(keep, 0.0, pltpu.roll(o, 32, 1)).astype(BF16)
    k_ref[0, :, c0 + 2 * LANES:c0 + 3 * LANES] = jnp.where(keep, pltpu.roll(o, 96, 1), 0.0).astype(BF16)
    k_ref[0, :, c0 + 3 * LANES:c0 + 4 * LANES] = jnp.where(keep, 0.0, o).astype(BF16)


def _cum_gates_cols(g16, tri_lo, tri_up):
    lf = jax.nn.log_sigmoid(g16)
    fwd_col = lax.broadcasted_iota(jnp.int32, (1, ML_GATE_W), 1) < ML_GATE_W // 2
    out = []
    for c in range(g16.shape[0] // ML_CHUNK):
        parts = _split2(lf[c * ML_CHUNK:(c + 1) * ML_CHUNK])
        out.append(jnp.where(fwd_col, sum(_dot(tri_lo, p) for p in parts), sum(_dot(tri_up, p) for p in parts)))
    return jnp.concatenate(out, axis=0)


def _cum_gates_rows(gt16, tri_lo, tri_up):
    lf = jax.nn.log_sigmoid(gt16)
    fwd_row = lax.broadcasted_iota(jnp.int32, (ML_GATE_W, 1), 0) < ML_GATE_W // 2
    out = []
    for c in range(gt16.shape[1] // ML_CHUNK):
        parts = _split2(lf[:, c * ML_CHUNK:(c + 1) * ML_CHUNK])
        out.append(jnp.where(fwd_row, sum(_dot(p, tri_up) for p in parts), sum(_dot(p, tri_lo) for p in parts)))
    return jnp.concatenate(out, axis=1)


def _inproj_kernel(x_ref, mod_ref, n1w_ref, w_ref, wt_ref, wgt_ref, e_ref, et_ref, qkw_ref,
                   cos_ref, sin_ref, gb_ref, gbt_ref, tril_ref, triu_ref,
                   q_ref, k_ref, mk_ref, g_ref, bc_ref, vt_ref, mqt_ref, mvt_ref, sot_ref, sgat_ref, sgmt_ref, br_ref,
                   hn_ref):
    hn_ref[...] = _norm_modulate(x_ref[0], n1w_ref, mod_ref)
    hn = hn_ref[...]

    def ft(r0, height):
        return _dot_nt(wt_ref[r0:r0 + height, :], hn)

    acc = _dot(hn, w_ref[:, _P_QK:_P_QK + QK_W])
    g16 = _dot(hn, w_ref[:, _P_MG:_P_MG + LANES])[:, :ML_GATE_W] + gb_ref[...]
    gt16 = _dot_nt(wgt_ref[...], hn) + gbt_ref[...]
    g_ref[0] = g16
    mk_ref[0] = _dot(hn, w_ref[:, _P_MK:_P_MK + ML_QK_W]).astype(BF16)
    ss = _dot((acc * acc).astype(BF16), e_ref[...])
    vt_ref[0] = ft(_R_V, ATT_KV_W).astype(BF16)
    sot_ref[0] = _sigmoid(ft(_R_MO, ML_V_W)).astype(BF16)
    rb = _dot(_head_rms_scale(ss), et_ref[...])
    sgat_ref[0] = _sigmoid(ft(_R_GA, D)).astype(BF16)
    bc_ref[0] = _cum_gates_cols(g16, tril_ref[...], triu_ref[...])

    qn = acc * rb * qkw_ref[...]
    cos = cos_ref[...]
    sin = sin_ref[...]
    for gi in range(QK_W // LANES):
        xs = qn[:, gi * LANES:(gi + 1) * LANES]
        o = xs * cos + pltpu.roll(xs, LANES // 2, 1) * sin
        if gi < ATT_Q_W // LANES:
            q_ref[0, :, gi * LANES:(gi + 1) * LANES] = o.astype(BF16)
        else:
            _store_k_variants(k_ref, gi - ATT_Q_W // LANES, o)

    sgmt_ref[0] = _sigmoid(ft(_R_GM, D)).astype(BF16)
    br_ref[0] = _cum_gates_rows(gt16, tril_ref[...], triu_ref[...])
    mqt_ref[0] = ft(_R_MQ, ML_QK_W).astype(BF16)
    mvt_ref[0] = ft(_R_MV, ML_V_W).astype(BF16)


def _ctxproj_kernel(x_ref, mod_ref, n1w_ref, w_ref, wt_ref, wgt_ref, e_ref, et_ref, qkw_ref,
                    gb_ref, gbt_ref, tril_ref, triu_ref,
                    k_ref, mk_ref, g_ref, bc_ref, vt_ref, mvt_ref, br_ref):
    hn = _norm_modulate(x_ref[0], n1w_ref, mod_ref)
    acc = _dot(hn, w_ref[:, _P_QK + ATT_Q_W:_P_QK + QK_W])
    g16 = _dot(hn, w_ref[:, _P_MG:_P_MG + LANES])[:, :ML_GATE_W] + gb_ref[...]
    gt16 = _dot_nt(wgt_ref[...], hn) + gbt_ref[...]
    g_ref[0] = g16
    mk_ref[0] = _dot(hn, w_ref[:, _P_MK:_P_MK + ML_QK_W]).astype(BF16)
    ss = _dot((acc * acc).astype(BF16), e_ref[ATT_Q_W:QK_W, :])
    vt_ref[0] = _dot_nt(wt_ref[_R_V:_R_V + ATT_KV_W, :], hn).astype(BF16)
    rb = _dot(_head_rms_scale(ss), et_ref[:, ATT_Q_W:QK_W])
    mvt_ref[0] = _dot_nt(wt_ref[_R_MV:_R_MV + ML_V_W, :], hn).astype(BF16)
    bc_ref[0] = _cum_gates_cols(g16, tril_ref[...], triu_ref[...])
    kn = acc * rb * qkw_ref[:, ATT_Q_W:QK_W]
    for pair in range(ATT_KV_W // LANES):
        _store_k_variants(k_ref, pair, kn[:, pair * LANES:(pair + 1) * LANES])
    br_ref[0] = _cum_gates_rows(gt16, tril_ref[...], triu_ref[...])


def _chunk_tri(lower):
    r = np.arange(ML_CHUNK)[:, None]
    c = np.arange(ML_CHUNK)[None, :]
    return jnp.asarray((c <= r) if lower else (c >= r), BF16)


def _const_spec(shape):
    return pl.BlockSpec(shape, lambda b, i: (0,) * len(shape))


def _inproj_call(x, mod6, n1w, w_p, w_t, wgt, e_mat, et_mat, qkw, cos_t, sin_t, gb, gbt, tm):
    B, L, _ = x.shape
    tril = _chunk_tri(True)
    triu = _chunk_tri(False)

    def rows(w):
        return pl.BlockSpec((1, tm, w), lambda b, i: (b, i, 0))

    def cols(h):
        return pl.BlockSpec((1, h, tm), lambda b, i: (b, 0, i))

    consts = [n1w, w_p, w_t, wgt, e_mat, et_mat, qkw]
    tail = [gb, gbt, tril, triu]
    in_specs = ([rows(D), _mod_spec(MOD_SH1, 2, lambda b, i: b)]
                + [_const_spec(a.shape) for a in consts]
                + [pl.BlockSpec((tm, LANES), lambda b, i: (i, 0))] * 2
                + [_const_spec(a.shape) for a in tail])
    out_specs = [rows(ATT_Q_W), rows(KX_W), rows(ML_QK_W), rows(ML_GATE_W), rows(ML_GATE_W),
                 cols(ATT_KV_W), cols(ML_QK_W), cols(ML_V_W), cols(ML_V_W), cols(D), cols(D), cols(ML_GATE_W)]
    out_shape = [
        jax.ShapeDtypeStruct((B, L, ATT_Q_W), BF16),
        jax.ShapeDtypeStruct((B, L, KX_W), BF16),
        jax.ShapeDtypeStruct((B, L, ML_QK_W), BF16),
        jax.ShapeDtypeStruct((B, L, ML_GATE_W), F32),
        jax.ShapeDtypeStruct((B, L, ML_GATE_W), F32),
        jax.ShapeDtypeStruct((B, ATT_KV_W, L), BF16),
        jax.ShapeDtypeStruct((B, ML_QK_W, L), BF16),
        jax.ShapeDtypeStruct((B, ML_V_W, L), BF16),
        jax.ShapeDtypeStruct((B, ML_V_W, L), BF16),
        jax.ShapeDtypeStruct((B, D, L), BF16),
        jax.ShapeDtypeStruct((B, D, L), BF16),
        jax.ShapeDtypeStruct((B, ML_GATE_W, L), F32),
    ]
    return pl.pallas_call(
        _inproj_kernel,
        grid=(B, L // tm),
        in_specs=in_specs,
        out_specs=out_specs,
        out_shape=out_shape,
        scratch_shapes=[pltpu.VMEM((tm, D), BF16)],
        compiler_params=_cparams(("arbitrary", "arbitrary")),
        name="inproj",
    )(x, mod6, *consts, cos_t, sin_t, *tail)


def _ctxproj_call(ctx, mod6, ctx_row, n1w, w_p, w_t, wgt, e_mat, et_mat, qkw, gb, gbt, tm):
    B, C, _ = ctx.shape
    tril = _chunk_tri(True)
    triu = _chunk_tri(False)

    def rows(w):
        return pl.BlockSpec((1, tm, w), lambda b, i: (b, i, 0))

    def cols(h):
        return pl.BlockSpec((1, h, tm), lambda b, i: (b, 0, i))

    consts = [n1w, w_p, w_t, wgt, e_mat, et_mat, qkw, gb, gbt, tril, triu]
    out_specs = [rows(KX_W), rows(ML_QK_W), rows(ML_GATE_W), rows(ML_GATE_W),
                 cols(ATT_KV_W), cols(ML_V_W), cols(ML_GATE_W)]
    out_shape = [
        jax.ShapeDtypeStruct((B, C, KX_W), BF16),
        jax.ShapeDtypeStruct((B, C, ML_QK_W), BF16),
        jax.ShapeDtypeStruct((B, C, ML_GATE_W), F32),
        jax.ShapeDtypeStruct((B, C, ML_GATE_W), F32),
        jax.ShapeDtypeStruct((B, ATT_KV_W, C), BF16),
        jax.ShapeDtypeStruct((B, ML_V_W, C), BF16),
        jax.ShapeDtypeStruct((B, ML_GATE_W, C), F32),
    ]
    return pl.pallas_call(
        _ctxproj_kernel,
        grid=(B, C // tm),
        in_specs=([rows(D), _mod_spec(MOD_SH1, 2, lambda b, i: ctx_row)]
                  + [_const_spec(a.shape) for a in consts]),
        out_specs=out_specs,
        out_shape=out_shape,
        compiler_params=_cparams(("arbitrary", "arbitrary")),
        name="ctxproj",
    )(ctx, mod6, *consts)


ATT_AHEAD = 12

ATT_QB = 4


def _attn_stream(sink_ref, q_ref, kc_ref, kp_ref, k0_ref, kn_ref, vc_ref, vp_ref, v0_ref, vn_ref, o_ref,
                 n_steps):
    i = pl.program_id(1)
    T = ATT_BLOCK
    hd = ATT_HEAD_DIM
    s_idx = lax.broadcasted_iota(jnp.int32, (T, 2 * T), 0)
    t_idx = lax.broadcasted_iota(jnp.int32, (T, 2 * T), 1) % T
    first = lax.broadcasted_iota(jnp.int32, (1, 2 * T), 1) < T

    k_own = k0_ref[0]
    v_own = v0_ref[0]
    k_blk = [kp_ref[0]] + [k_own[b * T:(b + 1) * T] for b in range(ATT_QB)] + [kn_ref[0]]
    v_blk = [vp_ref[0]] + [v_own[:, b * T:(b + 1) * T] for b in range(ATT_QB)] + [vn_ref[0]]
    ones_rows = jnp.ones((16, 3 * T + kc_ref.shape[1]), BF16)

    def window(qb):
        k_all = jnp.concatenate(k_blk[qb:qb + 3] + [kc_ref[0]], axis=0)
        vt_all = jnp.concatenate(v_blk[qb:qb + 3] + [vc_ref[0]], axis=1)
        ok_prev = (s_idx >= t_idx) & ((i > 0) if qb == 0 else True)
        ok_next = (s_idx <= t_idx) & ((i < n_steps - 1) if qb == ATT_QB - 1 else True)
        return k_all, vt_all, ok_prev, ok_next

    windows = [window(qb) for qb in range(ATT_QB)]
    per_qb = 2 * ATT_KV_HEADS

    def scores(n):
        qb, r = divmod(n, per_qb)
        kh, var = divmod(r, 2)
        q = q_ref[0, qb * T:(qb + 1) * T, :]
        q2 = jnp.concatenate([q[:, (2 * kh) * LANES:(2 * kh + 1) * LANES],
                              q[:, (2 * kh + 1) * LANES:(2 * kh + 2) * LANES]], axis=0)
        kk = windows[qb][0][:, (2 * kh + var) * LANES:(2 * kh + var + 1) * LANES]
        return _dot_nt(kk, q2)

    def finish(n, st):
        qb, r = divmod(n, per_qb)
        kh, var = divmod(r, 2)
        _, vt_all, ok_prev, ok_next = windows[qb]
        vt = vt_all[kh * hd:(kh + 1) * hd, :]
        st = jnp.concatenate([jnp.where(ok_prev, st[0:T], NEG_INF), st[T:2 * T],
                              jnp.where(ok_next, st[2 * T:3 * T], NEG_INF), st[3 * T:]], axis=0)
        h0 = ATT_GROUP * kh + var
        h1 = h0 + 2
        sink = jnp.where(first, sink_ref[h0], sink_ref[h1]) * LOG2E
        m = jnp.maximum(jnp.max(st, axis=0, keepdims=True), sink)
        p = jnp.exp2(st - m)
        ot = _dot(jnp.concatenate([vt, ones_rows], axis=0), p.astype(BF16))
        denom = ot[hd:hd + 1, :] + jnp.exp2(sink - m)
        ot = ot[0:hd, :] * (1.0 / denom)
        o_ref[0, h0 * hd:(h0 + 1) * hd, qb * T:(qb + 1) * T] = ot[:, 0:T].astype(BF16)
        o_ref[0, h1 * hd:(h1 + 1) * hd, qb * T:(qb + 1) * T] = ot[:, T:2 * T].astype(BF16)

    return ATT_QB * per_qb, scores, finish


def _attn_specs(L, C):
    T = ATT_BLOCK
    nb = L // T
    TQ = ATT_QB * T

    def edge(i, off):
        return jnp.clip(i * ATT_QB + (off if off < 0 else ATT_QB), 0, nb - 1)

    in_specs = [pl.BlockSpec(memory_space=pltpu.SMEM),
                pl.BlockSpec((1, TQ, ATT_Q_W), lambda b, i: (b, i, 0)),
                pl.BlockSpec((1, C, KX_W), lambda b, i: (b, 0, 0)),
                pl.BlockSpec((1, T, KX_W), lambda b, i: (b, edge(i, -1), 0)),
                pl.BlockSpec((1, TQ, KX_W), lambda b, i: (b, i, 0)),
                pl.BlockSpec((1, T, KX_W), lambda b, i: (b, edge(i, 1), 0)),
                pl.BlockSpec((1, ATT_KV_W, C), lambda b, i: (b, 0, 0)),
                pl.BlockSpec((1, ATT_KV_W, T), lambda b, i: (b, 0, edge(i, -1))),
                pl.BlockSpec((1, ATT_KV_W, TQ), lambda b, i: (b, 0, i)),
                pl.BlockSpec((1, ATT_KV_W, T), lambda b, i: (b, 0, edge(i, 1)))]
    return in_specs, pl.BlockSpec((1, ATT_Q_W, TQ), lambda b, i: (b, 0, i))


N_CHAIN = 2 * ML_HEADS


def _mlstm_load_state(c_ref, n_ref, m_ref):
    return [(c_ref[ci], n_ref[ci], m_ref[ci, 0:1, 0:1]) for ci in range(N_CHAIN)]


def _mlstm_phase1(dirs, state, item, with_h, cps):
    T = ML_CHUNK
    sub, rest = divmod(item, N_CHAIN)
    d, h = divmod(rest, ML_HEADS)
    qt_ref, k_ref, vt_ref, g_ref, bc_ref, br_ref, h_ref = dirs[d]
    row = lax.broadcasted_iota(jnp.int32, (T, T), 0)
    col = lax.broadcasted_iota(jnp.int32, (T, T), 1)
    mask = (col >= row) if d == 0 else (col <= row)
    last = T - 1 if d == 0 else 0
    sc = sub if d == 0 else cps - 1 - sub
    tok = slice(sc * T, (sc + 1) * T)
    ci = d * ML_HEADS + h
    gi = d * 2 * ML_HEADS + h
    fi = gi + ML_HEADS
    k = k_ref[0, tok, h * ML_QK_DIM:(h + 1) * ML_QK_DIM]
    vt = vt_ref[0, h * ML_V_DIM:(h + 1) * ML_V_DIM, tok]
    u_col = g_ref[0, tok, gi:gi + 1] - bc_ref[0, tok, fi:fi + 1]
    b_row = br_ref[0, fi:fi + 1, tok]
    ct_old, n_old, m_old = state[ci]
    qt = st = qn2 = None
    if with_h:
        qt = qt_ref[0, h * ML_QK_DIM:(h + 1) * ML_QK_DIM, tok]
        st = _dot(k, qt)
        top = lax.broadcasted_iota(jnp.int32, (8, ML_QK_DIM), 0) == 0
        n_hi = n_old.astype(BF16)
        n_lo = (n_old - n_hi.astype(F32)).astype(BF16)
        qn2 = _dot(jnp.where(top, n_hi, n_lo), qt)
    m_last = jnp.maximum(jnp.max(u_col, axis=0, keepdims=True), m_old)
    decay = jnp.exp(m_old - m_last)
    kw = (k.astype(F32) * jnp.exp(u_col - m_last)).astype(BF16)
    state[ci] = (decay * ct_old + _dot(vt, kw), decay * n_old + _dot(jnp.ones((8, T), BF16), kw),
                 b_row[:, last:last + 1] + m_last)
    return (h, h_ref, tok, mask, qt, vt, u_col, b_row, m_old, ct_old, st, qn2)


def _mlstm_phase2(chain):
    h, h_ref, tok, mask, qt, vt, u_col, b_row, m_old, ct_old, st, qn2 = chain
    umat = jnp.where(mask, u_col, -jnp.inf)
    m_row = jnp.maximum(jnp.max(umat, axis=0, keepdims=True), m_old)
    pt = st * jnp.exp(umat - m_row)
    w_int = jnp.exp(m_old - m_row)
    e_row = jnp.exp(-(b_row + m_row))
    nq = jnp.sum(pt, axis=0, keepdims=True) + w_int * (qn2[0:1, :] + qn2[1:2, :])
    den = jnp.maximum(jnp.abs(nq), e_row)
    lhs = jnp.concatenate([vt, ct_old.astype(BF16)], axis=1)
    rhs = jnp.concatenate([pt.astype(BF16), (qt.astype(F32) * w_int).astype(BF16)], axis=0)
    h_ref[0, h * ML_V_DIM:(h + 1) * ML_V_DIM, tok] = (_dot(lhs, rhs) * (1.0 / den)).astype(BF16)


def _mlstm_commit(state, c_ref, n_ref, m_ref):
    for ci, (c_new, n_new, m_new) in enumerate(state):
        c_ref[ci] = c_new
        n_ref[ci] = n_new
        m_ref[ci] = jnp.broadcast_to(m_new, (8, LANES))


def _mlstm_ctx_kernel(kf_ref, vtf_ref, gf_ref, bcf_ref, brf_ref, kb_ref, vtb_ref, gb_ref, bcb_ref, brb_ref,
                      c_ref, n_ref, m_ref, *, cps):
    @pl.when(pl.program_id(1) == 0)
    def _():
        c_ref[...] = jnp.zeros_like(c_ref)
        n_ref[...] = jnp.zeros_like(n_ref)
        m_ref[...] = jnp.zeros_like(m_ref)

    dirs = ((None, kf_ref, vtf_ref, gf_ref, bcf_ref, brf_ref, None),
            (None, kb_ref, vtb_ref, gb_ref, bcb_ref, brb_ref, None))
    state_refs = (c_ref.at[0], n_ref.at[0], m_ref.at[0])
    state = _mlstm_load_state(*state_refs)
    for item in range(cps * N_CHAIN):
        _mlstm_phase1(dirs, state, item, with_h=False, cps=cps)
    _mlstm_commit(state, *state_refs)


def _mixer_kernel(sink_ref, q_ref, kc_ref, kp_ref, k0_ref, kn_ref, vc_ref, vp_ref, v0_ref, vn_ref,
                  c0_ref, n0_ref, m0_ref, qtf_ref, kf_ref, vtf_ref, gf_ref, bcf_ref, brf_ref,
                  qtb_ref, kb_ref, vtb_ref, gb_ref, bcb_ref, brb_ref,
                  att_ref, hf_ref, hb_ref, c_ref, n_ref, m_ref, *, n_steps, cps):
    @pl.when(pl.program_id(1) == 0)
    def _():
        c_ref[...] = c0_ref[0]
        n_ref[...] = n0_ref[0]
        m_ref[...] = m0_ref[0]

    n_iter, scores, finish = _attn_stream(sink_ref, q_ref, kc_ref, kp_ref, k0_ref, kn_ref,
                                          vc_ref, vp_ref, v0_ref, vn_ref, att_ref, n_steps)
    pending = [scores(n) for n in range(ATT_AHEAD)]
    dirs = ((qtf_ref, kf_ref, vtf_ref, gf_ref, bcf_ref, brf_ref, hf_ref),
            (qtb_ref, kb_ref, vtb_ref, gb_ref, bcb_ref, brb_ref, hb_ref))
    state = _mlstm_load_state(c_ref, n_ref, m_ref)
    n_items = cps * N_CHAIN
    items = [_mlstm_phase1(dirs, state, it, with_h=True, cps=cps) for it in range(ML_AHEAD)]
    for n in range(max(n_iter, n_items)):
        if n < n_iter:
            st = pending.pop(0)
            if n + ATT_AHEAD < n_iter:
                pending.append(scores(n + ATT_AHEAD))
            finish(n, st)
        if n < n_items:
            if n + ML_AHEAD < n_items:
                items.append(_mlstm_phase1(dirs, state, n + ML_AHEAD, with_h=True, cps=cps))
            _mlstm_phase2(items.pop(0))
    _mlstm_commit(state, c_ref, n_ref, m_ref)


def _mlstm_specs(T, order, with_q):
    specs = [
        pl.BlockSpec((1, ML_QK_W, T), lambda b, j: (b, 0, order(j))),
        pl.BlockSpec((1, T, ML_QK_W), lambda b, j: (b, order(j), 0)),
        pl.BlockSpec((1, ML_V_W, T), lambda b, j: (b, 0, order(j))),
        pl.BlockSpec((1, T, ML_GATE_W), lambda b, j: (b, order(j), 0)),
        pl.BlockSpec((1, T, ML_GATE_W), lambda b, j: (b, order(j), 0)),
        pl.BlockSpec((1, ML_GATE_W, T), lambda b, j: (b, 0, order(j))),
    ]
    return specs if with_q else specs[1:]


_STATE_SHAPES = ((N_CHAIN, ML_V_DIM, ML_QK_DIM), (N_CHAIN, 8, ML_QK_DIM), (N_CHAIN, 8, LANES))


ML_AHEAD = 32
ML_CPS = 4
ML_CTX_CPS = 2


def _mlstm_ctx_call(mk, mvt, g, bc, br):
    B, C, _ = mk.shape
    T = ML_CTX_CPS * ML_CHUNK
    nc = C // T
    state_specs = [pl.BlockSpec((1,) + s, lambda b, j: (b, 0, 0, 0)) for s in _STATE_SHAPES]
    return pl.pallas_call(
        functools.partial(_mlstm_ctx_kernel, cps=ML_CTX_CPS),
        grid=(B, nc),
        in_specs=_mlstm_specs(T, lambda j: j, False) + _mlstm_specs(T, lambda j: nc - 1 - j, False),
        out_specs=state_specs,
        out_shape=[jax.ShapeDtypeStruct((B,) + s, F32) for s in _STATE_SHAPES],
        compiler_params=_cparams(("arbitrary", "arbitrary")),
        name="mlstm_ctx",
    )(mk, mvt, g, bc, br, mk, mvt, g, bc, br)


def _mixer_call(sink, q, kx, vt, kx_c, vt_c, state, mqt, mk, mvt, g, bc, br):
    B, L, _ = mk.shape
    C = kx_c.shape[1]
    T = TM_MIXER
    assert T == ATT_QB * ATT_BLOCK == ML_CPS * ML_CHUNK
    nc = L // T
    att_in, att_out = _attn_specs(L, C)
    state_specs = [pl.BlockSpec((1,) + s, lambda b, j: (b, 0, 0, 0)) for s in _STATE_SHAPES]
    out_specs = [att_out,
                 pl.BlockSpec((1, ML_V_W, T), lambda b, j: (b, 0, j)),
                 pl.BlockSpec((1, ML_V_W, T), lambda b, j: (b, 0, nc - 1 - j))]
    return pl.pallas_call(
        functools.partial(_mixer_kernel, n_steps=nc, cps=ML_CPS),
        grid=(B, nc),
        in_specs=(att_in + state_specs + _mlstm_specs(T, lambda j: j, True)
                  + _mlstm_specs(T, lambda j: nc - 1 - j, True)),
        out_specs=out_specs,
        out_shape=[jax.ShapeDtypeStruct((B, ATT_Q_W, L), BF16)] + [jax.ShapeDtypeStruct((B, ML_V_W, L), BF16)] * 2,
        scratch_shapes=[pltpu.VMEM(s, F32) for s in _STATE_SHAPES],
        compiler_params=_cparams(("arbitrary", "arbitrary")),
        name="mixer",
    )(sink, q, kx_c, kx, kx, kx, vt_c, vt, vt, vt, *state, mqt, mk, mvt, g, bc, br, mqt, mk, mvt, g, bc, br)


def _merge_kernel(att_ref, hf_ref, hb_ref, so_ref, sga_ref, sgm_ref, x_ref, g1_ref, mod2_ref, mlw_ref, n2w_ref,
                  wa_ref, wm_ref, wo_ref, xmid_ref, h2_ref, y_ref):
    s = pl.program_id(0)
    last = pl.num_programs(0) - 1
    cur = s % 2

    def body(do_branch, do_out):
        if do_out:
            y2 = _dot_tn(y_ref[1 - cur], wo_ref[...])
        if do_branch:
            ya = _dot_tn(wa_ref[...], att_ref[0])
            ht = hf_ref[0].astype(F32) + hb_ref[0].astype(F32)
            parts = []
            for h in range(ML_HEADS):
                seg = ht[h * ML_V_DIM:(h + 1) * ML_V_DIM, :]
                ms = jnp.mean(seg * seg, axis=0, keepdims=True)
                parts.append(seg * lax.rsqrt(ms + EPS))
            ml = (jnp.concatenate(parts, axis=0) * mlw_ref[...] * so_ref[0].astype(F32)).astype(BF16)
            ym = _dot_tn(wm_ref[...], ml)
        if do_out:
            xm = x_ref[0] + g1_ref[0, 0] * y2
            xmid_ref[0] = xm
            ms = jnp.mean(xm * xm, axis=-1, keepdims=True)
            h2 = xm * lax.rsqrt(ms + EPS) * n2w_ref[...]
            h2_ref[0] = (h2 * (1.0 + mod2_ref[1, 0]) + mod2_ref[0, 0]).astype(BF16)
        if do_branch:
            y_ref[cur] = (sga_ref[0].astype(F32) * ya + sgm_ref[0].astype(F32) * ym).astype(BF16)

    pl.when(s == 0)(lambda: body(True, False))
    pl.when((s > 0) & (s < last))(lambda: body(True, True))
    pl.when(s == last)(lambda: body(False, True))


def _merge_call(att_t, hf_t, hb_t, so_t, sga_t, sgm_t, x, mod6, mlw_b, n2w, wa, wm, wo, tm):
    B, L, _ = x.shape
    nt = L // tm
    n_all = B * nt

    def tile_in(s):
        t = jnp.minimum(s, n_all - 1)
        return t // nt, t % nt

    def tile_out(s):
        t = jnp.maximum(s - 1, 0)
        return t // nt, t % nt

    def in_t(h):
        return pl.BlockSpec((1, h, tm), lambda s: (tile_in(s)[0], 0, tile_in(s)[1]))

    def out_rows(w):
        return pl.BlockSpec((1, tm, w), lambda s: (*tile_out(s), 0))

    def const(shape):
        return pl.BlockSpec(shape, lambda s: (0,) * len(shape))

    return pl.pallas_call(
        _merge_kernel,
        grid=(n_all + 1,),
        in_specs=[in_t(D), in_t(D), in_t(D), in_t(D), in_t(D), in_t(D), out_rows(D),
                  _mod_spec(MOD_G1, 1, lambda s: tile_out(s)[0]),
                  _mod_spec(MOD_SH2, 2, lambda s: tile_out(s)[0]),
                  const((D, tm)), const((1, D)), const((D, D)), const((D, D)), const((D, D))],
        out_specs=[out_rows(D), out_rows(D)],
        out_shape=[jax.ShapeDtypeStruct((B, L, D), F32), jax.ShapeDtypeStruct((B, L, D), BF16)],
        scratch_shapes=[pltpu.VMEM((2, D, tm), BF16)],
        compiler_params=_cparams(("arbitrary",)),
        name="merge",
    )(att_t, hf_t, hb_t, so_t, sga_t, sgm_t, x, mod6, mod6, mlw_b, n2w, wa, wm, wo)


HALO = 16
FFN_AHEAD = 1


def _ffn_kernel(h_ref, hp_ref, hn_ref, xmid_ref, mod_ref, wup_ref, cw_ref, cb_ref, wdn_ref, o_ref,
                act_ref, *, n_tiles, tn, dn):
    s = pl.program_id(0)
    last = pl.num_programs(0) - 1

    @pl.when(s == 0)
    def _():
        _ffn_body(h_ref, hp_ref, hn_ref, xmid_ref, mod_ref, wup_ref, cw_ref, cb_ref, wdn_ref, o_ref, act_ref,
                  n_tiles=n_tiles, tn=tn, dn=dn, do_up=True, do_down=False)

    @pl.when((s > 0) & (s < last))
    def _():
        _ffn_body(h_ref, hp_ref, hn_ref, xmid_ref, mod_ref, wup_ref, cw_ref, cb_ref, wdn_ref, o_ref, act_ref,
                  n_tiles=n_tiles, tn=tn, dn=dn, do_up=True, do_down=True)

    @pl.when(s == last)
    def _():
        _ffn_body(h_ref, hp_ref, hn_ref, xmid_ref, mod_ref, wup_ref, cw_ref, cb_ref, wdn_ref, o_ref, act_ref,
                  n_tiles=n_tiles, tn=tn, dn=dn, do_up=False, do_down=True)


def _ffn_body(h_ref, hp_ref, hn_ref, xmid_ref, mod_ref, wup_ref, cw_ref, cb_ref, wdn_ref, o_ref, act_ref,
              *, n_tiles, tn, dn, do_up, do_down):
    s = pl.program_id(0)
    i = s % n_tiles
    cur = s % 2
    tm = h_ref.shape[1]
    n_chunks = D_FF // tn
    n_dn = D // dn
    act_prev = act_ref[1 - cur] if do_down else None

    def down(k):
        cols = slice(k * dn, (k + 1) * dn)
        o_ref[0, :, cols] = xmid_ref[0, :, cols] + mod_ref[0, 0, :, cols] * _dot(act_prev, wdn_ref[:, cols])

    if not do_up:
        for k in range(n_dn):
            down(k)
        return

    h = h_ref[0]
    prev_row = jnp.where(i > 0, hp_ref[0].astype(F32)[HALO - 1:HALO, :], 0.0)
    next_row = jnp.where(i < n_tiles - 1, hn_ref[0].astype(F32)[0:1, :], 0.0)
    top = lax.broadcasted_iota(jnp.int32, (16, D), 0) < 8
    edge = jnp.where(top, prev_row, next_row).astype(BF16)
    row8 = lax.broadcasted_iota(jnp.int32, (8, tn), 0)
    h_ext = jnp.concatenate([h, edge], axis=0)

    def up(c0):
        u_ext = _dot(h_ext, wup_ref[:, c0:c0 + tn])
        return u_ext[:tm], u_ext[tm:]

    def conv(u, ue, c0):
        below = pltpu.roll(u, 1, 0)
        above = pltpu.roll(u, tm - 1, 0)
        below = jnp.concatenate([jnp.where(row8 == 0, ue[0:8], below[0:8]), below[8:]], axis=0)
        above = jnp.concatenate([above[:tm - 8], jnp.where(row8 == 7, ue[8:16], above[tm - 8:])], axis=0)
        cw = cw_ref[:, c0:c0 + tn]
        return cb_ref[:, c0:c0 + tn] + below * cw[0:1] + u * cw[1:2] + above * cw[2:3]

    pending = [(up(c * tn), up(D_FF + c * tn)) for c in range(FFN_AHEAD)]
    done = 0
    for c in range(n_chunks):
        (ua, uae), (ug, uge) = pending.pop(0)
        if c + FFN_AHEAD < n_chunks:
            pending.append((up((c + FFN_AHEAD) * tn), up(D_FF + (c + FFN_AHEAD) * tn)))
        while do_down and done * n_chunks < (c + 1) * n_dn:
            down(done)
            done += 1
        a = conv(ua, uae, c * tn)
        hg = conv(ug, uge, D_FF + c * tn)
        act_ref[cur, :, c * tn:(c + 1) * tn] = ((hg + hg * jnp.tanh(hg)) * a).astype(BF16)


def _ffn_call(h2, xmid, g2, wup, cw, cb, wdn, tm, tn, dn):
    B, L, _ = xmid.shape
    nt = L // tm
    n_all = B * nt
    hb = tm // HALO
    nhb = L // HALO

    def tile_in(s):
        t = jnp.minimum(s, n_all - 1)
        return t // nt, t % nt

    def tile_out(s):
        t = jnp.maximum(s - 1, 0)
        return t // nt, t % nt

    def in_spec():
        return pl.BlockSpec((1, tm, D), lambda s: (*tile_in(s), 0))

    def out_spec():
        return pl.BlockSpec((1, tm, D), lambda s: (*tile_out(s), 0))

    def prev_halo(s):
        b, i = tile_in(s)
        return b, jnp.maximum(i * hb - 1, 0), 0

    def next_halo(s):
        b, i = tile_in(s)
        return b, jnp.minimum((i + 1) * hb, nhb - 1), 0

    def const(shape):
        return pl.BlockSpec(shape, lambda s: (0,) * len(shape))

    return pl.pallas_call(
        functools.partial(_ffn_kernel, n_tiles=nt, tn=tn, dn=dn),
        grid=(n_all + 1,),
        in_specs=[in_spec(),
                  pl.BlockSpec((1, HALO, D), prev_halo),
                  pl.BlockSpec((1, HALO, D), next_halo),
                  out_spec(),
                  _mod_spec(MOD_G2, 1, lambda s: tile_out(s)[0]),
                  const(wup.shape), const(cw.shape), const(cb.shape), const(wdn.shape)],
        out_specs=out_spec(),
        out_shape=jax.ShapeDtypeStruct((B, L, D), F32),
        scratch_shapes=[pltpu.VMEM((2, tm, D_FF), BF16)],
        compiler_params=_cparams(("arbitrary",)),
        name="ffn",
    )(h2, h2, h2, xmid, g2, wup, cw, cb, wdn)


def _pair_perm(n_heads):
    half = ATT_HEAD_DIM // 2
    idx = []
    for p in range(n_heads // 2):
        for sub in range(4):
            head = 2 * p + (sub % 2)
            d0 = (sub // 2) * half
            idx.extend(head * ATT_HEAD_DIM + d0 + e for e in range(half))
    return np.asarray(idx, np.int32)


def _rope_tables(L):
    f32 = np.float32
    rows = L // GRID_W
    row = np.repeat(np.arange(rows, dtype=f32), GRID_W)
    col = np.tile(np.arange(GRID_W, dtype=f32), rows)
    n_freq = ATT_HEAD_DIM // 4
    inv_freq = (f32(ROPE_BASE) ** (-np.arange(n_freq, dtype=f32) / f32(n_freq))).astype(f32)
    ang = np.concatenate([row[:, None] * inv_freq, col[:, None] * inv_freq], axis=-1).astype(f32)
    cos = np.tile(np.cos(ang).astype(f32), (1, 4))
    sin = np.tile(np.sin(ang).astype(f32), (1, 4))
    sign = np.where(np.arange(LANES) < LANES // 2, -1.0, 1.0).astype(f32)
    return jnp.asarray(cos), jnp.asarray(sin * sign)


def kernel(x, c, ctx, c_ctx, w_mod, b_mod, norm1_w, w_in, q_norm_w, k_norm_w, attn_sink, ml_gate_b, ml_norm_w,
           w_branch_att, w_branch_ml, w_out, norm2_w, w_up, conv_w, conv_b, w_down):
    B, L, _ = x.shape
    C = ctx.shape[1]
    assert L % GRID_W == 0 and C % TM_CTX == 0
    assert all(L % t == 0 for t in (TM_INPROJ, TM_MIXER, TM_MERGE, TM_FFN))
    l = 0

    n_rows = -(-(B + 1) // 16) * 16
    cc = jnp.concatenate([c, c_ctx[None, :], jnp.zeros((n_rows - B - 1, D), F32)], axis=0)
    mod6 = _mod_call(cc, w_mod[l], b_mod[l][None, :])

    w = w_in[l]
    qperm = _pair_perm(ATT_HEADS)
    kperm = _pair_perm(ATT_KV_HEADS)
    def pair_cols(wc, n_heads):
        half = ATT_HEAD_DIM // 2
        wc = wc.reshape(D, n_heads // 2, 2, 2, half).transpose(0, 1, 3, 2, 4)
        return wc.reshape(D, n_heads * ATT_HEAD_DIM)

    w_q = pair_cols(w[:, _O_AQ:_O_AQ + ATT_Q_W], ATT_HEADS)
    w_k = pair_cols(w[:, _O_AK:_O_AK + ATT_KV_W], ATT_KV_HEADS)
    w_g = jnp.pad(w[:, _O_MG:_O_MG + ML_GATE_W], ((0, 0), (0, LANES - ML_GATE_W)))
    w_mk = w[:, _O_MK:_O_MK + ML_QK_W] * (ML_QK_DIM ** -0.5)
    w_p = jnp.concatenate([w_q, w_k, w_mk, w_g], axis=1).astype(BF16)
    w_t = jnp.concatenate([w[:, _O_AV:_O_AV + ATT_KV_W], w[:, _O_MQ:_O_MQ + ML_QK_W], w[:, _O_MV:_O_MV + ML_V_W],
                           w[:, _O_MO:_O_MO + ML_V_W], w[:, _O_GA:_O_GA + D], w[:, _O_GM:_O_GM + D]],
                          axis=1).T.astype(BF16)
    wgt = w[:, _O_MG:_O_MG + ML_GATE_W].T.astype(BF16)

    head_of_col = np.concatenate([qperm // ATT_HEAD_DIM, ATT_HEADS + kperm // ATT_HEAD_DIM])
    e_np = (head_of_col[:, None] == np.arange(LANES)[None, :]).astype(np.float32)
    e_mat = jnp.asarray(e_np, BF16)
    et_mat = jnp.asarray(np.concatenate([e_np.T, e_np.T], axis=0), BF16)
    def pair_tiled(wn, n_heads):
        half = ATT_HEAD_DIM // 2
        return jnp.tile(jnp.concatenate([wn[:half], wn[:half], wn[half:], wn[half:]]), n_heads // 2)

    qkw = jnp.concatenate([pair_tiled(q_norm_w[l], ATT_HEADS) * (ATT_SCALE * LOG2E),
                           pair_tiled(k_norm_w[l], ATT_KV_HEADS)])[None, :]
    cos_t, sin_t = _rope_tables(L)
    gb = ml_gate_b[l].reshape(1, ML_GATE_W)
    gbt = ml_gate_b[l].reshape(ML_GATE_W, 1)
    n1w = norm1_w[l][None, :]

    kx_c, mk_c, g_c, bc_c, vt_c, mvt_c, br_c = _ctxproj_call(
        ctx, mod6, B, n1w, w_p, w_t, wgt, e_mat, et_mat, qkw, gb, gbt, tm=TM_CTX)
    q, kx, mk, g, bc, vt, mqt, mvt, sot, sgat, sgmt, br = _inproj_call(
        x, mod6, n1w, w_p, w_t, wgt, e_mat, et_mat, qkw, cos_t, sin_t, gb, gbt, tm=TM_INPROJ)

    state = _mlstm_ctx_call(mk_c, mvt_c, g_c, bc_c, br_c)
    att_t, hf_t, hb_t = _mixer_call(attn_sink[l], q, kx, vt, kx_c, vt_c, state, mqt, mk, mvt, g, bc, br)

    mlw_b = jnp.broadcast_to(ml_norm_w[l][:, None], (ML_V_W, TM_MERGE))
    xmid, h2 = _merge_call(att_t, hf_t, hb_t, sot, sgat, sgmt, x, mod6, mlw_b, norm2_w[l][None, :],
                           w_branch_att[l].astype(BF16), w_branch_ml[l].astype(BF16), w_out[l].astype(BF16),
                           tm=TM_MERGE)
    gate_half = jnp.where(jnp.arange(2 * D_FF) < D_FF, 1.0, 0.5).astype(F32)
    out = _ffn_call(h2, xmid, mod6, w_up[l].astype(BF16), conv_w[l] * gate_half, (conv_b[l] * gate_half)[None, :],
                    w_down[l].astype(BF16), tm=TM_FFN, tn=FFN_UP_COLS, dn=FFN_DOWN_COLS)
    return out
```

```python
import functools

import jax
import jax.numpy as jnp
import numpy as np
from jax import lax
from jax.experimental import pallas as pl
from jax.experimental.pallas import tpu as pltpu

D = 1024
GRID_W = 64
ATT_HEADS = 16
ATT_KV_HEADS = 4
ATT_HEAD_DIM = 64
ATT_GROUP = ATT_HEADS // ATT_KV_HEADS
ATT_BLOCK = 128
WINDOW = 128
ROPE_BASE = 10000.0
ATT_SCALE = ATT_HEAD_DIM ** -0.5
LOG2E = 1.4426950408889634
ML_HEADS = 4
ML_QK_DIM = 128
ML_V_DIM = 256
ML_CHUNK = 128
D_FF = 2816
EPS = 1e-6
NEG_INF = -1e30

ATT_Q_W = ATT_HEADS * ATT_HEAD_DIM
ATT_KV_W = ATT_KV_HEADS * ATT_HEAD_DIM
ML_QK_W = ML_HEADS * ML_QK_DIM
ML_V_W = ML_HEADS * ML_V_DIM
ML_GATE_W = 2 * 2 * ML_HEADS

LANES = 128
KX_W = ATT_KV_HEADS * 2 * LANES
VMEM_LIMIT = 56 * 1024 * 1024

TM_CTX = 256
TM_INPROJ = 512
TM_MIXER = 512
TM_MERGE = 512
TM_FFN = 512
FFN_UP_COLS = 256
FFN_DOWN_COLS = 256

BF16 = jnp.bfloat16
F32 = jnp.float32

_O_AQ = 0
_O_AK = _O_AQ + ATT_Q_W
_O_AV = _O_AK + ATT_KV_W
_O_MQ = _O_AV + ATT_KV_W
_O_MK = _O_MQ + ML_QK_W
_O_MV = _O_MK + ML_QK_W
_O_MO = _O_MV + ML_V_W
_O_MG = _O_MO + ML_V_W
_O_GA = _O_MG + ML_GATE_W
_O_GM = _O_GA + D

QK_W = ATT_Q_W + ATT_KV_W
_P_QK = 0
_P_MK = _P_QK + QK_W
_P_MG = _P_MK + ML_QK_W
_P_END = _P_MG + LANES
_R_V = 0
_R_MQ = _O_MQ - _O_AV
_R_MV = _O_MV - _O_AV
_R_MO = _O_MO - _O_AV
_R_MG = _O_MG - _O_AV
_R_GA = _O_GA - _O_AV
_R_GM = _O_GM - _O_AV


def _dot(a, b):
    return jnp.dot(a, b, preferred_element_type=F32)


def _dot_nt(a, b):
    return lax.dot_general(a, b, (((1,), (1,)), ((), ())), preferred_element_type=F32)


def _dot_tn(a, b):
    return lax.dot_general(a, b, (((0,), (0,)), ((), ())), preferred_element_type=F32)


def _cparams(sem):
    return pltpu.CompilerParams(dimension_semantics=sem, vmem_limit_bytes=VMEM_LIMIT)


def _mod_kernel(c_ref, w_ref, b_ref, o_ref):
    c = c_ref[...]
    a = c * jax.nn.sigmoid(c)
    a_hi = a.astype(BF16)
    a_lo = (a - a_hi.astype(F32)).astype(BF16)
    w = w_ref[...]
    w_hi = w.astype(BF16)
    w_lo = (w - w_hi.astype(F32)).astype(BF16)
    rows = a.shape[0]
    both = _dot(jnp.concatenate([a_hi, a_lo], axis=0), w_hi)
    o_ref[0, :, 0, :] = both[:rows] + both[rows:] + _dot(a_hi, w_lo) + b_ref[...]


MOD_SH1, MOD_SC1, MOD_SH2, MOD_SC2, MOD_G1, MOD_G2 = range(6)


def _mod_call(cc, w_mod, b_mod):
    rows = cc.shape[0]
    n_seg = w_mod.shape[1] // D
    assert n_seg == 6

    def out_pos(j):
        return jnp.where(j == 2, MOD_G1, jnp.where((j == 3) | (j == 4), j - 1, j))

    return pl.pallas_call(
        _mod_kernel,
        grid=(n_seg,),
        in_specs=[pl.BlockSpec((rows, D), lambda j: (0, 0)),
                  pl.BlockSpec((D, D), lambda j: (0, j)),
                  pl.BlockSpec((1, D), lambda j: (0, j))],
        out_specs=pl.BlockSpec((1, rows, 1, D), lambda j: (out_pos(j), 0, 0, 0)),
        out_shape=jax.ShapeDtypeStruct((n_seg, rows, 1, D), F32),
        compiler_params=_cparams(("arbitrary",)),
        name="mod",
    )(cc, w_mod, b_mod)


def _mod_spec(seg, n_seg, row_of):
    return pl.BlockSpec((n_seg, 1, 1, D), lambda *ids: (seg // n_seg, row_of(*ids), 0, 0))


def _split2(x):
    x1 = x.astype(BF16)
    x2 = (x - x1.astype(F32)).astype(BF16)
    return x1, x2


def _norm_modulate(x, n1w_ref, mod_ref):
    ms = jnp.mean(x * x, axis=-1, keepdims=True)
    y = x * lax.rsqrt(ms + EPS) * n1w_ref[...]
    return (y * (1.0 + mod_ref[1, 0]) + mod_ref[0, 0]).astype(BF16)


def _sigmoid(x):
    return 0.5 * jnp.tanh(0.5 * x) + 0.5


def _head_rms_scale(ss):
    r = lax.rsqrt(ss * (1.0 / ATT_HEAD_DIM) + EPS)
    r_hi = r.astype(BF16)
    r_lo = (r - r_hi.astype(F32)).astype(BF16)
    return jnp.concatenate([r_hi, r_lo], axis=1)


def _store_k_variants(k_ref, pair, o):
    lane = lax.broadcasted_iota(jnp.int32, (1, LANES), 1)
    keep = ((lane // 32) % 2) == 0
    c0 = 4 * pair * LANES
    k_ref[0, :, c0:c0 + LANES] = jnp.where(keep, o, 0.0).astype(BF16)
    k_ref[0, :, c0 + LANES:c0 + 2 * LANES] = jnp.where(keep, 0.0, pltpu.roll(o, 32, 1)).astype(BF16)
    k_ref[0, :, c0 + 2 * LANES:c0 + 3 * LANES] = jnp.where(keep, pltpu.roll(o, 96, 1), 0.0).astype(BF16)
    k_ref[0, :, c0 + 3 * LANES:c0 + 4 * LANES] = jnp.where(keep, 0.0, o).astype(BF16)


def _cum_gates_cols(g16, tri_lo, tri_up):
    lf = jax.nn.log_sigmoid(g16)
    fwd_col = lax.broadcasted_iota(jnp.int32, (1, ML_GATE_W), 1) < ML_GATE_W // 2
    out = []
    for c in range(g16.shape[0] // ML_CHUNK):
        parts = _split2(lf[c * ML_CHUNK:(c + 1) * ML_CHUNK])
        out.append(jnp.where(fwd_col, sum(_dot(tri_lo, p) for p in parts), sum(_dot(tri_up, p) for p in parts)))
    return jnp.concatenate(out, axis=0)


def _cum_gates_rows(gt16, tri_lo, tri_up):
    lf = jax.nn.log_sigmoid(gt16)
    fwd_row = lax.broadcasted_iota(jnp.int32, (ML_GATE_W, 1), 0) < ML_GATE_W // 2
    out = []
    for c in range(gt16.shape[1] // ML_CHUNK):
        parts = _split2(lf[:, c * ML_CHUNK:(c + 1) * ML_CHUNK])
        out.append(jnp.where(fwd_row, sum(_dot(p, tri_up) for p in parts), sum(_dot(p, tri_lo) for p in parts)))
    return jnp.concatenate(out, axis=1)


def _inproj_kernel(x_ref, mod_ref, n1w_ref, w_ref, wt_ref, e_ref, et_ref, qkw_ref,
                   cos_ref, sin_ref, gb_ref, gbt_ref, tril_ref, triu_ref,
                   q_ref, k_ref, mk_ref, g_ref, bc_ref, vt_ref, mqt_ref, mvt_ref, sot_ref, sgat_ref, sgmt_ref, br_ref,
                   hn_ref):
    hn_ref[...] = _norm_modulate(x_ref[0], n1w_ref, mod_ref)
    hn = hn_ref[...]

    def ft(r0, height):
        return _dot_nt(wt_ref[r0:r0 + height, :], hn)

    acc = _dot(hn, w_ref[:, _P_QK:_P_QK + QK_W])
    g16 = _dot(hn, w_ref[:, _P_MG:_P_MG + LANES])[:, :ML_GATE_W] + gb_ref[...]
    gt16 = ft(_R_MG, ML_GATE_W) + gbt_ref[...]
    g_ref[0] = g16
    mk_ref[0] = _dot(hn, w_ref[:, _P_MK:_P_MK + ML_QK_W]).astype(BF16)
    ss = _dot((acc * acc).astype(BF16), e_ref[...])
    vt_ref[0] = ft(_R_V, ATT_KV_W).astype(BF16)
    sot_ref[0] = _sigmoid(ft(_R_MO, ML_V_W)).astype(BF16)
    rb = _dot(_head_rms_scale(ss), et_ref[...])
    sgat_ref[0] = _sigmoid(ft(_R_GA, D)).astype(BF16)
    bc_ref[0] = _cum_gates_cols(g16, tril_ref[...], triu_ref[...])

    qn = acc * rb * qkw_ref[...]
    cos = cos_ref[...]
    sin = sin_ref[...]
    for gi in range(QK_W // LANES):
        xs = qn[:, gi * LANES:(gi + 1) * LANES]
        o = xs * cos + pltpu.roll(xs, LANES // 2, 1) * sin
        if gi < ATT_Q_W // LANES:
            q_ref[0, :, gi * LANES:(gi + 1) * LANES] = o.astype(BF16)
        else:
            _store_k_variants(k_ref, gi - ATT_Q_W // LANES, o)

    sgmt_ref[0] = _sigmoid(ft(_R_GM, D)).astype(BF16)
    br_ref[0] = _cum_gates_rows(gt16, tril_ref[...], triu_ref[...])
    mqt_ref[0] = ft(_R_MQ, ML_QK_W).astype(BF16)
    mvt_ref[0] = ft(_R_MV, ML_V_W).astype(BF16)


def _ctxproj_kernel(x_ref, mod_ref, n1w_ref, w_ref, wt_ref, e_ref, et_ref, qkw_ref,
                    gb_ref, gbt_ref, tril_ref, triu_ref,
                    k_ref, mk_ref, g_ref, bc_ref, vt_ref, mvt_ref, br_ref):
    hn = _norm_modulate(x_ref[0], n1w_ref, mod_ref)
    acc = _dot(hn, w_ref[:, _P_QK + ATT_Q_W:_P_QK + QK_W])
    g16 = _dot(hn, w_ref[:, _P_MG:_P_MG + LANES])[:, :ML_GATE_W] + gb_ref[...]
    gt16 = _dot_nt(wt_ref[_R_MG:_R_MG + ML_GATE_W, :], hn) + gbt_ref[...]
    g_ref[0] = g16
    mk_ref[0] = _dot(hn, w_ref[:, _P_MK:_P_MK + ML_QK_W]).astype(BF16)
    ss = _dot((acc * acc).astype(BF16), e_ref[ATT_Q_W:QK_W, :])
    vt_ref[0] = _dot_nt(wt_ref[_R_V:_R_V + ATT_KV_W, :], hn).astype(BF16)
    rb = _dot(_head_rms_scale(ss), et_ref[:, ATT_Q_W:QK_W])
    mvt_ref[0] = _dot_nt(wt_ref[_R_MV:_R_MV + ML_V_W, :], hn).astype(BF16)
    bc_ref[0] = _cum_gates_cols(g16, tril_ref[...], triu_ref[...])
    kn = acc * rb * qkw_ref[:, ATT_Q_W:QK_W]
    for pair in range(ATT_KV_W // LANES):
        _store_k_variants(k_ref, pair, kn[:, pair * LANES:(pair + 1) * LANES])
    br_ref[0] = _cum_gates_rows(gt16, tril_ref[...], triu_ref[...])


def _chunk_tri(lower):
    r = np.arange(ML_CHUNK)[:, None]
    c = np.arange(ML_CHUNK)[None, :]
    return jnp.asarray((c <= r) if lower else (c >= r), BF16)


def _const_spec(shape):
    return pl.BlockSpec(shape, lambda b, i: (0,) * len(shape))


def _inproj_call(x, mod6, n1w, w_p, w_t, e_mat, et_mat, qkw, cos_t, sin_t, gb, gbt, tm):
    B, L, _ = x.shape
    tril = _chunk_tri(True)
    triu = _chunk_tri(False)

    def rows(w):
        return pl.BlockSpec((1, tm, w), lambda b, i: (b, i, 0))

    def cols(h):
        return pl.BlockSpec((1, h, tm), lambda b, i: (b, 0, i))

    consts = [n1w, w_p, w_t, e_mat, et_mat, qkw]
    tail = [gb, gbt, tril, triu]
    in_specs = ([rows(D), _mod_spec(MOD_SH1, 2, lambda b, i: b)]
                + [_const_spec(a.shape) for a in consts]
                + [pl.BlockSpec((tm, LANES), lambda b, i: (i, 0))] * 2
                + [_const_spec(a.shape) for a in tail])
    out_specs = [rows(ATT_Q_W), rows(KX_W), rows(ML_QK_W), rows(ML_GATE_W), rows(ML_GATE_W),
                 cols(ATT_KV_W), cols(ML_QK_W), cols(ML_V_W), cols(ML_V_W), cols(D), cols(D), cols(ML_GATE_W)]
    out_shape = [
        jax.ShapeDtypeStruct((B, L, ATT_Q_W), BF16),
        jax.ShapeDtypeStruct((B, L, KX_W), BF16),
        jax.ShapeDtypeStruct((B, L, ML_QK_W), BF16),
        jax.ShapeDtypeStruct((B, L, ML_GATE_W), F32),
        jax.ShapeDtypeStruct((B, L, ML_GATE_W), F32),
        jax.ShapeDtypeStruct((B, ATT_KV_W, L), BF16),
        jax.ShapeDtypeStruct((B, ML_QK_W, L), BF16),
        jax.ShapeDtypeStruct((B, ML_V_W, L), BF16),
        jax.ShapeDtypeStruct((B, ML_V_W, L), BF16),
        jax.ShapeDtypeStruct((B, D, L), BF16),
        jax.ShapeDtypeStruct((B, D, L), BF16),
        jax.ShapeDtypeStruct((B, ML_GATE_W, L), F32),
    ]
    return pl.pallas_call(
        _inproj_kernel,
        grid=(B, L // tm),
        in_specs=in_specs,
        out_specs=out_specs,
        out_shape=out_shape,
        scratch_shapes=[pltpu.VMEM((tm, D), BF16)],
        compiler_params=_cparams(("arbitrary", "arbitrary")),
        name="inproj",
    )(x, mod6, *consts, cos_t, sin_t, *tail)


def _ctxproj_call(ctx, mod6, ctx_row, n1w, w_p, w_t, e_mat, et_mat, qkw, gb, gbt, tm):
    B, C, _ = ctx.shape
    tril = _chunk_tri(True)
    triu = _chunk_tri(False)

    def rows(w):
        return pl.BlockSpec((1, tm, w), lambda b, i: (b, i, 0))

    def cols(h):
        return pl.BlockSpec((1, h, tm), lambda b, i: (b, 0, i))

    consts = [n1w, w_p, w_t, e_mat, et_mat, qkw, gb, gbt, tril, triu]
    out_specs = [rows(KX_W), rows(ML_QK_W), rows(ML_GATE_W), rows(ML_GATE_W),
                 cols(ATT_KV_W), cols(ML_V_W), cols(ML_GATE_W)]
    out_shape = [
        jax.ShapeDtypeStruct((B, C, KX_W), BF16),
        jax.ShapeDtypeStruct((B, C, ML_QK_W), BF16),
        jax.ShapeDtypeStruct((B, C, ML_GATE_W), F32),
        jax.ShapeDtypeStruct((B, C, ML_GATE_W), F32),
        jax.ShapeDtypeStruct((B, ATT_KV_W, C), BF16),
        jax.ShapeDtypeStruct((B, ML_V_W, C), BF16),
        jax.ShapeDtypeStruct((B, ML_GATE_W, C), F32),
    ]
    return pl.pallas_call(
        _ctxproj_kernel,
        grid=(B, C // tm),
        in_specs=([rows(D), _mod_spec(MOD_SH1, 2, lambda b, i: ctx_row)]
                  + [_const_spec(a.shape) for a in consts]),
        out_specs=out_specs,
        out_shape=out_shape,
        compiler_params=_cparams(("arbitrary", "arbitrary")),
        name="ctxproj",
    )(ctx, mod6, *consts)


ATT_AHEAD = 12

ATT_QB = 4


def _attn_stream(sink_ref, q_ref, kc_ref, kp_ref, k0_ref, kn_ref, vc_ref, vp_ref, v0_ref, vn_ref, o_ref,
                 n_steps):
    i = pl.program_id(1)
    T = ATT_BLOCK
    hd = ATT_HEAD_DIM
    s_idx = lax.broadcasted_iota(jnp.int32, (T, 2 * T), 0)
    t_idx = lax.broadcasted_iota(jnp.int32, (T, 2 * T), 1) % T
    first = lax.broadcasted_iota(jnp.int32, (1, 2 * T), 1) < T

    k_own = k0_ref[0]
    v_own = v0_ref[0]
    k_blk = [kp_ref[0]] + [k_own[b * T:(b + 1) * T] for b in range(ATT_QB)] + [kn_ref[0]]
    v_blk = [vp_ref[0]] + [v_own[:, b * T:(b + 1) * T] for b in range(ATT_QB)] + [vn_ref[0]]
    ones_rows = jnp.ones((16, 3 * T + kc_ref.shape[1]), BF16)

    def window(qb):
        k_all = jnp.concatenate(k_blk[qb:qb + 3] + [kc_ref[0]], axis=0)
        vt_all = jnp.concatenate(v_blk[qb:qb + 3] + [vc_ref[0]], axis=1)
        ok_prev = (s_idx >= t_idx) & ((i > 0) if qb == 0 else True)
        ok_next = (s_idx <= t_idx) & ((i < n_steps - 1) if qb == ATT_QB - 1 else True)
        return k_all, vt_all, ok_prev, ok_next

    windows = [window(qb) for qb in range(ATT_QB)]
    per_qb = 2 * ATT_KV_HEADS

    def scores(n):
        qb, r = divmod(n, per_qb)
        kh, var = divmod(r, 2)
        q = q_ref[0, qb * T:(qb + 1) * T, :]
        q2 = jnp.concatenate([q[:, (2 * kh) * LANES:(2 * kh + 1) * LANES],
                              q[:, (2 * kh + 1) * LANES:(2 * kh + 2) * LANES]], axis=0)
        kk = windows[qb][0][:, (2 * kh + var) * LANES:(2 * kh + var + 1) * LANES]
        return _dot_nt(kk, q2)

    def finish(n, st):
        qb, r = divmod(n, per_qb)
        kh, var = divmod(r, 2)
        _, vt_all, ok_prev, ok_next = windows[qb]
        vt = vt_all[kh * hd:(kh + 1) * hd, :]
        st = jnp.concatenate([jnp.where(ok_prev, st[0:T], NEG_INF), st[T:2 * T],
                              jnp.where(ok_next, st[2 * T:3 * T], NEG_INF), st[3 * T:]], axis=0)
        h0 = ATT_GROUP * kh + var
        h1 = h0 + 2
        sink = jnp.where(first, sink_ref[h0], sink_ref[h1]) * LOG2E
        m = jnp.maximum(jnp.max(st, axis=0, keepdims=True), sink)
        p = jnp.exp2(st - m)
        ot = _dot(jnp.concatenate([vt, ones_rows], axis=0), p.astype(BF16))
        denom = ot[hd:hd + 1, :] + jnp.exp2(sink - m)
        ot = ot[0:hd, :] * (1.0 / denom)
        o_ref[0, h0 * hd:(h0 + 1) * hd, qb * T:(qb + 1) * T] = ot[:, 0:T].astype(BF16)
        o_ref[0, h1 * hd:(h1 + 1) * hd, qb * T:(qb + 1) * T] = ot[:, T:2 * T].astype(BF16)

    return ATT_QB * per_qb, scores, finish


def _attn_specs(L, C):
    T = ATT_BLOCK
    nb = L // T
    TQ = ATT_QB * T

    def edge(i, off):
        return jnp.clip(i * ATT_QB + (off if off < 0 else ATT_QB), 0, nb - 1)

    in_specs = [pl.BlockSpec(memory_space=pltpu.SMEM),
                pl.BlockSpec((1, TQ, ATT_Q_W), lambda b, i: (b, i, 0)),
                pl.BlockSpec((1, C, KX_W), lambda b, i: (b, 0, 0)),
                pl.BlockSpec((1, T, KX_W), lambda b, i: (b, edge(i, -1), 0)),
                pl.BlockSpec((1, TQ, KX_W), lambda b, i: (b, i, 0)),
                pl.BlockSpec((1, T, KX_W), lambda b, i: (b, edge(i, 1), 0)),
                pl.BlockSpec((1, ATT_KV_W, C), lambda b, i: (b, 0, 0)),
                pl.BlockSpec((1, ATT_KV_W, T), lambda b, i: (b, 0, edge(i, -1))),
                pl.BlockSpec((1, ATT_KV_W, TQ), lambda b, i: (b, 0, i)),
                pl.BlockSpec((1, ATT_KV_W, T), lambda b, i: (b, 0, edge(i, 1)))]
    return in_specs, pl.BlockSpec((1, ATT_Q_W, TQ), lambda b, i: (b, 0, i))


N_CHAIN = 2 * ML_HEADS


def _mlstm_load_state(c_ref, n_ref, m_ref):
    return [(c_ref[ci], n_ref[ci], m_ref[ci, 0:1, 0:1]) for ci in range(N_CHAIN)]


def _mlstm_phase1(dirs, state, item, with_h, cps):
    T = ML_CHUNK
    sub, rest = divmod(item, N_CHAIN)
    d, h = divmod(rest, ML_HEADS)
    qt_ref, k_ref, vt_ref, g_ref, bc_ref, br_ref, h_ref = dirs[d]
    row = lax.broadcasted_iota(jnp.int32, (T, T), 0)
    col = lax.broadcasted_iota(jnp.int32, (T, T), 1)
    mask = (col >= row) if d == 0 else (col <= row)
    last = T - 1 if d == 0 else 0
    sc = sub if d == 0 else cps - 1 - sub
    tok = slice(sc * T, (sc + 1) * T)
    ci = d * ML_HEADS + h
    gi = d * 2 * ML_HEADS + h
    fi = gi + ML_HEADS
    k = k_ref[0, tok, h * ML_QK_DIM:(h + 1) * ML_QK_DIM]
    vt = vt_ref[0, h * ML_V_DIM:(h + 1) * ML_V_DIM, tok]
    u_col = g_ref[0, tok, gi:gi + 1] - bc_ref[0, tok, fi:fi + 1]
    b_row = br_ref[0, fi:fi + 1, tok]
    ct_old, n_old, m_old = state[ci]
    qt = st = qn2 = None
    if with_h:
        qt = qt_ref[0, h * ML_QK_DIM:(h + 1) * ML_QK_DIM, tok]
        st = _dot(k, qt)
        top = lax.broadcasted_iota(jnp.int32, (8, ML_QK_DIM), 0) == 0
        n_hi = n_old.astype(BF16)
        n_lo = (n_old - n_hi.astype(F32)).astype(BF16)
        qn2 = _dot(jnp.where(top, n_hi, n_lo), qt)
    m_last = jnp.maximum(jnp.max(u_col, axis=0, keepdims=True), m_old)
    decay = jnp.exp(m_old - m_last)
    kw = (k.astype(F32) * jnp.exp(u_col - m_last)).astype(BF16)
    state[ci] = (decay * ct_old + _dot(vt, kw), decay * n_old + _dot(jnp.ones((8, T), BF16), kw),
                 b_row[:, last:last + 1] + m_last)
    return (h, h_ref, tok, mask, qt, vt, u_col, b_row, m_old, ct_old, st, qn2)


def _mlstm_phase2(chain):
    h, h_ref, tok, mask, qt, vt, u_col, b_row, m_old, ct_old, st, qn2 = chain
    umat = jnp.where(mask, u_col, -jnp.inf)
    m_row = jnp.maximum(jnp.max(umat, axis=0, keepdims=True), m_old)
    pt = st * jnp.exp(umat - m_row)
    w_int = jnp.exp(m_old - m_row)
    e_row = jnp.exp(-(b_row + m_row))
    nq = jnp.sum(pt, axis=0, keepdims=True) + w_int * (qn2[0:1, :] + qn2[1:2, :])
    den = jnp.maximum(jnp.abs(nq), e_row)
    lhs = jnp.concatenate([vt, ct_old.astype(BF16)], axis=1)
    rhs = jnp.concatenate([pt.astype(BF16), (qt.astype(F32) * w_int).astype(BF16)], axis=0)
    h_ref[0, h * ML_V_DIM:(h + 1) * ML_V_DIM, tok] = (_dot(lhs, rhs) * (1.0 / den)).astype(BF16)


def _mlstm_commit(state, c_ref, n_ref, m_ref):
    for ci, (c_new, n_new, m_new) in enumerate(state):
        c_ref[ci] = c_new
        n_ref[ci] = n_new
        m_ref[ci] = jnp.broadcast_to(m_new, (8, LANES))


def _mlstm_ctx_kernel(kf_ref, vtf_ref, gf_ref, bcf_ref, brf_ref, kb_ref, vtb_ref, gb_ref, bcb_ref, brb_ref,
                      c_ref, n_ref, m_ref, *, cps):
    @pl.when(pl.program_id(1) == 0)
    def _():
        c_ref[...] = jnp.zeros_like(c_ref)
        n_ref[...] = jnp.zeros_like(n_ref)
        m_ref[...] = jnp.zeros_like(m_ref)

    dirs = ((None, kf_ref, vtf_ref, gf_ref, bcf_ref, brf_ref, None),
            (None, kb_ref, vtb_ref, gb_ref, bcb_ref, brb_ref, None))
    state_refs = (c_ref.at[0], n_ref.at[0], m_ref.at[0])
    state = _mlstm_load_state(*state_refs)
    for item in range(cps * N_CHAIN):
        _mlstm_phase1(dirs, state, item, with_h=False, cps=cps)
    _mlstm_commit(state, *state_refs)


def _mixer_kernel(sink_ref, q_ref, kc_ref, kp_ref, k0_ref, kn_ref, vc_ref, vp_ref, v0_ref, vn_ref,
                  c0_ref, n0_ref, m0_ref, qtf_ref, kf_ref, vtf_ref, gf_ref, bcf_ref, brf_ref,
                  qtb_ref, kb_ref, vtb_ref, gb_ref, bcb_ref, brb_ref,
                  att_ref, hf_ref, hb_ref, c_ref, n_ref, m_ref, *, n_steps, cps):
    @pl.when(pl.program_id(1) == 0)
    def _():
        c_ref[...] = c0_ref[0]
        n_ref[...] = n0_ref[0]
        m_ref[...] = m0_ref[0]

    n_iter, scores, finish = _attn_stream(sink_ref, q_ref, kc_ref, kp_ref, k0_ref, kn_ref,
                                          vc_ref, vp_ref, v0_ref, vn_ref, att_ref, n_steps)
    pending = [scores(n) for n in range(ATT_AHEAD)]
    dirs = ((qtf_ref, kf_ref, vtf_ref, gf_ref, bcf_ref, brf_ref, hf_ref),
            (qtb_ref, kb_ref, vtb_ref, gb_ref, bcb_ref, brb_ref, hb_ref))
    state = _mlstm_load_state(c_ref, n_ref, m_ref)
    n_items = cps * N_CHAIN
    items = [_mlstm_phase1(dirs, state, it, with_h=True, cps=cps) for it in range(ML_AHEAD)]
    for n in range(max(n_iter, n_items)):
        if n < n_iter:
            st = pending.pop(0)
            if n + ATT_AHEAD < n_iter:
                pending.append(scores(n + ATT_AHEAD))
            finish(n, st)
        if n < n_items:
            if n + ML_AHEAD < n_items:
                items.append(_mlstm_phase1(dirs, state, n + ML_AHEAD, with_h=True, cps=cps))
            _mlstm_phase2(items.pop(0))
    _mlstm_commit(state, c_ref, n_ref, m_ref)


def _mlstm_specs(T, order, with_q):
    specs = [
        pl.BlockSpec((1, ML_QK_W, T), lambda b, j: (b, 0, order(j))),
        pl.BlockSpec((1, T, ML_QK_W), lambda b, j: (b, order(j), 0)),
        pl.BlockSpec((1, ML_V_W, T), lambda b, j: (b, 0, order(j))),
        pl.BlockSpec((1, T, ML_GATE_W), lambda b, j: (b, order(j), 0)),
        pl.BlockSpec((1, T, ML_GATE_W), lambda b, j: (b, order(j), 0)),
        pl.BlockSpec((1, ML_GATE_W, T), lambda b, j: (b, 0, order(j))),
    ]
    return specs if with_q else specs[1:]


_STATE_SHAPES = ((N_CHAIN, ML_V_DIM, ML_QK_DIM), (N_CHAIN, 8, ML_QK_DIM), (N_CHAIN, 8, LANES))


ML_AHEAD = 32
ML_CPS = 4
ML_CTX_CPS = 2


def _mlstm_ctx_call(mk, mvt, g, bc, br):
    B, C, _ = mk.shape
    T = ML_CTX_CPS * ML_CHUNK
    nc = C // T
    state_specs = [pl.BlockSpec((1,) + s, lambda b, j: (b, 0, 0, 0)) for s in _STATE_SHAPES]
    return pl.pallas_call(
        functools.partial(_mlstm_ctx_kernel, cps=ML_CTX_CPS),
        grid=(B, nc),
        in_specs=_mlstm_specs(T, lambda j: j, False) + _mlstm_specs(T, lambda j: nc - 1 - j, False),
        out_specs=state_specs,
        out_shape=[jax.ShapeDtypeStruct((B,) + s, F32) for s in _STATE_SHAPES],
        compiler_params=_cparams(("arbitrary", "arbitrary")),
        name="mlstm_ctx",
    )(mk, mvt, g, bc, br, mk, mvt, g, bc, br)


def _mixer_call(sink, q, kx, vt, kx_c, vt_c, state, mqt, mk, mvt, g, bc, br):
    B, L, _ = mk.shape
    C = kx_c.shape[1]
    T = TM_MIXER
    assert T == ATT_QB * ATT_BLOCK == ML_CPS * ML_CHUNK
    nc = L // T
    att_in, att_out = _attn_specs(L, C)
    state_specs = [pl.BlockSpec((1,) + s, lambda b, j: (b, 0, 0, 0)) for s in _STATE_SHAPES]
    out_specs = [att_out,
                 pl.BlockSpec((1, ML_V_W, T), lambda b, j: (b, 0, j)),
                 pl.BlockSpec((1, ML_V_W, T), lambda b, j: (b, 0, nc - 1 - j))]
    return pl.pallas_call(
        functools.partial(_mixer_kernel, n_steps=nc, cps=ML_CPS),
        grid=(B, nc),
        in_specs=(att_in + state_specs + _mlstm_specs(T, lambda j: j, True)
                  + _mlstm_specs(T, lambda j: nc - 1 - j, True)),
        out_specs=out_specs,
        out_shape=[jax.ShapeDtypeStruct((B, ATT_Q_W, L), BF16)] + [jax.ShapeDtypeStruct((B, ML_V_W, L), BF16)] * 2,
        scratch_shapes=[pltpu.VMEM(s, F32) for s in _STATE_SHAPES],
        compiler_params=_cparams(("arbitrary", "arbitrary")),
        name="mixer",
    )(sink, q, kx_c, kx, kx, kx, vt_c, vt, vt, vt, *state, mqt, mk, mvt, g, bc, br, mqt, mk, mvt, g, bc, br)


def _merge_kernel(att_ref, hf_ref, hb_ref, so_ref, sga_ref, sgm_ref, x_ref, g1_ref, mod2_ref, mlw_ref, n2w_ref,
                  wa_ref, wm_ref, wo_ref, xmid_ref, h2_ref, y_ref):
    s = pl.program_id(0)
    last = pl.num_programs(0) - 1
    cur = s % 2

    def body(do_branch, do_out):
        if do_out:
            y2 = _dot_tn(y_ref[1 - cur], wo_ref[...])
        if do_branch:
            ya = _dot_tn(wa_ref[...], att_ref[0])
            ht = hf_ref[0].astype(F32) + hb_ref[0].astype(F32)
            parts = []
            for h in range(ML_HEADS):
                seg = ht[h * ML_V_DIM:(h + 1) * ML_V_DIM, :]
                ms = jnp.mean(seg * seg, axis=0, keepdims=True)
                parts.append(seg * lax.rsqrt(ms + EPS))
            ml = (jnp.concatenate(parts, axis=0) * mlw_ref[...] * so_ref[0].astype(F32)).astype(BF16)
            ym = _dot_tn(wm_ref[...], ml)
        if do_out:
            xm = x_ref[0] + g1_ref[0, 0] * y2
            xmid_ref[0] = xm
            ms = jnp.mean(xm * xm, axis=-1, keepdims=True)
            h2 = xm * lax.rsqrt(ms + EPS) * n2w_ref[...]
            h2_ref[0] = (h2 * (1.0 + mod2_ref[1, 0]) + mod2_ref[0, 0]).astype(BF16)
        if do_branch:
            y_ref[cur] = (sga_ref[0].astype(F32) * ya + sgm_ref[0].astype(F32) * ym).astype(BF16)

    pl.when(s == 0)(lambda: body(True, False))
    pl.when((s > 0) & (s < last))(lambda: body(True, True))
    pl.when(s == last)(lambda: body(False, True))


def _merge_call(att_t, hf_t, hb_t, so_t, sga_t, sgm_t, x, mod6, mlw_b, n2w, wa, wm, wo, tm):
    B, L, _ = x.shape
    nt = L // tm
    n_all = B * nt

    def tile_in(s):
        t = jnp.minimum(s, n_all - 1)
        return t // nt, t % nt

    def tile_out(s):
        t = jnp.maximum(s - 1, 0)
        return t // nt, t % nt

    def in_t(h):
        return pl.BlockSpec((1, h, tm), lambda s: (tile_in(s)[0], 0, tile_in(s)[1]))

    def out_rows(w):
        return pl.BlockSpec((1, tm, w), lambda s: (*tile_out(s), 0))

    def const(shape):
        return pl.BlockSpec(shape, lambda s: (0,) * len(shape))

    return pl.pallas_call(
        _merge_kernel,
        grid=(n_all + 1,),
        in_specs=[in_t(D), in_t(D), in_t(D), in_t(D), in_t(D), in_t(D), out_rows(D),
                  _mod_spec(MOD_G1, 1, lambda s: tile_out(s)[0]),
                  _mod_spec(MOD_SH2, 2, lambda s: tile_out(s)[0]),
                  const((D, tm)), const((1, D)), const((D, D)), const((D, D)), const((D, D))],
        out_specs=[out_rows(D), out_rows(D)],
        out_shape=[jax.ShapeDtypeStruct((B, L, D), F32), jax.ShapeDtypeStruct((B, L, D), BF16)],
        scratch_shapes=[pltpu.VMEM((2, D, tm), BF16)],
        compiler_params=_cparams(("arbitrary",)),
        name="merge",
    )(att_t, hf_t, hb_t, so_t, sga_t, sgm_t, x, mod6, mod6, mlw_b, n2w, wa, wm, wo)


HALO = 16
FFN_AHEAD = 1


def _ffn_kernel(h_ref, hp_ref, hn_ref, xmid_ref, mod_ref, wup_ref, cw_ref, cb_ref, wdn_ref, o_ref,
                act_ref, *, n_tiles, tn, dn):
    s = pl.program_id(0)
    last = pl.num_programs(0) - 1

    @pl.when(s == 0)
    def _():
        _ffn_body(h_ref, hp_ref, hn_ref, xmid_ref, mod_ref, wup_ref, cw_ref, cb_ref, wdn_ref, o_ref, act_ref,
                  n_tiles=n_tiles, tn=tn, dn=dn, do_up=True, do_down=False)

    @pl.when((s > 0) & (s < last))
    def _():
        _ffn_body(h_ref, hp_ref, hn_ref, xmid_ref, mod_ref, wup_ref, cw_ref, cb_ref, wdn_ref, o_ref, act_ref,
                  n_tiles=n_tiles, tn=tn, dn=dn, do_up=True, do_down=True)

    @pl.when(s == last)
    def _():
        _ffn_body(h_ref, hp_ref, hn_ref, xmid_ref, mod_ref, wup_ref, cw_ref, cb_ref, wdn_ref, o_ref, act_ref,
                  n_tiles=n_tiles, tn=tn, dn=dn, do_up=False, do_down=True)


def _ffn_body(h_ref, hp_ref, hn_ref, xmid_ref, mod_ref, wup_ref, cw_ref, cb_ref, wdn_ref, o_ref, act_ref,
              *, n_tiles, tn, dn, do_up, do_down):
    s = pl.program_id(0)
    i = s % n_tiles
    cur = s % 2
    tm = h_ref.shape[1]
    n_chunks = D_FF // tn
    n_dn = D // dn
    act_prev = act_ref[1 - cur] if do_down else None

    def down(k):
        cols = slice(k * dn, (k + 1) * dn)
        o_ref[0, :, cols] = xmid_ref[0, :, cols] + mod_ref[0, 0, :, cols] * _dot(act_prev, wdn_ref[:, cols])

    if not do_up:
        for k in range(n_dn):
            down(k)
        return

    h = h_ref[0]
    prev_row = jnp.where(i > 0, hp_ref[0].astype(F32)[HALO - 1:HALO, :], 0.0)
    next_row = jnp.where(i < n_tiles - 1, hn_ref[0].astype(F32)[0:1, :], 0.0)
    top = lax.broadcasted_iota(jnp.int32, (16, D), 0) < 8
    edge = jnp.where(top, prev_row, next_row).astype(BF16)
    row8 = lax.broadcasted_iota(jnp.int32, (8, tn), 0)
    h_ext = jnp.concatenate([h, edge], axis=0)

    def up(c0):
        u_ext = _dot(h_ext, wup_ref[:, c0:c0 + tn])
        return u_ext[:tm], u_ext[tm:]

    def conv(u, ue, c0):
        below = pltpu.roll(u, 1, 0)
        above = pltpu.roll(u, tm - 1, 0)
        below = jnp.concatenate([jnp.where(row8 == 0, ue[0:8], below[0:8]), below[8:]], axis=0)
        above = jnp.concatenate([above[:tm - 8], jnp.where(row8 == 7, ue[8:16], above[tm - 8:])], axis=0)
        cw = cw_ref[:, c0:c0 + tn]
        return cb_ref[:, c0:c0 + tn] + below * cw[0:1] + u * cw[1:2] + above * cw[2:3]

    pending = [(up(c * tn), up(D_FF + c * tn)) for c in range(FFN_AHEAD)]
    done = 0
    for c in range(n_chunks):
        (ua, uae), (ug, uge) = pending.pop(0)
        if c + FFN_AHEAD < n_chunks:
            pending.append((up((c + FFN_AHEAD) * tn), up(D_FF + (c + FFN_AHEAD) * tn)))
        while do_down and done * n_chunks < (c + 1) * n_dn:
            down(done)
            done += 1
        a = conv(ua, uae, c * tn)
        hg = conv(ug, uge, D_FF + c * tn)
        act_ref[cur, :, c * tn:(c + 1) * tn] = ((hg + hg * jnp.tanh(hg)) * a).astype(BF16)


def _ffn_call(h2, xmid, g2, wup, cw, cb, wdn, tm, tn, dn):
    B, L, _ = xmid.shape
    nt = L // tm
    n_all = B * nt
    hb = tm // HALO
    nhb = L // HALO

    def tile_in(s):
        t = jnp.minimum(s, n_all - 1)
        return t // nt, t % nt

    def tile_out(s):
        t = jnp.maximum(s - 1, 0)
        return t // nt, t % nt

    def in_spec():
        return pl.BlockSpec((1, tm, D), lambda s: (*tile_in(s), 0))

    def out_spec():
        return pl.BlockSpec((1, tm, D), lambda s: (*tile_out(s), 0))

    def prev_halo(s):
        b, i = tile_in(s)
        return b, jnp.maximum(i * hb - 1, 0), 0

    def next_halo(s):
        b, i = tile_in(s)
        return b, jnp.minimum((i + 1) * hb, nhb - 1), 0

    def const(shape):
        return pl.BlockSpec(shape, lambda s: (0,) * len(shape))

    return pl.pallas_call(
        functools.partial(_ffn_kernel, n_tiles=nt, tn=tn, dn=dn),
        grid=(n_all + 1,),
        in_specs=[in_spec(),
                  pl.BlockSpec((1, HALO, D), prev_halo),
                  pl.BlockSpec((1, HALO, D), next_halo),
                  out_spec(),
                  _mod_spec(MOD_G2, 1, lambda s: tile_out(s)[0]),
                  const(wup.shape), const(cw.shape), const(cb.shape), const(wdn.shape)],
        out_specs=out_spec(),
        out_shape=jax.ShapeDtypeStruct((B, L, D), F32),
        scratch_shapes=[pltpu.VMEM((2, tm, D_FF), BF16)],
        compiler_params=_cparams(("arbitrary",)),
        name="ffn",
    )(h2, h2, h2, xmid, g2, wup, cw, cb, wdn)


def _pair_perm(n_heads):
    half = ATT_HEAD_DIM // 2
    idx = []
    for p in range(n_heads // 2):
        for sub in range(4):
            head = 2 * p + (sub % 2)
            d0 = (sub // 2) * half
            idx.extend(head * ATT_HEAD_DIM + d0 + e for e in range(half))
    return np.asarray(idx, np.int32)


def _rope_tables(L):
    f32 = np.float32
    rows = L // GRID_W
    row = np.repeat(np.arange(rows, dtype=f32), GRID_W)
    col = np.tile(np.arange(GRID_W, dtype=f32), rows)
    n_freq = ATT_HEAD_DIM // 4
    inv_freq = (f32(ROPE_BASE) ** (-np.arange(n_freq, dtype=f32) / f32(n_freq))).astype(f32)
    ang = np.concatenate([row[:, None] * inv_freq, col[:, None] * inv_freq], axis=-1).astype(f32)
    cos = np.tile(np.cos(ang).astype(f32), (1, 4))
    sin = np.tile(np.sin(ang).astype(f32), (1, 4))
    sign = np.where(np.arange(LANES) < LANES // 2, -1.0, 1.0).astype(f32)
    return jnp.asarray(cos), jnp.asarray(sin * sign)


def kernel(x, c, ctx, c_ctx, w_mod, b_mod, norm1_w, w_in, q_norm_w, k_norm_w, attn_sink, ml_gate_b, ml_norm_w,
           w_branch_att, w_branch_ml, w_out, norm2_w, w_up, conv_w, conv_b, w_down):
    B, L, _ = x.shape
    C = ctx.shape[1]
    assert L % GRID_W == 0 and C % TM_CTX == 0
    assert all(L % t == 0 for t in (TM_INPROJ, TM_MIXER, TM_MERGE, TM_FFN))
    l = 0

    n_rows = -(-(B + 1) // 16) * 16
    cc = jnp.concatenate([c, c_ctx[None, :], jnp.zeros((n_rows - B - 1, D), F32)], axis=0)
    mod6 = _mod_call(cc, w_mod[l], b_mod[l][None, :])

    w = w_in[l]
    qperm = _pair_perm(ATT_HEADS)
    kperm = _pair_perm(ATT_KV_HEADS)
    def pair_cols(wc, n_heads):
        half = ATT_HEAD_DIM // 2
        wc = wc.reshape(D, n_heads // 2, 2, 2, half).transpose(0, 1, 3, 2, 4)
        return wc.reshape(D, n_heads * ATT_HEAD_DIM)

    w_q = pair_cols(w[:, _O_AQ:_O_AQ + ATT_Q_W], ATT_HEADS)
    w_k = pair_cols(w[:, _O_AK:_O_AK + ATT_KV_W], ATT_KV_HEADS)
    w_g = jnp.pad(w[:, _O_MG:_O_MG + ML_GATE_W], ((0, 0), (0, LANES - ML_GATE_W)))
    w_mk = w[:, _O_MK:_O_MK + ML_QK_W] * (ML_QK_DIM ** -0.5)
    w_p = jnp.concatenate([w_q, w_k, w_mk, w_g], axis=1).astype(BF16)
    w_t = w[:, _O_AV:].T.astype(BF16)

    head_of_col = np.concatenate([qperm // ATT_HEAD_DIM, ATT_HEADS + kperm // ATT_HEAD_DIM])
    e_np = (head_of_col[:, None] == np.arange(LANES)[None, :]).astype(np.float32)
    e_mat = jnp.asarray(e_np, BF16)
    et_mat = jnp.asarray(np.concatenate([e_np.T, e_np.T], axis=0), BF16)
    def pair_tiled(wn, n_heads):
        half = ATT_HEAD_DIM // 2
        return jnp.tile(jnp.concatenate([wn[:half], wn[:half], wn[half:], wn[half:]]), n_heads // 2)

    qkw = jnp.concatenate([pair_tiled(q_norm_w[l], ATT_HEADS) * (ATT_SCALE * LOG2E),
                           pair_tiled(k_norm_w[l], ATT_KV_HEADS)])[None, :]
    cos_t, sin_t = _rope_tables(L)
    gb = ml_gate_b[l].reshape(1, ML_GATE_W)
    gbt = ml_gate_b[l].reshape(ML_GATE_W, 1)
    n1w = norm1_w[l][None, :]

    kx_c, mk_c, g_c, bc_c, vt_c, mvt_c, br_c = _ctxproj_call(
        ctx, mod6, B, n1w, w_p, w_t, e_mat, et_mat, qkw, gb, gbt, tm=TM_CTX)
    q, kx, mk, g, bc, vt, mqt, mvt, sot, sgat, sgmt, br = _inproj_call(
        x, mod6, n1w, w_p, w_t, e_mat, et_mat, qkw, cos_t, sin_t, gb, gbt, tm=TM_INPROJ)

    state = _mlstm_ctx_call(mk_c, mvt_c, g_c, bc_c, br_c)
    att_t, hf_t, hb_t = _mixer_call(attn_sink[l], q, kx, vt, kx_c, vt_c, state, mqt, mk, mvt, g, bc, br)

    mlw_b = jnp.broadcast_to(ml_norm_w[l][:, None], (ML_V_W, TM_MERGE))
    xmid, h2 = _merge_call(att_t, hf_t, hb_t, sot, sgat, sgmt, x, mod6, mlw_b, norm2_w[l][None, :],
                           w_branch_att[l].astype(BF16), w_branch_ml[l].astype(BF16), w_out[l].astype(BF16),
                           tm=TM_MERGE)
    gate_half = jnp.where(jnp.arange(2 * D_FF) < D_FF, 1.0, 0.5).astype(F32)
    out = _ffn_call(h2, xmid, mod6, w_up[l].astype(BF16), conv_w[l] * gate_half, (conv_b[l] * gate_half)[None, :],
                    w_down[l].astype(BF16), tm=TM_FFN, tn=FFN_UP_COLS, dn=FFN_DOWN_COLS)
    return out
```

```python
import functools

import jax
import jax.numpy as jnp
import numpy as np
from jax import lax
from jax.experimental import pallas as pl
from jax.experimental.pallas import tpu as pltpu

D = 1024
GRID_W = 64
ATT_HEADS = 16
ATT_KV_HEADS = 4
ATT_HEAD_DIM = 64
ATT_GROUP = ATT_HEADS // ATT_KV_HEADS
ATT_BLOCK = 128
WINDOW = 128
ROPE_BASE = 10000.0
ATT_SCALE = ATT_HEAD_DIM ** -0.5
LOG2E = 1.4426950408889634
ML_HEADS = 4
ML_QK_DIM = 128
ML_V_DIM = 256
ML_CHUNK = 128
D_FF = 2816
EPS = 1e-6
NEG_INF = -1e30

ATT_Q_W = ATT_HEADS * ATT_HEAD_DIM
ATT_KV_W = ATT_KV_HEADS * ATT_HEAD_DIM
ML_QK_W = ML_HEADS * ML_QK_DIM
ML_V_W = ML_HEADS * ML_V_DIM
ML_GATE_W = 2 * 2 * ML_HEADS

LANES = 128
KX_W = ATT_KV_HEADS * 2 * LANES
VMEM_LIMIT = 56 * 1024 * 1024

TM_CTX = 256
TM_INPROJ = 512
TM_MIXER = 512
TM_MERGE = 512
TM_FFN = 512
FFN_UP_COLS = 256
FFN_DOWN_COLS = 256

BF16 = jnp.bfloat16
F32 = jnp.float32

_O_AQ = 0
_O_AK = _O_AQ + ATT_Q_W
_O_AV = _O_AK + ATT_KV_W
_O_MQ = _O_AV + ATT_KV_W
_O_MK = _O_MQ + ML_QK_W
_O_MV = _O_MK + ML_QK_W
_O_MO = _O_MV + ML_V_W
_O_MG = _O_MO + ML_V_W
_O_GA = _O_MG + ML_GATE_W
_O_GM = _O_GA + D

QK_W = ATT_Q_W + ATT_KV_W
_P_QK = 0
_P_MK = _P_QK + QK_W
_P_MG = _P_MK + ML_QK_W
_P_END = _P_MG + LANES
_R_V = 0
_R_MQ = _O_MQ - _O_AV
_R_MV = _O_MV - _O_AV
_R_MO = _O_MO - _O_AV
_R_MG = _O_MG - _O_AV
_R_GA = _O_GA - _O_AV
_R_GM = _O_GM - _O_AV


def _dot(a, b):
    return jnp.dot(a, b, preferred_element_type=F32)


def _dot_nt(a, b):
    return lax.dot_general(a, b, (((1,), (1,)), ((), ())), preferred_element_type=F32)


def _dot_tn(a, b):
    return lax.dot_general(a, b, (((0,), (0,)), ((), ())), preferred_element_type=F32)


def _cparams(sem):
    return pltpu.CompilerParams(dimension_semantics=sem, vmem_limit_bytes=VMEM_LIMIT)


def _mod_kernel(c_ref, w_ref, b_ref, o_ref):
    c = c_ref[...]
    a = c * jax.nn.sigmoid(c)
    a_hi = a.astype(BF16)
    a_lo = (a - a_hi.astype(F32)).astype(BF16)
    w = w_ref[...]
    w_hi = w.astype(BF16)
    w_lo = (w - w_hi.astype(F32)).astype(BF16)
    rows = a.shape[0]
    both = _dot(jnp.concatenate([a_hi, a_lo], axis=0), w_hi)
    o_ref[0, :, 0, :] = both[:rows] + both[rows:] + _dot(a_hi, w_lo) + b_ref[...]


MOD_SH1, MOD_SC1, MOD_SH2, MOD_SC2, MOD_G1, MOD_G2 = range(6)


def _mod_call(cc, w_mod, b_mod):
    rows = cc.shape[0]
    n_seg = w_mod.shape[1] // D
    assert n_seg == 6

    def out_pos(j):
        return jnp.where(j == 2, MOD_G1, jnp.where((j == 3) | (j == 4), j - 1, j))

    return pl.pallas_call(
        _mod_kernel,
        grid=(n_seg,),
        in_specs=[pl.BlockSpec((rows, D), lambda j: (0, 0)),
                  pl.BlockSpec((D, D), lambda j: (0, j)),
                  pl.BlockSpec((1, D), lambda j: (0, j))],
        out_specs=pl.BlockSpec((1, rows, 1, D), lambda j: (out_pos(j), 0, 0, 0)),
        out_shape=jax.ShapeDtypeStruct((n_seg, rows, 1, D), F32),
        compiler_params=_cparams(("arbitrary",)),
        name="mod",
    )(cc, w_mod, b_mod)


def _mod_spec(seg, n_seg, row_of):
    return pl.BlockSpec((n_seg, 1, 1, D), lambda *ids: (seg // n_seg, row_of(*ids), 0, 0))


def _split2(x):
    x1 = x.astype(BF16)
    x2 = (x - x1.astype(F32)).astype(BF16)
    return x1, x2


def _norm_modulate(x, n1w_ref, mod_ref):
    ms = jnp.mean(x * x, axis=-1, keepdims=True)
    y = x * lax.rsqrt(ms + EPS) * n1w_ref[...]
    return (y * (1.0 + mod_ref[1, 0]) + mod_ref[0, 0]).astype(BF16)


def _sigmoid(x):
    return 0.5 * jnp.tanh(0.5 * x) + 0.5


def _head_rms_scale(ss):
    r = lax.rsqrt(ss * (1.0 / ATT_HEAD_DIM) + EPS)
    r_hi = r.astype(BF16)
    r_lo = (r - r_hi.astype(F32)).astype(BF16)
    return jnp.concatenate([r_hi, r_lo], axis=1)


def _store_k_variants(k_ref, pair, o):
    lane = lax.broadcasted_iota(jnp.int32, (1, LANES), 1)
    keep = ((lane // 32) % 2) == 0
    c0 = 4 * pair * LANES
    k_ref[0, :, c0:c0 + LANES] = jnp.where(keep, o, 0.0).astype(BF16)
    k_ref[0, :, c0 + LANES:c0 + 2 * LANES] = jnp.where(keep, 0.0, pltpu.roll(o, 32, 1)).astype(BF16)
    k_ref[0, :, c0 + 2 * LANES:c0 + 3 * LANES] = jnp.where(keep, pltpu.roll(o, 96, 1), 0.0).astype(BF16)
    k_ref[0, :, c0 + 3 * LANES:c0 + 4 * LANES] = jnp.where(keep, 0.0, o).astype(BF16)


def _cum_gates_cols(g16, tri_lo, tri_up):
    lf = jax.nn.log_sigmoid(g16)
    fwd_col = lax.broadcasted_iota(jnp.int32, (1, ML_GATE_W), 1) < ML_GATE_W // 2
    out = []
    for c in range(g16.shape[0] // ML_CHUNK):
        parts = _split2(lf[c * ML_CHUNK:(c + 1) * ML_CHUNK])
        out.append(jnp.where(fwd_col, sum(_dot(tri_lo, p) for p in parts), sum(_dot(tri_up, p) for p in parts)))
    return jnp.concatenate(out, axis=0)


def _cum_gates_rows(gt16, tri_lo, tri_up):
    lf = jax.nn.log_sigmoid(gt16)
    fwd_row = lax.broadcasted_iota(jnp.int32, (ML_GATE_W, 1), 0) < ML_GATE_W // 2
    out = []
    for c in range(gt16.shape[1] // ML_CHUNK):
        parts = _split2(lf[:, c * ML_CHUNK:(c + 1) * ML_CHUNK])
        out.append(jnp.where(fwd_row, sum(_dot(p, tri_up) for p in parts), sum(_dot(p, tri_lo) for p in parts)))
    return jnp.concatenate(out, axis=1)


def _inproj_kernel(x_ref, mod_ref, n1w_ref, w_ref, wt_ref, e_ref, et_ref, qkw_ref,
                   cos_ref, sin_ref, gb_ref, gbt_ref, tril_ref, triu_ref,
                   q_ref, k_ref, mk_ref, g_ref, bc_ref, vt_ref, mqt_ref, mvt_ref, sot_ref, sgat_ref, sgmt_ref, br_ref,
                   hn_ref):
    hn_ref[...] = _norm_modulate(x_ref[0], n1w_ref, mod_ref)
    hn = hn_ref[...]

    def ft(r0, height):
        return _dot_nt(wt_ref[r0:r0 + height, :], hn)

    acc = _dot(hn, w_ref[:, _P_QK:_P_QK + QK_W])
    mkg = _dot(hn, w_ref[:, _P_MK:_P_END])
    g16 = mkg[:, ML_QK_W:ML_QK_W + ML_GATE_W] + gb_ref[...]
    g_ref[0] = g16
    mk_ref[0] = mkg[:, :ML_QK_W].astype(BF16)
    ss = _dot((acc * acc).astype(BF16), e_ref[...])
    vt_ref[0] = ft(_R_V, ATT_KV_W).astype(BF16)
    mog = ft(_R_MO, ML_V_W + ML_GATE_W)
    sot_ref[0] = _sigmoid(mog[:ML_V_W]).astype(BF16)
    gt16 = mog[ML_V_W:] + gbt_ref[...]
    rb = _dot(_head_rms_scale(ss), et_ref[...])
    sgat_ref[0] = _sigmoid(ft(_R_GA, D)).astype(BF16)
    bc_ref[0] = _cum_gates_cols(g16, tril_ref[...], triu_ref[...])

    qn = acc * rb * qkw_ref[...]
    cos = cos_ref[...]
    sin = sin_ref[...]
    for gi in range(QK_W // LANES):
        xs = qn[:, gi * LANES:(gi + 1) * LANES]
        o = xs * cos + pltpu.roll(xs, LANES // 2, 1) * sin
        if gi < ATT_Q_W // LANES:
            q_ref[0, :, gi * LANES:(gi + 1) * LANES] = o.astype(BF16)
        else:
            _store_k_variants(k_ref, gi - ATT_Q_W // LANES, o)

    sgmt_ref[0] = _sigmoid(ft(_R_GM, D)).astype(BF16)
    br_ref[0] = _cum_gates_rows(gt16, tril_ref[...], triu_ref[...])
    mqt_ref[0] = ft(_R_MQ, ML_QK_W).astype(BF16)
    mvt_ref[0] = ft(_R_MV, ML_V_W).astype(BF16)


def _ctxproj_kernel(x_ref, mod_ref, n1w_ref, w_ref, wt_ref, e_ref, et_ref, qkw_ref,
                    gb_ref, gbt_ref, tril_ref, triu_ref,
                    k_ref, mk_ref, g_ref, bc_ref, vt_ref, mvt_ref, br_ref):
    hn = _norm_modulate(x_ref[0], n1w_ref, mod_ref)
    acc = _dot(hn, w_ref[:, _P_QK + ATT_Q_W:_P_QK + QK_W])
    mkg = _dot(hn, w_ref[:, _P_MK:_P_END])
    g16 = mkg[:, ML_QK_W:ML_QK_W + ML_GATE_W] + gb_ref[...]
    gt16 = _dot_nt(wt_ref[_R_MG:_R_MG + ML_GATE_W, :], hn) + gbt_ref[...]
    g_ref[0] = g16
    mk_ref[0] = mkg[:, :ML_QK_W].astype(BF16)
    ss = _dot((acc * acc).astype(BF16), e_ref[ATT_Q_W:QK_W, :])
    vt_ref[0] = _dot_nt(wt_ref[_R_V:_R_V + ATT_KV_W, :], hn).astype(BF16)
    rb = _dot(_head_rms_scale(ss), et_ref[:, ATT_Q_W:QK_W])
    mvt_ref[0] = _dot_nt(wt_ref[_R_MV:_R_MV + ML_V_W, :], hn).astype(BF16)
    bc_ref[0] = _cum_gates_cols(g16, tril_ref[...], triu_ref[...])
    kn = acc * rb * qkw_ref[:, ATT_Q_W:QK_W]
    for pair in range(ATT_KV_W // LANES):
        _store_k_variants(k_ref, pair, kn[:, pair * LANES:(pair + 1) * LANES])
    br_ref[0] = _cum_gates_rows(gt16, tril_ref[...], triu_ref[...])


def _chunk_tri(lower):
    r = np.arange(ML_CHUNK)[:, None]
    c = np.arange(ML_CHUNK)[None, :]
    return jnp.asarray((c <= r) if lower else (c >= r), BF16)


def _const_spec(shape):
    return pl.BlockSpec(shape, lambda b, i: (0,) * len(shape))


def _inproj_call(x, mod6, n1w, w_p, w_t, e_mat, et_mat, qkw, cos_t, sin_t, gb, gbt, tm):
    B, L, _ = x.shape
    tril = _chunk_tri(True)
    triu = _chunk_tri(False)

    def rows(w):
        return pl.BlockSpec((1, tm, w), lambda b, i: (b, i, 0))

    def cols(h):
        return pl.BlockSpec((1, h, tm), lambda b, i: (b, 0, i))

    consts = [n1w, w_p, w_t, e_mat, et_mat, qkw]
    tail = [gb, gbt, tril, triu]
    in_specs = ([rows(D), _mod_spec(MOD_SH1, 2, lambda b, i: b)]
                + [_const_spec(a.shape) for a in consts]
                + [pl.BlockSpec((tm, LANES), lambda b, i: (i, 0))] * 2
                + [_const_spec(a.shape) for a in tail])
    out_specs = [rows(ATT_Q_W), rows(KX_W), rows(ML_QK_W), rows(ML_GATE_W), rows(ML_GATE_W),
                 cols(ATT_KV_W), cols(ML_QK_W), cols(ML_V_W), cols(ML_V_W), cols(D), cols(D), cols(ML_GATE_W)]
    out_shape = [
        jax.ShapeDtypeStruct((B, L, ATT_Q_W), BF16),
        jax.ShapeDtypeStruct((B, L, KX_W), BF16),
        jax.ShapeDtypeStruct((B, L, ML_QK_W), BF16),
        jax.ShapeDtypeStruct((B, L, ML_GATE_W), F32),
        jax.ShapeDtypeStruct((B, L, ML_GATE_W), F32),
        jax.ShapeDtypeStruct((B, ATT_KV_W, L), BF16),
        jax.ShapeDtypeStruct((B, ML_QK_W, L), BF16),
        jax.ShapeDtypeStruct((B, ML_V_W, L), BF16),
        jax.ShapeDtypeStruct((B, ML_V_W, L), BF16),
        jax.ShapeDtypeStruct((B, D, L), BF16),
        jax.ShapeDtypeStruct((B, D, L), BF16),
        jax.ShapeDtypeStruct((B, ML_GATE_W, L), F32),
    ]
    return pl.pallas_call(
        _inproj_kernel,
        grid=(B, L // tm),
        in_specs=in_specs,
        out_specs=out_specs,
        out_shape=out_shape,
        scratch_shapes=[pltpu.VMEM((tm, D), BF16)],
        compiler_params=_cparams(("arbitrary", "arbitrary")),
        name="inproj",
    )(x, mod6, *consts, cos_t, sin_t, *tail)


def _ctxproj_call(ctx, mod6, ctx_row, n1w, w_p, w_t, e_mat, et_mat, qkw, gb, gbt, tm):
    B, C, _ = ctx.shape
    tril = _chunk_tri(True)
    triu = _chunk_tri(False)

    def rows(w):
        return pl.BlockSpec((1, tm, w), lambda b, i: (b, i, 0))

    def cols(h):
        return pl.BlockSpec((1, h, tm), lambda b, i: (b, 0, i))

    consts = [n1w, w_p, w_t, e_mat, et_mat, qkw, gb, gbt, tril, triu]
    out_specs = [rows(KX_W), rows(ML_QK_W), rows(ML_GATE_W), rows(ML_GATE_W),
                 cols(ATT_KV_W), cols(ML_V_W), cols(ML_GATE_W)]
    out_shape = [
        jax.ShapeDtypeStruct((B, C, KX_W), BF16),
        jax.ShapeDtypeStruct((B, C, ML_QK_W), BF16),
        jax.ShapeDtypeStruct((B, C, ML_GATE_W), F32),
        jax.ShapeDtypeStruct((B, C, ML_GATE_W), F32),
        jax.ShapeDtypeStruct((B, ATT_KV_W, C), BF16),
        jax.ShapeDtypeStruct((B, ML_V_W, C), BF16),
        jax.ShapeDtypeStruct((B, ML_GATE_W, C), F32),
    ]
    return pl.pallas_call(
        _ctxproj_kernel,
        grid=(B, C // tm),
        in_specs=([rows(D), _mod_spec(MOD_SH1, 2, lambda b, i: ctx_row)]
                  + [_const_spec(a.shape) for a in consts]),
        out_specs=out_specs,
        out_shape=out_shape,
        compiler_params=_cparams(("arbitrary", "arbitrary")),
        name="ctxproj",
    )(ctx, mod6, *consts)


ATT_AHEAD = 12

ATT_QB = 4


def _attn_stream(sink_ref, q_ref, kc_ref, kp_ref, k0_ref, kn_ref, vc_ref, vp_ref, v0_ref, vn_ref, o_ref,
                 n_steps):
    i = pl.program_id(1)
    T = ATT_BLOCK
    hd = ATT_HEAD_DIM
    s_idx = lax.broadcasted_iota(jnp.int32, (T, 2 * T), 0)
    t_idx = lax.broadcasted_iota(jnp.int32, (T, 2 * T), 1) % T
    first = lax.broadcasted_iota(jnp.int32, (1, 2 * T), 1) < T

    k_own = k0_ref[0]
    v_own = v0_ref[0]
    k_blk = [kp_ref[0]] + [k_own[b * T:(b + 1) * T] for b in range(ATT_QB)] + [kn_ref[0]]
    v_blk = [vp_ref[0]] + [v_own[:, b * T:(b + 1) * T] for b in range(ATT_QB)] + [vn_ref[0]]
    ones_rows = jnp.ones((16, 3 * T + kc_ref.shape[1]), BF16)

    def window(qb):
        k_all = jnp.concatenate(k_blk[qb:qb + 3] + [kc_ref[0]], axis=0)
        vt_all = jnp.concatenate(v_blk[qb:qb + 3] + [vc_ref[0]], axis=1)
        ok_prev = (s_idx >= t_idx) & ((i > 0) if qb == 0 else True)
        ok_next = (s_idx <= t_idx) & ((i < n_steps - 1) if qb == ATT_QB - 1 else True)
        return k_all, vt_all, ok_prev, ok_next

    windows = [window(qb) for qb in range(ATT_QB)]
    per_qb = 2 * ATT_KV_HEADS

    def scores(n):
        qb, r = divmod(n, per_qb)
        kh, var = divmod(r, 2)
        q = q_ref[0, qb * T:(qb + 1) * T, :]
        q2 = jnp.concatenate([q[:, (2 * kh) * LANES:(2 * kh + 1) * LANES],
                              q[:, (2 * kh + 1) * LANES:(2 * kh + 2) * LANES]], axis=0)
        kk = windows[qb][0][:, (2 * kh + var) * LANES:(2 * kh + var + 1) * LANES]
        return _dot_nt(kk, q2)

    def finish(n, st):
        qb, r = divmod(n, per_qb)
        kh, var = divmod(r, 2)
        _, vt_all, ok_prev, ok_next = windows[qb]
        vt = vt_all[kh * hd:(kh + 1) * hd, :]
        st = jnp.concatenate([jnp.where(ok_prev, st[0:T], NEG_INF), st[T:2 * T],
                              jnp.where(ok_next, st[2 * T:3 * T], NEG_INF), st[3 * T:]], axis=0)
        h0 = ATT_GROUP * kh + var
        h1 = h0 + 2
        sink = jnp.where(first, sink_ref[h0], sink_ref[h1]) * LOG2E
        m = jnp.maximum(jnp.max(st, axis=0, keepdims=True), sink)
        p = jnp.exp2(st - m)
        ot = _dot(jnp.concatenate([vt, ones_rows], axis=0), p.astype(BF16))
        denom = ot[hd:hd + 1, :] + jnp.exp2(sink - m)
        ot = ot[0:hd, :] * (1.0 / denom)
        o_ref[0, h0 * hd:(h0 + 1) * hd, qb * T:(qb + 1) * T] = ot[:, 0:T].astype(BF16)
        o_ref[0, h1 * hd:(h1 + 1) * hd, qb * T:(qb + 1) * T] = ot[:, T:2 * T].astype(BF16)

    return ATT_QB * per_qb, scores, finish


def _attn_specs(L, C):
    T = ATT_BLOCK
    nb = L // T
    TQ = ATT_QB * T

    def edge(i, off):
        return jnp.clip(i * ATT_QB + (off if off < 0 else ATT_QB), 0, nb - 1)

    in_specs = [pl.BlockSpec(memory_space=pltpu.SMEM),
                pl.BlockSpec((1, TQ, ATT_Q_W), lambda b, i: (b, i, 0)),
                pl.BlockSpec((1, C, KX_W), lambda b, i: (b, 0, 0)),
                pl.BlockSpec((1, T, KX_W), lambda b, i: (b, edge(i, -1), 0)),
                pl.BlockSpec((1, TQ, KX_W), lambda b, i: (b, i, 0)),
                pl.BlockSpec((1, T, KX_W), lambda b, i: (b, edge(i, 1), 0)),
                pl.BlockSpec((1, ATT_KV_W, C), lambda b, i: (b, 0, 0)),
                pl.BlockSpec((1, ATT_KV_W, T), lambda b, i: (b, 0, edge(i, -1))),
                pl.BlockSpec((1, ATT_KV_W, TQ), lambda b, i: (b, 0, i)),
                pl.BlockSpec((1, ATT_KV_W, T), lambda b, i: (b, 0, edge(i, 1)))]
    return in_specs, pl.BlockSpec((1, ATT_Q_W, TQ), lambda b, i: (b, 0, i))


N_CHAIN = 2 * ML_HEADS


def _mlstm_load_state(c_ref, n_ref, m_ref):
    return [(c_ref[ci], n_ref[ci], m_ref[ci, 0:1, 0:1]) for ci in range(N_CHAIN)]


def _mlstm_phase1(dirs, state, item, with_h, cps):
    T = ML_CHUNK
    sub, rest = divmod(item, N_CHAIN)
    d, h = divmod(rest, ML_HEADS)
    qt_ref, k_ref, vt_ref, g_ref, bc_ref, br_ref, h_ref = dirs[d]
    row = lax.broadcasted_iota(jnp.int32, (T, T), 0)
    col = lax.broadcasted_iota(jnp.int32, (T, T), 1)
    mask = (col >= row) if d == 0 else (col <= row)
    last = T - 1 if d == 0 else 0
    sc = sub if d == 0 else cps - 1 - sub
    tok = slice(sc * T, (sc + 1) * T)
    ci = d * ML_HEADS + h
    gi = d * 2 * ML_HEADS + h
    fi = gi + ML_HEADS
    k = k_ref[0, tok, h * ML_QK_DIM:(h + 1) * ML_QK_DIM]
    vt = vt_ref[0, h * ML_V_DIM:(h + 1) * ML_V_DIM, tok]
    u_col = g_ref[0, tok, gi:gi + 1] - bc_ref[0, tok, fi:fi + 1]
    b_row = br_ref[0, fi:fi + 1, tok]
    ct_old, n_old, m_old = state[ci]
    qt = st = qn2 = None
    if with_h:
        qt = qt_ref[0, h * ML_QK_DIM:(h + 1) * ML_QK_DIM, tok]
        st = _dot(k, qt)
        top = lax.broadcasted_iota(jnp.int32, (8, ML_QK_DIM), 0) == 0
        n_hi = n_old.astype(BF16)
        n_lo = (n_old - n_hi.astype(F32)).astype(BF16)
        qn2 = _dot(jnp.where(top, n_hi, n_lo), qt)
    m_last = jnp.maximum(jnp.max(u_col, axis=0, keepdims=True), m_old)
    decay = jnp.exp(m_old - m_last)
    kw = (k.astype(F32) * jnp.exp(u_col - m_last)).astype(BF16)
    state[ci] = (decay * ct_old + _dot(vt, kw), decay * n_old + _dot(jnp.ones((8, T), BF16), kw),
                 b_row[:, last:last + 1] + m_last)
    return (h, h_ref, tok, mask, qt, vt, u_col, b_row, m_old, ct_old, st, qn2)


def _mlstm_phase2(chain):
    h, h_ref, tok, mask, qt, vt, u_col, b_row, m_old, ct_old, st, qn2 = chain
    umat = jnp.where(mask, u_col, -jnp.inf)
    m_row = jnp.maximum(jnp.max(umat, axis=0, keepdims=True), m_old)
    pt = st * jnp.exp(umat - m_row)
    w_int = jnp.exp(m_old - m_row)
    e_row = jnp.exp(-(b_row + m_row))
    nq = jnp.sum(pt, axis=0, keepdims=True) + w_int * (qn2[0:1, :] + qn2[1:2, :])
    den = jnp.maximum(jnp.abs(nq), e_row)
    lhs = jnp.concatenate([vt, ct_old.astype(BF16)], axis=1)
    rhs = jnp.concatenate([pt.astype(BF16), (qt.astype(F32) * w_int).astype(BF16)], axis=0)
    h_ref[0, h * ML_V_DIM:(h + 1) * ML_V_DIM, tok] = (_dot(lhs, rhs) * (1.0 / den)).astype(BF16)


def _mlstm_commit(state, c_ref, n_ref, m_ref):
    for ci, (c_new, n_new, m_new) in enumerate(state):
        c_ref[ci] = c_new
        n_ref[ci] = n_new
        m_ref[ci] = jnp.broadcast_to(m_new, (8, LANES))


def _mlstm_ctx_kernel(kf_ref, vtf_ref, gf_ref, bcf_ref, brf_ref, kb_ref, vtb_ref, gb_ref, bcb_ref, brb_ref,
                      c_ref, n_ref, m_ref, *, cps):
    @pl.when(pl.program_id(1) == 0)
    def _():
        c_ref[...] = jnp.zeros_like(c_ref)
        n_ref[...] = jnp.zeros_like(n_ref)
        m_ref[...] = jnp.zeros_like(m_ref)

    dirs = ((None, kf_ref, vtf_ref, gf_ref, bcf_ref, brf_ref, None),
            (None, kb_ref, vtb_ref, gb_ref, bcb_ref, brb_ref, None))
    state_refs = (c_ref.at[0], n_ref.at[0], m_ref.at[0])
    state = _mlstm_load_state(*state_refs)
    for item in range(cps * N_CHAIN):
        _mlstm_phase1(dirs, state, item, with_h=False, cps=cps)
    _mlstm_commit(state, *state_refs)


def _mixer_kernel(sink_ref, q_ref, kc_ref, kp_ref, k0_ref, kn_ref, vc_ref, vp_ref, v0_ref, vn_ref,
                  c0_ref, n0_ref, m0_ref, qtf_ref, kf_ref, vtf_ref, gf_ref, bcf_ref, brf_ref,
                  qtb_ref, kb_ref, vtb_ref, gb_ref, bcb_ref, brb_ref,
                  att_ref, hf_ref, hb_ref, c_ref, n_ref, m_ref, *, n_steps, cps):
    @pl.when(pl.program_id(1) == 0)
    def _():
        c_ref[...] = c0_ref[0]
        n_ref[...] = n0_ref[0]
        m_ref[...] = m0_ref[0]

    n_iter, scores, finish = _attn_stream(sink_ref, q_ref, kc_ref, kp_ref, k0_ref, kn_ref,
                                          vc_ref, vp_ref, v0_ref, vn_ref, att_ref, n_steps)
    pending = [scores(n) for n in range(ATT_AHEAD)]
    dirs = ((qtf_ref, kf_ref, vtf_ref, gf_ref, bcf_ref, brf_ref, hf_ref),
            (qtb_ref, kb_ref, vtb_ref, gb_ref, bcb_ref, brb_ref, hb_ref))
    state = _mlstm_load_state(c_ref, n_ref, m_ref)
    n_items = cps * N_CHAIN
    items = [_mlstm_phase1(dirs, state, it, with_h=True, cps=cps) for it in range(ML_AHEAD)]
    for n in range(max(n_iter, n_items)):
        if n < n_iter:
            st = pending.pop(0)
            if n + ATT_AHEAD < n_iter:
                pending.append(scores(n + ATT_AHEAD))
            finish(n, st)
        if n < n_items:
            if n + ML_AHEAD < n_items:
                items.append(_mlstm_phase1(dirs, state, n + ML_AHEAD, with_h=True, cps=cps))
            _mlstm_phase2(items.pop(0))
    _mlstm_commit(state, c_ref, n_ref, m_ref)


def _mlstm_specs(T, order, with_q):
    specs = [
        pl.BlockSpec((1, ML_QK_W, T), lambda b, j: (b, 0, order(j))),
        pl.BlockSpec((1, T, ML_QK_W), lambda b, j: (b, order(j), 0)),
        pl.BlockSpec((1, ML_V_W, T), lambda b, j: (b, 0, order(j))),
        pl.BlockSpec((1, T, ML_GATE_W), lambda b, j: (b, order(j), 0)),
        pl.BlockSpec((1, T, ML_GATE_W), lambda b, j: (b, order(j), 0)),
        pl.BlockSpec((1, ML_GATE_W, T), lambda b, j: (b, 0, order(j))),
    ]
    return specs if with_q else specs[1:]


_STATE_SHAPES = ((N_CHAIN, ML_V_DIM, ML_QK_DIM), (N_CHAIN, 8, ML_QK_DIM), (N_CHAIN, 8, LANES))


ML_AHEAD = 32
ML_CPS = 4
ML_CTX_CPS = 2


def _mlstm_ctx_call(mk, mvt, g, bc, br):
    B, C, _ = mk.shape
    T = ML_CTX_CPS * ML_CHUNK
    nc = C // T
    state_specs = [pl.BlockSpec((1,) + s, lambda b, j: (b, 0, 0, 0)) for s in _STATE_SHAPES]
    return pl.pallas_call(
        functools.partial(_mlstm_ctx_kernel, cps=ML_CTX_CPS),
        grid=(B, nc),
        in_specs=_mlstm_specs(T, lambda j: j, False) + _mlstm_specs(T, lambda j: nc - 1 - j, False),
        out_specs=state_specs,
        out_shape=[jax.ShapeDtypeStruct((B,) + s, F32) for s in _STATE_SHAPES],
        compiler_params=_cparams(("arbitrary", "arbitrary")),
        name="mlstm_ctx",
    )(mk, mvt, g, bc, br, mk, mvt, g, bc, br)


def _mixer_call(sink, q, kx, vt, kx_c, vt_c, state, mqt, mk, mvt, g, bc, br):
    B, L, _ = mk.shape
    C = kx_c.shape[1]
    T = TM_MIXER
    assert T == ATT_QB * ATT_BLOCK == ML_CPS * ML_CHUNK
    nc = L // T
    att_in, att_out = _attn_specs(L, C)
    state_specs = [pl.BlockSpec((1,) + s, lambda b, j: (b, 0, 0, 0)) for s in _STATE_SHAPES]
    out_specs = [att_out,
                 pl.BlockSpec((1, ML_V_W, T), lambda b, j: (b, 0, j)),
                 pl.BlockSpec((1, ML_V_W, T), lambda b, j: (b, 0, nc - 1 - j))]
    return pl.pallas_call(
        functools.partial(_mixer_kernel, n_steps=nc, cps=ML_CPS),
        grid=(B, nc),
        in_specs=(att_in + state_specs + _mlstm_specs(T, lambda j: j, True)
                  + _mlstm_specs(T, lambda j: nc - 1 - j, True)),
        out_specs=out_specs,
        out_shape=[jax.ShapeDtypeStruct((B, ATT_Q_W, L), BF16)] + [jax.ShapeDtypeStruct((B, ML_V_W, L), BF16)] * 2,
        scratch_shapes=[pltpu.VMEM(s, F32) for s in _STATE_SHAPES],
        compiler_params=_cparams(("arbitrary", "arbitrary")),
        name="mixer",
    )(sink, q, kx_c, kx, kx, kx, vt_c, vt, vt, vt, *state, mqt, mk, mvt, g, bc, br, mqt, mk, mvt, g, bc, br)


def _merge_kernel(att_ref, hf_ref, hb_ref, so_ref, sga_ref, sgm_ref, x_ref, g1_ref, mod2_ref, mlw_ref, n2w_ref,
                  wa_ref, wm_ref, wo_ref, xmid_ref, h2_ref, y_ref):
    s = pl.program_id(0)
    last = pl.num_programs(0) - 1
    cur = s % 2

    def body(do_branch, do_out):
        if do_out:
            y2 = _dot_tn(y_ref[1 - cur], wo_ref[...])
        if do_branch:
            ya = _dot_tn(wa_ref[...], att_ref[0])
            ht = hf_ref[0].astype(F32) + hb_ref[0].astype(F32)
            parts = []
            for h in range(ML_HEADS):
                seg = ht[h * ML_V_DIM:(h + 1) * ML_V_DIM, :]
                ms = jnp.mean(seg * seg, axis=0, keepdims=True)
                parts.append(seg * lax.rsqrt(ms + EPS))
            ml = (jnp.concatenate(parts, axis=0) * mlw_ref[...] * so_ref[0].astype(F32)).astype(BF16)
            ym = _dot_tn(wm_ref[...], ml)
        if do_out:
            xm = x_ref[0] + g1_ref[0, 0] * y2
            xmid_ref[0] = xm
            ms = jnp.mean(xm * xm, axis=-1, keepdims=True)
            h2 = xm * lax.rsqrt(ms + EPS) * n2w_ref[...]
            h2_ref[0] = (h2 * (1.0 + mod2_ref[1, 0]) + mod2_ref[0, 0]).astype(BF16)
        if do_branch:
            y_ref[cur] = (sga_ref[0].astype(F32) * ya + sgm_ref[0].astype(F32) * ym).astype(BF16)

    pl.when(s == 0)(lambda: body(True, False))
    pl.when((s > 0) & (s < last))(lambda: body(True, True))
    pl.when(s == last)(lambda: body(False, True))


def _merge_call(att_t, hf_t, hb_t, so_t, sga_t, sgm_t, x, mod6, mlw_b, n2w, wa, wm, wo, tm):
    B, L, _ = x.shape
    nt = L // tm
    n_all = B * nt

    def tile_in(s):
        t = jnp.minimum(s, n_all - 1)
        return t // nt, t % nt

    def tile_out(s):
        t = jnp.maximum(s - 1, 0)
        return t // nt, t % nt

    def in_t(h):
        return pl.BlockSpec((1, h, tm), lambda s: (tile_in(s)[0], 0, tile_in(s)[1]))

    def out_rows(w):
        return pl.BlockSpec((1, tm, w), lambda s: (*tile_out(s), 0))

    def const(shape):
        return pl.BlockSpec(shape, lambda s: (0,) * len(shape))

    return pl.pallas_call(
        _merge_kernel,
        grid=(n_all + 1,),
        in_specs=[in_t(D), in_t(D), in_t(D), in_t(D), in_t(D), in_t(D), out_rows(D),
                  _mod_spec(MOD_G1, 1, lambda s: tile_out(s)[0]),
                  _mod_spec(MOD_SH2, 2, lambda s: tile_out(s)[0]),
                  const((D, tm)), const((1, D)), const((D, D)), const((D, D)), const((D, D))],
        out_specs=[out_rows(D), out_rows(D)],
        out_shape=[jax.ShapeDtypeStruct((B, L, D), F32), jax.ShapeDtypeStruct((B, L, D), BF16)],
        scratch_shapes=[pltpu.VMEM((2, D, tm), BF16)],
        compiler_params=_cparams(("arbitrary",)),
        name="merge",
    )(att_t, hf_t, hb_t, so_t, sga_t, sgm_t, x, mod6, mod6, mlw_b, n2w, wa, wm, wo)


HALO = 16
FFN_AHEAD = 1


def _ffn_kernel(h_ref, hp_ref, hn_ref, xmid_ref, mod_ref, wup_ref, cw_ref, cb_ref, wdn_ref, o_ref,
                act_ref, *, n_tiles, tn, dn):
    s = pl.program_id(0)
    last = pl.num_programs(0) - 1

    @pl.when(s == 0)
    def _():
        _ffn_body(h_ref, hp_ref, hn_ref, xmid_ref, mod_ref, wup_ref, cw_ref, cb_ref, wdn_ref, o_ref, act_ref,
                  n_tiles=n_tiles, tn=tn, dn=dn, do_up=True, do_down=False)

    @pl.when((s > 0) & (s < last))
    def _():
        _ffn_body(h_ref, hp_ref, hn_ref, xmid_ref, mod_ref, wup_ref, cw_ref, cb_ref, wdn_ref, o_ref, act_ref,
                  n_tiles=n_tiles, tn=tn, dn=dn, do_up=True, do_down=True)

    @pl.when(s == last)
    def _():
        _ffn_body(h_ref, hp_ref, hn_ref, xmid_ref, mod_ref, wup_ref, cw_ref, cb_ref, wdn_ref, o_ref, act_ref,
                  n_tiles=n_tiles, tn=tn, dn=dn, do_up=False, do_down=True)


def _ffn_body(h_ref, hp_ref, hn_ref, xmid_ref, mod_ref, wup_ref, cw_ref, cb_ref, wdn_ref, o_ref, act_ref,
              *, n_tiles, tn, dn, do_up, do_down):
    s = pl.program_id(0)
    i = s % n_tiles
    cur = s % 2
    tm = h_ref.shape[1]
    n_chunks = D_FF // tn
    n_dn = D // dn
    act_prev = act_ref[1 - cur] if do_down else None

    def down(k):
        cols = slice(k * dn, (k + 1) * dn)
        o_ref[0, :, cols] = xmid_ref[0, :, cols] + mod_ref[0, 0, :, cols] * _dot(act_prev, wdn_ref[:, cols])

    if not do_up:
        for k in range(n_dn):
            down(k)
        return

    h = h_ref[0]
    prev_row = jnp.where(i > 0, hp_ref[0].astype(F32)[HALO - 1:HALO, :], 0.0)
    next_row = jnp.where(i < n_tiles - 1, hn_ref[0].astype(F32)[0:1, :], 0.0)
    top = lax.broadcasted_iota(jnp.int32, (16, D), 0) < 8
    edge = jnp.where(top, prev_row, next_row).astype(BF16)
    row8 = lax.broadcasted_iota(jnp.int32, (8, tn), 0)
    h_ext = jnp.concatenate([h, edge], axis=0)

    def up(c0):
        u_ext = _dot(h_ext, wup_ref[:, c0:c0 + tn])
        return u_ext[:tm], u_ext[tm:]

    def conv(u, ue, c0):
        below = pltpu.roll(u, 1, 0)
        above = pltpu.roll(u, tm - 1, 0)
        below = jnp.concatenate([jnp.where(row8 == 0, ue[0:8], below[0:8]), below[8:]], axis=0)
        above = jnp.concatenate([above[:tm - 8], jnp.where(row8 == 7, ue[8:16], above[tm - 8:])], axis=0)
        cw = cw_ref[:, c0:c0 + tn]
        return cb_ref[:, c0:c0 + tn] + below * cw[0:1] + u * cw[1:2] + above * cw[2:3]

    pending = [(up(c * tn), up(D_FF + c * tn)) for c in range(FFN_AHEAD)]
    done = 0
    for c in range(n_chunks):
        (ua, uae), (ug, uge) = pending.pop(0)
        if c + FFN_AHEAD < n_chunks:
            pending.append((up((c + FFN_AHEAD) * tn), up(D_FF + (c + FFN_AHEAD) * tn)))
        while do_down and done * n_chunks < (c + 1) * n_dn:
            down(done)
            done += 1
        a = conv(ua, uae, c * tn)
        hg = conv(ug, uge, D_FF + c * tn)
        act_ref[cur, :, c * tn:(c + 1) * tn] = ((hg + hg * jnp.tanh(hg)) * a).astype(BF16)


def _ffn_call(h2, xmid, g2, wup, cw, cb, wdn, tm, tn, dn):
    B, L, _ = xmid.shape
    nt = L // tm
    n_all = B * nt
    hb = tm // HALO
    nhb = L // HALO

    def tile_in(s):
        t = jnp.minimum(s, n_all - 1)
        return t // nt, t % nt

    def tile_out(s):
        t = jnp.maximum(s - 1, 0)
        return t // nt, t % nt

    def in_spec():
        return pl.BlockSpec((1, tm, D), lambda s: (*tile_in(s), 0))

    def out_spec():
        return pl.BlockSpec((1, tm, D), lambda s: (*tile_out(s), 0))

    def prev_halo(s):
        b, i = tile_in(s)
        return b, jnp.maximum(i * hb - 1, 0), 0

    def next_halo(s):
        b, i = tile_in(s)
        return b, jnp.minimum((i + 1) * hb, nhb - 1), 0

    def const(shape):
        return pl.BlockSpec(shape, lambda s: (0,) * len(shape))

    return pl.pallas_call(
        functools.partial(_ffn_kernel, n_tiles=nt, tn=tn, dn=dn),
        grid=(n_all + 1,),
        in_specs=[in_spec(),
                  pl.BlockSpec((1, HALO, D), prev_halo),
                  pl.BlockSpec((1, HALO, D), next_halo),
                  out_spec(),
                  _mod_spec(MOD_G2, 1, lambda s: tile_out(s)[0]),
                  const(wup.shape), const(cw.shape), const(cb.shape), const(wdn.shape)],
        out_specs=out_spec(),
        out_shape=jax.ShapeDtypeStruct((B, L, D), F32),
        scratch_shapes=[pltpu.VMEM((2, tm, D_FF), BF16)],
        compiler_params=_cparams(("arbitrary",)),
        name="ffn",
    )(h2, h2, h2, xmid, g2, wup, cw, cb, wdn)


def _pair_perm(n_heads):
    half = ATT_HEAD_DIM // 2
    idx = []
    for p in range(n_heads // 2):
        for sub in range(4):
            head = 2 * p + (sub % 2)
            d0 = (sub // 2) * half
            idx.extend(head * ATT_HEAD_DIM + d0 + e for e in range(half))
    return np.asarray(idx, np.int32)


def _rope_tables(L):
    f32 = np.float32
    rows = L // GRID_W
    row = np.repeat(np.arange(rows, dtype=f32), GRID_W)
    col = np.tile(np.arange(GRID_W, dtype=f32), rows)
    n_freq = ATT_HEAD_DIM // 4
    inv_freq = (f32(ROPE_BASE) ** (-np.arange(n_freq, dtype=f32) / f32(n_freq))).astype(f32)
    ang = np.concatenate([row[:, None] * inv_freq, col[:, None] * inv_freq], axis=-1).astype(f32)
    cos = np.tile(np.cos(ang).astype(f32), (1, 4))
    sin = np.tile(np.sin(ang).astype(f32), (1, 4))
    sign = np.where(np.arange(LANES) < LANES // 2, -1.0, 1.0).astype(f32)
    return jnp.asarray(cos), jnp.asarray(sin * sign)


def kernel(x, c, ctx, c_ctx, w_mod, b_mod, norm1_w, w_in, q_norm_w, k_norm_w, attn_sink, ml_gate_b, ml_norm_w,
           w_branch_att, w_branch_ml, w_out, norm2_w, w_up, conv_w, conv_b, w_down):
    B, L, _ = x.shape
    C = ctx.shape[1]
    assert L % GRID_W == 0 and C % TM_CTX == 0
    assert all(L % t == 0 for t in (TM_INPROJ, TM_MIXER, TM_MERGE, TM_FFN))
    l = 0

    n_rows = -(-(B + 1) // 16) * 16
    cc = jnp.concatenate([c, c_ctx[None, :], jnp.zeros((n_rows - B - 1, D), F32)], axis=0)
    mod6 = _mod_call(cc, w_mod[l], b_mod[l][None, :])

    w = w_in[l]
    qperm = _pair_perm(ATT_HEADS)
    kperm = _pair_perm(ATT_KV_HEADS)
    def pair_cols(wc, n_heads):
        half = ATT_HEAD_DIM // 2
        wc = wc.reshape(D, n_heads // 2, 2, 2, half).transpose(0, 1, 3, 2, 4)
        return wc.reshape(D, n_heads * ATT_HEAD_DIM)

    w_q = pair_cols(w[:, _O_AQ:_O_AQ + ATT_Q_W], ATT_HEADS)
    w_k = pair_cols(w[:, _O_AK:_O_AK + ATT_KV_W], ATT_KV_HEADS)
    w_g = jnp.pad(w[:, _O_MG:_O_MG + ML_GATE_W], ((0, 0), (0, LANES - ML_GATE_W)))
    w_mk = w[:, _O_MK:_O_MK + ML_QK_W] * (ML_QK_DIM ** -0.5)
    w_p = jnp.concatenate([w_q, w_k, w_mk, w_g], axis=1).astype(BF16)
    w_t = w[:, _O_AV:].T.astype(BF16)

    head_of_col = np.concatenate([qperm // ATT_HEAD_DIM, ATT_HEADS + kperm // ATT_HEAD_DIM])
    e_np = (head_of_col[:, None] == np.arange(LANES)[None, :]).astype(np.float32)
    e_mat = jnp.asarray(e_np, BF16)
    et_mat = jnp.asarray(np.concatenate([e_np.T, e_np.T], axis=0), BF16)
    def pair_tiled(wn, n_heads):
        half = ATT_HEAD_DIM // 2
        return jnp.tile(jnp.concatenate([wn[:half], wn[:half], wn[half:], wn[half:]]), n_heads // 2)

    qkw = jnp.concatenate([pair_tiled(q_norm_w[l], ATT_HEADS) * (ATT_SCALE * LOG2E),
                           pair_tiled(k_norm_w[l], ATT_KV_HEADS)])[None, :]
    cos_t, sin_t = _rope_tables(L)
    gb = ml_gate_b[l].reshape(1, ML_GATE_W)
    gbt = ml_gate_b[l].reshape(ML_GATE_W, 1)
    n1w = norm1_w[l][None, :]

    kx_c, mk_c, g_c, bc_c, vt_c, mvt_c, br_c = _ctxproj_call(
        ctx, mod6, B, n1w, w_p, w_t, e_mat, et_mat, qkw, gb, gbt, tm=TM_CTX)
    q, kx, mk, g, bc, vt, mqt, mvt, sot, sgat, sgmt, br = _inproj_call(
        x, mod6, n1w, w_p, w_t, e_mat, et_mat, qkw, cos_t, sin_t, gb, gbt, tm=TM_INPROJ)

    state = _mlstm_ctx_call(mk_c, mvt_c, g_c, bc_c, br_c)
    att_t, hf_t, hb_t = _mixer_call(attn_sink[l], q, kx, vt, kx_c, vt_c, state, mqt, mk, mvt, g, bc, br)

    mlw_b = jnp.broadcast_to(ml_norm_w[l][:, None], (ML_V_W, TM_MERGE))
    xmid, h2 = _merge_call(att_t, hf_t, hb_t, sot, sgat, sgmt, x, mod6, mlw_b, norm2_w[l][None, :],
                           w_branch_att[l].astype(BF16), w_branch_ml[l].astype(BF16), w_out[l].astype(BF16),
                           tm=TM_MERGE)
    gate_half = jnp.where(jnp.arange(2 * D_FF) < D_FF, 1.0, 0.5).astype(F32)
    out = _ffn_call(h2, xmid, mod6, w_up[l].astype(BF16), conv_w[l] * gate_half, (conv_b[l] * gate_half)[None, :],
                    w_down[l].astype(BF16), tm=TM_FFN, tn=FFN_UP_COLS, dn=FFN_DOWN_COLS)
    return out
```

```python
import functools

import jax
import jax.numpy as jnp
import numpy as np
from jax import lax
from jax.experimental import pallas as pl
from jax.experimental.pallas import tpu as pltpu

D = 1024
GRID_W = 64
ATT_HEADS = 16
ATT_KV_HEADS = 4
ATT_HEAD_DIM = 64
ATT_GROUP = ATT_HEADS // ATT_KV_HEADS
ATT_BLOCK = 128
WINDOW = 128
ROPE_BASE = 10000.0
ATT_SCALE = ATT_HEAD_DIM ** -0.5
LOG2E = 1.4426950408889634
ML_HEADS = 4
ML_QK_DIM = 128
ML_V_DIM = 256
ML_CHUNK = 128
D_FF = 2816
EPS = 1e-6
NEG_INF = -1e30

ATT_Q_W = ATT_HEADS * ATT_HEAD_DIM
ATT_KV_W = ATT_KV_HEADS * ATT_HEAD_DIM
ML_QK_W = ML_HEADS * ML_QK_DIM
ML_V_W = ML_HEADS * ML_V_DIM
ML_GATE_W = 2 * 2 * ML_HEADS

LANES = 128
KX_W = ATT_KV_HEADS * 2 * LANES
VMEM_LIMIT = 56 * 1024 * 1024

TM_CTX = 256
TM_INPROJ = 512
TM_MIXER = 512
TM_MERGE = 512
TM_FFN = 512
FFN_UP_COLS = 256
FFN_DOWN_COLS = 256

BF16 = jnp.bfloat16
F32 = jnp.float32

_O_AQ = 0
_O_AK = _O_AQ + ATT_Q_W
_O_AV = _O_AK + ATT_KV_W
_O_MQ = _O_AV + ATT_KV_W
_O_MK = _O_MQ + ML_QK_W
_O_MV = _O_MK + ML_QK_W
_O_MO = _O_MV + ML_V_W
_O_MG = _O_MO + ML_V_W
_O_GA = _O_MG + ML_GATE_W
_O_GM = _O_GA + D

QK_W = ATT_Q_W + ATT_KV_W
_P_QK = 0
_P_MK = _P_QK + QK_W
_P_MG = _P_MK + ML_QK_W
_P_END = _P_MG + LANES
_R_V = 0
_R_MQ = _O_MQ - _O_AV
_R_MV = _O_MV - _O_AV
_R_MO = _O_MO - _O_AV
_R_MG = _O_MG - _O_AV
_R_GA = _O_GA - _O_AV
_R_GM = _O_GM - _O_AV


def _dot(a, b):
    return jnp.dot(a, b, preferred_element_type=F32)


def _dot_nt(a, b):
    return lax.dot_general(a, b, (((1,), (1,)), ((), ())), preferred_element_type=F32)


def _dot_tn(a, b):
    return lax.dot_general(a, b, (((0,), (0,)), ((), ())), preferred_element_type=F32)


def _cparams(sem):
    return pltpu.CompilerParams(dimension_semantics=sem, vmem_limit_bytes=VMEM_LIMIT)


def _mod_kernel(c_ref, w_ref, b_ref, o_ref):
    c = c_ref[...]
    a = c * jax.nn.sigmoid(c)
    a_hi = a.astype(BF16)
    a_lo = (a - a_hi.astype(F32)).astype(BF16)
    w = w_ref[...]
    w_hi = w.astype(BF16)
    w_lo = (w - w_hi.astype(F32)).astype(BF16)
    rows = a.shape[0]
    both = _dot(jnp.concatenate([a_hi, a_lo], axis=0), w_hi)
    o_ref[0, :, 0, :] = both[:rows] + both[rows:] + _dot(a_hi, w_lo) + b_ref[...]


MOD_SH1, MOD_SC1, MOD_SH2, MOD_SC2, MOD_G1, MOD_G2 = range(6)


def _mod_call(cc, w_mod, b_mod):
    rows = cc.shape[0]
    n_seg = w_mod.shape[1] // D
    assert n_seg == 6

    def out_pos(j):
        return jnp.where(j == 2, MOD_G1, jnp.where((j == 3) | (j == 4), j - 1, j))

    return pl.pallas_call(
        _mod_kernel,
        grid=(n_seg,),
        in_specs=[pl.BlockSpec((rows, D), lambda j: (0, 0)),
                  pl.BlockSpec((D, D), lambda j: (0, j)),
                  pl.BlockSpec((1, D), lambda j: (0, j))],
        out_specs=pl.BlockSpec((1, rows, 1, D), lambda j: (out_pos(j), 0, 0, 0)),
        out_shape=jax.ShapeDtypeStruct((n_seg, rows, 1, D), F32),
        compiler_params=_cparams(("arbitrary",)),
        name="mod",
    )(cc, w_mod, b_mod)


def _mod_spec(seg, n_seg, row_of):
    return pl.BlockSpec((n_seg, 1, 1, D), lambda *ids: (seg // n_seg, row_of(*ids), 0, 0))


def _split2(x):
    x1 = x.astype(BF16)
    x2 = (x - x1.astype(F32)).astype(BF16)
    return x1, x2


def _norm_modulate(x, n1w_ref, mod_ref):
    ms = jnp.mean(x * x, axis=-1, keepdims=True)
    y = x * lax.rsqrt(ms + EPS) * n1w_ref[...]
    return (y * (1.0 + mod_ref[1, 0]) + mod_ref[0, 0]).astype(BF16)


def _sigmoid(x):
    return 0.5 * jnp.tanh(0.5 * x) + 0.5


def _head_rms_scale(ss):
    r = lax.rsqrt(ss * (1.0 / ATT_HEAD_DIM) + EPS)
    r_hi = r.astype(BF16)
    r_lo = (r - r_hi.astype(F32)).astype(BF16)
    return jnp.concatenate([r_hi, r_lo], axis=1)


def _store_k_variants(k_ref, pair, o):
    lane = lax.broadcasted_iota(jnp.int32, (1, LANES), 1)
    keep = ((lane // 32) % 2) == 0
    c0 = 4 * pair * LANES
    k_ref[0, :, c0:c0 + LANES] = jnp.where(keep, o, 0.0).astype(BF16)
    k_ref[0, :, c0 + LANES:c0 + 2 * LANES] = jnp.where(keep, 0.0, pltpu.roll(o, 32, 1)).astype(BF16)
    k_ref[0, :, c0 + 2 * LANES:c0 + 3 * LANES] = jnp.where(keep, pltpu.roll(o, 96, 1), 0.0).astype(BF16)
    k_ref[0, :, c0 + 3 * LANES:c0 + 4 * LANES] = jnp.where(keep, 0.0, o).astype(BF16)


def _cum_gates_cols(g16, tri_col):
    T = ML_CHUNK
    lf = jax.nn.log_sigmoid(g16)
    fwd_col = lax.broadcasted_iota(jnp.int32, (1, ML_GATE_W), 1) < ML_GATE_W // 2
    out = []
    for c in range(g16.shape[0] // T):
        r = sum(_dot(tri_col, p) for p in _split2(lf[c * T:(c + 1) * T]))
        out.append(jnp.where(fwd_col, r[:T], r[T:]))
    return jnp.concatenate(out, axis=0)


def _cum_gates_rows(gt16, tri_row):
    T = ML_CHUNK
    lf = jax.nn.log_sigmoid(gt16)
    fwd_row = lax.broadcasted_iota(jnp.int32, (ML_GATE_W, 1), 0) < ML_GATE_W // 2
    out = []
    for c in range(gt16.shape[1] // T):
        r = _dot(jnp.concatenate(_split2(lf[:, c * T:(c + 1) * T]), axis=0), tri_row)
        s = r[:ML_GATE_W] + r[ML_GATE_W:]
        out.append(jnp.where(fwd_row, s[:, :T], s[:, T:]))
    return jnp.concatenate(out, axis=1)


def _inproj_kernel(x_ref, mod_ref, n1w_ref, w_ref, wt_ref, e_ref, et_ref, qkw_ref,
                   cos_ref, sin_ref, gb_ref, gbt_ref, tricol_ref, trirow_ref,
                   q_ref, k_ref, mk_ref, g_ref, bc_ref, vt_ref, mqt_ref, mvt_ref, sot_ref, sgat_ref, sgmt_ref, br_ref,
                   hn_ref):
    hn_ref[...] = _norm_modulate(x_ref[0], n1w_ref, mod_ref)
    hn = hn_ref[...]

    def ft(r0, height):
        return _dot_nt(wt_ref[r0:r0 + height, :], hn)

    acc = _dot(hn, w_ref[:, _P_QK:_P_QK + QK_W])
    mkg = _dot(hn, w_ref[:, _P_MK:_P_END])
    g16 = mkg[:, ML_QK_W:ML_QK_W + ML_GATE_W] + gb_ref[...]
    g_ref[0] = g16
    mk_ref[0] = mkg[:, :ML_QK_W].astype(BF16)
    ss = _dot((acc * acc).astype(BF16), e_ref[...])
    vt_ref[0] = ft(_R_V, ATT_KV_W).astype(BF16)
    mog = ft(_R_MO, ML_V_W + ML_GATE_W)
    sot_ref[0] = _sigmoid(mog[:ML_V_W]).astype(BF16)
    gt16 = mog[ML_V_W:] + gbt_ref[...]
    rb = _dot(_head_rms_scale(ss), et_ref[...])
    sgat_ref[0] = _sigmoid(ft(_R_GA, D)).astype(BF16)
    bc_ref[0] = _cum_gates_cols(g16, tricol_ref[...])

    qn = acc * rb * qkw_ref[...]
    cos = cos_ref[...]
    sin = sin_ref[...]
    for gi in range(QK_W // LANES):
        xs = qn[:, gi * LANES:(gi + 1) * LANES]
        o = xs * cos + pltpu.roll(xs, LANES // 2, 1) * sin
        if gi < ATT_Q_W // LANES:
            q_ref[0, :, gi * LANES:(gi + 1) * LANES] = o.astype(BF16)
        else:
            _store_k_variants(k_ref, gi - ATT_Q_W // LANES, o)

    sgmt_ref[0] = _sigmoid(ft(_R_GM, D)).astype(BF16)
    br_ref[0] = _cum_gates_rows(gt16, trirow_ref[...])
    mqt_ref[0] = ft(_R_MQ, ML_QK_W).astype(BF16)
    mvt_ref[0] = ft(_R_MV, ML_V_W).astype(BF16)


def _ctxproj_kernel(x_ref, mod_ref, n1w_ref, w_ref, wt_ref, e_ref, et_ref, qkw_ref,
                    gb_ref, gbt_ref, tricol_ref, trirow_ref,
                    k_ref, mk_ref, g_ref, bc_ref, vt_ref, mvt_ref, br_ref):
    hn = _norm_modulate(x_ref[0], n1w_ref, mod_ref)
    acc = _dot(hn, w_ref[:, _P_QK + ATT_Q_W:_P_QK + QK_W])
    mkg = _dot(hn, w_ref[:, _P_MK:_P_END])
    g16 = mkg[:, ML_QK_W:ML_QK_W + ML_GATE_W] + gb_ref[...]
    gt16 = _dot_nt(wt_ref[_R_MG:_R_MG + ML_GATE_W, :], hn) + gbt_ref[...]
    g_ref[0] = g16
    mk_ref[0] = mkg[:, :ML_QK_W].astype(BF16)
    ss = _dot((acc * acc).astype(BF16), e_ref[ATT_Q_W:QK_W, :])
    vt_ref[0] = _dot_nt(wt_ref[_R_V:_R_V + ATT_KV_W, :], hn).astype(BF16)
    rb = _dot(_head_rms_scale(ss), et_ref[:, ATT_Q_W:QK_W])
    mvt_ref[0] = _dot_nt(wt_ref[_R_MV:_R_MV + ML_V_W, :], hn).astype(BF16)
    bc_ref[0] = _cum_gates_cols(g16, tricol_ref[...])
    kn = acc * rb * qkw_ref[:, ATT_Q_W:QK_W]
    for pair in range(ATT_KV_W // LANES):
        _store_k_variants(k_ref, pair, kn[:, pair * LANES:(pair + 1) * LANES])
    br_ref[0] = _cum_gates_rows(gt16, trirow_ref[...])


def _chunk_tri():
    r = np.arange(ML_CHUNK)[:, None]
    c = np.arange(ML_CHUNK)[None, :]
    lower, upper = (c <= r), (c >= r)
    return (jnp.asarray(np.concatenate([lower, upper], axis=0), BF16),
            jnp.asarray(np.concatenate([upper, lower], axis=1), BF16))


def _const_spec(shape):
    return pl.BlockSpec(shape, lambda b, i: (0,) * len(shape))


def _inproj_call(x, mod6, n1w, w_p, w_t, e_mat, et_mat, qkw, cos_t, sin_t, gb, gbt, tm):
    B, L, _ = x.shape
    tril, triu = _chunk_tri()

    def rows(w):
        return pl.BlockSpec((1, tm, w), lambda b, i: (b, i, 0))

    def cols(h):
        return pl.BlockSpec((1, h, tm), lambda b, i: (b, 0, i))

    consts = [n1w, w_p, w_t, e_mat, et_mat, qkw]
    tail = [gb, gbt, tril, triu]
    in_specs = ([rows(D), _mod_spec(MOD_SH1, 2, lambda b, i: b)]
                + [_const_spec(a.shape) for a in consts]
                + [pl.BlockSpec((tm, LANES), lambda b, i: (i, 0))] * 2
                + [_const_spec(a.shape) for a in tail])
    out_specs = [rows(ATT_Q_W), rows(KX_W), rows(ML_QK_W), rows(ML_GATE_W), rows(ML_GATE_W),
                 cols(ATT_KV_W), cols(ML_QK_W), cols(ML_V_W), cols(ML_V_W), cols(D), cols(D), cols(ML_GATE_W)]
    out_shape = [
        jax.ShapeDtypeStruct((B, L, ATT_Q_W), BF16),
        jax.ShapeDtypeStruct((B, L, KX_W), BF16),
        jax.ShapeDtypeStruct((B, L, ML_QK_W), BF16),
        jax.ShapeDtypeStruct((B, L, ML_GATE_W), F32),
        jax.ShapeDtypeStruct((B, L, ML_GATE_W), F32),
        jax.ShapeDtypeStruct((B, ATT_KV_W, L), BF16),
        jax.ShapeDtypeStruct((B, ML_QK_W, L), BF16),
        jax.ShapeDtypeStruct((B, ML_V_W, L), BF16),
        jax.ShapeDtypeStruct((B, ML_V_W, L), BF16),
        jax.ShapeDtypeStruct((B, D, L), BF16),
        jax.ShapeDtypeStruct((B, D, L), BF16),
        jax.ShapeDtypeStruct((B, ML_GATE_W, L), F32),
    ]
    return pl.pallas_call(
        _inproj_kernel,
        grid=(B, L // tm),
        in_specs=in_specs,
        out_specs=out_specs,
        out_shape=out_shape,
        scratch_shapes=[pltpu.VMEM((tm, D), BF16)],
        compiler_params=_cparams(("arbitrary", "arbitrary")),
        name="inproj",
    )(x, mod6, *consts, cos_t, sin_t, *tail)


def _ctxproj_call(ctx, mod6, ctx_row, n1w, w_p, w_t, e_mat, et_mat, qkw, gb, gbt, tm):
    B, C, _ = ctx.shape
    tril, triu = _chunk_tri()

    def rows(w):
        return pl.BlockSpec((1, tm, w), lambda b, i: (b, i, 0))

    def cols(h):
        return pl.BlockSpec((1, h, tm), lambda b, i: (b, 0, i))

    consts = [n1w, w_p, w_t, e_mat, et_mat, qkw, gb, gbt, tril, triu]
    out_specs = [rows(KX_W), rows(ML_QK_W), rows(ML_GATE_W), rows(ML_GATE_W),
                 cols(ATT_KV_W), cols(ML_V_W), cols(ML_GATE_W)]
    out_shape = [
        jax.ShapeDtypeStruct((B, C, KX_W), BF16),
        jax.ShapeDtypeStruct((B, C, ML_QK_W), BF16),
        jax.ShapeDtypeStruct((B, C, ML_GATE_W), F32),
        jax.ShapeDtypeStruct((B, C, ML_GATE_W), F32),
        jax.ShapeDtypeStruct((B, ATT_KV_W, C), BF16),
        jax.ShapeDtypeStruct((B, ML_V_W, C), BF16),
        jax.ShapeDtypeStruct((B, ML_GATE_W, C), F32),
    ]
    return pl.pallas_call(
        _ctxproj_kernel,
        grid=(B, C // tm),
        in_specs=([rows(D), _mod_spec(MOD_SH1, 2, lambda b, i: ctx_row)]
                  + [_const_spec(a.shape) for a in consts]),
        out_specs=out_specs,
        out_shape=out_shape,
        compiler_params=_cparams(("arbitrary", "arbitrary")),
        name="ctxproj",
    )(ctx, mod6, *consts)


ATT_AHEAD = 12

ATT_QB = 4


def _attn_stream(sink_ref, q_ref, kc_ref, kp_ref, k0_ref, kn_ref, vc_ref, vp_ref, v0_ref, vn_ref, o_ref,
                 n_steps):
    i = pl.program_id(1)
    T = ATT_BLOCK
    hd = ATT_HEAD_DIM
    s_idx = lax.broadcasted_iota(jnp.int32, (T, 2 * T), 0)
    t_idx = lax.broadcasted_iota(jnp.int32, (T, 2 * T), 1) % T
    first = lax.broadcasted_iota(jnp.int32, (1, 2 * T), 1) < T

    k_own = k0_ref[0]
    v_own = v0_ref[0]
    k_blk = [kp_ref[0]] + [k_own[b * T:(b + 1) * T] for b in range(ATT_QB)] + [kn_ref[0]]
    v_blk = [vp_ref[0]] + [v_own[:, b * T:(b + 1) * T] for b in range(ATT_QB)] + [vn_ref[0]]
    ones_rows = jnp.ones((16, 3 * T + kc_ref.shape[1]), BF16)

    def window(qb):
        k_all = jnp.concatenate(k_blk[qb:qb + 3] + [kc_ref[0]], axis=0)
        vt_all = jnp.concatenate(v_blk[qb:qb + 3] + [vc_ref[0]], axis=1)
        ok_prev = (s_idx >= t_idx) & ((i > 0) if qb == 0 else True)
        ok_next = (s_idx <= t_idx) & ((i < n_steps - 1) if qb == ATT_QB - 1 else True)
        return k_all, vt_all, ok_prev, ok_next

    windows = [window(qb) for qb in range(ATT_QB)]
    per_qb = 2 * ATT_KV_HEADS

    def scores(n):
        qb, r = divmod(n, per_qb)
        kh, var = divmod(r, 2)
        q = q_ref[0, qb * T:(qb + 1) * T, :]
        q2 = jnp.concatenate([q[:, (2 * kh) * LANES:(2 * kh + 1) * LANES],
                              q[:, (2 * kh + 1) * LANES:(2 * kh + 2) * LANES]], axis=0)
        kk = windows[qb][0][:, (2 * kh + var) * LANES:(2 * kh + var + 1) * LANES]
        return _dot_nt(kk, q2)

    def finish(n, st):
        qb, r = divmod(n, per_qb)
        kh, var = divmod(r, 2)
        _, vt_all, ok_prev, ok_next = windows[qb]
        vt = vt_all[kh * hd:(kh + 1) * hd, :]
        st = jnp.concatenate([jnp.where(ok_prev, st[0:T], NEG_INF), st[T:2 * T],
                              jnp.where(ok_next, st[2 * T:3 * T], NEG_INF), st[3 * T:]], axis=0)
        h0 = ATT_GROUP * kh + var
        h1 = h0 + 2
        sink = jnp.where(first, sink_ref[h0], sink_ref[h1]) * LOG2E
        m = jnp.maximum(jnp.max(st, axis=0, keepdims=True), sink)
        p = jnp.exp2(st - m)
        ot = _dot(jnp.concatenate([vt, ones_rows], axis=0), p.astype(BF16))
        denom = ot[hd:hd + 1, :] + jnp.exp2(sink - m)
        ot = ot[0:hd, :] * (1.0 / denom)
        o_ref[0, h0 * hd:(h0 + 1) * hd, qb * T:(qb + 1) * T] = ot[:, 0:T].astype(BF16)
        o_ref[0, h1 * hd:(h1 + 1) * hd, qb * T:(qb + 1) * T] = ot[:, T:2 * T].astype(BF16)

    return ATT_QB * per_qb, scores, finish


def _attn_specs(L, C):
    T = ATT_BLOCK
    nb = L // T
    TQ = ATT_QB * T

    def edge(i, off):
        return jnp.clip(i * ATT_QB + (off if off < 0 else ATT_QB), 0, nb - 1)

    in_specs = [pl.BlockSpec(memory_space=pltpu.SMEM),
                pl.BlockSpec((1, TQ, ATT_Q_W), lambda b, i: (b, i, 0)),
                pl.BlockSpec((1, C, KX_W), lambda b, i: (b, 0, 0)),
                pl.BlockSpec((1, T, KX_W), lambda b, i: (b, edge(i, -1), 0)),
                pl.BlockSpec((1, TQ, KX_W), lambda b, i: (b, i, 0)),
                pl.BlockSpec((1, T, KX_W), lambda b, i: (b, edge(i, 1), 0)),
                pl.BlockSpec((1, ATT_KV_W, C), lambda b, i: (b, 0, 0)),
                pl.BlockSpec((1, ATT_KV_W, T), lambda b, i: (b, 0, edge(i, -1))),
                pl.BlockSpec((1, ATT_KV_W, TQ), lambda b, i: (b, 0, i)),
                pl.BlockSpec((1, ATT_KV_W, T), lambda b, i: (b, 0, edge(i, 1)))]
    return in_specs, pl.BlockSpec((1, ATT_Q_W, TQ), lambda b, i: (b, 0, i))


N_CHAIN = 2 * ML_HEADS


def _mlstm_load_state(c_ref, n_ref, m_ref):
    return [(c_ref[ci], n_ref[ci], m_ref[ci, 0:1, 0:1]) for ci in range(N_CHAIN)]


def _mlstm_phase1(dirs, state, item, with_h, cps):
    T = ML_CHUNK
    sub, rest = divmod(item, N_CHAIN)
    d, h = divmod(rest, ML_HEADS)
    qt_ref, k_ref, vt_ref, g_ref, bc_ref, br_ref, h_ref = dirs[d]
    row = lax.broadcasted_iota(jnp.int32, (T, T), 0)
    col = lax.broadcasted_iota(jnp.int32, (T, T), 1)
    mask = (col >= row) if d == 0 else (col <= row)
    last = T - 1 if d == 0 else 0
    sc = sub if d == 0 else cps - 1 - sub
    tok = slice(sc * T, (sc + 1) * T)
    ci = d * ML_HEADS + h
    gi = d * 2 * ML_HEADS + h
    fi = gi + ML_HEADS
    k = k_ref[0, tok, h * ML_QK_DIM:(h + 1) * ML_QK_DIM]
    vt = vt_ref[0, h * ML_V_DIM:(h + 1) * ML_V_DIM, tok]
    u_col = g_ref[0, tok, gi:gi + 1] - bc_ref[0, tok, fi:fi + 1]
    b_row = br_ref[0, fi:fi + 1, tok]
    ct_old, n_old, m_old = state[ci]
    qt = st = qn2 = None
    if with_h:
        qt = qt_ref[0, h * ML_QK_DIM:(h + 1) * ML_QK_DIM, tok]
        st = _dot(k, qt)
        top = lax.broadcasted_iota(jnp.int32, (8, ML_QK_DIM), 0) == 0
        n_hi = n_old.astype(BF16)
        n_lo = (n_old - n_hi.astype(F32)).astype(BF16)
        qn2 = _dot(jnp.where(top, n_hi, n_lo), qt)
    m_last = jnp.maximum(jnp.max(u_col, axis=0, keepdims=True), m_old)
    decay = jnp.exp(m_old - m_last)
    kw = (k.astype(F32) * jnp.exp(u_col - m_last)).astype(BF16)
    state[ci] = (decay * ct_old + _dot(vt, kw), decay * n_old + _dot(jnp.ones((8, T), BF16), kw),
                 b_row[:, last:last + 1] + m_last)
    return (h, h_ref, tok, mask, qt, vt, u_col, b_row, m_old, ct_old, st, qn2)


def _mlstm_phase2(chain):
    h, h_ref, tok, mask, qt, vt, u_col, b_row, m_old, ct_old, st, qn2 = chain
    umat = jnp.where(mask, u_col, -jnp.inf)
    m_row = jnp.maximum(jnp.max(umat, axis=0, keepdims=True), m_old)
    pt = st * jnp.exp(umat - m_row)
    w_int = jnp.exp(m_old - m_row)
    e_row = jnp.exp(-(b_row + m_row))
    nq = jnp.sum(pt, axis=0, keepdims=True) + w_int * (qn2[0:1, :] + qn2[1:2, :])
    den = jnp.maximum(jnp.abs(nq), e_row)
    lhs = jnp.concatenate([vt, ct_old.astype(BF16)], axis=1)
    rhs = jnp.concatenate([pt.astype(BF16), (qt.astype(F32) * w_int).astype(BF16)], axis=0)
    h_ref[0, h * ML_V_DIM:(h + 1) * ML_V_DIM, tok] = (_dot(lhs, rhs) * (1.0 / den)).astype(BF16)


def _mlstm_commit(state, c_ref, n_ref, m_ref):
    for ci, (c_new, n_new, m_new) in enumerate(state):
        c_ref[ci] = c_new
        n_ref[ci] = n_new
        m_ref[ci] = jnp.broadcast_to(m_new, (8, LANES))


def _mlstm_ctx_kernel(kf_ref, vtf_ref, gf_ref, bcf_ref, brf_ref, kb_ref, vtb_ref, gb_ref, bcb_ref, brb_ref,
                      c_ref, n_ref, m_ref, *, cps):
    @pl.when(pl.program_id(1) == 0)
    def _():
        c_ref[...] = jnp.zeros_like(c_ref)
        n_ref[...] = jnp.zeros_like(n_ref)
        m_ref[...] = jnp.zeros_like(m_ref)

    dirs = ((None, kf_ref, vtf_ref, gf_ref, bcf_ref, brf_ref, None),
            (None, kb_ref, vtb_ref, gb_ref, bcb_ref, brb_ref, None))
    state_refs = (c_ref.at[0], n_ref.at[0], m_ref.at[0])
    state = _mlstm_load_state(*state_refs)
    for item in range(cps * N_CHAIN):
        _mlstm_phase1(dirs, state, item, with_h=False, cps=cps)
    _mlstm_commit(state, *state_refs)


def _mixer_kernel(sink_ref, q_ref, kc_ref, kp_ref, k0_ref, kn_ref, vc_ref, vp_ref, v0_ref, vn_ref,
                  c0_ref, n0_ref, m0_ref, qtf_ref, kf_ref, vtf_ref, gf_ref, bcf_ref, brf_ref,
                  qtb_ref, kb_ref, vtb_ref, gb_ref, bcb_ref, brb_ref,
                  att_ref, hf_ref, hb_ref, c_ref, n_ref, m_ref, *, n_steps, cps):
    @pl.when(pl.program_id(1) == 0)
    def _():
        c_ref[...] = c0_ref[0]
        n_ref[...] = n0_ref[0]
        m_ref[...] = m0_ref[0]

    n_iter, scores, finish = _attn_stream(sink_ref, q_ref, kc_ref, kp_ref, k0_ref, kn_ref,
                                          vc_ref, vp_ref, v0_ref, vn_ref, att_ref, n_steps)
    pending = [scores(n) for n in range(ATT_AHEAD)]
    dirs = ((qtf_ref, kf_ref, vtf_ref, gf_ref, bcf_ref, brf_ref, hf_ref),
            (qtb_ref, kb_ref, vtb_ref, gb_ref, bcb_ref, brb_ref, hb_ref))
    state = _mlstm_load_state(c_ref, n_ref, m_ref)
    n_items = cps * N_CHAIN
    items = [_mlstm_phase1(dirs, state, it, with_h=True, cps=cps) for it in range(ML_AHEAD)]
    for n in range(max(n_iter, n_items)):
        if n < n_iter:
            st = pending.pop(0)
            if n + ATT_AHEAD < n_iter:
                pending.append(scores(n + ATT_AHEAD))
            finish(n, st)
        if n < n_items:
            if n + ML_AHEAD < n_items:
                items.append(_mlstm_phase1(dirs, state, n + ML_AHEAD, with_h=True, cps=cps))
            _mlstm_phase2(items.pop(0))
    _mlstm_commit(state, c_ref, n_ref, m_ref)


def _mlstm_specs(T, order, with_q):
    specs = [
        pl.BlockSpec((1, ML_QK_W, T), lambda b, j: (b, 0, order(j))),
        pl.BlockSpec((1, T, ML_QK_W), lambda b, j: (b, order(j), 0)),
        pl.BlockSpec((1, ML_V_W, T), lambda b, j: (b, 0, order(j))),
        pl.BlockSpec((1, T, ML_GATE_W), lambda b, j: (b, order(j), 0)),
        pl.BlockSpec((1, T, ML_GATE_W), lambda b, j: (b, order(j), 0)),
        pl.BlockSpec((1, ML_GATE_W, T), lambda b, j: (b, 0, order(j))),
    ]
    return specs if with_q else specs[1:]


_STATE_SHAPES = ((N_CHAIN, ML_V_DIM, ML_QK_DIM), (N_CHAIN, 8, ML_QK_DIM), (N_CHAIN, 8, LANES))


ML_AHEAD = 32
ML_CPS = 4
ML_CTX_CPS = 2


def _mlstm_ctx_call(mk, mvt, g, bc, br):
    B, C, _ = mk.shape
    T = ML_CTX_CPS * ML_CHUNK
    nc = C // T
    state_specs = [pl.BlockSpec((1,) + s, lambda b, j: (b, 0, 0, 0)) for s in _STATE_SHAPES]
    return pl.pallas_call(
        functools.partial(_mlstm_ctx_kernel, cps=ML_CTX_CPS),
        grid=(B, nc),
        in_specs=_mlstm_specs(T, lambda j: j, False) + _mlstm_specs(T, lambda j: nc - 1 - j, False),
        out_specs=state_specs,
        out_shape=[jax.ShapeDtypeStruct((B,) + s, F32) for s in _STATE_SHAPES],
        compiler_params=_cparams(("arbitrary", "arbitrary")),
        name="mlstm_ctx",
    )(mk, mvt, g, bc, br, mk, mvt, g, bc, br)


def _mixer_call(sink, q, kx, vt, kx_c, vt_c, state, mqt, mk, mvt, g, bc, br):
    B, L, _ = mk.shape
    C = kx_c.shape[1]
    T = TM_MIXER
    assert T == ATT_QB * ATT_BLOCK == ML_CPS * ML_CHUNK
    nc = L // T
    att_in, att_out = _attn_specs(L, C)
    state_specs = [pl.BlockSpec((1,) + s, lambda b, j: (b, 0, 0, 0)) for s in _STATE_SHAPES]
    out_specs = [att_out,
                 pl.BlockSpec((1, ML_V_W, T), lambda b, j: (b, 0, j)),
                 pl.BlockSpec((1, ML_V_W, T), lambda b, j: (b, 0, nc - 1 - j))]
    return pl.pallas_call(
        functools.partial(_mixer_kernel, n_steps=nc, cps=ML_CPS),
        grid=(B, nc),
        in_specs=(att_in + state_specs + _mlstm_specs(T, lambda j: j, True)
                  + _mlstm_specs(T, lambda j: nc - 1 - j, True)),
        out_specs=out_specs,
        out_shape=[jax.ShapeDtypeStruct((B, ATT_Q_W, L), BF16)] + [jax.ShapeDtypeStruct((B, ML_V_W, L), BF16)] * 2,
        scratch_shapes=[pltpu.VMEM(s, F32) for s in _STATE_SHAPES],
        compiler_params=_cparams(("arbitrary", "arbitrary")),
        name="mixer",
    )(sink, q, kx_c, kx, kx, kx, vt_c, vt, vt, vt, *state, mqt, mk, mvt, g, bc, br, mqt, mk, mvt, g, bc, br)


def _merge_kernel(att_ref, hf_ref, hb_ref, so_ref, sga_ref, sgm_ref, x_ref, g1_ref, mod2_ref, mlw_ref, n2w_ref,
                  wa_ref, wm_ref, wo_ref, xmid_ref, h2_ref, y_ref):
    s = pl.program_id(0)
    last = pl.num_programs(0) - 1
    cur = s % 2

    def body(do_branch, do_out):
        if do_out:
            y2 = _dot_tn(y_ref[1 - cur], wo_ref[...])
        if do_branch:
            ya = _dot_tn(wa_ref[...], att_ref[0])
            ht = hf_ref[0].astype(F32) + hb_ref[0].astype(F32)
            parts = []
            for h in range(ML_HEADS):
                seg = ht[h * ML_V_DIM:(h + 1) * ML_V_DIM, :]
                ms = jnp.mean(seg * seg, axis=0, keepdims=True)
                parts.append(seg * lax.rsqrt(ms + EPS))
            ml = (jnp.concatenate(parts, axis=0) * mlw_ref[...] * so_ref[0].astype(F32)).astype(BF16)
            ym = _dot_tn(wm_ref[...], ml)
        if do_out:
            xm = x_ref[0] + g1_ref[0, 0] * y2
            xmid_ref[0] = xm
            ms = jnp.mean(xm * xm, axis=-1, keepdims=True)
            h2 = xm * lax.rsqrt(ms + EPS) * n2w_ref[...]
            h2_ref[0] = (h2 * (1.0 + mod2_ref[1, 0]) + mod2_ref[0, 0]).astype(BF16)
        if do_branch:
            y_ref[cur] = (sga_ref[0].astype(F32) * ya + sgm_ref[0].astype(F32) * ym).astype(BF16)

    pl.when(s == 0)(lambda: body(True, False))
    pl.when((s > 0) & (s < last))(lambda: body(True, True))
    pl.when(s == last)(lambda: body(False, True))


def _merge_call(att_t, hf_t, hb_t, so_t, sga_t, sgm_t, x, mod6, mlw_b, n2w, wa, wm, wo, tm):
    B, L, _ = x.shape
    nt = L // tm
    n_all = B * nt

    def tile_in(s):
        t = jnp.minimum(s, n_all - 1)
        return t // nt, t % nt

    def tile_out(s):
        t = jnp.maximum(s - 1, 0)
        return t // nt, t % nt

    def in_t(h):
        return pl.BlockSpec((1, h, tm), lambda s: (tile_in(s)[0], 0, tile_in(s)[1]))

    def out_rows(w):
        return pl.BlockSpec((1, tm, w), lambda s: (*tile_out(s), 0))

    def const(shape):
        return pl.BlockSpec(shape, lambda s: (0,) * len(shape))

    return pl.pallas_call(
        _merge_kernel,
        grid=(n_all + 1,),
        in_specs=[in_t(D), in_t(D), in_t(D), in_t(D), in_t(D), in_t(D), out_rows(D),
                  _mod_spec(MOD_G1, 1, lambda s: tile_out(s)[0]),
                  _mod_spec(MOD_SH2, 2, lambda s: tile_out(s)[0]),
                  const((D, tm)), const((1, D)), const((D, D)), const((D, D)), const((D, D))],
        out_specs=[out_rows(D), out_rows(D)],
        out_shape=[jax.ShapeDtypeStruct((B, L, D), F32), jax.ShapeDtypeStruct((B, L, D), BF16)],
        scratch_shapes=[pltpu.VMEM((2, D, tm), BF16)],
        compiler_params=_cparams(("arbitrary",)),
        name="merge",
    )(att_t, hf_t, hb_t, so_t, sga_t, sgm_t, x, mod6, mod6, mlw_b, n2w, wa, wm, wo)


HALO = 16
FFN_AHEAD = 1


def _ffn_kernel(h_ref, hp_ref, hn_ref, xmid_ref, mod_ref, wup_ref, cw_ref, cb_ref, wdn_ref, o_ref,
                act_ref, *, n_tiles, tn, dn):
    s = pl.program_id(0)
    last = pl.num_programs(0) - 1

    @pl.when(s == 0)
    def _():
        _ffn_body(h_ref, hp_ref, hn_ref, xmid_ref, mod_ref, wup_ref, cw_ref, cb_ref, wdn_ref, o_ref, act_ref,
                  n_tiles=n_tiles, tn=tn, dn=dn, do_up=True, do_down=False)

    @pl.when((s > 0) & (s < last))
    def _():
        _ffn_body(h_ref, hp_ref, hn_ref, xmid_ref, mod_ref, wup_ref, cw_ref, cb_ref, wdn_ref, o_ref, act_ref,
                  n_tiles=n_tiles, tn=tn, dn=dn, do_up=True, do_down=True)

    @pl.when(s == last)
    def _():
        _ffn_body(h_ref, hp_ref, hn_ref, xmid_ref, mod_ref, wup_ref, cw_ref, cb_ref, wdn_ref, o_ref, act_ref,
                  n_tiles=n_tiles, tn=tn, dn=dn, do_up=False, do_down=True)


def _ffn_body(h_ref, hp_ref, hn_ref, xmid_ref, mod_ref, wup_ref, cw_ref, cb_ref, wdn_ref, o_ref, act_ref,
              *, n_tiles, tn, dn, do_up, do_down):
    s = pl.program_id(0)
    i = s % n_tiles
    cur = s % 2
    tm = h_ref.shape[1]
    n_chunks = D_FF // tn
    n_dn = D // dn
    act_prev = act_ref[1 - cur] if do_down else None

    def down(k):
        cols = slice(k * dn, (k + 1) * dn)
        o_ref[0, :, cols] = xmid_ref[0, :, cols] + mod_ref[0, 0, :, cols] * _dot(act_prev, wdn_ref[:, cols])

    if not do_up:
        for k in range(n_dn):
            down(k)
        return

    h = h_ref[0]
    prev_row = jnp.where(i > 0, hp_ref[0].astype(F32)[HALO - 1:HALO, :], 0.0)
    next_row = jnp.where(i < n_tiles - 1, hn_ref[0].astype(F32)[0:1, :], 0.0)
    top = lax.broadcasted_iota(jnp.int32, (16, D), 0) < 8
    edge = jnp.where(top, prev_row, next_row).astype(BF16)
    row8 = lax.broadcasted_iota(jnp.int32, (8, tn), 0)
    h_ext = jnp.concatenate([h, edge], axis=0)

    def up(c0):
        u_ext = _dot(h_ext, wup_ref[:, c0:c0 + tn])
        return u_ext[:tm], u_ext[tm:]

    def conv(u, ue, c0):
        below = pltpu.roll(u, 1, 0)
        above = pltpu.roll(u, tm - 1, 0)
        below = jnp.concatenate([jnp.where(row8 == 0, ue[0:8], below[0:8]), below[8:]], axis=0)
        above = jnp.concatenate([above[:tm - 8], jnp.where(row8 == 7, ue[8:16], above[tm - 8:])], axis=0)
        cw = cw_ref[:, c0:c0 + tn]
        return cb_ref[:, c0:c0 + tn] + below * cw[0:1] + u * cw[1:2] + above * cw[2:3]

    pending = [(up(c * tn), up(D_FF + c * tn)) for c in range(FFN_AHEAD)]
    done = 0
    for c in range(n_chunks):
        (ua, uae), (ug, uge) = pending.pop(0)
        if c + FFN_AHEAD < n_chunks:
            pending.append((up((c + FFN_AHEAD) * tn), up(D_FF + (c + FFN_AHEAD) * tn)))
        while do_down and done * n_chunks < (c + 1) * n_dn:
            down(done)
            done += 1
        a = conv(ua, uae, c * tn)
        hg = conv(ug, uge, D_FF + c * tn)
        act_ref[cur, :, c * tn:(c + 1) * tn] = ((hg + hg * jnp.tanh(hg)) * a).astype(BF16)


def _ffn_call(h2, xmid, g2, wup, cw, cb, wdn, tm, tn, dn):
    B, L, _ = xmid.shape
    nt = L // tm
    n_all = B * nt
    hb = tm // HALO
    nhb = L // HALO

    def tile_in(s):
        t = jnp.minimum(s, n_all - 1)
        return t // nt, t % nt

    def tile_out(s):
        t = jnp.maximum(s - 1, 0)
        return t // nt, t % nt

    def in_spec():
        return pl.BlockSpec((1, tm, D), lambda s: (*tile_in(s), 0))

    def out_spec():
        return pl.BlockSpec((1, tm, D), lambda s: (*tile_out(s), 0))

    def prev_halo(s):
        b, i = tile_in(s)
        return b, jnp.maximum(i * hb - 1, 0), 0

    def next_halo(s):
        b, i = tile_in(s)
        return b, jnp.minimum((i + 1) * hb, nhb - 1), 0

    def const(shape):
        return pl.BlockSpec(shape, lambda s: (0,) * len(shape))

    return pl.pallas_call(
        functools.partial(_ffn_kernel, n_tiles=nt, tn=tn, dn=dn),
        grid=(n_all + 1,),
        in_specs=[in_spec(),
                  pl.BlockSpec((1, HALO, D), prev_halo),
                  pl.BlockSpec((1, HALO, D), next_halo),
                  out_spec(),
                  _mod_spec(MOD_G2, 1, lambda s: tile_out(s)[0]),
                  const(wup.shape), const(cw.shape), const(cb.shape), const(wdn.shape)],
        out_specs=out_spec(),
        out_shape=jax.ShapeDtypeStruct((B, L, D), F32),
        scratch_shapes=[pltpu.VMEM((2, tm, D_FF), BF16)],
        compiler_params=_cparams(("arbitrary",)),
        name="ffn",
    )(h2, h2, h2, xmid, g2, wup, cw, cb, wdn)


def _pair_perm(n_heads):
    half = ATT_HEAD_DIM // 2
    idx = []
    for p in range(n_heads // 2):
        for sub in range(4):
            head = 2 * p + (sub % 2)
            d0 = (sub // 2) * half
            idx.extend(head * ATT_HEAD_DIM + d0 + e for e in range(half))
    return np.asarray(idx, np.int32)


def _rope_tables(L):
    f32 = np.float32
    rows = L // GRID_W
    row = np.repeat(np.arange(rows, dtype=f32), GRID_W)
    col = np.tile(np.arange(GRID_W, dtype=f32), rows)
    n_freq = ATT_HEAD_DIM // 4
    inv_freq = (f32(ROPE_BASE) ** (-np.arange(n_freq, dtype=f32) / f32(n_freq))).astype(f32)
    ang = np.concatenate([row[:, None] * inv_freq, col[:, None] * inv_freq], axis=-1).astype(f32)
    cos = np.tile(np.cos(ang).astype(f32), (1, 4))
    sin = np.tile(np.sin(ang).astype(f32), (1, 4))
    sign = np.where(np.arange(LANES) < LANES // 2, -1.0, 1.0).astype(f32)
    return jnp.asarray(cos), jnp.asarray(sin * sign)


def kernel(x, c, ctx, c_ctx, w_mod, b_mod, norm1_w, w_in, q_norm_w, k_norm_w, attn_sink, ml_gate_b, ml_norm_w,
           w_branch_att, w_branch_ml, w_out, norm2_w, w_up, conv_w, conv_b, w_down):
    B, L, _ = x.shape
    C = ctx.shape[1]
    assert L % GRID_W == 0 and C % TM_CTX == 0
    assert all(L % t == 0 for t in (TM_INPROJ, TM_MIXER, TM_MERGE, TM_FFN))
    l = 0

    n_rows = -(-(B + 1) // 16) * 16
    cc = jnp.concatenate([c, c_ctx[None, :], jnp.zeros((n_rows - B - 1, D), F32)], axis=0)
    mod6 = _mod_call(cc, w_mod[l], b_mod[l][None, :])

    w = w_in[l]
    qperm = _pair_perm(ATT_HEADS)
    kperm = _pair_perm(ATT_KV_HEADS)
    def pair_cols(wc, n_heads):
        half = ATT_HEAD_DIM // 2
        wc = wc.reshape(D, n_heads // 2, 2, 2, half).transpose(0, 1, 3, 2, 4)
        return wc.reshape(D, n_heads * ATT_HEAD_DIM)

    w_q = pair_cols(w[:, _O_AQ:_O_AQ + ATT_Q_W], ATT_HEADS)
    w_k = pair_cols(w[:, _O_AK:_O_AK + ATT_KV_W], ATT_KV_HEADS)
    w_g = jnp.pad(w[:, _O_MG:_O_MG + ML_GATE_W], ((0, 0), (0, LANES - ML_GATE_W)))
    w_mk = w[:, _O_MK:_O_MK + ML_QK_W] * (ML_QK_DIM ** -0.5)
    w_p = jnp.concatenate([w_q, w_k, w_mk, w_g], axis=1).astype(BF16)
    w_t = w[:, _O_AV:].T.astype(BF16)

    head_of_col = np.concatenate([qperm // ATT_HEAD_DIM, ATT_HEADS + kperm // ATT_HEAD_DIM])
    e_np = (head_of_col[:, None] == np.arange(LANES)[None, :]).astype(np.float32)
    e_mat = jnp.asarray(e_np, BF16)
    et_mat = jnp.asarray(np.concatenate([e_np.T, e_np.T], axis=0), BF16)
    def pair_tiled(wn, n_heads):
        half = ATT_HEAD_DIM // 2
        return jnp.tile(jnp.concatenate([wn[:half], wn[:half], wn[half:], wn[half:]]), n_heads // 2)

    qkw = jnp.concatenate([pair_tiled(q_norm_w[l], ATT_HEADS) * (ATT_SCALE * LOG2E),
                           pair_tiled(k_norm_w[l], ATT_KV_HEADS)])[None, :]
    cos_t, sin_t = _rope_tables(L)
    gb = ml_gate_b[l].reshape(1, ML_GATE_W)
    gbt = ml_gate_b[l].reshape(ML_GATE_W, 1)
    n1w = norm1_w[l][None, :]

    kx_c, mk_c, g_c, bc_c, vt_c, mvt_c, br_c = _ctxproj_call(
        ctx, mod6, B, n1w, w_p, w_t, e_mat, et_mat, qkw, gb, gbt, tm=TM_CTX)
    q, kx, mk, g, bc, vt, mqt, mvt, sot, sgat, sgmt, br = _inproj_call(
        x, mod6, n1w, w_p, w_t, e_mat, et_mat, qkw, cos_t, sin_t, gb, gbt, tm=TM_INPROJ)

    state = _mlstm_ctx_call(mk_c, mvt_c, g_c, bc_c, br_c)
    att_t, hf_t, hb_t = _mixer_call(attn_sink[l], q, kx, vt, kx_c, vt_c, state, mqt, mk, mvt, g, bc, br)

    mlw_b = jnp.broadcast_to(ml_norm_w[l][:, None], (ML_V_W, TM_MERGE))
    xmid, h2 = _merge_call(att_t, hf_t, hb_t, sot, sgat, sgmt, x, mod6, mlw_b, norm2_w[l][None, :],
                           w_branch_att[l].astype(BF16), w_branch_ml[l].astype(BF16), w_out[l].astype(BF16),
                           tm=TM_MERGE)
    gate_half = jnp.where(jnp.arange(2 * D_FF) < D_FF, 1.0, 0.5).astype(F32)
    out = _ffn_call(h2, xmid, mod6, w_up[l].astype(BF16), conv_w[l] * gate_half, (conv_b[l] * gate_half)[None, :],
                    w_down[l].astype(BF16), tm=TM_FFN, tn=FFN_UP_COLS, dn=FFN_DOWN_COLS)
    return out
```

```python
import functools

import jax
import jax.numpy as jnp
import numpy as np
from jax import lax
from jax.experimental import pallas as pl
from jax.experimental.pallas import tpu as pltpu

D = 1024
GRID_W = 64
ATT_HEADS = 16
ATT_KV_HEADS = 4
ATT_HEAD_DIM = 64
ATT_GROUP = ATT_HEADS // ATT_KV_HEADS
ATT_BLOCK = 128
WINDOW = 128
ROPE_BASE = 10000.0
ATT_SCALE = ATT_HEAD_DIM ** -0.5
LOG2E = 1.4426950408889634
ML_HEADS = 4
ML_QK_DIM = 128
ML_V_DIM = 256
ML_CHUNK = 128
D_FF = 2816
EPS = 1e-6
NEG_INF = -1e30

ATT_Q_W = ATT_HEADS * ATT_HEAD_DIM
ATT_KV_W = ATT_KV_HEADS * ATT_HEAD_DIM
ML_QK_W = ML_HEADS * ML_QK_DIM
ML_V_W = ML_HEADS * ML_V_DIM
ML_GATE_W = 2 * 2 * ML_HEADS

LANES = 128
KX_W = ATT_KV_HEADS * 2 * LANES
VMEM_LIMIT = 56 * 1024 * 1024

TM_CTX = 256
TM_INPROJ = 512
TM_MIXER = 512
TM_MERGE = 512
TM_FFN = 512
FFN_UP_COLS = 256
FFN_DOWN_COLS = 256

BF16 = jnp.bfloat16
F32 = jnp.float32

_O_AQ = 0
_O_AK = _O_AQ + ATT_Q_W
_O_AV = _O_AK + ATT_KV_W
_O_MQ = _O_AV + ATT_KV_W
_O_MK = _O_MQ + ML_QK_W
_O_MV = _O_MK + ML_QK_W
_O_MO = _O_MV + ML_V_W
_O_MG = _O_MO + ML_V_W
_O_GA = _O_MG + ML_GATE_W
_O_GM = _O_GA + D

QK_W = ATT_Q_W + ATT_KV_W
_P_QK = 0
_P_MK = _P_QK + QK_W
_P_MG = _P_MK + ML_QK_W
_P_END = _P_MG + LANES
_R_V = 0
_R_MQ = _O_MQ - _O_AV
_R_MV = _O_MV - _O_AV
_R_MO = _O_MO - _O_AV
_R_MG = _O_MG - _O_AV
_R_GA = _O_GA - _O_AV
_R_GM = _O_GM - _O_AV


def _dot(a, b):
    return jnp.dot(a, b, preferred_element_type=F32)


def _dot_nt(a, b):
    return lax.dot_general(a, b, (((1,), (1,)), ((), ())), preferred_element_type=F32)


def _dot_tn(a, b):
    return lax.dot_general(a, b, (((0,), (0,)), ((), ())), preferred_element_type=F32)


def _cparams(sem):
    return pltpu.CompilerParams(dimension_semantics=sem, vmem_limit_bytes=VMEM_LIMIT)


def _mod_kernel(c_ref, w_ref, b_ref, o_ref):
    c = c_ref[...]
    a = c * jax.nn.sigmoid(c)
    a_hi = a.astype(BF16)
    a_lo = (a - a_hi.astype(F32)).astype(BF16)
    w = w_ref[...]
    w_hi = w.astype(BF16)
    w_lo = (w - w_hi.astype(F32)).astype(BF16)
    rows = a.shape[0]
    both = _dot(jnp.concatenate([a_hi, a_lo], axis=0), w_hi)
    o_ref[0, :, 0, :] = both[:rows] + both[rows:] + _dot(a_hi, w_lo) + b_ref[...]


MOD_SH1, MOD_SC1, MOD_SH2, MOD_SC2, MOD_G1, MOD_G2 = range(6)


def _mod_call(cc, w_mod, b_mod):
    rows = cc.shape[0]
    n_seg = w_mod.shape[1] // D
    assert n_seg == 6

    def out_pos(j):
        return jnp.where(j == 2, MOD_G1, jnp.where((j == 3) | (j == 4), j - 1, j))

    return pl.pallas_call(
        _mod_kernel,
        grid=(n_seg,),
        in_specs=[pl.BlockSpec((rows, D), lambda j: (0, 0)),
                  pl.BlockSpec((D, D), lambda j: (0, j)),
                  pl.BlockSpec((1, D), lambda j: (0, j))],
        out_specs=pl.BlockSpec((1, rows, 1, D), lambda j: (out_pos(j), 0, 0, 0)),
        out_shape=jax.ShapeDtypeStruct((n_seg, rows, 1, D), F32),
        compiler_params=_cparams(("arbitrary",)),
        name="mod",
    )(cc, w_mod, b_mod)


def _mod_spec(seg, n_seg, row_of):
    return pl.BlockSpec((n_seg, 1, 1, D), lambda *ids: (seg // n_seg, row_of(*ids), 0, 0))


def _split2(x):
    x1 = x.astype(BF16)
    x2 = (x - x1.astype(F32)).astype(BF16)
    return x1, x2


def _norm_modulate(x, n1w_ref, mod_ref):
    ms = jnp.mean(x * x, axis=-1, keepdims=True)
    y = x * lax.rsqrt(ms + EPS) * n1w_ref[...]
    return (y * (1.0 + mod_ref[1, 0]) + mod_ref[0, 0]).astype(BF16)


def _sigmoid(x):
    return 0.5 * jnp.tanh(0.5 * x) + 0.5


def _head_rms_scale(ss):
    r = lax.rsqrt(ss * (1.0 / ATT_HEAD_DIM) + EPS)
    r_hi = r.astype(BF16)
    r_lo = (r - r_hi.astype(F32)).astype(BF16)
    return jnp.concatenate([r_hi, r_lo], axis=1)


def _store_k_variants(k_ref, pair, o):
    lane = lax.broadcasted_iota(jnp.int32, (1, LANES), 1)
    keep = ((lane // 32) % 2) == 0
    c0 = 4 * pair * LANES
    k_ref[0, :, c0:c0 + LANES] = jnp.where(keep, o, 0.0).astype(BF16)
    k_ref[0, :, c0 + LANES:c0 + 2 * LANES] = jnp.where(keep, 0.0, pltpu.roll(o, 32, 1)).astype(BF16)
    k_ref[0, :, c0 + 2 * LANES:c0 + 3 * LANES] = jnp.where(keep, pltpu.roll(o, 96, 1), 0.0).astype(BF16)
    k_ref[0, :, c0 + 3 * LANES:c0 + 4 * LANES] = jnp.where(keep, 0.0, o).astype(BF16)


def _cum_gates_cols(g16, tri_col):
    T = ML_CHUNK
    lf = jax.nn.log_sigmoid(g16)
    fwd_col = lax.broadcasted_iota(jnp.int32, (1, ML_GATE_W), 1) < ML_GATE_W // 2
    out = []
    for c in range(g16.shape[0] // T):
        r = sum(_dot(tri_col, p) for p in _split2(lf[c * T:(c + 1) * T]))
        out.append(jnp.where(fwd_col, r[:T], r[T:]))
    return jnp.concatenate(out, axis=0)


def _cum_gates_rows(gt16, tri_row):
    T = ML_CHUNK
    lf = jax.nn.log_sigmoid(gt16)
    fwd_row = lax.broadcasted_iota(jnp.int32, (ML_GATE_W, 1), 0) < ML_GATE_W // 2
    out = []
    for c in range(gt16.shape[1] // T):
        r = _dot(jnp.concatenate(_split2(lf[:, c * T:(c + 1) * T]), axis=0), tri_row)
        s = r[:ML_GATE_W] + r[ML_GATE_W:]
        out.append(jnp.where(fwd_row, s[:, :T], s[:, T:]))
    return jnp.concatenate(out, axis=1)


def _inproj_kernel(x_ref, mod_ref, n1w_ref, w_ref, wt_ref, e_ref, et_ref, qkw_ref,
                   cos_ref, sin_ref, gb_ref, gbt_ref, tricol_ref, trirow_ref,
                   q_ref, k_ref, mk_ref, g_ref, bc_ref, vt_ref, mqt_ref, mvt_ref, sot_ref, sgat_ref, sgmt_ref, br_ref,
                   hn_ref):
    hn_ref[...] = _norm_modulate(x_ref[0], n1w_ref, mod_ref)
    hn = hn_ref[...]

    def ft(r0, height):
        return _dot_nt(wt_ref[r0:r0 + height, :], hn)

    acc = _dot(hn, w_ref[:, _P_QK:_P_QK + QK_W])
    mkg = _dot(hn, w_ref[:, _P_MK:_P_END])
    g16 = mkg[:, ML_QK_W:ML_QK_W + ML_GATE_W] + gb_ref[...]
    g_ref[0] = g16
    mk_ref[0] = mkg[:, :ML_QK_W].astype(BF16)
    ss = _dot((acc * acc).astype(BF16), e_ref[...])
    vt_ref[0] = ft(_R_V, ATT_KV_W).astype(BF16)
    mog = ft(_R_MO, ML_V_W + ML_GATE_W)
    sot_ref[0] = _sigmoid(mog[:ML_V_W]).astype(BF16)
    gt16 = mog[ML_V_W:] + gbt_ref[...]
    rb = _dot(_head_rms_scale(ss), et_ref[...])
    sgat_ref[0] = _sigmoid(ft(_R_GA, D)).astype(BF16)
    bc_ref[0] = _cum_gates_cols(g16, tricol_ref[...])

    qn = acc * rb * qkw_ref[...]
    cos = cos_ref[...]
    sin = sin_ref[...]
    for gi in range(QK_W // LANES):
        xs = qn[:, gi * LANES:(gi + 1) * LANES]
        o = xs * cos + pltpu.roll(xs, LANES // 2, 1) * sin
        if gi < ATT_Q_W // LANES:
            q_ref[0, :, gi * LANES:(gi + 1) * LANES] = o.astype(BF16)
        else:
            _store_k_variants(k_ref, gi - ATT_Q_W // LANES, o)

    sgmt_ref[0] = _sigmoid(ft(_R_GM, D)).astype(BF16)
    br_ref[0] = _cum_gates_rows(gt16, trirow_ref[...])
    mqt_ref[0] = ft(_R_MQ, ML_QK_W).astype(BF16)
    mvt_ref[0] = ft(_R_MV, ML_V_W).astype(BF16)


def _ctxproj_kernel(x_ref, mod_ref, n1w_ref, w_ref, wt_ref, e_ref, et_ref, qkw_ref,
                    gb_ref, gbt_ref, tricol_ref, trirow_ref,
                    k_ref, mk_ref, g_ref, bc_ref, vt_ref, mvt_ref, br_ref):
    hn = _norm_modulate(x_ref[0], n1w_ref, mod_ref)
    acc = _dot(hn, w_ref[:, _P_QK + ATT_Q_W:_P_QK + QK_W])
    mkg = _dot(hn, w_ref[:, _P_MK:_P_END])
    g16 = mkg[:, ML_QK_W:ML_QK_W + ML_GATE_W] + gb_ref[...]
    gt16 = _dot_nt(wt_ref[_R_MG:_R_MG + ML_GATE_W, :], hn) + gbt_ref[...]
    g_ref[0] = g16
    mk_ref[0] = mkg[:, :ML_QK_W].astype(BF16)
    ss = _dot((acc * acc).astype(BF16), e_ref[ATT_Q_W:QK_W, :])
    vt_ref[0] = _dot_nt(wt_ref[_R_V:_R_V + ATT_KV_W, :], hn).astype(BF16)
    rb = _dot(_head_rms_scale(ss), et_ref[:, ATT_Q_W:QK_W])
    mvt_ref[0] = _dot_nt(wt_ref[_R_MV:_R_MV + ML_V_W, :], hn).astype(BF16)
    bc_ref[0] = _cum_gates_cols(g16, tricol_ref[...])
    kn = acc * rb * qkw_ref[:, ATT_Q_W:QK_W]
    for pair in range(ATT_KV_W // LANES):
        _store_k_variants(k_ref, pair, kn[:, pair * LANES:(pair + 1) * LANES])
    br_ref[0] = _cum_gates_rows(gt16, trirow_ref[...])


def _chunk_tri():
    r = np.arange(ML_CHUNK)[:, None]
    c = np.arange(ML_CHUNK)[None, :]
    lower, upper = (c <= r), (c >= r)
    return (jnp.asarray(np.concatenate([lower, upper], axis=0), BF16),
            jnp.asarray(np.concatenate([upper, lower], axis=1), BF16))


def _const_spec(shape):
    return pl.BlockSpec(shape, lambda b, i: (0,) * len(shape))


def _inproj_call(x, mod6, n1w, w_p, w_t, e_mat, et_mat, qkw, cos_t, sin_t, gb, gbt, tm):
    B, L, _ = x.shape
    tril, triu = _chunk_tri()

    def rows(w):
        return pl.BlockSpec((1, tm, w), lambda b, i: (b, i, 0))

    def cols(h):
        return pl.BlockSpec((1, h, tm), lambda b, i: (b, 0, i))

    consts = [n1w, w_p, w_t, e_mat, et_mat, qkw]
    tail = [gb, gbt, tril, triu]
    in_specs = ([rows(D), _mod_spec(MOD_SH1, 2, lambda b, i: b)]
                + [_const_spec(a.shape) for a in consts]
                + [pl.BlockSpec((tm, LANES), lambda b, i: (i, 0))] * 2
                + [_const_spec(a.shape) for a in tail])
    out_specs = [rows(ATT_Q_W), rows(KX_W), rows(ML_QK_W), rows(ML_GATE_W), rows(ML_GATE_W),
                 cols(ATT_KV_W), cols(ML_QK_W), cols(ML_V_W), cols(ML_V_W), cols(D), cols(D), cols(ML_GATE_W)]
    out_shape = [
        jax.ShapeDtypeStruct((B, L, ATT_Q_W), BF16),
        jax.ShapeDtypeStruct((B, L, KX_W), BF16),
        jax.ShapeDtypeStruct((B, L, ML_QK_W), BF16),
        jax.ShapeDtypeStruct((B, L, ML_GATE_W), F32),
        jax.ShapeDtypeStruct((B, L, ML_GATE_W), F32),
        jax.ShapeDtypeStruct((B, ATT_KV_W, L), BF16),
        jax.ShapeDtypeStruct((B, ML_QK_W, L), BF16),
        jax.ShapeDtypeStruct((B, ML_V_W, L), BF16),
        jax.ShapeDtypeStruct((B, ML_V_W, L), BF16),
        jax.ShapeDtypeStruct((B, D, L), BF16),
        jax.ShapeDtypeStruct((B, D, L), BF16),
        jax.ShapeDtypeStruct((B, ML_GATE_W, L), F32),
    ]
    return pl.pallas_call(
        _inproj_kernel,
        grid=(B, L // tm),
        in_specs=in_specs,
        out_specs=out_specs,
        out_shape=out_shape,
        scratch_shapes=[pltpu.VMEM((tm, D), BF16)],
        compiler_params=_cparams(("arbitrary", "arbitrary")),
        name="inproj",
    )(x, mod6, *consts, cos_t, sin_t, *tail)


def _ctxproj_call(ctx, mod6, ctx_row, n1w, w_p, w_t, e_mat, et_mat, qkw, gb, gbt, tm):
    B, C, _ = ctx.shape
    tril, triu = _chunk_tri()

    def rows(w):
        return pl.BlockSpec((1, tm, w), lambda b, i: (b, i, 0))

    def cols(h):
        return pl.BlockSpec((1, h, tm), lambda b, i: (b, 0, i))

    consts = [n1w, w_p, w_t, e_mat, et_mat, qkw, gb, gbt, tril, triu]
    out_specs = [rows(KX_W), rows(ML_QK_W), rows(ML_GATE_W), rows(ML_GATE_W),
                 cols(ATT_KV_W), cols(ML_V_W), cols(ML_GATE_W)]
    out_shape = [
        jax.ShapeDtypeStruct((B, C, KX_W), BF16),
        jax.ShapeDtypeStruct((B, C, ML_QK_W), BF16),
        jax.ShapeDtypeStruct((B, C, ML_GATE_W), F32),
        jax.ShapeDtypeStruct((B, C, ML_GATE_W), F32),
        jax.ShapeDtypeStruct((B, ATT_KV_W, C), BF16),
        jax.ShapeDtypeStruct((B, ML_V_W, C), BF16),
        jax.ShapeDtypeStruct((B, ML_GATE_W, C), F32),
    ]
    return pl.pallas_call(
        _ctxproj_kernel,
        grid=(B, C // tm),
        in_specs=([rows(D), _mod_spec(MOD_SH1, 2, lambda b, i: ctx_row)]
                  + [_const_spec(a.shape) for a in consts]),
        out_specs=out_specs,
        out_shape=out_shape,
        compiler_params=_cparams(("arbitrary", "arbitrary")),
        name="ctxproj",
    )(ctx, mod6, *consts)


ATT_AHEAD = 12

ATT_QB = 4


def _attn_stream(sink_ref, q_ref, kc_ref, kp_ref, k0_ref, kn_ref, vc_ref, vp_ref, v0_ref, vn_ref, o_ref,
                 n_steps):
    i = pl.program_id(1)
    T = ATT_BLOCK
    hd = ATT_HEAD_DIM
    s_idx = lax.broadcasted_iota(jnp.int32, (T, 2 * T), 0)
    t_idx = lax.broadcasted_iota(jnp.int32, (T, 2 * T), 1) % T
    first = lax.broadcasted_iota(jnp.int32, (1, 2 * T), 1) < T

    k_own = k0_ref[0]
    v_own = v0_ref[0]
    k_blk = [kp_ref[0]] + [k_own[b * T:(b + 1) * T] for b in range(ATT_QB)] + [kn_ref[0]]
    v_blk = [vp_ref[0]] + [v_own[:, b * T:(b + 1) * T] for b in range(ATT_QB)] + [vn_ref[0]]
    ones_rows = jnp.ones((16, 3 * T + kc_ref.shape[1]), BF16)

    def window(qb):
        k_all = jnp.concatenate(k_blk[qb:qb + 3] + [kc_ref[0]], axis=0)
        vt_all = jnp.concatenate(v_blk[qb:qb + 3] + [vc_ref[0]], axis=1)
        ok_prev = (s_idx >= t_idx) & ((i > 0) if qb == 0 else True)
        ok_next = (s_idx <= t_idx) & ((i < n_steps - 1) if qb == ATT_QB - 1 else True)
        return k_all, vt_all, ok_prev, ok_next

    windows = [window(qb) for qb in range(ATT_QB)]
    per_qb = 2 * ATT_KV_HEADS

    def scores(n):
        qb, r = divmod(n, per_qb)
        kh, var = divmod(r, 2)
        q = q_ref[0, qb * T:(qb + 1) * T, :]
        q2 = jnp.concatenate([q[:, (2 * kh) * LANES:(2 * kh + 1) * LANES],
                              q[:, (2 * kh + 1) * LANES:(2 * kh + 2) * LANES]], axis=0)
        kk = windows[qb][0][:, (2 * kh + var) * LANES:(2 * kh + var + 1) * LANES]
        return _dot_nt(kk, q2)

    def finish(n, st):
        qb, r = divmod(n, per_qb)
        kh, var = divmod(r, 2)
        _, vt_all, ok_prev, ok_next = windows[qb]
        vt = vt_all[kh * hd:(kh + 1) * hd, :]
        st = jnp.concatenate([jnp.where(ok_prev, st[0:T], NEG_INF), st[T:2 * T],
                              jnp.where(ok_next, st[2 * T:3 * T], NEG_INF), st[3 * T:]], axis=0)
        h0 = ATT_GROUP * kh + var
        h1 = h0 + 2
        sink = jnp.where(first, sink_ref[h0], sink_ref[h1]) * LOG2E
        m = jnp.maximum(jnp.max(st, axis=0, keepdims=True), sink)
        p = jnp.exp2(st - m)
        ot = _dot(jnp.concatenate([vt, ones_rows], axis=0), p.astype(BF16))
        denom = ot[hd:hd + 1, :] + jnp.exp2(sink - m)
        ot = ot[0:hd, :] * (1.0 / denom)
        o_ref[0, h0 * hd:(h0 + 1) * hd, qb * T:(qb + 1) * T] = ot[:, 0:T].astype(BF16)
        o_ref[0, h1 * hd:(h1 + 1) * hd, qb * T:(qb + 1) * T] = ot[:, T:2 * T].astype(BF16)

    return ATT_QB * per_qb, scores, finish


def _attn_specs(L, C):
    T = ATT_BLOCK
    nb = L // T
    TQ = ATT_QB * T

    def edge(i, off):
        return jnp.clip(i * ATT_QB + (off if off < 0 else ATT_QB), 0, nb - 1)

    in_specs = [pl.BlockSpec(memory_space=pltpu.SMEM),
                pl.BlockSpec((1, TQ, ATT_Q_W), lambda b, i: (b, i, 0)),
                pl.BlockSpec((1, C, KX_W), lambda b, i: (b, 0, 0)),
                pl.BlockSpec((1, T, KX_W), lambda b, i: (b, edge(i, -1), 0)),
                pl.BlockSpec((1, TQ, KX_W), lambda b, i: (b, i, 0)),
                pl.BlockSpec((1, T, KX_W), lambda b, i: (b, edge(i, 1), 0)),
                pl.BlockSpec((1, ATT_KV_W, C), lambda b, i: (b, 0, 0)),
                pl.BlockSpec((1, ATT_KV_W, T), lambda b, i: (b, 0, edge(i, -1))),
                pl.BlockSpec((1, ATT_KV_W, TQ), lambda b, i: (b, 0, i)),
                pl.BlockSpec((1, ATT_KV_W, T), lambda b, i: (b, 0, edge(i, 1)))]
    return in_specs, pl.BlockSpec((1, ATT_Q_W, TQ), lambda b, i: (b, 0, i))


N_CHAIN = 2 * ML_HEADS


def _mlstm_load_state(c_ref, n_ref, m_ref):
    return [(c_ref[ci], n_ref[ci], m_ref[ci, 0:1, 0:1]) for ci in range(N_CHAIN)]


def _mlstm_phase1(dirs, state, item, with_h, cps):
    T = ML_CHUNK
    sub, rest = divmod(item, N_CHAIN)
    d, h = divmod(rest, ML_HEADS)
    qt_ref, k_ref, vt_ref, g_ref, bc_ref, br_ref, h_ref = dirs[d]
    row = lax.broadcasted_iota(jnp.int32, (T, T), 0)
    col = lax.broadcasted_iota(jnp.int32, (T, T), 1)
    mask = (col >= row) if d == 0 else (col <= row)
    last = T - 1 if d == 0 else 0
    sc = sub if d == 0 else cps - 1 - sub
    tok = slice(sc * T, (sc + 1) * T)
    ci = d * ML_HEADS + h
    gi = d * 2 * ML_HEADS + h
    fi = gi + ML_HEADS
    k = k_ref[0, tok, h * ML_QK_DIM:(h + 1) * ML_QK_DIM]
    vt = vt_ref[0, h * ML_V_DIM:(h + 1) * ML_V_DIM, tok]
    u_col = g_ref[0, tok, gi:gi + 1] - bc_ref[0, tok, fi:fi + 1]
    b_row = br_ref[0, fi:fi + 1, tok]
    ct_old, n_old, m_old = state[ci]
    qt = st = qn2 = None
    if with_h:
        qt = qt_ref[0, h * ML_QK_DIM:(h + 1) * ML_QK_DIM, tok]
        st = _dot(k, qt)
        top = lax.broadcasted_iota(jnp.int32, (8, ML_QK_DIM), 0) == 0
        n_hi = n_old.astype(BF16)
        n_lo = (n_old - n_hi.astype(F32)).astype(BF16)
        qn2 = _dot(jnp.where(top, n_hi, n_lo), qt)
    m_last = jnp.maximum(jnp.max(u_col, axis=0, keepdims=True), m_old)
    decay = jnp.exp(m_old - m_last)
    kw = (k.astype(F32) * jnp.exp(u_col - m_last)).astype(BF16)
    state[ci] = (decay * ct_old + _dot(vt, kw), decay * n_old + _dot(jnp.ones((8, T), BF16), kw),
                 b_row[:, last:last + 1] + m_last)
    return (h, h_ref, tok, mask, qt, vt, u_col, b_row, m_old, ct_old, st, qn2)


def _mlstm_phase2(chain):
    h, h_ref, tok, mask, qt, vt, u_col, b_row, m_old, ct_old, st, qn2 = chain
    umat = jnp.where(mask, u_col, -jnp.inf)
    m_row = jnp.maximum(jnp.max(umat, axis=0, keepdims=True), m_old)
    pt = st * jnp.exp(umat - m_row)
    w_int = jnp.exp(m_old - m_row)
    e_row = jnp.exp(-(b_row + m_row))
    nq = jnp.sum(pt, axis=0, keepdims=True) + w_int * (qn2[0:1, :] + qn2[1:2, :])
    den = jnp.maximum(jnp.abs(nq), e_row)
    lhs = jnp.concatenate([vt, ct_old.astype(BF16)], axis=1)
    rhs = jnp.concatenate([pt.astype(BF16), (qt.astype(F32) * w_int).astype(BF16)], axis=0)
    h_ref[0, h * ML_V_DIM:(h + 1) * ML_V_DIM, tok] = (_dot(lhs, rhs) * (1.0 / den)).astype(BF16)


def _mlstm_commit(state, c_ref, n_ref, m_ref):
    for ci, (c_new, n_new, m_new) in enumerate(state):
        c_ref[ci] = c_new
        n_ref[ci] = n_new
        m_ref[ci] = jnp.broadcast_to(m_new, (8, LANES))


def _mlstm_ctx_kernel(kf_ref, vtf_ref, gf_ref, bcf_ref, brf_ref, kb_ref, vtb_ref, gb_ref, bcb_ref, brb_ref,
                      c_ref, n_ref, m_ref, *, cps):
    @pl.when(pl.program_id(1) == 0)
    def _():
        c_ref[...] = jnp.zeros_like(c_ref)
        n_ref[...] = jnp.zeros_like(n_ref)
        m_ref[...] = jnp.zeros_like(m_ref)

    dirs = ((None, kf_ref, vtf_ref, gf_ref, bcf_ref, brf_ref, None),
            (None, kb_ref, vtb_ref, gb_ref, bcb_ref, brb_ref, None))
    state_refs = (c_ref.at[0], n_ref.at[0], m_ref.at[0])
    state = _mlstm_load_state(*state_refs)
    for item in range(cps * N_CHAIN):
        _mlstm_phase1(dirs, state, item, with_h=False, cps=cps)
    _mlstm_commit(state, *state_refs)


def _mixer_kernel(sink_ref, q_ref, kc_ref, kp_ref, k0_ref, kn_ref, vc_ref, vp_ref, v0_ref, vn_ref,
                  c0_ref, n0_ref, m0_ref, qtf_ref, kf_ref, vtf_ref, gf_ref, bcf_ref, brf_ref,
                  qtb_ref, kb_ref, vtb_ref, gb_ref, bcb_ref, brb_ref,
                  att_ref, hf_ref, hb_ref, c_ref, n_ref, m_ref, *, n_steps, cps):
    @pl.when(pl.program_id(1) == 0)
    def _():
        c_ref[...] = c0_ref[0]
        n_ref[...] = n0_ref[0]
        m_ref[...] = m0_ref[0]

    n_iter, scores, finish = _attn_stream(sink_ref, q_ref, kc_ref, kp_ref, k0_ref, kn_ref,
                                          vc_ref, vp_ref, v0_ref, vn_ref, att_ref, n_steps)
    pending = [scores(n) for n in range(ATT_AHEAD)]
    dirs = ((qtf_ref, kf_ref, vtf_ref, gf_ref, bcf_ref, brf_ref, hf_ref),
            (qtb_ref, kb_ref, vtb_ref, gb_ref, bcb_ref, brb_ref, hb_ref))
    state = _mlstm_load_state(c_ref, n_ref, m_ref)
    n_items = cps * N_CHAIN
    items = [_mlstm_phase1(dirs, state, it, with_h=True, cps=cps) for it in range(ML_AHEAD)]
    for n in range(max(n_iter, n_items)):
        if n < n_iter:
            st = pending.pop(0)
            if n + ATT_AHEAD < n_iter:
                pending.append(scores(n + ATT_AHEAD))
            finish(n, st)
        if n < n_items:
            if n + ML_AHEAD < n_items:
                items.append(_mlstm_phase1(dirs, state, n + ML_AHEAD, with_h=True, cps=cps))
            _mlstm_phase2(items.pop(0))
    _mlstm_commit(state, c_ref, n_ref, m_ref)


def _mlstm_specs(T, order, with_q):
    specs = [
        pl.BlockSpec((1, ML_QK_W, T), lambda b, j: (b, 0, order(j))),
        pl.BlockSpec((1, T, ML_QK_W), lambda b, j: (b, order(j), 0)),
        pl.BlockSpec((1, ML_V_W, T), lambda b, j: (b, 0, order(j))),
        pl.BlockSpec((1, T, ML_GATE_W), lambda b, j: (b, order(j), 0)),
        pl.BlockSpec((1, T, ML_GATE_W), lambda b, j: (b, order(j), 0)),
        pl.BlockSpec((1, ML_GATE_W, T), lambda b, j: (b, 0, order(j))),
    ]
    return specs if with_q else specs[1:]


_STATE_SHAPES = ((N_CHAIN, ML_V_DIM, ML_QK_DIM), (N_CHAIN, 8, ML_QK_DIM), (N_CHAIN, 8, LANES))


ML_AHEAD = 32
ML_CPS = 4
ML_CTX_CPS = 2


def _mlstm_ctx_call(mk, mvt, g, bc, br):
    B, C, _ = mk.shape
    T = ML_CTX_CPS * ML_CHUNK
    nc = C // T
    state_specs = [pl.BlockSpec((1,) + s, lambda b, j: (b, 0, 0, 0)) for s in _STATE_SHAPES]
    return pl.pallas_call(
        functools.partial(_mlstm_ctx_kernel, cps=ML_CTX_CPS),
        grid=(B, nc),
        in_specs=_mlstm_specs(T, lambda j: j, False) + _mlstm_specs(T, lambda j: nc - 1 - j, False),
        out_specs=state_specs,
        out_shape=[jax.ShapeDtypeStruct((B,) + s, F32) for s in _STATE_SHAPES],
        compiler_params=_cparams(("arbitrary", "arbitrary")),
        name="mlstm_ctx",
    )(mk, mvt, g, bc, br, mk, mvt, g, bc, br)


def _mixer_call(sink, q, kx, vt, kx_c, vt_c, state, mqt, mk, mvt, g, bc, br):
    B, L, _ = mk.shape
    C = kx_c.shape[1]
    T = TM_MIXER
    assert T == ATT_QB * ATT_BLOCK == ML_CPS * ML_CHUNK
    nc = L // T
    att_in, att_out = _attn_specs(L, C)
    state_specs = [pl.BlockSpec((1,) + s, lambda b, j: (b, 0, 0, 0)) for s in _STATE_SHAPES]
    out_specs = [att_out,
                 pl.BlockSpec((1, ML_V_W, T), lambda b, j: (b, 0, j)),
                 pl.BlockSpec((1, ML_V_W, T), lambda b, j: (b, 0, nc - 1 - j))]
    return pl.pallas_call(
        functools.partial(_mixer_kernel, n_steps=nc, cps=ML_CPS),
        grid=(B, nc),
        in_specs=(att_in + state_specs + _mlstm_specs(T, lambda j: j, True)
                  + _mlstm_specs(T, lambda j: nc - 1 - j, True)),
        out_specs=out_specs,
        out_shape=[jax.ShapeDtypeStruct((B, ATT_Q_W, L), BF16)] + [jax.ShapeDtypeStruct((B, ML_V_W, L), BF16)] * 2,
        scratch_shapes=[pltpu.VMEM(s, F32) for s in _STATE_SHAPES],
        compiler_params=_cparams(("arbitrary", "arbitrary")),
        name="mixer",
    )(sink, q, kx_c, kx, kx, kx, vt_c, vt, vt, vt, *state, mqt, mk, mvt, g, bc, br, mqt, mk, mvt, g, bc, br)


def _merge_kernel(att_ref, hf_ref, hb_ref, so_ref, sga_ref, sgm_ref, x_ref, g1_ref, mod2_ref, mlw_ref, n2w_ref,
                  wa_ref, wm_ref, wo_ref, xmid_ref, h2_ref, y_ref):
    s = pl.program_id(0)
    last = pl.num_programs(0) - 1
    cur = s % 2

    def body(do_branch, do_out):
        if do_out:
            y2 = _dot_tn(y_ref[1 - cur], wo_ref[...])
        if do_branch:
            ya = _dot_tn(wa_ref[...], att_ref[0])
            ht = hf_ref[0].astype(F32) + hb_ref[0].astype(F32)
            parts = []
            for h in range(ML_HEADS):
                seg = ht[h * ML_V_DIM:(h + 1) * ML_V_DIM, :]
                ms = jnp.mean(seg * seg, axis=0, keepdims=True)
                parts.append(seg * lax.rsqrt(ms + EPS))
            ml = (jnp.concatenate(parts, axis=0) * mlw_ref[...] * so_ref[0].astype(F32)).astype(BF16)
            ym = _dot_tn(wm_ref[...], ml)
        if do_out:
            xm = x_ref[0] + g1_ref[0, 0] * y2
            xmid_ref[0] = xm
            ms = jnp.mean(xm * xm, axis=-1, keepdims=True)
            h2 = xm * lax.rsqrt(ms + EPS) * n2w_ref[...]
            h2_ref[0] = (h2 * (1.0 + mod2_ref[1, 0]) + mod2_ref[0, 0]).astype(BF16)
        if do_branch:
            y_ref[cur] = (sga_ref[0].astype(F32) * ya + sgm_ref[0].astype(F32) * ym).astype(BF16)

    pl.when(s == 0)(lambda: body(True, False))
    pl.when((s > 0) & (s < last))(lambda: body(True, True))
    pl.when(s == last)(lambda: body(False, True))


def _merge_call(att_t, hf_t, hb_t, so_t, sga_t, sgm_t, x, mod6, mlw_b, n2w, wa, wm, wo, tm):
    B, L, _ = x.shape
    nt = L // tm
    n_all = B * nt

    def tile_in(s):
        t = jnp.minimum(s, n_all - 1)
        return t // nt, t % nt

    def tile_out(s):
        t = jnp.maximum(s - 1, 0)
        return t // nt, t % nt

    def in_t(h):
        return pl.BlockSpec((1, h, tm), lambda s: (tile_in(s)[0], 0, tile_in(s)[1]))

    def out_rows(w):
        return pl.BlockSpec((1, tm, w), lambda s: (*tile_out(s), 0))

    def const(shape):
        return pl.BlockSpec(shape, lambda s: (0,) * len(shape))

    return pl.pallas_call(
        _merge_kernel,
        grid=(n_all + 1,),
        in_specs=[in_t(D), in_t(D), in_t(D), in_t(D), in_t(D), in_t(D), out_rows(D),
                  _mod_spec(MOD_G1, 1, lambda s: tile_out(s)[0]),
                  _mod_spec(MOD_SH2, 2, lambda s: tile_out(s)[0]),
                  const((D, tm)), const((1, D)), const((D, D)), const((D, D)), const((D, D))],
        out_specs=[out_rows(D), out_rows(D)],
        out_shape=[jax.ShapeDtypeStruct((B, L, D), F32), jax.ShapeDtypeStruct((B, L, D), BF16)],
        scratch_shapes=[pltpu.VMEM((2, D, tm), BF16)],
        compiler_params=_cparams(("arbitrary",)),
        name="merge",
    )(att_t, hf_t, hb_t, so_t, sga_t, sgm_t, x, mod6, mod6, mlw_b, n2w, wa, wm, wo)


HALO = 16
FFN_AHEAD = 1


def _ffn_kernel(h_ref, hp_ref, hn_ref, xmid_ref, mod_ref, wup_ref, cw_ref, cb_ref, wdn_ref, o_ref,
                act_ref, *, n_tiles, tn, dn):
    s = pl.program_id(0)
    last = pl.num_programs(0) - 1

    @pl.when(s == 0)
    def _():
        _ffn_body(h_ref, hp_ref, hn_ref, xmid_ref, mod_ref, wup_ref, cw_ref, cb_ref, wdn_ref, o_ref, act_ref,
                  n_tiles=n_tiles, tn=tn, dn=dn, do_up=True, do_down=False)

    @pl.when((s > 0) & (s < last))
    def _():
        _ffn_body(h_ref, hp_ref, hn_ref, xmid_ref, mod_ref, wup_ref, cw_ref, cb_ref, wdn_ref, o_ref, act_ref,
                  n_tiles=n_tiles, tn=tn, dn=dn, do_up=True, do_down=True)

    @pl.when(s == last)
    def _():
        _ffn_body(h_ref, hp_ref, hn_ref, xmid_ref, mod_ref, wup_ref, cw_ref, cb_ref, wdn_ref, o_ref, act_ref,
                  n_tiles=n_tiles, tn=tn, dn=dn, do_up=False, do_down=True)


def _ffn_body(h_ref, hp_ref, hn_ref, xmid_ref, mod_ref, wup_ref, cw_ref, cb_ref, wdn_ref, o_ref, act_ref,
              *, n_tiles, tn, dn, do_up, do_down):
    s = pl.program_id(0)
    i = s % n_tiles
    cur = s % 2
    tm = h_ref.shape[1]
    n_chunks = D_FF // tn
    n_dn = D // dn
    act_prev = act_ref[1 - cur] if do_down else None

    def down(k):
        cols = slice(k * dn, (k + 1) * dn)
        o_ref[0, :, cols] = xmid_ref[0, :, cols] + mod_ref[0, 0, :, cols] * _dot(act_prev, wdn_ref[:, cols])

    if not do_up:
        for k in range(n_dn):
            down(k)
        return

    h = h_ref[0]
    prev_row = jnp.where(i > 0, hp_ref[0].astype(F32)[HALO - 1:HALO, :], 0.0)
    next_row = jnp.where(i < n_tiles - 1, hn_ref[0].astype(F32)[0:1, :], 0.0)
    top = lax.broadcasted_iota(jnp.int32, (16, D), 0) < 8
    edge = jnp.where(top, prev_row, next_row).astype(BF16)
    row8 = lax.broadcasted_iota(jnp.int32, (8, tn), 0)
    h_ext = jnp.concatenate([h, edge], axis=0)

    def up(c0):
        u_ext = _dot(h_ext, wup_ref[:, c0:c0 + tn])
        return u_ext[:tm], u_ext[tm:]

    def conv(u, ue, c0):
        below = pltpu.roll(u, 1, 0)
        above = pltpu.roll(u, tm - 1, 0)
        below = jnp.concatenate([jnp.where(row8 == 0, ue[0:8], below[0:8]), below[8:]], axis=0)
        above = jnp.concatenate([above[:tm - 8], jnp.where(row8 == 7, ue[8:16], above[tm - 8:])], axis=0)
        cw = cw_ref[:, c0:c0 + tn]
        return cb_ref[:, c0:c0 + tn] + below * cw[0:1] + u * cw[1:2] + above * cw[2:3]

    pending = [(up(c * tn), up(D_FF + c * tn)) for c in range(FFN_AHEAD)]
    done = 0
    for c in range(n_chunks):
        (ua, uae), (ug, uge) = pending.pop(0)
        if c + FFN_AHEAD < n_chunks:
            pending.append((up((c + FFN_AHEAD) * tn), up(D_FF + (c + FFN_AHEAD) * tn)))
        while do_down and done * n_chunks < (c + 1) * n_dn:
            down(done)
            done += 1
        a = conv(ua, uae, c * tn)
        hg = conv(ug, uge, D_FF + c * tn)
        act_ref[cur, :, c * tn:(c + 1) * tn] = ((hg + hg * jnp.tanh(hg)) * a).astype(BF16)


def _ffn_call(h2, xmid, g2, wup, cw, cb, wdn, tm, tn, dn):
    B, L, _ = xmid.shape
    nt = L // tm
    n_all = B * nt
    hb = tm // HALO
    nhb = L // HALO

    def tile_in(s):
        t = jnp.minimum(s, n_all - 1)
        return t // nt, t % nt

    def tile_out(s):
        t = jnp.maximum(s - 1, 0)
        return t // nt, t % nt

    def in_spec():
        return pl.BlockSpec((1, tm, D), lambda s: (*tile_in(s), 0))

    def out_spec():
        return pl.BlockSpec((1, tm, D), lambda s: (*tile_out(s), 0))

    def prev_halo(s):
        b, i = tile_in(s)
        return b, jnp.maximum(i * hb - 1, 0), 0

    def next_halo(s):
        b, i = tile_in(s)
        return b, jnp.minimum((i + 1) * hb, nhb - 1), 0

    def const(shape):
        return pl.BlockSpec(shape, lambda s: (0,) * len(shape))

    return pl.pallas_call(
        functools.partial(_ffn_kernel, n_tiles=nt, tn=tn, dn=dn),
        grid=(n_all + 1,),
        in_specs=[in_spec(),
                  pl.BlockSpec((1, HALO, D), prev_halo),
                  pl.BlockSpec((1, HALO, D), next_halo),
                  out_spec(),
                  _mod_spec(MOD_G2, 1, lambda s: tile_out(s)[0]),
                  const(wup.shape), const(cw.shape), const(cb.shape), const(wdn.shape)],
        out_specs=out_spec(),
        out_shape=jax.ShapeDtypeStruct((B, L, D), F32),
        scratch_shapes=[pltpu.VMEM((2, tm, D_FF), BF16)],
        compiler_params=_cparams(("arbitrary",)),
        name="ffn",
    )(h2, h2, h2, xmid, g2, wup, cw, cb, wdn)


def _pair_perm(n_heads):
    half = ATT_HEAD_DIM // 2
    idx = []
    for p in range(n_heads // 2):
        for sub in range(4):
            head = 2 * p + (sub % 2)
            d0 = (sub // 2) * half
            idx.extend(head * ATT_HEAD_DIM + d0 + e for e in range(half))
    return np.asarray(idx, np.int32)


def _rope_tables(L):
    f32 = np.float32
    rows = L // GRID_W
    row = np.repeat(np.arange(rows, dtype=f32), GRID_W)
    col = np.tile(np.arange(GRID_W, dtype=f32), rows)
    n_freq = ATT_HEAD_DIM // 4
    inv_freq = (f32(ROPE_BASE) ** (-np.arange(n_freq, dtype=f32) / f32(n_freq))).astype(f32)
    ang = np.concatenate([row[:, None] * inv_freq, col[:, None] * inv_freq], axis=-1).astype(f32)
    cos = np.tile(np.cos(ang).astype(f32), (1, 4))
    sin = np.tile(np.sin(ang).astype(f32), (1, 4))
    sign = np.where(np.arange(LANES) < LANES // 2, -1.0, 1.0).astype(f32)
    return jnp.asarray(cos), jnp.asarray(sin * sign)


def kernel(x, c, ctx, c_ctx, w_mod, b_mod, norm1_w, w_in, q_norm_w, k_norm_w, attn_sink, ml_gate_b, ml_norm_w,
           w_branch_att, w_branch_ml, w_out, norm2_w, w_up, conv_w, conv_b, w_down):
    B, L, _ = x.shape
    C = ctx.shape[1]
    assert L % GRID_W == 0 and C % TM_CTX == 0
    assert all(L % t == 0 for t in (TM_INPROJ, TM_MIXER, TM_MERGE, TM_FFN))
    assert w_in.shape[0] == 1
    l = 0

    n_rows = -(-(B + 1) // 16) * 16
    cc = jnp.concatenate([c, c_ctx[None, :], jnp.zeros((n_rows - B - 1, D), F32)], axis=0)
    mod6 = _mod_call(cc, w_mod[l], b_mod[l][None, :])

    w = w_in[l]
    qperm = _pair_perm(ATT_HEADS)
    kperm = _pair_perm(ATT_KV_HEADS)
    def pair_cols(wc, n_heads):
        half = ATT_HEAD_DIM // 2
        wc = wc.reshape(D, n_heads // 2, 2, 2, half).transpose(0, 1, 3, 2, 4)
        return wc.reshape(D, n_heads * ATT_HEAD_DIM)

    w_q = pair_cols(w[:, _O_AQ:_O_AQ + ATT_Q_W], ATT_HEADS)
    w_k = pair_cols(w[:, _O_AK:_O_AK + ATT_KV_W], ATT_KV_HEADS)
    w_g = jnp.pad(w[:, _O_MG:_O_MG + ML_GATE_W], ((0, 0), (0, LANES - ML_GATE_W)))
    w_mk = w[:, _O_MK:_O_MK + ML_QK_W] * (ML_QK_DIM ** -0.5)
    w_p = jnp.concatenate([w_q, w_k, w_mk, w_g], axis=1).astype(BF16)
    w_t = w[:, _O_AV:].T.astype(BF16)

    head_of_col = np.concatenate([qperm // ATT_HEAD_DIM, ATT_HEADS + kperm // ATT_HEAD_DIM])
    e_np = (head_of_col[:, None] == np.arange(LANES)[None, :]).astype(np.float32)
    e_mat = jnp.asarray(e_np, BF16)
    et_mat = jnp.asarray(np.concatenate([e_np.T, e_np.T], axis=0), BF16)
    def pair_tiled(wn, n_heads):
        half = ATT_HEAD_DIM // 2
        return jnp.tile(jnp.concatenate([wn[:half], wn[:half], wn[half:], wn[half:]]), n_heads // 2)

    qkw = jnp.concatenate([pair_tiled(q_norm_w[l], ATT_HEADS) * (ATT_SCALE * LOG2E),
                           pair_tiled(k_norm_w[l], ATT_KV_HEADS)])[None, :]
    cos_t, sin_t = _rope_tables(L)
    gb = ml_gate_b[l].reshape(1, ML_GATE_W)
    gbt = ml_gate_b[l].reshape(ML_GATE_W, 1)
    n1w = norm1_w[l][None, :]

    kx_c, mk_c, g_c, bc_c, vt_c, mvt_c, br_c = _ctxproj_call(
        ctx, mod6, B, n1w, w_p, w_t, e_mat, et_mat, qkw, gb, gbt, tm=TM_CTX)
    q, kx, mk, g, bc, vt, mqt, mvt, sot, sgat, sgmt, br = _inproj_call(
        x, mod6, n1w, w_p, w_t, e_mat, et_mat, qkw, cos_t, sin_t, gb, gbt, tm=TM_INPROJ)

    state = _mlstm_ctx_call(mk_c, mvt_c, g_c, bc_c, br_c)
    att_t, hf_t, hb_t = _mixer_call(attn_sink[l], q, kx, vt, kx_c, vt_c, state, mqt, mk, mvt, g, bc, br)

    mlw_b = jnp.broadcast_to(ml_norm_w[l][:, None], (ML_V_W, TM_MERGE))
    xmid, h2 = _merge_call(att_t, hf_t, hb_t, sot, sgat, sgmt, x, mod6, mlw_b, norm2_w[l][None, :],
                           w_branch_att[l].astype(BF16), w_branch_ml[l].astype(BF16), w_out[l].astype(BF16),
                           tm=TM_MERGE)
    gate_half = jnp.where(jnp.arange(2 * D_FF) < D_FF, 1.0, 0.5).astype(F32)
    out = _ffn_call(h2, xmid, mod6, w_up[l].astype(BF16), conv_w[l] * gate_half, (conv_b[l] * gate_half)[None, :],
                    w_down[l].astype(BF16), tm=TM_FFN, tn=FFN_UP_COLS, dn=FFN_DOWN_COLS)
    return out
```
